```python
import jax, jax.numpy as jnp
from jax import lax
import numpy as np

D_MODEL = 1024
BATCH = 2
SEQ = 8192
DEPTH = 1
DEC_BATCH = 8
DEC_SEQ = 16
PAST_LEN = 1024

CHUNK = 64
Q_BLOCK = 128
N_HEADS = 8
QK_NOPE = 64
QK_ROPE = 32
V_HEAD = 64
Q_LORA = 256
KV_LORA = 128
MLA_WIDTH = N_HEADS * V_HEAD
CONV_WIDTH = 512
CONV_K = 3
ROPE_BASE = 10000.0
EPS = 1e-6
SM_SCALE = (QK_NOPE + QK_ROPE) ** -0.5
NEG_INF = -1e30
IN_COLS = Q_LORA + KV_LORA + QK_ROPE + MLA_WIDTH + 4 * CONV_WIDTH + 2 * D_MODEL

kernel_name = "hybrid_mla_shortconv_stream_step"


def _rms_norm(x, g):
    xf = x.astype(jnp.float32)
    y = xf * lax.rsqrt(jnp.mean(xf * xf, axis=-1, keepdims=True) + EPS)
    return (y * g.astype(jnp.float32)).astype(x.dtype)


def _rope(x, pos):
    half = QK_ROPE // 2
    inv = ROPE_BASE ** (-jnp.arange(half, dtype=jnp.float32) / half)
    ang = pos.astype(jnp.float32)[:, None] * inv[None, :]
    cos, sin = jnp.cos(ang), jnp.sin(ang)
    if x.ndim == 4:
        cos, sin = cos[:, None, :], sin[:, None, :]
    xf = x.astype(jnp.float32)
    x1, x2 = xf[..., :half], xf[..., half:]
    return jnp.concatenate([x1 * cos - x2 * sin, x2 * cos + x1 * sin], axis=-1).astype(x.dtype)


def _split_cols(p):
    sizes = (Q_LORA, KV_LORA, QK_ROPE, MLA_WIDTH, CONV_WIDTH, CONV_WIDTH, CONV_WIDTH, CONV_WIDTH, D_MODEL, D_MODEL)
    out, start = [], 0
    for s in sizes:
        out.append(p[..., start:start + s])
        start += s
    return out


def _attend(q_abs, q_pe, c_kv, k_pe, qpos, kpos):
    s = (jnp.einsum('bthc,bsc->bhts', q_abs, c_kv)
         + jnp.einsum('bthr,bsr->bhts', q_pe, k_pe)).astype(jnp.float32) * SM_SCALE
    visible = (kpos[None, :] // CHUNK) <= (qpos[:, None] // CHUNK)
    s = jnp.where(visible[None, None], s, NEG_INF)
    p = jax.nn.softmax(s, axis=-1).astype(c_kv.dtype)
    return jnp.einsum('bhts,bsc->bthc', p, c_kv)


def _layer(x, pos, past_ckv, past_kpe, past_conv, blocked,
           pre_g, w_in, q_g, w_uq, kv_g, w_uk, w_uv, w_o_mla, conv_w, w_o_conv, w_out, post_g):
    b, t, _ = x.shape
    h = _rms_norm(x, pre_g)
    q_lat, kv_lat, k_rope, g_mla, c_b, c_c, c_x, g_conv, m_mla, m_conv = _split_cols(h @ w_in)

    q = (_rms_norm(q_lat, q_g) @ w_uq).reshape(b, t, N_HEADS, QK_NOPE + QK_ROPE)
    q_nope = q[..., :QK_NOPE]
    q_pe = _rope(q[..., QK_NOPE:], pos)
    c_kv = _rms_norm(kv_lat, kv_g)
    k_pe = _rope(k_rope, pos)
    q_abs = jnp.einsum('bthd,chd->bthc', q_nope, w_uk)
    if past_ckv is None:
        keys_c, keys_r, kpos = c_kv, k_pe, pos
    else:
        keys_c = jnp.concatenate([past_ckv, c_kv], axis=1)
        keys_r = jnp.concatenate([past_kpe, k_pe], axis=1)
        kpos = jnp.concatenate([jnp.arange(past_ckv.shape[1], dtype=jnp.int32), pos])
    if blocked:
        nb = t // Q_BLOCK

        def to_blocks(a):
            return jnp.moveaxis(a.reshape((b, nb, Q_BLOCK) + a.shape[2:]), 1, 0)

        o_lat = lax.map(lambda xs: _attend(xs[0], xs[1], keys_c, keys_r, xs[2], kpos),
                        (to_blocks(q_abs), to_blocks(q_pe), pos.reshape(nb, Q_BLOCK)))
        o_lat = jnp.moveaxis(o_lat, 0, 1).reshape(b, t, N_HEADS, KV_LORA)
    else:
        o_lat = _attend(q_abs, q_pe, keys_c, keys_r, pos, kpos)
    o = jnp.einsum('bthc,chd->bthd', o_lat, w_uv).reshape(b, t, MLA_WIDTH)
    branch_a = (o * jax.nn.silu(g_mla)) @ w_o_mla

    u = c_c * c_x
    u_ext = jnp.concatenate([past_conv, u], axis=1)
    conv = conv_w[0] * u_ext[:, 0:t]
    for k in range(1, CONV_K):
        conv = conv + conv_w[k] * u_ext[:, k:k + t]
    branch_b = (c_b * conv * jax.nn.silu(g_conv)) @ w_o_conv

    merged = jax.nn.sigmoid(m_mla) * branch_a + jax.nn.sigmoid(m_conv) * branch_b
    y = x + _rms_norm(merged @ w_out, post_g)
    return y, c_kv, k_pe, u_ext[:, -(CONV_K - 1):]


def setup_inputs(seed: int = 0) -> dict:
    key = jax.random.key(seed)
    ks = jax.random.split(key, 17)
    f32 = jnp.float32
    nrm = lambda k, shape, s=1.0: jax.random.normal(k, shape, f32) * s
    gain = lambda k, n: 1.0 + 0.01 * jax.random.normal(k, (DEPTH, n), f32)
    return {
        "x_prompt": nrm(ks[0], (BATCH, SEQ, D_MODEL)),
        "x_sample": nrm(ks[1], (DEC_BATCH, DEC_SEQ, D_MODEL)),
        "cache_kv_latent": nrm(ks[2], (DEPTH, DEC_BATCH, PAST_LEN, KV_LORA)),
        "cache_k_rope": nrm(ks[3], (DEPTH, DEC_BATCH, PAST_LEN, QK_ROPE)),
        "state_conv": nrm(ks[4], (DEPTH, DEC_BATCH, CONV_K - 1, CONV_WIDTH)),
        "pre_norm": gain(ks[5], D_MODEL),
        "w_in": nrm(ks[6], (DEPTH, D_MODEL, IN_COLS), D_MODEL ** -0.5),
        "q_norm": gain(ks[7], Q_LORA),
        "w_uq": nrm(ks[8], (DEPTH, Q_LORA, N_HEADS * (QK_NOPE + QK_ROPE)), Q_LORA ** -0.5),
        "kv_norm": gain(ks[9], KV_LORA),
        "w_uk": nrm(ks[10], (DEPTH, KV_LORA, N_HEADS, QK_NOPE), KV_LORA ** -0.5),
        "w_uv": nrm(ks[11], (DEPTH, KV_LORA, N_HEADS, V_HEAD), KV_LORA ** -0.5),
        "w_o_mla": nrm(ks[12], (DEPTH, MLA_WIDTH, D_MODEL), MLA_WIDTH ** -0.5),
        "conv_w": nrm(ks[13], (DEPTH, CONV_K, CONV_WIDTH), CONV_K ** -0.5),
        "w_o_conv": nrm(ks[14], (DEPTH, CONV_WIDTH, D_MODEL), CONV_WIDTH ** -0.5),
        "w_out": nrm(ks[15], (DEPTH, D_MODEL, D_MODEL), D_MODEL ** -0.5),
        "post_norm": gain(ks[16], D_MODEL),
    }


def reference(x_prompt, x_sample, cache_kv_latent, cache_k_rope, state_conv,
              pre_norm, w_in, q_norm, w_uq, kv_norm, w_uk, w_uv, w_o_mla, conv_w, w_o_conv, w_out, post_norm):
    yp, ys = x_prompt, x_sample
    pos_p = jnp.arange(x_prompt.shape[1], dtype=jnp.int32)
    pos_s = cache_kv_latent.shape[2] + jnp.arange(x_sample.shape[1], dtype=jnp.int32)
    ckv_p, kpe_p, cv_p, ckv_s, kpe_s, cv_s = [], [], [], [], [], []
    for l in range(DEPTH):
        w = (pre_norm[l], w_in[l], q_norm[l], w_uq[l], kv_norm[l], w_uk[l], w_uv[l],
             w_o_mla[l], conv_w[l], w_o_conv[l], w_out[l], post_norm[l])
        pad = jnp.zeros((yp.shape[0], CONV_K - 1, CONV_WIDTH), yp.dtype)
        yp, a, r, c = _layer(yp, pos_p, None, None, pad, True, *w)
        ckv_p.append(a); kpe_p.append(r); cv_p.append(c)
        ys, a, r, c = _layer(ys, pos_s, cache_kv_latent[l], cache_k_rope[l], state_conv[l], False, *w)
        ckv_s.append(a); kpe_s.append(r); cv_s.append(c)
    return (yp, ys, jnp.stack(ckv_p), jnp.stack(kpe_p), jnp.stack(cv_p),
            jnp.stack(ckv_s), jnp.stack(kpe_s), jnp.stack(cv_s))
```

```python
import functools

import jax
import jax.numpy as jnp
from jax import lax
from jax.experimental import pallas as pl
from jax.experimental.pallas import tpu as pltpu

N_HEADS = 8
QK_NOPE = 64
QK_ROPE = 32
V_HEAD = 64
Q_LORA = 256
KV_LORA = 128
MLA_WIDTH = N_HEADS * V_HEAD
CONV_WIDTH = 512
CONV_K = 3
D_MODEL = 1024
CHUNK = 64
ROPE_BASE = 10000.0
EPS = 1e-6
SM_SCALE = (QK_NOPE + QK_ROPE) ** -0.5
NEG_INF = -1e30
LOG2E = 1.4426950408889634

LANES = 128
KEY_WIDTH = 2 * LANES
HEADS_PER_ROPE_BLOCK = LANES // QK_ROPE
Q_SCALE = SM_SCALE * LOG2E
VMEM_LIMIT_BYTES = 56 * 1024 * 1024

_O_QL = 0
_O_KV = _O_QL + Q_LORA
_O_GM = _O_KV + KV_LORA + LANES
_O_CB = _O_GM + MLA_WIDTH
_O_CC = _O_CB + CONV_WIDTH
_O_CX = _O_CC + CONV_WIDTH
_O_GC = _O_CX + CONV_WIDTH
_O_MM = _O_GC + CONV_WIDTH
_O_MC = _O_MM + D_MODEL
_IN_COLS = _O_MC + D_MODEL

_F32 = jnp.float32
_BF16 = jnp.bfloat16


def _rms(v, g):
    return v * lax.rsqrt(jnp.mean(v * v, axis=-1, keepdims=True) + EPS) * g


def _silu(v):
    return v * jax.nn.sigmoid(v)


def _dot(a, b):
    return jnp.dot(a, b, preferred_element_type=_F32)


def _proj_kernel(x_ref, cos_ref, sn1_ref, sn2_ref, st_ref, pre_g_ref, w_in_ref, q_g_ref,
                 w_uq_ref, w_uk_ref, kv_g_ref, conv_w_ref, w_oc_ref,
                 q_ref, kc_ref, ckv_ref, kpe_ref, ga_ref, sa_ref, mb_ref, cst_ref,
                 ub_ref, carry_ref, *, n_seq, carry_state):
    tm = x_ref.shape[1]
    seq_len = tm // n_seq
    x = x_ref[0]
    h = _rms(x, pre_g_ref[...]).astype(_BF16)

    def proj(off, n):
        return _dot(h, w_in_ref[:, off:off + n])

    cos = cos_ref[...]
    sn1 = sn1_ref[...]
    sn2 = sn2_ref[...]

    def rope(v):
        return (v * cos + pltpu.roll(v, QK_ROPE // 2, 1) * sn1
                + pltpu.roll(v, LANES - QK_ROPE // 2, 1) * sn2)

    qn = _rms(proj(_O_QL, Q_LORA), q_g_ref[...]).astype(_BF16)
    q = _dot(qn, w_uq_ref[...])
    q_abs = _dot(q[:, :N_HEADS * QK_NOPE].astype(_BF16), w_uk_ref[...])
    lane_group = lax.broadcasted_iota(jnp.int32, (1, LANES), 1) // QK_ROPE
    n_rope_blocks = N_HEADS // HEADS_PER_ROPE_BLOCK
    q_rot = [rope(q[:, N_HEADS * QK_NOPE + j * LANES:N_HEADS * QK_NOPE + (j + 1) * LANES]) * Q_SCALE
             for j in range(n_rope_blocks)]
    for hd in range(N_HEADS):
        q_ref[0, hd, :, 0:LANES] = (q_abs[:, hd * LANES:(hd + 1) * LANES] * Q_SCALE).astype(_BF16)
        q_ref[0, hd, :, LANES:KEY_WIDTH] = jnp.where(
            lane_group == hd % HEADS_PER_ROPE_BLOCK, q_rot[hd // HEADS_PER_ROPE_BLOCK], 0.0).astype(_BF16)

    kvr = proj(_O_KV, KEY_WIDTH)
    ckv = _rms(kvr[:, :KV_LORA], kv_g_ref[...])
    k_rot = rope(kvr[:, KV_LORA:])
    ckv_ref[0] = ckv
    kpe_ref[0] = k_rot[:, :QK_ROPE]
    kc_ref[0, :, 0:KV_LORA] = ckv.astype(_BF16)
    kc_ref[0, :, KV_LORA:KEY_WIDTH] = k_rot.astype(_BF16)

    ga_ref[0] = _silu(proj(_O_GM, MLA_WIDTH)).astype(_BF16)
    sa_ref[0] = jax.nn.sigmoid(proj(_O_MM, D_MODEL)).astype(_BF16)

    u = proj(_O_CC, CONV_WIDTH) * proj(_O_CX, CONV_WIDTH)
    w0 = conv_w_ref[0:1, :]
    w1 = conv_w_ref[1:2, :]
    w2 = conv_w_ref[2:3, :]
    convs = []
    if carry_state:
        @pl.when(pl.program_id(1) == 0)
        def _():
            carry_ref[0:CONV_K - 1, :] = st_ref[0]
    for j in range(n_seq):
        u_j = u[j * seq_len:(j + 1) * seq_len]
        prev = carry_ref[0:CONV_K - 1, :] if carry_state else st_ref[j]
        ub_ref[j, 8 - (CONV_K - 1):8, :] = prev
        ub_ref[j, 8:8 + seq_len, :] = u_j
        convs.append(w0 * ub_ref[j, 6:6 + seq_len, :] + w1 * ub_ref[j, 7:7 + seq_len, :] + w2 * u_j)
        new_state = u_j[seq_len - (CONV_K - 1):, :]
        cst_ref[j] = new_state
        if carry_state:
            carry_ref[0:CONV_K - 1, :] = new_state
    conv = convs[0] if n_seq == 1 else jnp.concatenate(convs, axis=0)
    bb = (proj(_O_CB, CONV_WIDTH) * conv * _silu(proj(_O_GC, CONV_WIDTH))).astype(_BF16)
    branch_b = _dot(bb, w_oc_ref[...])
    mb_ref[0] = (jax.nn.sigmoid(proj(_O_MC, D_MODEL)) * branch_b).astype(_BF16)


def _project(x, cos, sn1, sn2, state, w, *, tm, n_seq, carry_state):
    b, t, _ = x.shape
    nt = t // tm
    seq_len = tm // n_seq
    n_state = state.shape[0] // b if carry_state else state.shape[0]

    def rows(width, dtype):
        return (jax.ShapeDtypeStruct((b, t, width), dtype),
                pl.BlockSpec((1, tm, width), lambda bi, ti: (bi, ti, 0)))

    def full(a):
        return pl.BlockSpec(a.shape, lambda bi, ti: (0,) * a.ndim)

    outs = [
        (jax.ShapeDtypeStruct((b, N_HEADS, t, KEY_WIDTH), _BF16),
         pl.BlockSpec((1, N_HEADS, tm, KEY_WIDTH), lambda bi, ti: (bi, 0, ti, 0))),
        rows(KEY_WIDTH, _BF16),
        rows(KV_LORA, _F32),
        rows(QK_ROPE, _F32),
        rows(MLA_WIDTH, _BF16),
        rows(D_MODEL, _BF16),
        rows(D_MODEL, _BF16),
        (jax.ShapeDtypeStruct((b * n_state, CONV_K - 1, CONV_WIDTH), _F32),
         pl.BlockSpec((n_state, CONV_K - 1, CONV_WIDTH), lambda bi, ti: (bi, 0, 0))),
    ]
    table_spec = pl.BlockSpec((tm, LANES), lambda bi, ti: (ti, 0))
    weights = (w["pre_g"], w["w_in"], w["q_g"], w["w_uq"], w["w_uk"], w["kv_g"], w["conv_w"], w["w_oc"])
    return pl.pallas_call(
        functools.partial(_proj_kernel, n_seq=n_seq, carry_state=carry_state),
        grid=(b, nt),
        in_specs=[pl.BlockSpec((1, tm, D_MODEL), lambda bi, ti: (bi, ti, 0)),
                  table_spec, table_spec, table_spec,
                  pl.BlockSpec((n_state, CONV_K - 1, CONV_WIDTH), lambda bi, ti: (bi, 0, 0))]
                 + [full(a) for a in weights],
        out_specs=[o[1] for o in outs],
        out_shape=[o[0] for o in outs],
        scratch_shapes=[pltpu.VMEM((n_seq, seq_len + 8, CONV_WIDTH), _F32),
                        pltpu.VMEM((8, CONV_WIDTH), _F32)],
        compiler_params=pltpu.CompilerParams(
            dimension_semantics=("arbitrary", "arbitrary"), vmem_limit_bytes=VMEM_LIMIT_BYTES),
        name="proj_prompt" if carry_state else "proj_sample",
    )(x, cos, sn1, sn2, state, *weights)


def _scores(q, k):
    return lax.dot_general(q, k, (((1,), (1,)), ((), ())), preferred_element_type=_F32)


def _attn_first(q, k, m_ref, l_ref, acc_ref, visible=None):
    s = _scores(q, k)
    if visible is not None:
        s = jnp.where(visible, s, NEG_INF)
    m = jnp.max(s, axis=1, keepdims=True)
    p = jnp.exp2(s - m)
    m_ref[...] = jnp.broadcast_to(m, m_ref.shape)
    l_ref[...] = jnp.broadcast_to(jnp.sum(p, axis=1, keepdims=True), l_ref.shape)
    acc_ref[...] = _dot(p.astype(_BF16), k[:, :KV_LORA])


def _attn_update(q, k, m_ref, l_ref, acc_ref):
    s = _scores(q, k)
    m_prev = m_ref[...]
    m_new = jnp.maximum(m_prev, jnp.max(s, axis=1, keepdims=True))
    alpha = jnp.exp2(m_prev - m_new)
    p = jnp.exp2(s - jnp.concatenate([m_new] * (s.shape[1] // LANES), axis=1))
    l_ref[...] = alpha * l_ref[...] + jnp.sum(p, axis=1, keepdims=True)
    acc_ref[...] = alpha * acc_ref[...] + _dot(p.astype(_BF16), k[:, :KV_LORA])
    m_ref[...] = m_new


def _merge_heads(l_ref, acc_ref, rows):
    o = acc_ref[...] / l_ref[...]
    return jnp.concatenate([o[hd * rows:(hd + 1) * rows] for hd in range(N_HEADS)], axis=1)


def _epilogue(o_lat, ga, sa, mb, x, w_uv_ref, w_om_ref, w_out_ref, post_g_ref):
    o = _dot(o_lat.astype(_BF16), w_uv_ref[...])
    branch_a = _dot((o * ga.astype(_F32)).astype(_BF16), w_om_ref[...])
    merged = sa.astype(_F32) * branch_a + mb.astype(_F32)
    z = _dot(merged.astype(_BF16), w_out_ref[...])
    return x + _rms(z, post_g_ref[...])


def _attn_prompt_kernel(q_ref, kc_ref, ga_ref, sa_ref, mb_ref, x_ref,
                        w_uv_ref, w_om_ref, w_out_ref, post_g_ref, y_ref,
                        m_ref, l_ref, acc_ref, *, tq, tk):
    i = pl.program_id(1)
    rows = N_HEADS * tq
    q = q_ref[0].reshape(rows, KEY_WIDTH)

    d0 = pl.multiple_of(i * tq, tq)
    q_chunk = (lax.broadcasted_iota(jnp.int32, (rows, tq), 0) % tq) // CHUNK
    k_chunk = lax.broadcasted_iota(jnp.int32, (rows, tq), 1) // CHUNK
    _attn_first(q, kc_ref[0, pl.ds(d0, tq), :], m_ref, l_ref, acc_ref, visible=k_chunk <= q_chunk)

    per = tk // tq
    n_main = i // per

    def body(j, carry):
        k0 = pl.multiple_of(j * tk, tk)
        _attn_update(q, kc_ref[0, pl.ds(k0, tk), :], m_ref, l_ref, acc_ref)
        return carry

    lax.fori_loop(0, n_main, body, 0)
    for r in range(per - 1):
        @pl.when(i % per > r)
        def _():
            k0 = pl.multiple_of((n_main * per + r) * tq, tq)
            _attn_update(q, kc_ref[0, pl.ds(k0, tq), :], m_ref, l_ref, acc_ref)

    o_lat = _merge_heads(l_ref, acc_ref, tq)
    y_ref[0] = _epilogue(o_lat, ga_ref[0], sa_ref[0], mb_ref[0], x_ref[0],
                         w_uv_ref, w_om_ref, w_out_ref, post_g_ref)


def _attend_prompt(q, kc, ga, sa, mb, x, w, *, tq, tk):
    b, t, _ = x.shape
    rows = N_HEADS * tq

    def row_spec(width):
        return pl.BlockSpec((1, tq, width), lambda bi, qi: (bi, qi, 0))

    def full(a):
        return pl.BlockSpec(a.shape, lambda bi, qi: (0,) * a.ndim)

    weights = (w["w_uv"], w["w_om"], w["w_out"], w["post_g"])
    return pl.pallas_call(
        functools.partial(_attn_prompt_kernel, tq=tq, tk=tk),
        grid=(b, t // tq),
        in_specs=[pl.BlockSpec((1, N_HEADS, tq, KEY_WIDTH), lambda bi, qi: (bi, 0, qi, 0)),
                  pl.BlockSpec((1, t, KEY_WIDTH), lambda bi, qi: (bi, 0, 0)),
                  row_spec(MLA_WIDTH), row_spec(D_MODEL), row_spec(D_MODEL), row_spec(D_MODEL)]
                 + [full(a) for a in weights],
        out_specs=row_spec(D_MODEL),
        out_shape=jax.ShapeDtypeStruct((b, t, D_MODEL), _F32),
        scratch_shapes=[pltpu.VMEM((rows, LANES), _F32), pltpu.VMEM((rows, LANES), _F32),
                        pltpu.VMEM((rows, KV_LORA), _F32)],
        compiler_params=pltpu.CompilerParams(
            dimension_semantics=("arbitrary", "arbitrary"), vmem_limit_bytes=VMEM_LIMIT_BYTES),
        name="attn_prompt",
    )(q, kc, ga, sa, mb, x, *weights)


def _attn_sample_kernel(q_ref, kpast_ref, knew_ref, ga_ref, sa_ref, mb_ref, x_ref,
                        w_uv_ref, w_om_ref, w_out_ref, post_g_ref, y_ref,
                        m_ref, l_ref, acc_ref, o_ref, *, new_visible):
    bi = pl.program_id(0)
    t_new = knew_ref.shape[1]
    rows = N_HEADS * t_new
    q = q_ref[0].reshape(rows, KEY_WIDTH)
    if new_visible is None:
        visible = None
    else:
        q_pos = lax.broadcasted_iota(jnp.int32, (rows, t_new), 0) % t_new
        k_pos = lax.broadcasted_iota(jnp.int32, (rows, t_new), 1)
        visible = (k_pos + new_visible[0]) // CHUNK <= (q_pos + new_visible[0]) // CHUNK
    _attn_first(q, knew_ref[0], m_ref, l_ref, acc_ref, visible=visible)
    _attn_update(q, kpast_ref[0], m_ref, l_ref, acc_ref)
    r0 = pl.multiple_of(bi * t_new, t_new)
    o_ref[pl.ds(r0, t_new), :] = _merge_heads(l_ref, acc_ref, t_new)

    @pl.when(bi == pl.num_programs(0) - 1)
    def _():
        y_ref[...] = _epilogue(o_ref[...], ga_ref[...], sa_ref[...], mb_ref[...], x_ref[...],
                               w_uv_ref, w_om_ref, w_out_ref, post_g_ref)


def _attend_sample(q, k_past, k_new, ga, sa, mb, x, w, *, past_len):
    nb, t_new, _ = k_new.shape
    n_rows = nb * t_new
    rows = N_HEADS * t_new
    last_q, first_q = past_len + t_new - 1, past_len
    new_visible = None if last_q // CHUNK == first_q // CHUNK else (past_len,)

    def full(a):
        return pl.BlockSpec(a.shape, lambda bi: (0,) * a.ndim)

    weights = (w["w_uv"], w["w_om"], w["w_out"], w["post_g"])
    return pl.pallas_call(
        functools.partial(_attn_sample_kernel, new_visible=new_visible),
        grid=(nb,),
        in_specs=[pl.BlockSpec((1, N_HEADS, t_new, KEY_WIDTH), lambda bi: (0, 0, bi, 0)),
                  pl.BlockSpec((1, past_len, KEY_WIDTH), lambda bi: (bi, 0, 0)),
                  pl.BlockSpec((1, t_new, KEY_WIDTH), lambda bi: (bi, 0, 0)),
                  full(ga), full(sa), full(mb), full(x)] + [full(a) for a in weights],
        out_specs=pl.BlockSpec((n_rows, D_MODEL), lambda bi: (0, 0)),
        out_shape=jax.ShapeDtypeStruct((n_rows, D_MODEL), _F32),
        scratch_shapes=[pltpu.VMEM((rows, LANES), _F32), pltpu.VMEM((rows, LANES), _F32),
                        pltpu.VMEM((rows, KV_LORA), _F32), pltpu.VMEM((n_rows, N_HEADS * KV_LORA), _F32)],
        compiler_params=pltpu.CompilerParams(
            dimension_semantics=("arbitrary",), vmem_limit_bytes=VMEM_LIMIT_BYTES),
        name="attn_sample",
    )(q, k_past, k_new, ga, sa, mb, x, *weights)


def _block_diag(blocks):
    n = len(blocks)
    r, c = blocks[0].shape
    out = jnp.zeros((n * r, n * c), blocks[0].dtype)
    for j, blk in enumerate(blocks):
        out = out.at[j * r:(j + 1) * r, j * c:(j + 1) * c].set(blk)
    return out


def _prep_weights(pre_norm, w_in, q_norm, w_uq, kv_norm, w_uk, w_uv, w_o_mla, conv_w, w_o_conv, w_out, post_norm):
    o_kr = Q_LORA + KV_LORA
    o_rest = o_kr + QK_ROPE
    k_rope_cols = w_in[:, o_kr:o_rest]
    w_in_r = jnp.concatenate(
        [w_in[:, :o_kr]] + [k_rope_cols] * HEADS_PER_ROPE_BLOCK + [w_in[:, o_rest:]], axis=1)
    assert w_in_r.shape[1] == _IN_COLS
    wq = w_uq.reshape(Q_LORA, N_HEADS, QK_NOPE + QK_ROPE)
    w_uq_r = jnp.concatenate([wq[:, :, :QK_NOPE].reshape(Q_LORA, N_HEADS * QK_NOPE),
                              wq[:, :, QK_NOPE:].reshape(Q_LORA, N_HEADS * QK_ROPE)], axis=1)
    return {
        "pre_g": pre_norm.reshape(1, D_MODEL),
        "w_in": w_in_r.astype(_BF16),
        "q_g": q_norm.reshape(1, Q_LORA),
        "w_uq": w_uq_r.astype(_BF16),
        "w_uk": _block_diag([w_uk[:, hd, :].T for hd in range(N_HEADS)]).astype(_BF16),
        "kv_g": kv_norm.reshape(1, KV_LORA),
        "conv_w": conv_w,
        "w_oc": w_o_conv.astype(_BF16),
        "w_uv": _block_diag([w_uv[:, hd, :] for hd in range(N_HEADS)]).astype(_BF16),
        "w_om": w_o_mla.astype(_BF16),
        "w_out": w_out.astype(_BF16),
        "post_g": post_norm.reshape(1, D_MODEL),
    }


def _rope_tables(pos):
    half = QK_ROPE // 2
    inv = ROPE_BASE ** (-jnp.arange(half, dtype=_F32) / half)
    ang = pos.astype(_F32)[:, None] * inv[None, :]
    cos, sin = jnp.cos(ang), jnp.sin(ang)
    zero = jnp.zeros_like(sin)
    reps = LANES // QK_ROPE
    return (jnp.tile(jnp.concatenate([cos, cos], axis=1), (1, reps)),
            jnp.tile(jnp.concatenate([zero, sin], axis=1), (1, reps)),
            jnp.tile(jnp.concatenate([-sin, zero], axis=1), (1, reps)))


def _cached_keys(ckv, kpe):
    return jnp.concatenate([ckv] + [kpe] * HEADS_PER_ROPE_BLOCK, axis=-1).astype(_BF16)


PROMPT_ROW_TILE = 512
PROMPT_Q_TILE = 256
PROMPT_K_TILE = 512


def kernel(x_prompt, x_sample, cache_kv_latent, cache_k_rope, state_conv, pre_norm, w_in, q_norm, w_uq, kv_norm,
           w_uk, w_uv, w_o_mla, conv_w, w_o_conv, w_out, post_norm):
    depth = pre_norm.shape[0]
    assert depth == 1
    b, t, _ = x_prompt.shape
    nb, t_new, _ = x_sample.shape
    past_len = cache_kv_latent.shape[2]
    lyr = 0
    w = _prep_weights(pre_norm[lyr], w_in[lyr], q_norm[lyr], w_uq[lyr], kv_norm[lyr], w_uk[lyr], w_uv[lyr],
                      w_o_mla[lyr], conv_w[lyr], w_o_conv[lyr], w_out[lyr], post_norm[lyr])

    tabs = _rope_tables(jnp.arange(t, dtype=jnp.int32))
    zero_state = jnp.zeros((b, CONV_K - 1, CONV_WIDTH), _F32)
    q, kc, ckv_p, kpe_p, ga, sa, mb, cv_p = _project(
        x_prompt, *tabs, zero_state, w, tm=PROMPT_ROW_TILE, n_seq=1, carry_state=True)
    y_p = _attend_prompt(q, kc, ga, sa, mb, x_prompt, w, tq=PROMPT_Q_TILE, tk=PROMPT_K_TILE)

    n_rows = nb * t_new
    pos_s = past_len + jnp.arange(t_new, dtype=jnp.int32)
    tabs_s = tuple(jnp.tile(tb, (nb, 1)) for tb in _rope_tables(pos_s))
    xs = x_sample.reshape(1, n_rows, D_MODEL)
    q_s, kc_s, ckv_s, kpe_s, ga_s, sa_s, mb_s, cv_s = _project(
        xs, *tabs_s, state_conv[lyr], w, tm=n_rows, n_seq=nb, carry_state=False)
    k_past = _cached_keys(cache_kv_latent[lyr], cache_k_rope[lyr])
    y_s = _attend_sample(q_s, k_past, kc_s.reshape(nb, t_new, KEY_WIDTH),
                         ga_s[0], sa_s[0], mb_s[0], xs[0], w, past_len=past_len)

    return (y_p, y_s.reshape(nb, t_new, D_MODEL),
            ckv_p[None], kpe_p[None], cv_p[None],
            ckv_s.reshape(1, nb, t_new, KV_LORA), kpe_s.reshape(1, nb, t_new, QK_ROPE), cv_s[None])
```

```python
import functools

import jax
import jax.numpy as jnp
from jax import lax
from jax.experimental import pallas as pl
from jax.experimental.pallas import tpu as pltpu

N_HEADS = 8
QK_NOPE = 64
QK_ROPE = 32
V_HEAD = 64
Q_LORA = 256
KV_LORA = 128
MLA_WIDTH = N_HEADS * V_HEAD
CONV_WIDTH = 512
CONV_K = 3
D_MODEL = 1024
CHUNK = 64
ROPE_BASE = 10000.0
EPS = 1e-6
SM_SCALE = (QK_NOPE + QK_ROPE) ** -0.5
NEG_INF = -1e30
LOG2E = 1.4426950408889634

LANES = 128
MXU_DIM = 256
REDUCE_SLAB = 64
HALF_ROPE = QK_ROPE // 2
KEY_DIM = KV_LORA + QK_ROPE
KEY_WIDTH_S = 2 * LANES
HEADS_PER_ROPE_BLOCK = LANES // QK_ROPE
Q_SCALE = SM_SCALE * LOG2E
VMEM_LIMIT_BYTES = 56 * 1024 * 1024

_O_GM = 0
_O_CB = _O_GM + MLA_WIDTH
_O_CC = _O_CB + CONV_WIDTH
_O_CX = _O_CC + CONV_WIDTH
_O_GC = _O_CX + CONV_WIDTH
_O_MM = _O_GC + CONV_WIDTH
_O_MC = _O_MM + D_MODEL
_TAIL_COLS = _O_MC + D_MODEL

_F32 = jnp.float32
_BF16 = jnp.bfloat16
_NT = (((1,), (1,)), ((), ()))


def _rms(v, g):
    return v * lax.rsqrt(jnp.mean(v * v, axis=-1, keepdims=True) + EPS) * g


def _silu(v):
    return v * jax.nn.sigmoid(v)


def _dot(a, b):
    return jnp.dot(a, b, preferred_element_type=_F32)


def _dot_nt(a, b):
    return lax.dot_general(a, b, _NT, preferred_element_type=_F32)


def _gates_and_conv(h, w_b_ref, st_ref, conv_w_ref, w_oc_ref, ga_ref, sa_ref, mb_ref, cst_ref,
                    ub_ref, carry_ref, *, n_seq, carry_state):
    seq_len = h.shape[0] // n_seq

    def proj(off, n):
        return _dot(h, w_b_ref[:, off:off + n])

    ga_ref[0] = _silu(proj(_O_GM, MLA_WIDTH)).astype(_BF16)
    sa_ref[0] = jax.nn.sigmoid(proj(_O_MM, D_MODEL)).astype(_BF16)

    u = proj(_O_CC, CONV_WIDTH) * proj(_O_CX, CONV_WIDTH)
    w0 = conv_w_ref[0:1, :]
    w1 = conv_w_ref[1:2, :]
    w2 = conv_w_ref[2:3, :]
    if carry_state:
        @pl.when(pl.program_id(1) == 0)
        def _():
            carry_ref[0:CONV_K - 1, :] = st_ref[0]
    convs = []
    for j in range(n_seq):
        u_j = u[j * seq_len:(j + 1) * seq_len]
        prev = carry_ref[0:CONV_K - 1, :] if carry_state else st_ref[j]
        ub_ref[j, 8 - (CONV_K - 1):8, :] = prev
        ub_ref[j, 8:8 + seq_len, :] = u_j
        convs.append(w0 * ub_ref[j, 6:6 + seq_len, :] + w1 * ub_ref[j, 7:7 + seq_len, :] + w2 * u_j)
        new_state = u_j[seq_len - (CONV_K - 1):, :]
        cst_ref[j] = new_state
        if carry_state:
            carry_ref[0:CONV_K - 1, :] = new_state
    conv = convs[0] if n_seq == 1 else jnp.concatenate(convs, axis=0)
    bb = (proj(_O_CB, CONV_WIDTH) * conv * _silu(proj(_O_GC, CONV_WIDTH))).astype(_BF16)
    branch_b = _dot(bb, w_oc_ref[...])
    mb_ref[0] = (jax.nn.sigmoid(proj(_O_MC, D_MODEL)) * branch_b).astype(_BF16)


def _proj_prompt_kernel(x_ref, cos_ref, sin_ref, st_ref, pre_g_ref, w_q_ref, w_kv_ref, w_b_ref, q_g_ref,
                        w_uqn_ref, w_uqp_t_ref, w_uk_t_ref, kv_g_ref, conv_w_ref, w_oc_ref,
                        qt_ref, kc_ref, vt_ref, ckv_ref, kpet_ref, ga_ref, sa_ref, mb_ref, cst_ref,
                        ub_ref, carry_ref, *, tq):
    tm = x_ref.shape[1]
    h = _rms(x_ref[0], pre_g_ref[...]).astype(_BF16)
    cos_t = cos_ref[...]
    sin_t = sin_ref[...]

    qn = _rms(_dot(h, w_q_ref[...]), q_g_ref[...]).astype(_BF16)
    q_nope = _dot(qn, w_uqn_ref[...]).astype(_BF16)
    q_abs_t = _dot_nt(w_uk_t_ref[...], q_nope)
    q_pe_t = _dot_nt(w_uqp_t_ref[...], qn).reshape(N_HEADS, QK_ROPE, tm)
    x1 = q_pe_t[:, :HALF_ROPE, :]
    x2 = q_pe_t[:, HALF_ROPE:, :]
    r1 = (x1 * cos_t - x2 * sin_t) * Q_SCALE
    r2 = (x2 * cos_t + x1 * sin_t) * Q_SCALE
    for j in range(tm // tq):
        tok = slice(j * tq, (j + 1) * tq)
        for hd in range(N_HEADS):
            col = slice(hd * tq, (hd + 1) * tq)
            qt_ref[0, j, 0:KV_LORA, col] = (q_abs_t[hd * KV_LORA:(hd + 1) * KV_LORA, tok] * Q_SCALE).astype(_BF16)
            qt_ref[0, j, KV_LORA:KV_LORA + HALF_ROPE, col] = r1[hd][:, tok].astype(_BF16)
            qt_ref[0, j, KV_LORA + HALF_ROPE:KEY_DIM, col] = r2[hd][:, tok].astype(_BF16)

    kvr = _dot(h, w_kv_ref[...])
    ckv = _rms(kvr[:, :KV_LORA], kv_g_ref[...])
    ckv_ref[0] = ckv
    ckv_t = ckv.T.astype(_BF16)
    for u in range(tm // MXU_DIM):
        vt_ref[0, u] = ckv_t[:, u * MXU_DIM:(u + 1) * MXU_DIM]
    kr_t = kvr[:, KV_LORA:].T
    k1 = kr_t[0:HALF_ROPE]
    k2 = kr_t[HALF_ROPE:QK_ROPE]
    kpe_t = jnp.concatenate([k1 * cos_t - k2 * sin_t, k2 * cos_t + k1 * sin_t], axis=0)
    kpet_ref[0] = kpe_t
    kpe = jnp.concatenate([kpe_t, jnp.zeros((LANES - QK_ROPE, tm), _F32)], axis=0).T
    kc_ref[0, :, 0:KV_LORA] = ckv.astype(_BF16)
    kc_ref[0, :, KV_LORA:KEY_DIM] = kpe[:, :QK_ROPE].astype(_BF16)

    _gates_and_conv(h, w_b_ref, st_ref, conv_w_ref, w_oc_ref, ga_ref, sa_ref, mb_ref, cst_ref,
                    ub_ref, carry_ref, n_seq=1, carry_state=True)


def _project_prompt(x, cos_t, sin_t, state, w, *, tm, tq):
    b, t, _ = x.shape

    def rows(width, dtype):
        return (jax.ShapeDtypeStruct((b, t, width), dtype),
                pl.BlockSpec((1, tm, width), lambda bi, ti: (bi, ti, 0)))

    def full(a):
        return pl.BlockSpec(a.shape, lambda bi, ti: (0,) * a.ndim)

    outs = [
        (jax.ShapeDtypeStruct((b, t // tq, KEY_DIM, N_HEADS * tq), _BF16),
         pl.BlockSpec((1, tm // tq, KEY_DIM, N_HEADS * tq), lambda bi, ti: (bi, ti, 0, 0))),
        rows(KEY_DIM, _BF16),
        (jax.ShapeDtypeStruct((b, t // MXU_DIM, KV_LORA, MXU_DIM), _BF16),
         pl.BlockSpec((1, tm // MXU_DIM, KV_LORA, MXU_DIM), lambda bi, ti: (bi, ti, 0, 0))),
        rows(KV_LORA, _F32),
        (jax.ShapeDtypeStruct((b, QK_ROPE, t), _F32),
         pl.BlockSpec((1, QK_ROPE, tm), lambda bi, ti: (bi, 0, ti))),
        rows(MLA_WIDTH, _BF16),
        rows(D_MODEL, _BF16),
        rows(D_MODEL, _BF16),
        (jax.ShapeDtypeStruct((b, CONV_K - 1, CONV_WIDTH), _F32),
         pl.BlockSpec((1, CONV_K - 1, CONV_WIDTH), lambda bi, ti: (bi, 0, 0))),
    ]
    table_spec = pl.BlockSpec((HALF_ROPE, tm), lambda bi, ti: (0, ti))
    weights = (w["pre_g"], w["w_q"], w["w_kv"], w["w_b"], w["q_g"], w["w_uqn"], w["w_uqp_t"], w["w_uk_t"],
               w["kv_g"], w["conv_w"], w["w_oc"])
    return pl.pallas_call(
        functools.partial(_proj_prompt_kernel, tq=tq),
        grid=(b, t // tm),
        in_specs=[pl.BlockSpec((1, tm, D_MODEL), lambda bi, ti: (bi, ti, 0)),
                  table_spec, table_spec,
                  pl.BlockSpec((1, CONV_K - 1, CONV_WIDTH), lambda bi, ti: (bi, 0, 0))]
                 + [full(a) for a in weights],
        out_specs=[o[1] for o in outs],
        out_shape=[o[0] for o in outs],
        scratch_shapes=[pltpu.VMEM((1, tm + 8, CONV_WIDTH), _F32),
                        pltpu.VMEM((8, CONV_WIDTH), _F32)],
        compiler_params=pltpu.CompilerParams(
            dimension_semantics=("arbitrary", "arbitrary"), vmem_limit_bytes=VMEM_LIMIT_BYTES),
        name="proj_prompt",
    )(x, cos_t, sin_t, state, *weights)


def _proj_sample_kernel(x_ref, cos_ref, sn1_ref, sn2_ref, st_ref, pre_g_ref, w_q_ref, w_kv_ref, w_b_ref, q_g_ref,
                        w_uq_ref, w_uk_ref, kv_g_ref, conv_w_ref, w_oc_ref,
                        q_ref, kc_ref, ckv_ref, kpe_ref, ga_ref, sa_ref, mb_ref, cst_ref,
                        ub_ref, carry_ref, *, n_seq):
    h = _rms(x_ref[0], pre_g_ref[...]).astype(_BF16)
    cos = cos_ref[...]
    sn1 = sn1_ref[...]
    sn2 = sn2_ref[...]

    def rope(v):
        return (v * cos + pltpu.roll(v, HALF_ROPE, 1) * sn1
                + pltpu.roll(v, LANES - HALF_ROPE, 1) * sn2)

    qn = _rms(_dot(h, w_q_ref[...]), q_g_ref[...]).astype(_BF16)
    q = _dot(qn, w_uq_ref[...])
    q_abs = _dot(q[:, :N_HEADS * QK_NOPE].astype(_BF16), w_uk_ref[...])
    lane_group = lax.broadcasted_iota(jnp.int32, (1, LANES), 1) // QK_ROPE
    n_rope_blocks = N_HEADS // HEADS_PER_ROPE_BLOCK
    q_rot = [rope(q[:, N_HEADS * QK_NOPE + j * LANES:N_HEADS * QK_NOPE + (j + 1) * LANES]) * Q_SCALE
             for j in range(n_rope_blocks)]
    for hd in range(N_HEADS):
        q_ref[0, hd, :, 0:LANES] = (q_abs[:, hd * LANES:(hd + 1) * LANES] * Q_SCALE).astype(_BF16)
        q_ref[0, hd, :, LANES:KEY_WIDTH_S] = jnp.where(
            lane_group == hd % HEADS_PER_ROPE_BLOCK, q_rot[hd // HEADS_PER_ROPE_BLOCK], 0.0).astype(_BF16)

    kvr = _dot(h, w_kv_ref[...])
    ckv = _rms(kvr[:, :KV_LORA], kv_g_ref[...])
    k_rot = rope(kvr[:, KV_LORA:])
    ckv_ref[0] = ckv
    kpe_ref[0] = k_rot[:, :QK_ROPE]
    kc_ref[0, :, 0:KV_LORA] = ckv.astype(_BF16)
    kc_ref[0, :, KV_LORA:KEY_WIDTH_S] = k_rot.astype(_BF16)

    _gates_and_conv(h, w_b_ref, st_ref, conv_w_ref, w_oc_ref, ga_ref, sa_ref, mb_ref, cst_ref,
                    ub_ref, carry_ref, n_seq=n_seq, carry_state=False)


def _project_sample(x, cos, sn1, sn2, state, w, *, n_seq):
    _, tm, _ = x.shape

    def rows(width, dtype):
        return (jax.ShapeDtypeStruct((1, tm, width), dtype), pl.BlockSpec((1, tm, width), lambda i: (0, 0, 0)))

    def full(a):
        return pl.BlockSpec(a.shape, lambda i: (0,) * a.ndim)

    outs = [
        (jax.ShapeDtypeStruct((1, N_HEADS, tm, KEY_WIDTH_S), _BF16),
         pl.BlockSpec((1, N_HEADS, tm, KEY_WIDTH_S), lambda i: (0, 0, 0, 0))),
        rows(KEY_WIDTH_S, _BF16), rows(KV_LORA, _F32), rows(QK_ROPE, _F32),
        rows(MLA_WIDTH, _BF16), rows(D_MODEL, _BF16), rows(D_MODEL, _BF16),
        (jax.ShapeDtypeStruct((n_seq, CONV_K - 1, CONV_WIDTH), _F32),
         pl.BlockSpec((n_seq, CONV_K - 1, CONV_WIDTH), lambda i: (0, 0, 0))),
    ]
    weights = (w["pre_g"], w["w_q"], w["w_kv_s"], w["w_b"], w["q_g"], w["w_uq"], w["w_uk"],
               w["kv_g"], w["conv_w"], w["w_oc"])
    return pl.pallas_call(
        functools.partial(_proj_sample_kernel, n_seq=n_seq),
        grid=(1,),
        in_specs=[full(x), full(cos), full(sn1), full(sn2), full(state)] + [full(a) for a in weights],
        out_specs=[o[1] for o in outs],
        out_shape=[o[0] for o in outs],
        scratch_shapes=[pltpu.VMEM((n_seq, tm // n_seq + 8, CONV_WIDTH), _F32),
                        pltpu.VMEM((8, CONV_WIDTH), _F32)],
        compiler_params=pltpu.CompilerParams(
            dimension_semantics=("arbitrary",), vmem_limit_bytes=VMEM_LIMIT_BYTES),
        name="proj_sample",
    )(x, cos, sn1, sn2, state, *weights)


def _epilogue(o_lat, ga, sa, mb, x, w_uv_ref, w_om_ref, w_out_ref, post_g_ref):
    o = _dot(o_lat.astype(_BF16), w_uv_ref[...])
    branch_a = _dot((o * ga.astype(_F32)).astype(_BF16), w_om_ref[...])
    merged = sa.astype(_F32) * branch_a + mb.astype(_F32)
    z = _dot(merged.astype(_BF16), w_out_ref[...])
    return x + _rms(z, post_g_ref[...])


def _attn_t_step(qt_ref, k, vts, m_ref, l_ref, acc_ref, *, tq, first, visible=None):
    n_keys = k.shape[0]
    slabs = max(n_keys // REDUCE_SLAB, 1)

    def col_reduce(v, op):
        part = op(v.reshape(slabs, n_keys // slabs, tq), axis=0)
        return op(part, axis=0, keepdims=True)

    scores = [_dot(k, qt_ref[0, 0, :, hd * tq:(hd + 1) * tq]) for hd in range(N_HEADS)]
    for hd in range(N_HEADS):
        col = slice(hd * tq, (hd + 1) * tq)
        s = scores[hd]
        if visible is not None:
            s = jnp.where(visible, s, NEG_INF)
        m_cur = col_reduce(s, jnp.max)
        m_new = m_cur if first else jnp.maximum(m_ref[:, col], m_cur)
        p = jnp.exp2(s - m_new)
        l_cur = col_reduce(p, jnp.sum)
        pb = p.astype(_BF16)
        pv = _dot(vts[0], pb[0:MXU_DIM])
        for u in range(1, len(vts)):
            pv = pv + _dot(vts[u], pb[u * MXU_DIM:(u + 1) * MXU_DIM])
        if first:
            l_ref[:, col] = l_cur
            acc_ref[:, col] = pv
        else:
            alpha = jnp.exp2(m_ref[:, col] - m_new)
            l_ref[:, col] = alpha * l_ref[:, col] + l_cur
            acc_ref[:, col] = alpha * acc_ref[:, col] + pv
        m_ref[:, col] = m_new


def _attn_prompt_kernel(qt_ref, kc_ref, vt_ref, ga_ref, sa_ref, mb_ref, x_ref,
                        w_uv_ref, w_om_ref, w_out_ref, post_g_ref, y_ref,
                        m_ref, l_ref, acc_ref, *, tq, tk):
    i = pl.program_id(1)
    units = tq // MXU_DIM
    step = functools.partial(_attn_t_step, qt_ref, m_ref=m_ref, l_ref=l_ref, acc_ref=acc_ref, tq=tq)

    def load(tile, n_units):
        k0 = pl.multiple_of(tile * (n_units * MXU_DIM), n_units * MXU_DIM)
        k = kc_ref[0, pl.ds(k0, n_units * MXU_DIM), :]
        return k, [vt_ref[0, tile * n_units + u] for u in range(n_units)]

    k_chunk = lax.broadcasted_iota(jnp.int32, (tq, tq), 0) // CHUNK
    q_chunk = lax.broadcasted_iota(jnp.int32, (tq, tq), 1) // CHUNK
    k, vts = load(i, units)
    step(k, vts, first=True, visible=k_chunk <= q_chunk)

    per = tk // tq
    n_main = i // per

    def body(j, carry):
        kj, vj = load(j, per * units)
        step(kj, vj, first=False)
        return carry

    lax.fori_loop(0, n_main, body, 0)
    for r in range(per - 1):
        @pl.when(i % per > r)
        def _():
            kr, vr = load(n_main * per + r, units)
            step(kr, vr, first=False)

    o_t = acc_ref[...] / l_ref[...]
    o_lat = jnp.concatenate([o_t[:, hd * tq:(hd + 1) * tq].T for hd in range(N_HEADS)], axis=1)
    y_ref[0] = _epilogue(o_lat, ga_ref[0], sa_ref[0], mb_ref[0], x_ref[0],
                         w_uv_ref, w_om_ref, w_out_ref, post_g_ref)


def _attend_prompt(qt, kc, vt, ga, sa, mb, x, w, *, tq, tk):
    b, t, _ = x.shape
    cols = N_HEADS * tq

    def row_spec(width):
        return pl.BlockSpec((1, tq, width), lambda bi, qi: (bi, qi, 0))

    def full(a):
        return pl.BlockSpec(a.shape, lambda bi, qi: (0,) * a.ndim)

    weights = (w["w_uv"], w["w_om"], w["w_out"], w["post_g"])
    return pl.pallas_call(
        functools.partial(_attn_prompt_kernel, tq=tq, tk=tk),
        grid=(b, t // tq),
        in_specs=[pl.BlockSpec((1, 1, KEY_DIM, cols), lambda bi, qi: (bi, qi, 0, 0)),
                  pl.BlockSpec((1, t, KEY_DIM), lambda bi, qi: (bi, 0, 0)),
                  pl.BlockSpec((1, t // MXU_DIM, KV_LORA, MXU_DIM), lambda bi, qi: (bi, 0, 0, 0)),
                  row_spec(MLA_WIDTH), row_spec(D_MODEL), row_spec(D_MODEL), row_spec(D_MODEL)]
                 + [full(a) for a in weights],
        out_specs=row_spec(D_MODEL),
        out_shape=jax.ShapeDtypeStruct((b, t, D_MODEL), _F32),
        scratch_shapes=[pltpu.VMEM((1, cols), _F32), pltpu.VMEM((1, cols), _F32),
                        pltpu.VMEM((KV_LORA, cols), _F32)],
        compiler_params=pltpu.CompilerParams(
            dimension_semantics=("arbitrary", "arbitrary"), vmem_limit_bytes=VMEM_LIMIT_BYTES),
        name="attn_prompt",
    )(qt, kc, vt, ga, sa, mb, x, *weights)


def _attn_first(q, k, m_ref, l_ref, acc_ref, visible=None):
    s = _dot_nt(q, k)
    if visible is not None:
        s = jnp.where(visible, s, NEG_INF)
    m = jnp.max(s, axis=1, keepdims=True)
    p = jnp.exp2(s - m)
    m_ref[...] = jnp.broadcast_to(m, m_ref.shape)
    l_ref[...] = jnp.broadcast_to(jnp.sum(p, axis=1, keepdims=True), l_ref.shape)
    acc_ref[...] = _dot(p.astype(_BF16), k[:, :KV_LORA])


def _attn_update(q, k, m_ref, l_ref, acc_ref):
    s = _dot_nt(q, k)
    m_prev = m_ref[...]
    m_new = jnp.maximum(m_prev, jnp.max(s, axis=1, keepdims=True))
    alpha = jnp.exp2(m_prev - m_new)
    p = jnp.exp2(s - jnp.concatenate([m_new] * (s.shape[1] // LANES), axis=1))
    l_ref[...] = alpha * l_ref[...] + jnp.sum(p, axis=1, keepdims=True)
    acc_ref[...] = alpha * acc_ref[...] + _dot(p.astype(_BF16), k[:, :KV_LORA])
    m_ref[...] = m_new


def _attn_sample_kernel(q_ref, kpast_ref, knew_ref, ga_ref, sa_ref, mb_ref, x_ref,
                        w_uv_ref, w_om_ref, w_out_ref, post_g_ref, y_ref,
                        m_ref, l_ref, acc_ref, o_ref, *, new_visible):
    bi = pl.program_id(0)
    t_new = knew_ref.shape[1]
    rows = N_HEADS * t_new
    q = q_ref[0].reshape(rows, KEY_WIDTH_S)
    if new_visible is None:
        visible = None
    else:
        q_pos = lax.broadcasted_iota(jnp.int32, (rows, t_new), 0) % t_new
        k_pos = lax.broadcasted_iota(jnp.int32, (rows, t_new), 1)
        visible = (k_pos + new_visible[0]) // CHUNK <= (q_pos + new_visible[0]) // CHUNK
    _attn_first(q, knew_ref[0], m_ref, l_ref, acc_ref, visible=visible)
    _attn_update(q, kpast_ref[0], m_ref, l_ref, acc_ref)
    o = acc_ref[...] / l_ref[...]
    r0 = pl.multiple_of(bi * t_new, t_new)
    o_ref[pl.ds(r0, t_new), :] = jnp.concatenate(
        [o[hd * t_new:(hd + 1) * t_new] for hd in range(N_HEADS)], axis=1)

    @pl.when(bi == pl.num_programs(0) - 1)
    def _():
        y_ref[...] = _epilogue(o_ref[...], ga_ref[...], sa_ref[...], mb_ref[...], x_ref[...],
                               w_uv_ref, w_om_ref, w_out_ref, post_g_ref)


def _attend_sample(q, k_past, k_new, ga, sa, mb, x, w, *, past_len):
    nb, t_new, _ = k_new.shape
    n_rows = nb * t_new
    rows = N_HEADS * t_new
    last_q, first_q = past_len + t_new - 1, past_len
    new_visible = None if last_q // CHUNK == first_q // CHUNK else (past_len,)

    def full(a):
        return pl.BlockSpec(a.shape, lambda bi: (0,) * a.ndim)

    weights = (w["w_uv"], w["w_om"], w["w_out"], w["post_g"])
    return pl.pallas_call(
        functools.partial(_attn_sample_kernel, new_visible=new_visible),
        grid=(nb,),
        in_specs=[pl.BlockSpec((1, N_HEADS, t_new, KEY_WIDTH_S), lambda bi: (0, 0, bi, 0)),
                  pl.BlockSpec((1, past_len, KEY_WIDTH_S), lambda bi: (bi, 0, 0)),
                  pl.BlockSpec((1, t_new, KEY_WIDTH_S), lambda bi: (bi, 0, 0)),
                  full(ga), full(sa), full(mb), full(x)] + [full(a) for a in weights],
        out_specs=pl.BlockSpec((n_rows, D_MODEL), lambda bi: (0, 0)),
        out_shape=jax.ShapeDtypeStruct((n_rows, D_MODEL), _F32),
        scratch_shapes=[pltpu.VMEM((rows, LANES), _F32), pltpu.VMEM((rows, LANES), _F32),
                        pltpu.VMEM((rows, KV_LORA), _F32), pltpu.VMEM((n_rows, N_HEADS * KV_LORA), _F32)],
        compiler_params=pltpu.CompilerParams(
            dimension_semantics=("arbitrary",), vmem_limit_bytes=VMEM_LIMIT_BYTES),
        name="attn_sample",
    )(q, k_past, k_new, ga, sa, mb, x, *weights)


def _block_diag(blocks):
    n = len(blocks)
    r, c = blocks[0].shape
    rows = []
    for j, blk in enumerate(blocks):
        rows.append(jnp.pad(blk, ((0, 0), (j * c, (n - 1 - j) * c))))
    return jnp.concatenate(rows, axis=0)


def _prep_weights(pre_norm, w_in, q_norm, w_uq, kv_norm, w_uk, w_uv, w_o_mla, conv_w, w_o_conv, w_out, post_norm):
    o_kv = Q_LORA
    o_kr = o_kv + KV_LORA
    o_tail = o_kr + QK_ROPE
    assert w_in.shape[1] == o_tail + _TAIL_COLS
    w_kv_lat = w_in[:, o_kv:o_kr].astype(_BF16)
    w_kr = w_in[:, o_kr:o_tail].astype(_BF16)
    wq = w_uq.reshape(Q_LORA, N_HEADS, QK_NOPE + QK_ROPE)
    w_uqn = wq[:, :, :QK_NOPE].reshape(Q_LORA, N_HEADS * QK_NOPE).astype(_BF16)
    w_uqp = wq[:, :, QK_NOPE:].reshape(Q_LORA, N_HEADS * QK_ROPE).astype(_BF16)
    w_uk_bd = _block_diag([w_uk[:, hd, :].T.astype(_BF16) for hd in range(N_HEADS)])
    return {
        "pre_g": pre_norm.reshape(1, D_MODEL),
        "w_q": w_in[:, :o_kv].astype(_BF16),
        "w_kv": jnp.concatenate([w_kv_lat, w_kr, jnp.zeros((D_MODEL, LANES - QK_ROPE), _BF16)], axis=1),
        "w_kv_s": jnp.concatenate([w_kv_lat] + [w_kr] * HEADS_PER_ROPE_BLOCK, axis=1),
        "w_b": w_in[:, o_tail:].astype(_BF16),
        "q_g": q_norm.reshape(1, Q_LORA),
        "w_uqn": w_uqn,
        "w_uqp_t": w_uqp.T,
        "w_uq": jnp.concatenate([w_uqn, w_uqp], axis=1),
        "w_uk": w_uk_bd,
        "w_uk_t": w_uk_bd.T,
        "kv_g": kv_norm.reshape(1, KV_LORA),
        "conv_w": conv_w,
        "w_oc": w_o_conv.astype(_BF16),
        "w_uv": _block_diag([w_uv[:, hd, :].astype(_BF16) for hd in range(N_HEADS)]),
        "w_om": w_o_mla.astype(_BF16),
        "w_out": w_out.astype(_BF16),
        "post_g": post_norm.reshape(1, D_MODEL),
    }


def _rope_angles(pos):
    inv = ROPE_BASE ** (-jnp.arange(HALF_ROPE, dtype=_F32) / HALF_ROPE)
    ang = pos.astype(_F32)[:, None] * inv[None, :]
    return jnp.cos(ang), jnp.sin(ang)


def _rope_tables_rows(pos):
    cos, sin = _rope_angles(pos)
    zero = jnp.zeros_like(sin)
    reps = LANES // QK_ROPE
    return (jnp.tile(jnp.concatenate([cos, cos], axis=1), (1, reps)),
            jnp.tile(jnp.concatenate([zero, sin], axis=1), (1, reps)),
            jnp.tile(jnp.concatenate([-sin, zero], axis=1), (1, reps)))


def _cached_keys(ckv, kpe):
    return jnp.concatenate([ckv] + [kpe] * HEADS_PER_ROPE_BLOCK, axis=-1).astype(_BF16)


PROMPT_ROW_TILE = 512
PROMPT_Q_TILE = 256
PROMPT_K_TILE = 512


def kernel(x_prompt, x_sample, cache_kv_latent, cache_k_rope, state_conv, pre_norm, w_in, q_norm, w_uq, kv_norm,
           w_uk, w_uv, w_o_mla, conv_w, w_o_conv, w_out, post_norm):
    depth = pre_norm.shape[0]
    assert depth == 1
    b, t, _ = x_prompt.shape
    nb, t_new, _ = x_sample.shape
    past_len = cache_kv_latent.shape[2]
    lyr = 0
    w = _prep_weights(pre_norm[lyr], w_in[lyr], q_norm[lyr], w_uq[lyr], kv_norm[lyr], w_uk[lyr], w_uv[lyr],
                      w_o_mla[lyr], conv_w[lyr], w_o_conv[lyr], w_out[lyr], post_norm[lyr])

    cos, sin = _rope_angles(jnp.arange(t, dtype=jnp.int32))
    zero_state = jnp.zeros((b, CONV_K - 1, CONV_WIDTH), _F32)
    qt, kc, vt, ckv_p, kpe_t, ga, sa, mb, cv_p = _project_prompt(
        x_prompt, cos.T, sin.T, zero_state, w, tm=PROMPT_ROW_TILE, tq=PROMPT_Q_TILE)
    y_p = _attend_prompt(qt, kc, vt, ga, sa, mb, x_prompt, w, tq=PROMPT_Q_TILE, tk=PROMPT_K_TILE)

    n_rows = nb * t_new
    pos_s = past_len + jnp.arange(t_new, dtype=jnp.int32)
    tabs_s = tuple(jnp.tile(tb, (nb, 1)) for tb in _rope_tables_rows(pos_s))
    xs = x_sample.reshape(1, n_rows, D_MODEL)
    q_s, kc_s, ckv_s, kpe_s, ga_s, sa_s, mb_s, cv_s = _project_sample(
        xs, *tabs_s, state_conv[lyr], w, n_seq=nb)
    k_past = _cached_keys(cache_kv_latent[lyr], cache_k_rope[lyr])
    y_s = _attend_sample(q_s, k_past, kc_s.reshape(nb, t_new, KEY_WIDTH_S),
                         ga_s[0], sa_s[0], mb_s[0], xs[0], w, past_len=past_len)

    return (y_p, y_s.reshape(nb, t_new, D_MODEL),
            ckv_p[None], jnp.swapaxes(kpe_t, 1, 2)[None], cv_p[None],
            ckv_s.reshape(1, nb, t_new, KV_LORA), kpe_s.reshape(1, nb, t_new, QK_ROPE), cv_s[None])
```

```python
import functools

import numpy as np
import jax
import jax.numpy as jnp
from jax import lax
from jax.experimental import pallas as pl
from jax.experimental.pallas import tpu as pltpu

N_HEADS = 8
QK_NOPE = 64
QK_ROPE = 32
V_HEAD = 64
Q_LORA = 256
KV_LORA = 128
MLA_WIDTH = N_HEADS * V_HEAD
CONV_WIDTH = 512
CONV_K = 3
D_MODEL = 1024
CHUNK = 64
ROPE_BASE = 10000.0
EPS = 1e-6
SM_SCALE = (QK_NOPE + QK_ROPE) ** -0.5
NEG_INF = -1e30
LOG2E = 1.4426950408889634

LANES = 128
MXU_DIM = 256
REDUCE_SLAB = 64
STAGE1_LEAD = 2
HALF_ROPE = QK_ROPE // 2
KEY_DIM = KV_LORA + QK_ROPE
KEY_WIDTH_S = 2 * LANES
HEADS_PER_ROPE_BLOCK = LANES // QK_ROPE
Q_SCALE = SM_SCALE * LOG2E
VMEM_LIMIT_BYTES = 56 * 1024 * 1024

_O_GM = 0
_O_CB = _O_GM + MLA_WIDTH
_O_CC = _O_CB + CONV_WIDTH
_O_CX = _O_CC + CONV_WIDTH
_O_GC = _O_CX + CONV_WIDTH
_O_MM = _O_GC + CONV_WIDTH
_O_MC = _O_MM + D_MODEL
_TAIL_COLS = _O_MC + D_MODEL

_F32 = jnp.float32
_BF16 = jnp.bfloat16
_NT = (((1,), (1,)), ((), ()))


def _rms(v, g):
    return v * lax.rsqrt(jnp.mean(v * v, axis=-1, keepdims=True) + EPS) * g


def _silu(v):
    return v * jax.nn.sigmoid(v)


def _dot(a, b):
    return jnp.dot(a, b, preferred_element_type=_F32)


def _dot_nt(a, b):
    return lax.dot_general(a, b, _NT, preferred_element_type=_F32)


def _gates_and_conv(h, w_b_ref, st_ref, conv_w_ref, w_oc_ref, ga_ref, sa_ref, mb_ref, cst_ref,
                    ub_ref, carry_ref, *, n_seq, carry_state):
    seq_len = h.shape[0] // n_seq

    def proj(off, n):
        return _dot(h, w_b_ref[:, off:off + n])

    ga_ref[0] = _silu(proj(_O_GM, MLA_WIDTH)).astype(_BF16)
    sa_ref[0] = jax.nn.sigmoid(proj(_O_MM, D_MODEL)).astype(_BF16)

    u = proj(_O_CC, CONV_WIDTH) * proj(_O_CX, CONV_WIDTH)
    w0 = conv_w_ref[0:1, :]
    w1 = conv_w_ref[1:2, :]
    w2 = conv_w_ref[2:3, :]
    if carry_state:
        @pl.when(pl.program_id(1) == 0)
        def _():
            carry_ref[0:CONV_K - 1, :] = st_ref[0]
    convs = []
    for j in range(n_seq):
        u_j = u[j * seq_len:(j + 1) * seq_len]
        prev = carry_ref[0:CONV_K - 1, :] if carry_state else st_ref[j]
        ub_ref[j, 8 - (CONV_K - 1):8, :] = prev
        ub_ref[j, 8:8 + seq_len, :] = u_j
        convs.append(w0 * ub_ref[j, 6:6 + seq_len, :] + w1 * ub_ref[j, 7:7 + seq_len, :] + w2 * u_j)
        new_state = u_j[seq_len - (CONV_K - 1):, :]
        cst_ref[j] = new_state
        if carry_state:
            carry_ref[0:CONV_K - 1, :] = new_state
    conv = convs[0] if n_seq == 1 else jnp.concatenate(convs, axis=0)
    bb = (proj(_O_CB, CONV_WIDTH) * conv * _silu(proj(_O_GC, CONV_WIDTH))).astype(_BF16)
    branch_b = _dot(bb, w_oc_ref[...])
    mb_ref[0] = (jax.nn.sigmoid(proj(_O_MC, D_MODEL)) * branch_b).astype(_BF16)


def _proj_prompt_kernel(x_ref, cos_ref, sin_ref, st_ref, pre_g_ref, w_q_ref, w_kv_ref, w_b_ref, q_g_ref,
                        w_uqn_ref, w_uqp_t_ref, w_uk_t_ref, kv_g_ref, conv_w_ref, w_oc_ref,
                        qt_ref, kc_ref, vt_ref, ckv_ref, kpet_ref, ga_ref, sa_ref, mb_ref, cst_ref,
                        ub_ref, carry_ref, *, tq):
    tm = x_ref.shape[1]
    h = _rms(x_ref[0], pre_g_ref[...]).astype(_BF16)
    cos_t = cos_ref[...]
    sin_t = sin_ref[...]

    qn = _rms(_dot(h, w_q_ref[...]), q_g_ref[...]).astype(_BF16)
    q_nope = _dot(qn, w_uqn_ref[...]).astype(_BF16)
    q_abs_t = _dot_nt(w_uk_t_ref[...], q_nope)
    q_pe_t = _dot_nt(w_uqp_t_ref[...], qn).reshape(N_HEADS, QK_ROPE, tm)
    x1 = q_pe_t[:, :HALF_ROPE, :]
    x2 = q_pe_t[:, HALF_ROPE:, :]
    r1 = (x1 * cos_t - x2 * sin_t) * Q_SCALE
    r2 = (x2 * cos_t + x1 * sin_t) * Q_SCALE
    for j in range(tm // tq):
        tok = slice(j * tq, (j + 1) * tq)
        for hd in range(N_HEADS):
            col = slice(hd * tq, (hd + 1) * tq)
            qt_ref[0, j, 0:KV_LORA, col] = (q_abs_t[hd * KV_LORA:(hd + 1) * KV_LORA, tok] * Q_SCALE).astype(_BF16)
            qt_ref[0, j, KV_LORA:KV_LORA + HALF_ROPE, col] = r1[hd][:, tok].astype(_BF16)
            qt_ref[0, j, KV_LORA + HALF_ROPE:KEY_DIM, col] = r2[hd][:, tok].astype(_BF16)

    kvr = _dot(h, w_kv_ref[...])
    ckv = _rms(kvr[:, :KV_LORA], kv_g_ref[...])
    ckv_ref[0] = ckv
    ckv_t = ckv.T.astype(_BF16)
    for u in range(tm // MXU_DIM):
        vt_ref[0, u] = ckv_t[:, u * MXU_DIM:(u + 1) * MXU_DIM]
    kr_t = kvr[:, KV_LORA:].T
    k1 = kr_t[0:HALF_ROPE]
    k2 = kr_t[HALF_ROPE:QK_ROPE]
    kpe_t = jnp.concatenate([k1 * cos_t - k2 * sin_t, k2 * cos_t + k1 * sin_t], axis=0)
    kpet_ref[0] = kpe_t
    kpe = jnp.concatenate([kpe_t, jnp.zeros((LANES - QK_ROPE, tm), _F32)], axis=0).T
    kc_ref[0, :, 0:KV_LORA] = ckv.astype(_BF16)
    kc_ref[0, :, KV_LORA:KEY_DIM] = kpe[:, :QK_ROPE].astype(_BF16)

    _gates_and_conv(h, w_b_ref, st_ref, conv_w_ref, w_oc_ref, ga_ref, sa_ref, mb_ref, cst_ref,
                    ub_ref, carry_ref, n_seq=1, carry_state=True)


def _project_prompt(x, cos_t, sin_t, state, w, *, tm, tq):
    b, t, _ = x.shape

    def rows(width, dtype):
        return (jax.ShapeDtypeStruct((b, t, width), dtype),
                pl.BlockSpec((1, tm, width), lambda bi, ti: (bi, ti, 0)))

    def full(a):
        return pl.BlockSpec(a.shape, lambda bi, ti: (0,) * a.ndim)

    outs = [
        (jax.ShapeDtypeStruct((b, t // tq, KEY_DIM, N_HEADS * tq), _BF16),
         pl.BlockSpec((1, tm // tq, KEY_DIM, N_HEADS * tq), lambda bi, ti: (bi, ti, 0, 0))),
        rows(KEY_DIM, _BF16),
        (jax.ShapeDtypeStruct((b, t // MXU_DIM, KV_LORA, MXU_DIM), _BF16),
         pl.BlockSpec((1, tm // MXU_DIM, KV_LORA, MXU_DIM), lambda bi, ti: (bi, ti, 0, 0))),
        rows(KV_LORA, _F32),
        (jax.ShapeDtypeStruct((b, QK_ROPE, t), _F32),
         pl.BlockSpec((1, QK_ROPE, tm), lambda bi, ti: (bi, 0, ti))),
        rows(MLA_WIDTH, _BF16),
        rows(D_MODEL, _BF16),
        rows(D_MODEL, _BF16),
        (jax.ShapeDtypeStruct((b, CONV_K - 1, CONV_WIDTH), _F32),
         pl.BlockSpec((1, CONV_K - 1, CONV_WIDTH), lambda bi, ti: (bi, 0, 0))),
    ]
    table_spec = pl.BlockSpec((HALF_ROPE, tm), lambda bi, ti: (0, ti))
    weights = (w["pre_g"], w["w_q"], w["w_kv"], w["w_b"], w["q_g"], w["w_uqn"], w["w_uqp_t"], w["w_uk_t"],
               w["kv_g"], w["conv_w"], w["w_oc"])
    return pl.pallas_call(
        functools.partial(_proj_prompt_kernel, tq=tq),
        grid=(b, t // tm),
        in_specs=[pl.BlockSpec((1, tm, D_MODEL), lambda bi, ti: (bi, ti, 0)),
                  table_spec, table_spec,
                  pl.BlockSpec((1, CONV_K - 1, CONV_WIDTH), lambda bi, ti: (bi, 0, 0))]
                 + [full(a) for a in weights],
        out_specs=[o[1] for o in outs],
        out_shape=[o[0] for o in outs],
        scratch_shapes=[pltpu.VMEM((1, tm + 8, CONV_WIDTH), _F32),
                        pltpu.VMEM((8, CONV_WIDTH), _F32)],
        compiler_params=pltpu.CompilerParams(
            dimension_semantics=("arbitrary", "arbitrary"), vmem_limit_bytes=VMEM_LIMIT_BYTES),
        name="proj_prompt",
    )(x, cos_t, sin_t, state, *weights)


def _proj_sample_kernel(x_ref, cos_ref, sn1_ref, sn2_ref, st_ref, pre_g_ref, w_q_ref, w_kv_ref, w_b_ref, q_g_ref,
                        w_uq_ref, w_uk_ref, kv_g_ref, conv_w_ref, w_oc_ref,
                        q_ref, kc_ref, ckv_ref, kpe_ref, ga_ref, sa_ref, mb_ref, cst_ref,
                        ub_ref, carry_ref, *, n_seq):
    h = _rms(x_ref[0], pre_g_ref[...]).astype(_BF16)
    cos = cos_ref[...]
    sn1 = sn1_ref[...]
    sn2 = sn2_ref[...]

    def rope(v):
        return (v * cos + pltpu.roll(v, HALF_ROPE, 1) * sn1
                + pltpu.roll(v, LANES - HALF_ROPE, 1) * sn2)

    qn = _rms(_dot(h, w_q_ref[...]), q_g_ref[...]).astype(_BF16)
    q = _dot(qn, w_uq_ref[...])
    q_abs = _dot(q[:, :N_HEADS * QK_NOPE].astype(_BF16), w_uk_ref[...])
    lane_group = lax.broadcasted_iota(jnp.int32, (1, LANES), 1) // QK_ROPE
    n_rope_blocks = N_HEADS // HEADS_PER_ROPE_BLOCK
    q_rot = [rope(q[:, N_HEADS * QK_NOPE + j * LANES:N_HEADS * QK_NOPE + (j + 1) * LANES]) * Q_SCALE
             for j in range(n_rope_blocks)]
    for hd in range(N_HEADS):
        q_ref[0, hd, :, 0:LANES] = (q_abs[:, hd * LANES:(hd + 1) * LANES] * Q_SCALE).astype(_BF16)
        q_ref[0, hd, :, LANES:KEY_WIDTH_S] = jnp.where(
            lane_group == hd % HEADS_PER_ROPE_BLOCK, q_rot[hd // HEADS_PER_ROPE_BLOCK], 0.0).astype(_BF16)

    kvr = _dot(h, w_kv_ref[...])
    ckv = _rms(kvr[:, :KV_LORA], kv_g_ref[...])
    k_rot = rope(kvr[:, KV_LORA:])
    ckv_ref[0] = ckv
    kpe_ref[0] = k_rot[:, :QK_ROPE]
    kc_ref[0, :, 0:KV_LORA] = ckv.astype(_BF16)
    kc_ref[0, :, KV_LORA:KEY_WIDTH_S] = k_rot.astype(_BF16)

    _gates_and_conv(h, w_b_ref, st_ref, conv_w_ref, w_oc_ref, ga_ref, sa_ref, mb_ref, cst_ref,
                    ub_ref, carry_ref, n_seq=n_seq, carry_state=False)


def _project_sample(x, cos, sn1, sn2, state, w, *, n_seq):
    _, tm, _ = x.shape

    def rows(width, dtype):
        return (jax.ShapeDtypeStruct((1, tm, width), dtype), pl.BlockSpec((1, tm, width), lambda i: (0, 0, 0)))

    def full(a):
        return pl.BlockSpec(a.shape, lambda i: (0,) * a.ndim)

    outs = [
        (jax.ShapeDtypeStruct((1, N_HEADS, tm, KEY_WIDTH_S), _BF16),
         pl.BlockSpec((1, N_HEADS, tm, KEY_WIDTH_S), lambda i: (0, 0, 0, 0))),
        rows(KEY_WIDTH_S, _BF16), rows(KV_LORA, _F32), rows(QK_ROPE, _F32),
        rows(MLA_WIDTH, _BF16), rows(D_MODEL, _BF16), rows(D_MODEL, _BF16),
        (jax.ShapeDtypeStruct((n_seq, CONV_K - 1, CONV_WIDTH), _F32),
         pl.BlockSpec((n_seq, CONV_K - 1, CONV_WIDTH), lambda i: (0, 0, 0))),
    ]
    weights = (w["pre_g"], w["w_q"], w["w_kv_s"], w["w_b"], w["q_g"], w["w_uq"], w["w_uk"],
               w["kv_g"], w["conv_w"], w["w_oc"])
    return pl.pallas_call(
        functools.partial(_proj_sample_kernel, n_seq=n_seq),
        grid=(1,),
        in_specs=[full(x), full(cos), full(sn1), full(sn2), full(state)] + [full(a) for a in weights],
        out_specs=[o[1] for o in outs],
        out_shape=[o[0] for o in outs],
        scratch_shapes=[pltpu.VMEM((n_seq, tm // n_seq + 8, CONV_WIDTH), _F32),
                        pltpu.VMEM((8, CONV_WIDTH), _F32)],
        compiler_params=pltpu.CompilerParams(
            dimension_semantics=("arbitrary",), vmem_limit_bytes=VMEM_LIMIT_BYTES),
        name="proj_sample",
    )(x, cos, sn1, sn2, state, *weights)


def _epilogue(o_lat, ga, sa, mb, x, w_uv_ref, w_om_ref, w_out_ref, post_g_ref):
    o = _dot(o_lat.astype(_BF16), w_uv_ref[...])
    branch_a = _dot((o * ga.astype(_F32)).astype(_BF16), w_om_ref[...])
    merged = sa.astype(_F32) * branch_a + mb.astype(_F32)
    z = _dot(merged.astype(_BF16), w_out_ref[...])
    return x + _rms(z, post_g_ref[...])


def _col_reduce(v, op):
    n_keys, cols = v.shape
    slabs = max(n_keys // REDUCE_SLAB, 1)
    part = op(v.reshape(slabs, n_keys // slabs, cols), axis=0)
    return op(part, axis=0, keepdims=True)


def _attn_prompt_kernel(qt_ref, kc_ref, vt_ref, bias_ref, ga_ref, sa_ref, mb_ref, x_ref,
                        w_uv_ref, w_om_ref, w_out_ref, post_g_ref, y_ref,
                        m_ref, l_ref, acc_ref, s_ref, mc_ref, *, tq, tk):
    i = pl.program_id(1)
    units = tk // MXU_DIM
    q_tiles_per_k_tile = tk // tq
    edge = i // q_tiles_per_k_tile
    cols = [slice(hd * tq, (hd + 1) * tq) for hd in range(N_HEADS)]

    m_ref[...] = jnp.full(m_ref.shape, NEG_INF, _F32)
    l_ref[...] = jnp.zeros(l_ref.shape, _F32)
    acc_ref[...] = jnp.zeros(acc_ref.shape, _F32)

    def keys_of(tile):
        return kc_ref[0, pl.ds(pl.multiple_of(tile * tk, tk), tk), :]

    def values_of(tile):
        return [vt_ref[0, tile * units + u] for u in range(units)]

    def stage1(k, hd, bias=None):
        s = _dot(k, qt_ref[0, 0, :, cols[hd]])
        if bias is not None:
            s = s + bias
        s_ref[:, cols[hd]] = s
        mc_ref[:, cols[hd]] = _col_reduce(s, jnp.max)

    def stage2(vts, hd):
        col = cols[hd]
        m_prev = m_ref[:, col]
        m_new = jnp.maximum(m_prev, mc_ref[:, col])
        alpha = jnp.exp2(m_prev - m_new)
        p = jnp.exp2(s_ref[:, col] - m_new)
        l_ref[:, col] = alpha * l_ref[:, col] + _col_reduce(p, jnp.sum)
        pb = p.astype(_BF16)
        pv = _dot(vts[0], pb[0:MXU_DIM])
        for u in range(1, units):
            pv = pv + _dot(vts[u], pb[u * MXU_DIM:(u + 1) * MXU_DIM])
        acc_ref[:, col] = alpha * acc_ref[:, col] + pv
        m_ref[:, col] = m_new

    k_edge = keys_of(edge)
    for hd in range(N_HEADS):
        stage1(k_edge, hd, bias_ref[i % q_tiles_per_k_tile])

    def body(j, carry):
        vts = values_of(jnp.where(j == 0, edge, j - 1))
        k = keys_of(j)
        for hd in range(N_HEADS):
            stage2(vts, hd)
            stage1(k, hd)
        return carry

    lax.fori_loop(0, edge, body, 0)

    last_vts = values_of(jnp.where(edge == 0, edge, edge - 1))
    for hd in range(N_HEADS):
        stage2(last_vts, hd)

    o_t = acc_ref[...] / l_ref[...]
    o_lat = jnp.concatenate([o_t[:, col].T for col in cols], axis=1)
    y_ref[0] = _epilogue(o_lat, ga_ref[0], sa_ref[0], mb_ref[0], x_ref[0],
                         w_uv_ref, w_om_ref, w_out_ref, post_g_ref)


def _attend_prompt(qt, kc, vt, ga, sa, mb, x, w, *, tq, tk):
    b, t, _ = x.shape
    cols = N_HEADS * tq

    def row_spec(width):
        return pl.BlockSpec((1, tq, width), lambda bi, qi: (bi, qi, 0))

    def full(a):
        return pl.BlockSpec(a.shape, lambda bi, qi: (0,) * a.ndim)

    k_chunk = np.arange(tk)[None, :, None] // CHUNK
    q_chunk = (np.arange(tk // tq)[:, None, None] * tq + np.arange(tq)[None, None, :]) // CHUNK
    bias = jnp.asarray(np.where(k_chunk <= q_chunk, 0.0, NEG_INF), _F32)

    weights = (w["w_uv"], w["w_om"], w["w_out"], w["post_g"])
    return pl.pallas_call(
        functools.partial(_attn_prompt_kernel, tq=tq, tk=tk),
        grid=(b, t // tq),
        in_specs=[pl.BlockSpec((1, 1, KEY_DIM, cols), lambda bi, qi: (bi, qi, 0, 0)),
                  pl.BlockSpec((1, t, KEY_DIM), lambda bi, qi: (bi, 0, 0)),
                  pl.BlockSpec((1, t // MXU_DIM, KV_LORA, MXU_DIM), lambda bi, qi: (bi, 0, 0, 0)),
                  full(bias),
                  row_spec(MLA_WIDTH), row_spec(D_MODEL), row_spec(D_MODEL), row_spec(D_MODEL)]
                 + [full(a) for a in weights],
        out_specs=row_spec(D_MODEL),
        out_shape=jax.ShapeDtypeStruct((b, t, D_MODEL), _F32),
        scratch_shapes=[pltpu.VMEM((1, cols), _F32), pltpu.VMEM((1, cols), _F32),
                        pltpu.VMEM((KV_LORA, cols), _F32),
                        pltpu.VMEM((tk, cols), _F32), pltpu.VMEM((1, cols), _F32)],
        compiler_params=pltpu.CompilerParams(
            dimension_semantics=("arbitrary", "arbitrary"), vmem_limit_bytes=VMEM_LIMIT_BYTES),
        name="attn_prompt",
    )(qt, kc, vt, bias, ga, sa, mb, x, *weights)


def _attn_first(q, k, m_ref, l_ref, acc_ref, visible=None):
    s = _dot_nt(q, k)
    if visible is not None:
        s = jnp.where(visible, s, NEG_INF)
    m = jnp.max(s, axis=1, keepdims=True)
    p = jnp.exp2(s - m)
    m_ref[...] = jnp.broadcast_to(m, m_ref.shape)
    l_ref[...] = jnp.broadcast_to(jnp.sum(p, axis=1, keepdims=True), l_ref.shape)
    acc_ref[...] = _dot(p.astype(_BF16), k[:, :KV_LORA])


def _attn_update(q, k, m_ref, l_ref, acc_ref):
    s = _dot_nt(q, k)
    m_prev = m_ref[...]
    m_new = jnp.maximum(m_prev, jnp.max(s, axis=1, keepdims=True))
    alpha = jnp.exp2(m_prev - m_new)
    p = jnp.exp2(s - jnp.concatenate([m_new] * (s.shape[1] // LANES), axis=1))
    l_ref[...] = alpha * l_ref[...] + jnp.sum(p, axis=1, keepdims=True)
    acc_ref[...] = alpha * acc_ref[...] + _dot(p.astype(_BF16), k[:, :KV_LORA])
    m_ref[...] = m_new


def _attn_sample_kernel(q_ref, kpast_ref, knew_ref, ga_ref, sa_ref, mb_ref, x_ref,
                        w_uv_ref, w_om_ref, w_out_ref, post_g_ref, y_ref,
                        m_ref, l_ref, acc_ref, o_ref, *, new_visible):
    bi = pl.program_id(0)
    t_new = knew_ref.shape[1]
    rows = N_HEADS * t_new
    q = q_ref[0].reshape(rows, KEY_WIDTH_S)
    if new_visible is None:
        visible = None
    else:
        q_pos = lax.broadcasted_iota(jnp.int32, (rows, t_new), 0) % t_new
        k_pos = lax.broadcasted_iota(jnp.int32, (rows, t_new), 1)
        visible = (k_pos + new_visible[0]) // CHUNK <= (q_pos + new_visible[0]) // CHUNK
    _attn_first(q, knew_ref[0], m_ref, l_ref, acc_ref, visible=visible)
    _attn_update(q, kpast_ref[0], m_ref, l_ref, acc_ref)
    o = acc_ref[...] / l_ref[...]
    r0 = pl.multiple_of(bi * t_new, t_new)
    o_ref[pl.ds(r0, t_new), :] = jnp.concatenate(
        [o[hd * t_new:(hd + 1) * t_new] for hd in range(N_HEADS)], axis=1)

    @pl.when(bi == pl.num_programs(0) - 1)
    def _():
        y_ref[...] = _epilogue(o_ref[...], ga_ref[...], sa_ref[...], mb_ref[...], x_ref[...],
                               w_uv_ref, w_om_ref, w_out_ref, post_g_ref)


def _attend_sample(q, k_past, k_new, ga, sa, mb, x, w, *, past_len):
    nb, t_new, _ = k_new.shape
    n_rows = nb * t_new
    rows = N_HEADS * t_new
    last_q, first_q = past_len + t_new - 1, past_len
    new_visible = None if last_q // CHUNK == first_q // CHUNK else (past_len,)

    def full(a):
        return pl.BlockSpec(a.shape, lambda bi: (0,) * a.ndim)

    weights = (w["w_uv"], w["w_om"], w["w_out"], w["post_g"])
    return pl.pallas_call(
        functools.partial(_attn_sample_kernel, new_visible=new_visible),
        grid=(nb,),
        in_specs=[pl.BlockSpec((1, N_HEADS, t_new, KEY_WIDTH_S), lambda bi: (0, 0, bi, 0)),
                  pl.BlockSpec((1, past_len, KEY_WIDTH_S), lambda bi: (bi, 0, 0)),
                  pl.BlockSpec((1, t_new, KEY_WIDTH_S), lambda bi: (bi, 0, 0)),
                  full(ga), full(sa), full(mb), full(x)] + [full(a) for a in weights],
        out_specs=pl.BlockSpec((n_rows, D_MODEL), lambda bi: (0, 0)),
        out_shape=jax.ShapeDtypeStruct((n_rows, D_MODEL), _F32),
        scratch_shapes=[pltpu.VMEM((rows, LANES), _F32), pltpu.VMEM((rows, LANES), _F32),
                        pltpu.VMEM((rows, KV_LORA), _F32), pltpu.VMEM((n_rows, N_HEADS * KV_LORA), _F32)],
        compiler_params=pltpu.CompilerParams(
            dimension_semantics=("arbitrary",), vmem_limit_bytes=VMEM_LIMIT_BYTES),
        name="attn_sample",
    )(q, k_past, k_new, ga, sa, mb, x, *weights)


def _block_diag(blocks):
    n = len(blocks)
    r, c = blocks[0].shape
    rows = []
    for j, blk in enumerate(blocks):
        rows.append(jnp.pad(blk, ((0, 0), (j * c, (n - 1 - j) * c))))
    return jnp.concatenate(rows, axis=0)


def _prep_weights(pre_norm, w_in, q_norm, w_uq, kv_norm, w_uk, w_uv, w_o_mla, conv_w, w_o_conv, w_out, post_norm):
    o_kv = Q_LORA
    o_kr = o_kv + KV_LORA
    o_tail = o_kr + QK_ROPE
    assert w_in.shape[1] == o_tail + _TAIL_COLS
    w_kv_lat = w_in[:, o_kv:o_kr].astype(_BF16)
    w_kr = w_in[:, o_kr:o_tail].astype(_BF16)
    wq = w_uq.reshape(Q_LORA, N_HEADS, QK_NOPE + QK_ROPE)
    w_uqn = wq[:, :, :QK_NOPE].reshape(Q_LORA, N_HEADS * QK_NOPE).astype(_BF16)
    w_uqp = wq[:, :, QK_NOPE:].reshape(Q_LORA, N_HEADS * QK_ROPE).astype(_BF16)
    w_uk_bd = _block_diag([w_uk[:, hd, :].T.astype(_BF16) for hd in range(N_HEADS)])
    return {
        "pre_g": pre_norm.reshape(1, D_MODEL),
        "w_q": w_in[:, :o_kv].astype(_BF16),
        "w_kv": jnp.concatenate([w_kv_lat, w_kr, jnp.zeros((D_MODEL, LANES - QK_ROPE), _BF16)], axis=1),
        "w_kv_s": jnp.concatenate([w_kv_lat] + [w_kr] * HEADS_PER_ROPE_BLOCK, axis=1),
        "w_b": w_in[:, o_tail:].astype(_BF16),
        "q_g": q_norm.reshape(1, Q_LORA),
        "w_uqn": w_uqn,
        "w_uqp_t": w_uqp.T,
        "w_uq": jnp.concatenate([w_uqn, w_uqp], axis=1),
        "w_uk": w_uk_bd,
        "w_uk_t": w_uk_bd.T,
        "kv_g": kv_norm.reshape(1, KV_LORA),
        "conv_w": conv_w,
        "w_oc": w_o_conv.astype(_BF16),
        "w_uv": _block_diag([w_uv[:, hd, :].astype(_BF16) for hd in range(N_HEADS)]),
        "w_om": w_o_mla.astype(_BF16),
        "w_out": w_out.astype(_BF16),
        "post_g": post_norm.reshape(1, D_MODEL),
    }


def _rope_angles(pos):
    inv = ROPE_BASE ** (-jnp.arange(HALF_ROPE, dtype=_F32) / HALF_ROPE)
    ang = pos.astype(_F32)[:, None] * inv[None, :]
    return jnp.cos(ang), jnp.sin(ang)


def _rope_tables_rows(pos):
    cos, sin = _rope_angles(pos)
    zero = jnp.zeros_like(sin)
    reps = LANES // QK_ROPE
    return (jnp.tile(jnp.concatenate([cos, cos], axis=1), (1, reps)),
            jnp.tile(jnp.concatenate([zero, sin], axis=1), (1, reps)),
            jnp.tile(jnp.concatenate([-sin, zero], axis=1), (1, reps)))


def _cached_keys(ckv, kpe):
    return jnp.concatenate([ckv] + [kpe] * HEADS_PER_ROPE_BLOCK, axis=-1).astype(_BF16)


PROMPT_ROW_TILE = 512
PROMPT_Q_TILE = 256
PROMPT_K_TILE = 512


def kernel(x_prompt, x_sample, cache_kv_latent, cache_k_rope, state_conv, pre_norm, w_in, q_norm, w_uq, kv_norm,
           w_uk, w_uv, w_o_mla, conv_w, w_o_conv, w_out, post_norm):
    depth = pre_norm.shape[0]
    assert depth == 1
    b, t, _ = x_prompt.shape
    nb, t_new, _ = x_sample.shape
    past_len = cache_kv_latent.shape[2]
    lyr = 0
    w = _prep_weights(pre_norm[lyr], w_in[lyr], q_norm[lyr], w_uq[lyr], kv_norm[lyr], w_uk[lyr], w_uv[lyr],
                      w_o_mla[lyr], conv_w[lyr], w_o_conv[lyr], w_out[lyr], post_norm[lyr])

    cos, sin = _rope_angles(jnp.arange(t, dtype=jnp.int32))
    zero_state = jnp.zeros((b, CONV_K - 1, CONV_WIDTH), _F32)
    qt, kc, vt, ckv_p, kpe_t, ga, sa, mb, cv_p = _project_prompt(
        x_prompt, cos.T, sin.T, zero_state, w, tm=PROMPT_ROW_TILE, tq=PROMPT_Q_TILE)
    y_p = _attend_prompt(qt, kc, vt, ga, sa, mb, x_prompt, w, tq=PROMPT_Q_TILE, tk=PROMPT_K_TILE)

    n_rows = nb * t_new
    pos_s = past_len + jnp.arange(t_new, dtype=jnp.int32)
    tabs_s = tuple(jnp.tile(tb, (nb, 1)) for tb in _rope_tables_rows(pos_s))
    xs = x_sample.reshape(1, n_rows, D_MODEL)
    q_s, kc_s, ckv_s, kpe_s, ga_s, sa_s, mb_s, cv_s = _project_sample(
        xs, *tabs_s, state_conv[lyr], w, n_seq=nb)
    k_past = _cached_keys(cache_kv_latent[lyr], cache_k_rope[lyr])
    y_s = _attend_sample(q_s, k_past, kc_s.reshape(nb, t_new, KEY_WIDTH_S),
                         ga_s[0], sa_s[0], mb_s[0], xs[0], w, past_len=past_len)

    return (y_p, y_s.reshape(nb, t_new, D_MODEL),
            ckv_p[None], jnp.swapaxes(kpe_t, 1, 2)[None], cv_p[None],
            ckv_s.reshape(1, nb, t_new, KV_LORA), kpe_s.reshape(1, nb, t_new, QK_ROPE), cv_s[None])
```

```python
import functools

import numpy as np
import jax
import jax.numpy as jnp
from jax import lax
from jax.experimental import pallas as pl
from jax.experimental.pallas import tpu as pltpu

N_HEADS = 8
QK_NOPE = 64
QK_ROPE = 32
V_HEAD = 64
Q_LORA = 256
KV_LORA = 128
MLA_WIDTH = N_HEADS * V_HEAD
CONV_WIDTH = 512
CONV_K = 3
D_MODEL = 1024
CHUNK = 64
ROPE_BASE = 10000.0
EPS = 1e-6
SM_SCALE = (QK_NOPE + QK_ROPE) ** -0.5
NEG_INF = -1e30
LOG2E = 1.4426950408889634

LANES = 128
MXU_DIM = 256
REDUCE_SLAB = 64
STAGE1_LEAD = 2
HALF_ROPE = QK_ROPE // 2
KEY_DIM = KV_LORA + QK_ROPE
KEY_WIDTH_S = 2 * LANES
HEADS_PER_ROPE_BLOCK = LANES // QK_ROPE
Q_SCALE = SM_SCALE * LOG2E
VMEM_LIMIT_BYTES = 56 * 1024 * 1024

_O_GM = 0
_O_CB = _O_GM + MLA_WIDTH
_O_CC = _O_CB + CONV_WIDTH
_O_CX = _O_CC + CONV_WIDTH
_O_GC = _O_CX + CONV_WIDTH
_O_MM = _O_GC + CONV_WIDTH
_O_MC = _O_MM + D_MODEL
_TAIL_COLS = _O_MC + D_MODEL

_F32 = jnp.float32
_BF16 = jnp.bfloat16
_NT = (((1,), (1,)), ((), ()))


def _rms(v, g):
    return v * lax.rsqrt(jnp.mean(v * v, axis=-1, keepdims=True) + EPS) * g


def _silu(v):
    return v * jax.nn.sigmoid(v)


def _dot(a, b):
    return jnp.dot(a, b, preferred_element_type=_F32)


def _dot_nt(a, b):
    return lax.dot_general(a, b, _NT, preferred_element_type=_F32)


def _gates_and_conv(h, w_b_ref, st_ref, conv_w_ref, w_oc_ref, ga_ref, sa_ref, mb_ref, cst_ref,
                    ub_ref, carry_ref, *, n_seq, carry_state):
    seq_len = h.shape[0] // n_seq

    def proj(off, n):
        return _dot(h, w_b_ref[:, off:off + n])

    ga_ref[0] = _silu(proj(_O_GM, MLA_WIDTH)).astype(_BF16)
    sa_ref[0] = jax.nn.sigmoid(proj(_O_MM, D_MODEL)).astype(_BF16)

    u = proj(_O_CC, CONV_WIDTH) * proj(_O_CX, CONV_WIDTH)
    w0 = conv_w_ref[0:1, :]
    w1 = conv_w_ref[1:2, :]
    w2 = conv_w_ref[2:3, :]
    if carry_state:
        @pl.when(pl.program_id(1) == 0)
        def _():
            carry_ref[0:CONV_K - 1, :] = st_ref[0]
    convs = []
    for j in range(n_seq):
        u_j = u[j * seq_len:(j + 1) * seq_len]
        prev = carry_ref[0:CONV_K - 1, :] if carry_state else st_ref[j]
        ub_ref[j, 8 - (CONV_K - 1):8, :] = prev
        ub_ref[j, 8:8 + seq_len, :] = u_j
        convs.append(w0 * ub_ref[j, 6:6 + seq_len, :] + w1 * ub_ref[j, 7:7 + seq_len, :] + w2 * u_j)
        new_state = u_j[seq_len - (CONV_K - 1):, :]
        cst_ref[j] = new_state
        if carry_state:
            carry_ref[0:CONV_K - 1, :] = new_state
    conv = convs[0] if n_seq == 1 else jnp.concatenate(convs, axis=0)
    bb = (proj(_O_CB, CONV_WIDTH) * conv * _silu(proj(_O_GC, CONV_WIDTH))).astype(_BF16)
    branch_b = _dot(bb, w_oc_ref[...])
    mb_ref[0] = (jax.nn.sigmoid(proj(_O_MC, D_MODEL)) * branch_b).astype(_BF16)


def _proj_prompt_kernel(x_ref, cos_ref, sin_ref, st_ref, pre_g_ref, w_q_ref, w_kv_ref, w_b_ref, q_g_ref,
                        w_uqn_ref, w_uqp_t_ref, w_uk_t_ref, kv_g_ref, conv_w_ref, w_oc_ref,
                        qt_ref, kc_ref, vt_ref, ckv_ref, kpet_ref, ga_ref, sa_ref, mb_ref, cst_ref,
                        ub_ref, carry_ref, *, tq):
    tm = x_ref.shape[1]
    h = _rms(x_ref[0], pre_g_ref[...]).astype(_BF16)
    cos_t = cos_ref[...]
    sin_t = sin_ref[...]

    qn = _rms(_dot(h, w_q_ref[...]), q_g_ref[...]).astype(_BF16)
    q_nope = _dot(qn, w_uqn_ref[...]).astype(_BF16)
    q_abs_t = _dot_nt(w_uk_t_ref[...], q_nope)
    q_pe_t = _dot_nt(w_uqp_t_ref[...], qn).reshape(N_HEADS, QK_ROPE, tm)
    x1 = q_pe_t[:, :HALF_ROPE, :]
    x2 = q_pe_t[:, HALF_ROPE:, :]
    r1 = (x1 * cos_t - x2 * sin_t) * Q_SCALE
    r2 = (x2 * cos_t + x1 * sin_t) * Q_SCALE
    for j in range(tm // tq):
        tok = slice(j * tq, (j + 1) * tq)
        for hd in range(N_HEADS):
            qt_ref[0, j, hd, 0:KV_LORA, :] = (q_abs_t[hd * KV_LORA:(hd + 1) * KV_LORA, tok] * Q_SCALE).astype(_BF16)
            qt_ref[0, j, hd, KV_LORA:KV_LORA + HALF_ROPE, :] = r1[hd][:, tok].astype(_BF16)
            qt_ref[0, j, hd, KV_LORA + HALF_ROPE:KEY_DIM, :] = r2[hd][:, tok].astype(_BF16)

    kvr = _dot(h, w_kv_ref[...])
    ckv = _rms(kvr[:, :KV_LORA], kv_g_ref[...])
    ckv_ref[0] = ckv
    ckv_t = ckv.T.astype(_BF16)
    for u in range(tm // MXU_DIM):
        vt_ref[0, u] = ckv_t[:, u * MXU_DIM:(u + 1) * MXU_DIM]
    kr_t = kvr[:, KV_LORA:].T
    k1 = kr_t[0:HALF_ROPE]
    k2 = kr_t[HALF_ROPE:QK_ROPE]
    kpe_t = jnp.concatenate([k1 * cos_t - k2 * sin_t, k2 * cos_t + k1 * sin_t], axis=0)
    kpet_ref[0] = kpe_t
    kpe = jnp.concatenate([kpe_t, jnp.zeros((LANES - QK_ROPE, tm), _F32)], axis=0).T
    kc_ref[0, :, 0:KV_LORA] = ckv.astype(_BF16)
    kc_ref[0, :, KV_LORA:KEY_DIM] = kpe[:, :QK_ROPE].astype(_BF16)

    _gates_and_conv(h, w_b_ref, st_ref, conv_w_ref, w_oc_ref, ga_ref, sa_ref, mb_ref, cst_ref,
                    ub_ref, carry_ref, n_seq=1, carry_state=True)


def _project_prompt(x, cos_t, sin_t, state, w, *, tm, tq):
    b, t, _ = x.shape

    def rows(width, dtype):
        return (jax.ShapeDtypeStruct((b, t, width), dtype),
                pl.BlockSpec((1, tm, width), lambda bi, ti: (bi, ti, 0)))

    def full(a):
        return pl.BlockSpec(a.shape, lambda bi, ti: (0,) * a.ndim)

    outs = [
        (jax.ShapeDtypeStruct((b, t // tq, N_HEADS, KEY_DIM, tq), _BF16),
         pl.BlockSpec((1, tm // tq, N_HEADS, KEY_DIM, tq), lambda bi, ti: (bi, ti, 0, 0, 0))),
        rows(KEY_DIM, _BF16),
        (jax.ShapeDtypeStruct((b, t // MXU_DIM, KV_LORA, MXU_DIM), _BF16),
         pl.BlockSpec((1, tm // MXU_DIM, KV_LORA, MXU_DIM), lambda bi, ti: (bi, ti, 0, 0))),
        rows(KV_LORA, _F32),
        (jax.ShapeDtypeStruct((b, QK_ROPE, t), _F32),
         pl.BlockSpec((1, QK_ROPE, tm), lambda bi, ti: (bi, 0, ti))),
        rows(MLA_WIDTH, _BF16),
        rows(D_MODEL, _BF16),
        rows(D_MODEL, _BF16),
        (jax.ShapeDtypeStruct((b, CONV_K - 1, CONV_WIDTH), _F32),
         pl.BlockSpec((1, CONV_K - 1, CONV_WIDTH), lambda bi, ti: (bi, 0, 0))),
    ]
    table_spec = pl.BlockSpec((HALF_ROPE, tm), lambda bi, ti: (0, ti))
    weights = (w["pre_g"], w["w_q"], w["w_kv"], w["w_b"], w["q_g"], w["w_uqn"], w["w_uqp_t"], w["w_uk_t"],
               w["kv_g"], w["conv_w"], w["w_oc"])
    return pl.pallas_call(
        functools.partial(_proj_prompt_kernel, tq=tq),
        grid=(b, t // tm),
        in_specs=[pl.BlockSpec((1, tm, D_MODEL), lambda bi, ti: (bi, ti, 0)),
                  table_spec, table_spec,
                  pl.BlockSpec((1, CONV_K - 1, CONV_WIDTH), lambda bi, ti: (bi, 0, 0))]
                 + [full(a) for a in weights],
        out_specs=[o[1] for o in outs],
        out_shape=[o[0] for o in outs],
        scratch_shapes=[pltpu.VMEM((1, tm + 8, CONV_WIDTH), _F32),
                        pltpu.VMEM((8, CONV_WIDTH), _F32)],
        compiler_params=pltpu.CompilerParams(
            dimension_semantics=("arbitrary", "arbitrary"), vmem_limit_bytes=VMEM_LIMIT_BYTES),
        name="proj_prompt",
    )(x, cos_t, sin_t, state, *weights)


def _proj_sample_kernel(x_ref, cos_ref, sn1_ref, sn2_ref, st_ref, pre_g_ref, w_q_ref, w_kv_ref, w_b_ref, q_g_ref,
                        w_uq_ref, w_uk_ref, kv_g_ref, conv_w_ref, w_oc_ref,
                        q_ref, kc_ref, ckv_ref, kpe_ref, ga_ref, sa_ref, mb_ref, cst_ref,
                        ub_ref, carry_ref, *, n_seq):
    h = _rms(x_ref[0], pre_g_ref[...]).astype(_BF16)
    cos = cos_ref[...]
    sn1 = sn1_ref[...]
    sn2 = sn2_ref[...]

    def rope(v):
        return (v * cos + pltpu.roll(v, HALF_ROPE, 1) * sn1
                + pltpu.roll(v, LANES - HALF_ROPE, 1) * sn2)

    qn = _rms(_dot(h, w_q_ref[...]), q_g_ref[...]).astype(_BF16)
    q = _dot(qn, w_uq_ref[...])
    q_abs = _dot(q[:, :N_HEADS * QK_NOPE].astype(_BF16), w_uk_ref[...])
    lane_group = lax.broadcasted_iota(jnp.int32, (1, LANES), 1) // QK_ROPE
    n_rope_blocks = N_HEADS // HEADS_PER_ROPE_BLOCK
    q_rot = [rope(q[:, N_HEADS * QK_NOPE + j * LANES:N_HEADS * QK_NOPE + (j + 1) * LANES]) * Q_SCALE
             for j in range(n_rope_blocks)]
    for hd in range(N_HEADS):
        q_ref[0, hd, :, 0:LANES] = (q_abs[:, hd * LANES:(hd + 1) * LANES] * Q_SCALE).astype(_BF16)
        q_ref[0, hd, :, LANES:KEY_WIDTH_S] = jnp.where(
            lane_group == hd % HEADS_PER_ROPE_BLOCK, q_rot[hd // HEADS_PER_ROPE_BLOCK], 0.0).astype(_BF16)

    kvr = _dot(h, w_kv_ref[...])
    ckv = _rms(kvr[:, :KV_LORA], kv_g_ref[...])
    k_rot = rope(kvr[:, KV_LORA:])
    ckv_ref[0] = ckv
    kpe_ref[0] = k_rot[:, :QK_ROPE]
    kc_ref[0, :, 0:KV_LORA] = ckv.astype(_BF16)
    kc_ref[0, :, KV_LORA:KEY_WIDTH_S] = k_rot.astype(_BF16)

    _gates_and_conv(h, w_b_ref, st_ref, conv_w_ref, w_oc_ref, ga_ref, sa_ref, mb_ref, cst_ref,
                    ub_ref, carry_ref, n_seq=n_seq, carry_state=False)


def _project_sample(x, cos, sn1, sn2, state, w, *, n_seq):
    _, tm, _ = x.shape

    def rows(width, dtype):
        return (jax.ShapeDtypeStruct((1, tm, width), dtype), pl.BlockSpec((1, tm, width), lambda i: (0, 0, 0)))

    def full(a):
        return pl.BlockSpec(a.shape, lambda i: (0,) * a.ndim)

    outs = [
        (jax.ShapeDtypeStruct((1, N_HEADS, tm, KEY_WIDTH_S), _BF16),
         pl.BlockSpec((1, N_HEADS, tm, KEY_WIDTH_S), lambda i: (0, 0, 0, 0))),
        rows(KEY_WIDTH_S, _BF16), rows(KV_LORA, _F32), rows(QK_ROPE, _F32),
        rows(MLA_WIDTH, _BF16), rows(D_MODEL, _BF16), rows(D_MODEL, _BF16),
        (jax.ShapeDtypeStruct((n_seq, CONV_K - 1, CONV_WIDTH), _F32),
         pl.BlockSpec((n_seq, CONV_K - 1, CONV_WIDTH), lambda i: (0, 0, 0))),
    ]
    weights = (w["pre_g"], w["w_q"], w["w_kv_s"], w["w_b"], w["q_g"], w["w_uq"], w["w_uk"],
               w["kv_g"], w["conv_w"], w["w_oc"])
    return pl.pallas_call(
        functools.partial(_proj_sample_kernel, n_seq=n_seq),
        grid=(1,),
        in_specs=[full(x), full(cos), full(sn1), full(sn2), full(state)] + [full(a) for a in weights],
        out_specs=[o[1] for o in outs],
        out_shape=[o[0] for o in outs],
        scratch_shapes=[pltpu.VMEM((n_seq, tm // n_seq + 8, CONV_WIDTH), _F32),
                        pltpu.VMEM((8, CONV_WIDTH), _F32)],
        compiler_params=pltpu.CompilerParams(
            dimension_semantics=("arbitrary",), vmem_limit_bytes=VMEM_LIMIT_BYTES),
        name="proj_sample",
    )(x, cos, sn1, sn2, state, *weights)


def _epilogue(o_lat, ga, sa, mb, x, w_uv_ref, w_om_ref, w_out_ref, post_g_ref):
    o = _dot(o_lat.astype(_BF16), w_uv_ref[...])
    branch_a = _dot((o * ga.astype(_F32)).astype(_BF16), w_om_ref[...])
    merged = sa.astype(_F32) * branch_a + mb.astype(_F32)
    z = _dot(merged.astype(_BF16), w_out_ref[...])
    return x + _rms(z, post_g_ref[...])


def _col_reduce(v, op):
    n_keys, cols = v.shape
    slabs = max(n_keys // REDUCE_SLAB, 1)
    part = op(v.reshape(slabs, n_keys // slabs, cols), axis=0)
    return op(part, axis=0, keepdims=True)


def _attn_prompt_kernel(qt_ref, qn_ref, kc_ref, vt_ref, bias_ref, ga_ref, sa_ref, mb_ref, x_ref,
                        w_uv_ref, w_om_ref, w_out_ref, post_g_ref, y_ref,
                        m_ref, l_ref, acc_ref, s_ref, mc_ref, *, tq, tk):
    i = pl.program_id(1)
    units = tk // MXU_DIM
    q_tiles_per_k_tile = tk // tq
    edge = i // q_tiles_per_k_tile

    m_ref[...] = jnp.full(m_ref.shape, NEG_INF, _F32)
    l_ref[...] = jnp.zeros(l_ref.shape, _F32)
    acc_ref[...] = jnp.zeros(acc_ref.shape, _F32)

    def keys_of(tile):
        return kc_ref[0, pl.ds(pl.multiple_of(tile * tk, tk), tk), :]

    def values_of(tile):
        return [vt_ref[0, tile * units + u] for u in range(units)]

    def stage1(k, hd, bias=None, q_ref=qt_ref):
        s = _dot(k, q_ref[0, 0, hd])
        if bias is not None:
            s = s + bias
        s_ref[hd] = s
        mc_ref[hd] = _col_reduce(s, jnp.max)

    def stage2(vts, hd):
        m_prev = m_ref[hd]
        m_new = jnp.maximum(m_prev, mc_ref[hd])
        alpha = jnp.exp2(m_prev - m_new)
        p = jnp.exp2(s_ref[hd] - m_new)
        l_ref[hd] = alpha * l_ref[hd] + _col_reduce(p, jnp.sum)
        pb = p.astype(_BF16)
        pv = _dot(vts[0], pb[0:MXU_DIM])
        for u in range(1, units):
            pv = pv + _dot(vts[u], pb[u * MXU_DIM:(u + 1) * MXU_DIM])
        acc_ref[hd] = alpha * acc_ref[hd] + pv
        m_ref[hd] = m_new

    @pl.when(i == 0)
    def _():
        k_edge = keys_of(edge)
        for hd in range(N_HEADS):
            stage1(k_edge, hd, bias_ref[0])

    def body(j, carry):
        vts = values_of(jnp.where(j == 0, edge, j - 1))
        k = keys_of(j)
        for hd in range(N_HEADS):
            stage2(vts, hd)
            stage1(k, hd)
        return carry

    lax.fori_loop(0, edge, body, 0)

    last_vts = values_of(jnp.where(edge == 0, edge, edge - 1))
    nxt = jnp.minimum(i + 1, pl.num_programs(1) - 1)
    k_next = keys_of(nxt // q_tiles_per_k_tile)
    bias_next = bias_ref[nxt % q_tiles_per_k_tile]
    for hd in range(N_HEADS):
        stage2(last_vts, hd)
        stage1(k_next, hd, bias_next, qn_ref)

    o_lat = jnp.concatenate([(acc_ref[hd] / l_ref[hd]).T for hd in range(N_HEADS)], axis=1)
    y_ref[0] = _epilogue(o_lat, ga_ref[0], sa_ref[0], mb_ref[0], x_ref[0],
                         w_uv_ref, w_om_ref, w_out_ref, post_g_ref)


def _attend_prompt(qt, kc, vt, ga, sa, mb, x, w, *, tq, tk):
    b, t, _ = x.shape
    n_q = t // tq

    def row_spec(width):
        return pl.BlockSpec((1, tq, width), lambda bi, qi: (bi, qi, 0))

    def full(a):
        return pl.BlockSpec(a.shape, lambda bi, qi: (0,) * a.ndim)

    k_chunk = np.arange(tk)[None, :, None] // CHUNK
    q_chunk = (np.arange(tk // tq)[:, None, None] * tq + np.arange(tq)[None, None, :]) // CHUNK
    bias = jnp.asarray(np.where(k_chunk <= q_chunk, 0.0, NEG_INF), _F32)

    weights = (w["w_uv"], w["w_om"], w["w_out"], w["post_g"])
    return pl.pallas_call(
        functools.partial(_attn_prompt_kernel, tq=tq, tk=tk),
        grid=(b, t // tq),
        in_specs=[pl.BlockSpec((1, 1, N_HEADS, KEY_DIM, tq), lambda bi, qi: (bi, qi, 0, 0, 0)),
                  pl.BlockSpec((1, 1, N_HEADS, KEY_DIM, tq),
                               lambda bi, qi: (bi, jnp.minimum(qi + 1, n_q - 1), 0, 0, 0)),
                  pl.BlockSpec((1, t, KEY_DIM), lambda bi, qi: (bi, 0, 0)),
                  pl.BlockSpec((1, t // MXU_DIM, KV_LORA, MXU_DIM), lambda bi, qi: (bi, 0, 0, 0)),
                  full(bias),
                  row_spec(MLA_WIDTH), row_spec(D_MODEL), row_spec(D_MODEL), row_spec(D_MODEL)]
                 + [full(a) for a in weights],
        out_specs=row_spec(D_MODEL),
        out_shape=jax.ShapeDtypeStruct((b, t, D_MODEL), _F32),
        scratch_shapes=[pltpu.VMEM((N_HEADS, 1, tq), _F32), pltpu.VMEM((N_HEADS, 1, tq), _F32),
                        pltpu.VMEM((N_HEADS, KV_LORA, tq), _F32),
                        pltpu.VMEM((N_HEADS, tk, tq), _F32), pltpu.VMEM((N_HEADS, 1, tq), _F32)],
        compiler_params=pltpu.CompilerParams(
            dimension_semantics=("arbitrary", "arbitrary"), vmem_limit_bytes=VMEM_LIMIT_BYTES),
        name="attn_prompt",
    )(qt, qt, kc, vt, bias, ga, sa, mb, x, *weights)


def _attn_first(q, k, m_ref, l_ref, acc_ref, visible=None):
    s = _dot_nt(q, k)
    if visible is not None:
        s = jnp.where(visible, s, NEG_INF)
    m = jnp.max(s, axis=1, keepdims=True)
    p = jnp.exp2(s - m)
    m_ref[...] = jnp.broadcast_to(m, m_ref.shape)
    l_ref[...] = jnp.broadcast_to(jnp.sum(p, axis=1, keepdims=True), l_ref.shape)
    acc_ref[...] = _dot(p.astype(_BF16), k[:, :KV_LORA])


def _attn_update(q, k, m_ref, l_ref, acc_ref):
    s = _dot_nt(q, k)
    m_prev = m_ref[...]
    m_new = jnp.maximum(m_prev, jnp.max(s, axis=1, keepdims=True))
    alpha = jnp.exp2(m_prev - m_new)
    p = jnp.exp2(s - jnp.concatenate([m_new] * (s.shape[1] // LANES), axis=1))
    l_ref[...] = alpha * l_ref[...] + jnp.sum(p, axis=1, keepdims=True)
    acc_ref[...] = alpha * acc_ref[...] + _dot(p.astype(_BF16), k[:, :KV_LORA])
    m_ref[...] = m_new


def _attn_sample_kernel(q_ref, kpast_ref, knew_ref, ga_ref, sa_ref, mb_ref, x_ref,
                        w_uv_ref, w_om_ref, w_out_ref, post_g_ref, y_ref,
                        m_ref, l_ref, acc_ref, o_ref, *, new_visible):
    bi = pl.program_id(0)
    t_new = knew_ref.shape[1]
    rows = N_HEADS * t_new
    q = q_ref[0].reshape(rows, KEY_WIDTH_S)
    if new_visible is None:
        visible = None
    else:
        q_pos = lax.broadcasted_iota(jnp.int32, (rows, t_new), 0) % t_new
        k_pos = lax.broadcasted_iota(jnp.int32, (rows, t_new), 1)
        visible = (k_pos + new_visible[0]) // CHUNK <= (q_pos + new_visible[0]) // CHUNK
    _attn_first(q, knew_ref[0], m_ref, l_ref, acc_ref, visible=visible)
    _attn_update(q, kpast_ref[0], m_ref, l_ref, acc_ref)
    o = acc_ref[...] / l_ref[...]
    r0 = pl.multiple_of(bi * t_new, t_new)
    o_ref[pl.ds(r0, t_new), :] = jnp.concatenate(
        [o[hd * t_new:(hd + 1) * t_new] for hd in range(N_HEADS)], axis=1)

    @pl.when(bi == pl.num_programs(0) - 1)
    def _():
        y_ref[...] = _epilogue(o_ref[...], ga_ref[...], sa_ref[...], mb_ref[...], x_ref[...],
                               w_uv_ref, w_om_ref, w_out_ref, post_g_ref)


def _attend_sample(q, k_past, k_new, ga, sa, mb, x, w, *, past_len):
    nb, t_new, _ = k_new.shape
    n_rows = nb * t_new
    rows = N_HEADS * t_new
    last_q, first_q = past_len + t_new - 1, past_len
    new_visible = None if last_q // CHUNK == first_q // CHUNK else (past_len,)

    def full(a):
        return pl.BlockSpec(a.shape, lambda bi: (0,) * a.ndim)

    weights = (w["w_uv"], w["w_om"], w["w_out"], w["post_g"])
    return pl.pallas_call(
        functools.partial(_attn_sample_kernel, new_visible=new_visible),
        grid=(nb,),
        in_specs=[pl.BlockSpec((1, N_HEADS, t_new, KEY_WIDTH_S), lambda bi: (0, 0, bi, 0)),
                  pl.BlockSpec((1, past_len, KEY_WIDTH_S), lambda bi: (bi, 0, 0)),
                  pl.BlockSpec((1, t_new, KEY_WIDTH_S), lambda bi: (bi, 0, 0)),
                  full(ga), full(sa), full(mb), full(x)] + [full(a) for a in weights],
        out_specs=pl.BlockSpec((n_rows, D_MODEL), lambda bi: (0, 0)),
        out_shape=jax.ShapeDtypeStruct((n_rows, D_MODEL), _F32),
        scratch_shapes=[pltpu.VMEM((rows, LANES), _F32), pltpu.VMEM((rows, LANES), _F32),
                        pltpu.VMEM((rows, KV_LORA), _F32), pltpu.VMEM((n_rows, N_HEADS * KV_LORA), _F32)],
        compiler_params=pltpu.CompilerParams(
            dimension_semantics=("arbitrary",), vmem_limit_bytes=VMEM_LIMIT_BYTES),
        name="attn_sample",
    )(q, k_past, k_new, ga, sa, mb, x, *weights)


def _block_diag(blocks):
    n = len(blocks)
    r, c = blocks[0].shape
    rows = []
    for j, blk in enumerate(blocks):
        rows.append(jnp.pad(blk, ((0, 0), (j * c, (n - 1 - j) * c))))
    return jnp.concatenate(rows, axis=0)


def _prep_weights(pre_norm, w_in, q_norm, w_uq, kv_norm, w_uk, w_uv, w_o_mla, conv_w, w_o_conv, w_out, post_norm):
    o_kv = Q_LORA
    o_kr = o_kv + KV_LORA
    o_tail = o_kr + QK_ROPE
    assert w_in.shape[1] == o_tail + _TAIL_COLS
    w_kv_lat = w_in[:, o_kv:o_kr].astype(_BF16)
    w_kr = w_in[:, o_kr:o_tail].astype(_BF16)
    wq = w_uq.reshape(Q_LORA, N_HEADS, QK_NOPE + QK_ROPE)
    w_uqn = wq[:, :, :QK_NOPE].reshape(Q_LORA, N_HEADS * QK_NOPE).astype(_BF16)
    w_uqp = wq[:, :, QK_NOPE:].reshape(Q_LORA, N_HEADS * QK_ROPE).astype(_BF16)
    w_uk_bd = _block_diag([w_uk[:, hd, :].T.astype(_BF16) for hd in range(N_HEADS)])
    return {
        "pre_g": pre_norm.reshape(1, D_MODEL),
        "w_q": w_in[:, :o_kv].astype(_BF16),
        "w_kv": jnp.concatenate([w_kv_lat, w_kr, jnp.zeros((D_MODEL, LANES - QK_ROPE), _BF16)], axis=1),
        "w_kv_s": jnp.concatenate([w_kv_lat] + [w_kr] * HEADS_PER_ROPE_BLOCK, axis=1),
        "w_b": w_in[:, o_tail:].astype(_BF16),
        "q_g": q_norm.reshape(1, Q_LORA),
        "w_uqn": w_uqn,
        "w_uqp_t": w_uqp.T,
        "w_uq": jnp.concatenate([w_uqn, w_uqp], axis=1),
        "w_uk": w_uk_bd,
        "w_uk_t": w_uk_bd.T,
        "kv_g": kv_norm.reshape(1, KV_LORA),
        "conv_w": conv_w,
        "w_oc": w_o_conv.astype(_BF16),
        "w_uv": _block_diag([w_uv[:, hd, :].astype(_BF16) for hd in range(N_HEADS)]),
        "w_om": w_o_mla.astype(_BF16),
        "w_out": w_out.astype(_BF16),
        "post_g": post_norm.reshape(1, D_MODEL),
    }


def _rope_angles(pos):
    inv = ROPE_BASE ** (-jnp.arange(HALF_ROPE, dtype=_F32) / HALF_ROPE)
    ang = pos.astype(_F32)[:, None] * inv[None, :]
    return jnp.cos(ang), jnp.sin(ang)


def _rope_tables_rows(pos):
    cos, sin = _rope_angles(pos)
    zero = jnp.zeros_like(sin)
    reps = LANES // QK_ROPE
    return (jnp.tile(jnp.concatenate([cos, cos], axis=1), (1, reps)),
            jnp.tile(jnp.concatenate([zero, sin], axis=1), (1, reps)),
            jnp.tile(jnp.concatenate([-sin, zero], axis=1), (1, reps)))


def _cached_keys(ckv, kpe):
    return jnp.concatenate([ckv] + [kpe] * HEADS_PER_ROPE_BLOCK, axis=-1).astype(_BF16)


PROMPT_ROW_TILE = 512
PROMPT_Q_TILE = 256
PROMPT_K_TILE = 512


def kernel(x_prompt, x_sample, cache_kv_latent, cache_k_rope, state_conv, pre_norm, w_in, q_norm, w_uq, kv_norm,
           w_uk, w_uv, w_o_mla, conv_w, w_o_conv, w_out, post_norm):
    depth = pre_norm.shape[0]
    assert depth == 1
    b, t, _ = x_prompt.shape
    nb, t_new, _ = x_sample.shape
    past_len = cache_kv_latent.shape[2]
    lyr = 0
    w = _prep_weights(pre_norm[lyr], w_in[lyr], q_norm[lyr], w_uq[lyr], kv_norm[lyr], w_uk[lyr], w_uv[lyr],
                      w_o_mla[lyr], conv_w[lyr], w_o_conv[lyr], w_out[lyr], post_norm[lyr])

    cos, sin = _rope_angles(jnp.arange(t, dtype=jnp.int32))
    zero_state = jnp.zeros((b, CONV_K - 1, CONV_WIDTH), _F32)
    qt, kc, vt, ckv_p, kpe_t, ga, sa, mb, cv_p = _project_prompt(
        x_prompt, cos.T, sin.T, zero_state, w, tm=PROMPT_ROW_TILE, tq=PROMPT_Q_TILE)
    y_p = _attend_prompt(qt, kc, vt, ga, sa, mb, x_prompt, w, tq=PROMPT_Q_TILE, tk=PROMPT_K_TILE)

    n_rows = nb * t_new
    pos_s = past_len + jnp.arange(t_new, dtype=jnp.int32)
    tabs_s = tuple(jnp.tile(tb, (nb, 1)) for tb in _rope_tables_rows(pos_s))
    xs = x_sample.reshape(1, n_rows, D_MODEL)
    q_s, kc_s, ckv_s, kpe_s, ga_s, sa_s, mb_s, cv_s = _project_sample(
        xs, *tabs_s, state_conv[lyr], w, n_seq=nb)
    k_past = _cached_keys(cache_kv_latent[lyr], cache_k_rope[lyr])
    y_s = _attend_sample(q_s, k_past, kc_s.reshape(nb, t_new, KEY_WIDTH_S),
                         ga_s[0], sa_s[0], mb_s[0], xs[0], w, past_len=past_len)

    return (y_p, y_s.reshape(nb, t_new, D_MODEL),
            ckv_p[None], jnp.swapaxes(kpe_t, 1, 2)[None], cv_p[None],
            ckv_s.reshape(1, nb, t_new, KV_LORA), kpe_s.reshape(1, nb, t_new, QK_ROPE), cv_s[None])
```

```python
import functools

import numpy as np
import jax
import jax.numpy as jnp
from jax import lax
from jax.experimental import pallas as pl
from jax.experimental.pallas import tpu as pltpu

N_HEADS = 8
QK_NOPE = 64
QK_ROPE = 32
V_HEAD = 64
Q_LORA = 256
KV_LORA = 128
MLA_WIDTH = N_HEADS * V_HEAD
CONV_WIDTH = 512
CONV_K = 3
D_MODEL = 1024
CHUNK = 64
ROPE_BASE = 10000.0
EPS = 1e-6
SM_SCALE = (QK_NOPE + QK_ROPE) ** -0.5
NEG_INF = -1e30
LOG2E = 1.4426950408889634

LANES = 128
MXU_DIM = 256
COL_BLOCK = MXU_DIM
REDUCE_SLAB = 64
HALF_ROPE = QK_ROPE // 2
KEY_DIM = KV_LORA + QK_ROPE
KEY_WIDTH_S = 2 * LANES
HEADS_PER_ROPE_BLOCK = LANES // QK_ROPE
Q_SCALE = SM_SCALE * LOG2E
VMEM_LIMIT_BYTES = 56 * 1024 * 1024

_O_GM = 0
_O_CB = _O_GM + MLA_WIDTH
_O_CC = _O_CB + CONV_WIDTH
_O_CX = _O_CC + CONV_WIDTH
_O_GC = _O_CX + CONV_WIDTH
_O_MM = _O_GC + CONV_WIDTH
_O_MC = _O_MM + D_MODEL
_TAIL_COLS = _O_MC + D_MODEL

_F32 = jnp.float32
_BF16 = jnp.bfloat16
_NT = (((1,), (1,)), ((), ()))


def _rms(v, g):
    return v * lax.rsqrt(jnp.mean(v * v, axis=-1, keepdims=True) + EPS) * g


def _silu(v):
    return v * jax.nn.sigmoid(v)


def _dot(a, b):
    return jnp.dot(a, b, preferred_element_type=_F32)


def _dot_nt(a, b):
    return lax.dot_general(a, b, _NT, preferred_element_type=_F32)


def _gates_and_conv(h, w_b_ref, st_ref, conv_w_ref, w_oc_ref, ga_ref, sa_ref, mb_ref, cst_ref,
                    ub_ref, carry_ref, *, n_seq, carry_state):
    seq_len = h.shape[0] // n_seq

    def proj(off, n):
        return _dot(h, w_b_ref[:, off:off + n])

    ga_ref[0] = _silu(proj(_O_GM, MLA_WIDTH)).astype(_BF16)
    sa_ref[0] = jax.nn.sigmoid(proj(_O_MM, D_MODEL)).astype(_BF16)

    u = proj(_O_CC, CONV_WIDTH) * proj(_O_CX, CONV_WIDTH)
    w0 = conv_w_ref[0:1, :]
    w1 = conv_w_ref[1:2, :]
    w2 = conv_w_ref[2:3, :]
    if carry_state:
        @pl.when(pl.program_id(1) == 0)
        def _():
            carry_ref[0:CONV_K - 1, :] = st_ref[0]
    convs = []
    for j in range(n_seq):
        u_j = u[j * seq_len:(j + 1) * seq_len]
        prev = carry_ref[0:CONV_K - 1, :] if carry_state else st_ref[j]
        ub_ref[j, 8 - (CONV_K - 1):8, :] = prev
        ub_ref[j, 8:8 + seq_len, :] = u_j
        convs.append(w0 * ub_ref[j, 6:6 + seq_len, :] + w1 * ub_ref[j, 7:7 + seq_len, :] + w2 * u_j)
        new_state = u_j[seq_len - (CONV_K - 1):, :]
        cst_ref[j] = new_state
        if carry_state:
            carry_ref[0:CONV_K - 1, :] = new_state
    conv = convs[0] if n_seq == 1 else jnp.concatenate(convs, axis=0)
    bb = (proj(_O_CB, CONV_WIDTH) * conv * _silu(proj(_O_GC, CONV_WIDTH))).astype(_BF16)
    branch_b = _dot(bb, w_oc_ref[...])
    mb_ref[0] = (jax.nn.sigmoid(proj(_O_MC, D_MODEL)) * branch_b).astype(_BF16)


def _proj_prompt_kernel(x_ref, cos_ref, sin_ref, st_ref, pre_g_ref, w_q_ref, w_kv_ref, w_b_ref, q_g_ref,
                        w_uqn_ref, w_uqp_t_ref, w_uk_t_ref, kv_g_ref, conv_w_ref, w_oc_ref,
                        qt_ref, kc_ref, vt_ref, ckv_ref, kpet_ref, ga_ref, sa_ref, mb_ref, cst_ref,
                        ub_ref, carry_ref, *, tq):
    tm = x_ref.shape[1]
    h = _rms(x_ref[0], pre_g_ref[...]).astype(_BF16)
    cos_t = cos_ref[...]
    sin_t = sin_ref[...]

    qn = _rms(_dot(h, w_q_ref[...]), q_g_ref[...]).astype(_BF16)
    q_nope = _dot(qn, w_uqn_ref[...]).astype(_BF16)
    q_abs_t = _dot_nt(w_uk_t_ref[...], q_nope)
    q_pe_t = _dot_nt(w_uqp_t_ref[...], qn).reshape(N_HEADS, QK_ROPE, tm)
    x1 = q_pe_t[:, :HALF_ROPE, :]
    x2 = q_pe_t[:, HALF_ROPE:, :]
    r1 = (x1 * cos_t - x2 * sin_t) * Q_SCALE
    r2 = (x2 * cos_t + x1 * sin_t) * Q_SCALE
    groups = tq // COL_BLOCK
    for j in range(tm // tq):
        for hd in range(N_HEADS):
            for g in range(groups):
                tok = slice(j * tq + g * COL_BLOCK, j * tq + (g + 1) * COL_BLOCK)
                cb = hd * groups + g
                qt_ref[0, j, cb, 0:KV_LORA, :] = (
                    q_abs_t[hd * KV_LORA:(hd + 1) * KV_LORA, tok] * Q_SCALE).astype(_BF16)
                qt_ref[0, j, cb, KV_LORA:KV_LORA + HALF_ROPE, :] = r1[hd][:, tok].astype(_BF16)
                qt_ref[0, j, cb, KV_LORA + HALF_ROPE:KEY_DIM, :] = r2[hd][:, tok].astype(_BF16)

    kvr = _dot(h, w_kv_ref[...])
    ckv = _rms(kvr[:, :KV_LORA], kv_g_ref[...])
    ckv_ref[0] = ckv
    ckv_t = ckv.T.astype(_BF16)
    for u in range(tm // MXU_DIM):
        vt_ref[0, u] = ckv_t[:, u * MXU_DIM:(u + 1) * MXU_DIM]
    kr_t = kvr[:, KV_LORA:].T
    k1 = kr_t[0:HALF_ROPE]
    k2 = kr_t[HALF_ROPE:QK_ROPE]
    kpe_t = jnp.concatenate([k1 * cos_t - k2 * sin_t, k2 * cos_t + k1 * sin_t], axis=0)
    kpet_ref[0] = kpe_t
    kpe = jnp.concatenate([kpe_t, jnp.zeros((LANES - QK_ROPE, tm), _F32)], axis=0).T
    kc_ref[0, :, 0:KV_LORA] = ckv.astype(_BF16)
    kc_ref[0, :, KV_LORA:KEY_DIM] = kpe[:, :QK_ROPE].astype(_BF16)

    _gates_and_conv(h, w_b_ref, st_ref, conv_w_ref, w_oc_ref, ga_ref, sa_ref, mb_ref, cst_ref,
                    ub_ref, carry_ref, n_seq=1, carry_state=True)


def _project_prompt(x, cos_t, sin_t, state, w, *, tm, tq):
    b, t, _ = x.shape

    def rows(width, dtype):
        return (jax.ShapeDtypeStruct((b, t, width), dtype),
                pl.BlockSpec((1, tm, width), lambda bi, ti: (bi, ti, 0)))

    def full(a):
        return pl.BlockSpec(a.shape, lambda bi, ti: (0,) * a.ndim)

    outs = [
        (jax.ShapeDtypeStruct((b, t // tq, N_HEADS * tq // COL_BLOCK, KEY_DIM, COL_BLOCK), _BF16),
         pl.BlockSpec((1, tm // tq, N_HEADS * tq // COL_BLOCK, KEY_DIM, COL_BLOCK),
                      lambda bi, ti: (bi, ti, 0, 0, 0))),
        rows(KEY_DIM, _BF16),
        (jax.ShapeDtypeStruct((b, t // MXU_DIM, KV_LORA, MXU_DIM), _BF16),
         pl.BlockSpec((1, tm // MXU_DIM, KV_LORA, MXU_DIM), lambda bi, ti: (bi, ti, 0, 0))),
        rows(KV_LORA, _F32),
        (jax.ShapeDtypeStruct((b, QK_ROPE, t), _F32),
         pl.BlockSpec((1, QK_ROPE, tm), lambda bi, ti: (bi, 0, ti))),
        rows(MLA_WIDTH, _BF16),
        rows(D_MODEL, _BF16),
        rows(D_MODEL, _BF16),
        (jax.ShapeDtypeStruct((b, CONV_K - 1, CONV_WIDTH), _F32),
         pl.BlockSpec((1, CONV_K - 1, CONV_WIDTH), lambda bi, ti: (bi, 0, 0))),
    ]
    table_spec = pl.BlockSpec((HALF_ROPE, tm), lambda bi, ti: (0, ti))
    weights = (w["pre_g"], w["w_q"], w["w_kv"], w["w_b"], w["q_g"], w["w_uqn"], w["w_uqp_t"], w["w_uk_t"],
               w["kv_g"], w["conv_w"], w["w_oc"])
    return pl.pallas_call(
        functools.partial(_proj_prompt_kernel, tq=tq),
        grid=(b, t // tm),
        in_specs=[pl.BlockSpec((1, tm, D_MODEL), lambda bi, ti: (bi, ti, 0)),
                  table_spec, table_spec,
                  pl.BlockSpec((1, CONV_K - 1, CONV_WIDTH), lambda bi, ti: (bi, 0, 0))]
                 + [full(a) for a in weights],
        out_specs=[o[1] for o in outs],
        out_shape=[o[0] for o in outs],
        scratch_shapes=[pltpu.VMEM((1, tm + 8, CONV_WIDTH), _F32),
                        pltpu.VMEM((8, CONV_WIDTH), _F32)],
        compiler_params=pltpu.CompilerParams(
            dimension_semantics=("arbitrary", "arbitrary"), vmem_limit_bytes=VMEM_LIMIT_BYTES),
        name="proj_prompt",
    )(x, cos_t, sin_t, state, *weights)


def _proj_sample_kernel(x_ref, cos_ref, sn1_ref, sn2_ref, st_ref, pre_g_ref, w_q_ref, w_kv_ref, w_b_ref, q_g_ref,
                        w_uq_ref, w_uk_ref, kv_g_ref, conv_w_ref, w_oc_ref,
                        q_ref, kc_ref, ckv_ref, kpe_ref, ga_ref, sa_ref, mb_ref, cst_ref,
                        ub_ref, carry_ref, *, n_seq):
    h = _rms(x_ref[0], pre_g_ref[...]).astype(_BF16)
    cos = cos_ref[...]
    sn1 = sn1_ref[...]
    sn2 = sn2_ref[...]

    def rope(v):
        return (v * cos + pltpu.roll(v, HALF_ROPE, 1) * sn1
                + pltpu.roll(v, LANES - HALF_ROPE, 1) * sn2)

    qn = _rms(_dot(h, w_q_ref[...]), q_g_ref[...]).astype(_BF16)
    q = _dot(qn, w_uq_ref[...])
    q_abs = _dot(q[:, :N_HEADS * QK_NOPE].astype(_BF16), w_uk_ref[...])
    lane_group = lax.broadcasted_iota(jnp.int32, (1, LANES), 1) // QK_ROPE
    n_rope_blocks = N_HEADS // HEADS_PER_ROPE_BLOCK
    q_rot = [rope(q[:, N_HEADS * QK_NOPE + j * LANES:N_HEADS * QK_NOPE + (j + 1) * LANES]) * Q_SCALE
             for j in range(n_rope_blocks)]
    for hd in range(N_HEADS):
        q_ref[0, hd, :, 0:LANES] = (q_abs[:, hd * LANES:(hd + 1) * LANES] * Q_SCALE).astype(_BF16)
        q_ref[0, hd, :, LANES:KEY_WIDTH_S] = jnp.where(
            lane_group == hd % HEADS_PER_ROPE_BLOCK, q_rot[hd // HEADS_PER_ROPE_BLOCK], 0.0).astype(_BF16)

    kvr = _dot(h, w_kv_ref[...])
    ckv = _rms(kvr[:, :KV_LORA], kv_g_ref[...])
    k_rot = rope(kvr[:, KV_LORA:])
    ckv_ref[0] = ckv
    kpe_ref[0] = k_rot[:, :QK_ROPE]
    kc_ref[0, :, 0:KV_LORA] = ckv.astype(_BF16)
    kc_ref[0, :, KV_LORA:KEY_WIDTH_S] = k_rot.astype(_BF16)

    _gates_and_conv(h, w_b_ref, st_ref, conv_w_ref, w_oc_ref, ga_ref, sa_ref, mb_ref, cst_ref,
                    ub_ref, carry_ref, n_seq=n_seq, carry_state=False)


def _project_sample(x, cos, sn1, sn2, state, w, *, n_seq):
    _, tm, _ = x.shape

    def rows(width, dtype):
        return (jax.ShapeDtypeStruct((1, tm, width), dtype), pl.BlockSpec((1, tm, width), lambda i: (0, 0, 0)))

    def full(a):
        return pl.BlockSpec(a.shape, lambda i: (0,) * a.ndim)

    outs = [
        (jax.ShapeDtypeStruct((1, N_HEADS, tm, KEY_WIDTH_S), _BF16),
         pl.BlockSpec((1, N_HEADS, tm, KEY_WIDTH_S), lambda i: (0, 0, 0, 0))),
        rows(KEY_WIDTH_S, _BF16), rows(KV_LORA, _F32), rows(QK_ROPE, _F32),
        rows(MLA_WIDTH, _BF16), rows(D_MODEL, _BF16), rows(D_MODEL, _BF16),
        (jax.ShapeDtypeStruct((n_seq, CONV_K - 1, CONV_WIDTH), _F32),
         pl.BlockSpec((n_seq, CONV_K - 1, CONV_WIDTH), lambda i: (0, 0, 0))),
    ]
    weights = (w["pre_g"], w["w_q"], w["w_kv_s"], w["w_b"], w["q_g"], w["w_uq"], w["w_uk"],
               w["kv_g"], w["conv_w"], w["w_oc"])
    return pl.pallas_call(
        functools.partial(_proj_sample_kernel, n_seq=n_seq),
        grid=(1,),
        in_specs=[full(x), full(cos), full(sn1), full(sn2), full(state)] + [full(a) for a in weights],
        out_specs=[o[1] for o in outs],
        out_shape=[o[0] for o in outs],
        scratch_shapes=[pltpu.VMEM((n_seq, tm // n_seq + 8, CONV_WIDTH), _F32),
                        pltpu.VMEM((8, CONV_WIDTH), _F32)],
        compiler_params=pltpu.CompilerParams(
            dimension_semantics=("arbitrary",), vmem_limit_bytes=VMEM_LIMIT_BYTES),
        name="proj_sample",
    )(x, cos, sn1, sn2, state, *weights)


def _epilogue(o_lat, ga, sa, mb, x, w_uv_ref, w_om_ref, w_out_ref, post_g_ref):
    o = _dot(o_lat.astype(_BF16), w_uv_ref[...])
    branch_a = _dot((o * ga.astype(_F32)).astype(_BF16), w_om_ref[...])
    merged = sa.astype(_F32) * branch_a + mb.astype(_F32)
    z = _dot(merged.astype(_BF16), w_out_ref[...])
    return x + _rms(z, post_g_ref[...])


def _col_reduce(v, op):
    n_keys, cols = v.shape
    slabs = max(n_keys // REDUCE_SLAB, 1)
    part = op(v.reshape(slabs, n_keys // slabs, cols), axis=0)
    return op(part, axis=0, keepdims=True)


def _attn_prompt_kernel(qt_ref, qn_ref, kc_ref, vt_ref, bias_ref, ga_ref, sa_ref, mb_ref, x_ref,
                        w_uv_ref, w_om_ref, w_out_ref, post_g_ref, y_ref,
                        m_ref, l_ref, acc_ref, s_ref, mc_ref, *, tq, tk):
    i = pl.program_id(1)
    units = tk // MXU_DIM
    q_tiles_per_k_tile = tk // tq
    edge = i // q_tiles_per_k_tile
    groups = tq // COL_BLOCK
    n_cb = N_HEADS * groups

    m_ref[...] = jnp.full(m_ref.shape, NEG_INF, _F32)
    l_ref[...] = jnp.zeros(l_ref.shape, _F32)
    acc_ref[...] = jnp.zeros(acc_ref.shape, _F32)

    def keys_of(tile):
        return kc_ref[0, pl.ds(pl.multiple_of(tile * tk, tk), tk), :]

    def values_of(tile):
        return [vt_ref[0, tile * units + u] for u in range(units)]

    def stage1(k, cb, q_tile=None, q_ref=qt_ref):
        s = _dot(k, q_ref[0, 0, cb])
        if q_tile is not None:
            s = s + bias_ref[(q_tile % q_tiles_per_k_tile) * groups + cb % groups]
        s_ref[cb] = s
        mc_ref[cb] = _col_reduce(s, jnp.max)

    def stage2(vts, cb):
        m_prev = m_ref[cb]
        m_new = jnp.maximum(m_prev, mc_ref[cb])
        alpha = jnp.exp2(m_prev - m_new)
        p = jnp.exp2(s_ref[cb] - m_new)
        l_ref[cb] = alpha * l_ref[cb] + _col_reduce(p, jnp.sum)
        pb = p.astype(_BF16)
        pv = _dot(vts[0], pb[0:MXU_DIM])
        for u in range(1, units):
            pv = pv + _dot(vts[u], pb[u * MXU_DIM:(u + 1) * MXU_DIM])
        acc_ref[cb] = alpha * acc_ref[cb] + pv
        m_ref[cb] = m_new

    @pl.when(i == 0)
    def _():
        k_edge = keys_of(edge)
        for cb in range(n_cb):
            stage1(k_edge, cb, q_tile=i)

    def body(j, carry):
        vts = values_of(jnp.where(j == 0, edge, j - 1))
        k = keys_of(j)
        for cb in range(n_cb):
            stage2(vts, cb)
            stage1(k, cb)
        return carry

    lax.fori_loop(0, edge, body, 0)

    last_vts = values_of(jnp.where(edge == 0, edge, edge - 1))
    nxt = jnp.minimum(i + 1, pl.num_programs(1) - 1)
    k_next = keys_of(nxt // q_tiles_per_k_tile)
    for cb in range(n_cb):
        stage2(last_vts, cb)
        stage1(k_next, cb, q_tile=nxt, q_ref=qn_ref)

    def head_rows(hd):
        parts = [(acc_ref[hd * groups + g] / l_ref[hd * groups + g]).T for g in range(groups)]
        return parts[0] if groups == 1 else jnp.concatenate(parts, axis=0)

    o_lat = jnp.concatenate([head_rows(hd) for hd in range(N_HEADS)], axis=1)
    y_ref[0] = _epilogue(o_lat, ga_ref[0], sa_ref[0], mb_ref[0], x_ref[0],
                         w_uv_ref, w_om_ref, w_out_ref, post_g_ref)


def _attend_prompt(qt, kc, vt, ga, sa, mb, x, w, *, tq, tk):
    b, t, _ = x.shape
    n_q = t // tq

    def row_spec(width):
        return pl.BlockSpec((1, tq, width), lambda bi, qi: (bi, qi, 0))

    def full(a):
        return pl.BlockSpec(a.shape, lambda bi, qi: (0,) * a.ndim)

    k_chunk = np.arange(tk)[None, :, None] // CHUNK
    q_chunk = (np.arange(tk // COL_BLOCK)[:, None, None] * COL_BLOCK
               + np.arange(COL_BLOCK)[None, None, :]) // CHUNK
    bias = jnp.asarray(np.where(k_chunk <= q_chunk, 0.0, NEG_INF), _F32)
    n_cb = N_HEADS * tq // COL_BLOCK

    weights = (w["w_uv"], w["w_om"], w["w_out"], w["post_g"])
    return pl.pallas_call(
        functools.partial(_attn_prompt_kernel, tq=tq, tk=tk),
        grid=(b, t // tq),
        in_specs=[pl.BlockSpec((1, 1, n_cb, KEY_DIM, COL_BLOCK), lambda bi, qi: (bi, qi, 0, 0, 0)),
                  pl.BlockSpec((1, 1, n_cb, KEY_DIM, COL_BLOCK),
                               lambda bi, qi: (bi, jnp.minimum(qi + 1, n_q - 1), 0, 0, 0)),
                  pl.BlockSpec((1, t, KEY_DIM), lambda bi, qi: (bi, 0, 0)),
                  pl.BlockSpec((1, t // MXU_DIM, KV_LORA, MXU_DIM), lambda bi, qi: (bi, 0, 0, 0)),
                  full(bias),
                  row_spec(MLA_WIDTH), row_spec(D_MODEL), row_spec(D_MODEL), row_spec(D_MODEL)]
                 + [full(a) for a in weights],
        out_specs=row_spec(D_MODEL),
        out_shape=jax.ShapeDtypeStruct((b, t, D_MODEL), _F32),
        scratch_shapes=[pltpu.VMEM((n_cb, 1, COL_BLOCK), _F32), pltpu.VMEM((n_cb, 1, COL_BLOCK), _F32),
                        pltpu.VMEM((n_cb, KV_LORA, COL_BLOCK), _F32),
                        pltpu.VMEM((n_cb, tk, COL_BLOCK), _F32), pltpu.VMEM((n_cb, 1, COL_BLOCK), _F32)],
        compiler_params=pltpu.CompilerParams(
            dimension_semantics=("arbitrary", "arbitrary"), vmem_limit_bytes=VMEM_LIMIT_BYTES),
        name="attn_prompt",
    )(qt, qt, kc, vt, bias, ga, sa, mb, x, *weights)


def _attn_first(q, k, m_ref, l_ref, acc_ref, visible=None):
    s = _dot_nt(q, k)
    if visible is not None:
        s = jnp.where(visible, s, NEG_INF)
    m = jnp.max(s, axis=1, keepdims=True)
    p = jnp.exp2(s - m)
    m_ref[...] = jnp.broadcast_to(m, m_ref.shape)
    l_ref[...] = jnp.broadcast_to(jnp.sum(p, axis=1, keepdims=True), l_ref.shape)
    acc_ref[...] = _dot(p.astype(_BF16), k[:, :KV_LORA])


def _attn_update(q, k, m_ref, l_ref, acc_ref):
    s = _dot_nt(q, k)
    m_prev = m_ref[...]
    m_new = jnp.maximum(m_prev, jnp.max(s, axis=1, keepdims=True))
    alpha = jnp.exp2(m_prev - m_new)
    p = jnp.exp2(s - jnp.concatenate([m_new] * (s.shape[1] // LANES), axis=1))
    l_ref[...] = alpha * l_ref[...] + jnp.sum(p, axis=1, keepdims=True)
    acc_ref[...] = alpha * acc_ref[...] + _dot(p.astype(_BF16), k[:, :KV_LORA])
    m_ref[...] = m_new


def _attn_sample_kernel(q_ref, kpast_ref, knew_ref, ga_ref, sa_ref, mb_ref, x_ref,
                        w_uv_ref, w_om_ref, w_out_ref, post_g_ref, y_ref,
                        m_ref, l_ref, acc_ref, o_ref, *, new_visible):
    bi = pl.program_id(0)
    t_new = knew_ref.shape[1]
    rows = N_HEADS * t_new
    q = q_ref[0].reshape(rows, KEY_WIDTH_S)
    if new_visible is None:
        visible = None
    else:
        q_pos = lax.broadcasted_iota(jnp.int32, (rows, t_new), 0) % t_new
        k_pos = lax.broadcasted_iota(jnp.int32, (rows, t_new), 1)
        visible = (k_pos + new_visible[0]) // CHUNK <= (q_pos + new_visible[0]) // CHUNK
    _attn_first(q, knew_ref[0], m_ref, l_ref, acc_ref, visible=visible)
    _attn_update(q, kpast_ref[0], m_ref, l_ref, acc_ref)
    o = acc_ref[...] / l_ref[...]
    r0 = pl.multiple_of(bi * t_new, t_new)
    o_ref[pl.ds(r0, t_new), :] = jnp.concatenate(
        [o[hd * t_new:(hd + 1) * t_new] for hd in range(N_HEADS)], axis=1)

    @pl.when(bi == pl.num_programs(0) - 1)
    def _():
        y_ref[...] = _epilogue(o_ref[...], ga_ref[...], sa_ref[...], mb_ref[...], x_ref[...],
                               w_uv_ref, w_om_ref, w_out_ref, post_g_ref)


def _attend_sample(q, k_past, k_new, ga, sa, mb, x, w, *, past_len):
    nb, t_new, _ = k_new.shape
    n_rows = nb * t_new
    rows = N_HEADS * t_new
    last_q, first_q = past_len + t_new - 1, past_len
    new_visible = None if last_q // CHUNK == first_q // CHUNK else (past_len,)

    def full(a):
        return pl.BlockSpec(a.shape, lambda bi: (0,) * a.ndim)

    weights = (w["w_uv"], w["w_om"], w["w_out"], w["post_g"])
    return pl.pallas_call(
        functools.partial(_attn_sample_kernel, new_visible=new_visible),
        grid=(nb,),
        in_specs=[pl.BlockSpec((1, N_HEADS, t_new, KEY_WIDTH_S), lambda bi: (0, 0, bi, 0)),
                  pl.BlockSpec((1, past_len, KEY_WIDTH_S), lambda bi: (bi, 0, 0)),
                  pl.BlockSpec((1, t_new, KEY_WIDTH_S), lambda bi: (bi, 0, 0)),
                  full(ga), full(sa), full(mb), full(x)] + [full(a) for a in weights],
        out_specs=pl.BlockSpec((n_rows, D_MODEL), lambda bi: (0, 0)),
        out_shape=jax.ShapeDtypeStruct((n_rows, D_MODEL), _F32),
        scratch_shapes=[pltpu.VMEM((rows, LANES), _F32), pltpu.VMEM((rows, LANES), _F32),
                        pltpu.VMEM((rows, KV_LORA), _F32), pltpu.VMEM((n_rows, N_HEADS * KV_LORA), _F32)],
        compiler_params=pltpu.CompilerParams(
            dimension_semantics=("arbitrary",), vmem_limit_bytes=VMEM_LIMIT_BYTES),
        name="attn_sample",
    )(q, k_past, k_new, ga, sa, mb, x, *weights)


def _block_diag(blocks):
    n = len(blocks)
    r, c = blocks[0].shape
    rows = []
    for j, blk in enumerate(blocks):
        rows.append(jnp.pad(blk, ((0, 0), (j * c, (n - 1 - j) * c))))
    return jnp.concatenate(rows, axis=0)


def _prep_weights(pre_norm, w_in, q_norm, w_uq, kv_norm, w_uk, w_uv, w_o_mla, conv_w, w_o_conv, w_out, post_norm):
    o_kv = Q_LORA
    o_kr = o_kv + KV_LORA
    o_tail = o_kr + QK_ROPE
    assert w_in.shape[1] == o_tail + _TAIL_COLS
    w_kv_lat = w_in[:, o_kv:o_kr].astype(_BF16)
    w_kr = w_in[:, o_kr:o_tail].astype(_BF16)
    wq = w_uq.reshape(Q_LORA, N_HEADS, QK_NOPE + QK_ROPE)
    w_uqn = wq[:, :, :QK_NOPE].reshape(Q_LORA, N_HEADS * QK_NOPE).astype(_BF16)
    w_uqp = wq[:, :, QK_NOPE:].reshape(Q_LORA, N_HEADS * QK_ROPE).astype(_BF16)
    w_uk_bd = _block_diag([w_uk[:, hd, :].T.astype(_BF16) for hd in range(N_HEADS)])
    return {
        "pre_g": pre_norm.reshape(1, D_MODEL),
        "w_q": w_in[:, :o_kv].astype(_BF16),
        "w_kv": jnp.concatenate([w_kv_lat, w_kr, jnp.zeros((D_MODEL, LANES - QK_ROPE), _BF16)], axis=1),
        "w_kv_s": jnp.concatenate([w_kv_lat] + [w_kr] * HEADS_PER_ROPE_BLOCK, axis=1),
        "w_b": w_in[:, o_tail:].astype(_BF16),
        "q_g": q_norm.reshape(1, Q_LORA),
        "w_uqn": w_uqn,
        "w_uqp_t": w_uqp.T,
        "w_uq": jnp.concatenate([w_uqn, w_uqp], axis=1),
        "w_uk": w_uk_bd,
        "w_uk_t": w_uk_bd.T,
        "kv_g": kv_norm.reshape(1, KV_LORA),
        "conv_w": conv_w,
        "w_oc": w_o_conv.astype(_BF16),
        "w_uv": _block_diag([w_uv[:, hd, :].astype(_BF16) for hd in range(N_HEADS)]),
        "w_om": w_o_mla.astype(_BF16),
        "w_out": w_out.astype(_BF16),
        "post_g": post_norm.reshape(1, D_MODEL),
    }


def _rope_angles(pos):
    inv = ROPE_BASE ** (-jnp.arange(HALF_ROPE, dtype=_F32) / HALF_ROPE)
    ang = pos.astype(_F32)[:, None] * inv[None, :]
    return jnp.cos(ang), jnp.sin(ang)


def _rope_tables_rows(pos):
    cos, sin = _rope_angles(pos)
    zero = jnp.zeros_like(sin)
    reps = LANES // QK_ROPE
    return (jnp.tile(jnp.concatenate([cos, cos], axis=1), (1, reps)),
            jnp.tile(jnp.concatenate([zero, sin], axis=1), (1, reps)),
            jnp.tile(jnp.concatenate([-sin, zero], axis=1), (1, reps)))


def _cached_keys(ckv, kpe):
    return jnp.concatenate([ckv] + [kpe] * HEADS_PER_ROPE_BLOCK, axis=-1).astype(_BF16)


PROMPT_ROW_TILE = 512
PROMPT_Q_TILE = 512
PROMPT_K_TILE = 512


def kernel(x_prompt, x_sample, cache_kv_latent, cache_k_rope, state_conv, pre_norm, w_in, q_norm, w_uq, kv_norm,
           w_uk, w_uv, w_o_mla, conv_w, w_o_conv, w_out, post_norm):
    depth = pre_norm.shape[0]
    assert depth == 1
    b, t, _ = x_prompt.shape
    nb, t_new, _ = x_sample.shape
    past_len = cache_kv_latent.shape[2]
    lyr = 0
    w = _prep_weights(pre_norm[lyr], w_in[lyr], q_norm[lyr], w_uq[lyr], kv_norm[lyr], w_uk[lyr], w_uv[lyr],
                      w_o_mla[lyr], conv_w[lyr], w_o_conv[lyr], w_out[lyr], post_norm[lyr])

    cos, sin = _rope_angles(jnp.arange(t, dtype=jnp.int32))
    zero_state = jnp.zeros((b, CONV_K - 1, CONV_WIDTH), _F32)
    qt, kc, vt, ckv_p, kpe_t, ga, sa, mb, cv_p = _project_prompt(
        x_prompt, cos.T, sin.T, zero_state, w, tm=PROMPT_ROW_TILE, tq=PROMPT_Q_TILE)
    y_p = _attend_prompt(qt, kc, vt, ga, sa, mb, x_prompt, w, tq=PROMPT_Q_TILE, tk=PROMPT_K_TILE)

    n_rows = nb * t_new
    pos_s = past_len + jnp.arange(t_new, dtype=jnp.int32)
    tabs_s = tuple(jnp.tile(tb, (nb, 1)) for tb in _rope_tables_rows(pos_s))
    xs = x_sample.reshape(1, n_rows, D_MODEL)
    q_s, kc_s, ckv_s, kpe_s, ga_s, sa_s, mb_s, cv_s = _project_sample(
        xs, *tabs_s, state_conv[lyr], w, n_seq=nb)
    k_past = _cached_keys(cache_kv_latent[lyr], cache_k_rope[lyr])
    y_s = _attend_sample(q_s, k_past, kc_s.reshape(nb, t_new, KEY_WIDTH_S),
                         ga_s[0], sa_s[0], mb_s[0], xs[0], w, past_len=past_len)

    return (y_p, y_s.reshape(nb, t_new, D_MODEL),
            ckv_p[None], jnp.swapaxes(kpe_t, 1, 2)[None], cv_p[None],
            ckv_s.reshape(1, nb, t_new, KV_LORA), kpe_s.reshape(1, nb, t_new, QK_ROPE), cv_s[None])
```

```python
import functools

import numpy as np
import jax
import jax.numpy as jnp
from jax import lax
from jax.experimental import pallas as pl
from jax.experimental.pallas import tpu as pltpu

N_HEADS = 8
QK_NOPE = 64
QK_ROPE = 32
V_HEAD = 64
Q_LORA = 256
KV_LORA = 128
MLA_WIDTH = N_HEADS * V_HEAD
CONV_WIDTH = 512
CONV_K = 3
D_MODEL = 1024
CHUNK = 64
ROPE_BASE = 10000.0
EPS = 1e-6
SM_SCALE = (QK_NOPE + QK_ROPE) ** -0.5
NEG_INF = -1e30
LOG2E = 1.4426950408889634

LANES = 128
MXU_DIM = 256
COL_BLOCK = MXU_DIM
REDUCE_SLAB = 64
HALF_ROPE = QK_ROPE // 2
KEY_DIM = KV_LORA + QK_ROPE
HEADS_PER_ROPE_BLOCK = LANES // QK_ROPE
Q_SCALE = SM_SCALE * LOG2E
VMEM_LIMIT_BYTES = 56 * 1024 * 1024

_O_GM = 0
_O_CB = _O_GM + MLA_WIDTH
_O_CC = _O_CB + CONV_WIDTH
_O_CX = _O_CC + CONV_WIDTH
_O_GC = _O_CX + CONV_WIDTH
_O_MM = _O_GC + CONV_WIDTH
_O_MC = _O_MM + D_MODEL
_TAIL_COLS = _O_MC + D_MODEL

_F32 = jnp.float32
_BF16 = jnp.bfloat16
_NT = (((1,), (1,)), ((), ()))


def _rms(v, g):
    return v * lax.rsqrt(jnp.mean(v * v, axis=-1, keepdims=True) + EPS) * g


def _silu(v):
    return v * jax.nn.sigmoid(v)


def _dot(a, b):
    return jnp.dot(a, b, preferred_element_type=_F32)


def _dot_nt(a, b):
    return lax.dot_general(a, b, _NT, preferred_element_type=_F32)


def _gates_and_conv(h, w_b_ref, st_ref, conv_w_ref, w_oc_ref, ga_ref, sa_ref, mb_ref, cst_ref,
                    ub_ref, carry_ref, *, n_seq, carry_state):
    seq_len = h.shape[0] // n_seq

    def proj(off, n):
        return _dot(h, w_b_ref[:, off:off + n])

    ga_ref[0] = _silu(proj(_O_GM, MLA_WIDTH)).astype(_BF16)
    sa_ref[0] = jax.nn.sigmoid(proj(_O_MM, D_MODEL)).astype(_BF16)

    u = proj(_O_CC, CONV_WIDTH) * proj(_O_CX, CONV_WIDTH)
    w0 = conv_w_ref[0:1, :]
    w1 = conv_w_ref[1:2, :]
    w2 = conv_w_ref[2:3, :]
    if carry_state:
        @pl.when(pl.program_id(1) == 0)
        def _():
            carry_ref[0:CONV_K - 1, :] = st_ref[0]
    convs = []
    for j in range(n_seq):
        u_j = u[j * seq_len:(j + 1) * seq_len]
        prev = carry_ref[0:CONV_K - 1, :] if carry_state else st_ref[j]
        ub_ref[j, 8 - (CONV_K - 1):8, :] = prev
        ub_ref[j, 8:8 + seq_len, :] = u_j
        convs.append(w0 * ub_ref[j, 6:6 + seq_len, :] + w1 * ub_ref[j, 7:7 + seq_len, :] + w2 * u_j)
        new_state = u_j[seq_len - (CONV_K - 1):, :]
        cst_ref[j] = new_state
        if carry_state:
            carry_ref[0:CONV_K - 1, :] = new_state
    conv = convs[0] if n_seq == 1 else jnp.concatenate(convs, axis=0)
    bb = (proj(_O_CB, CONV_WIDTH) * conv * _silu(proj(_O_GC, CONV_WIDTH))).astype(_BF16)
    branch_b = _dot(bb, w_oc_ref[...])
    mb_ref[0] = (jax.nn.sigmoid(proj(_O_MC, D_MODEL)) * branch_b).astype(_BF16)


def _proj_prompt_kernel(x_ref, cos_ref, sin_ref, st_ref, pre_g_ref, w_q_ref, w_kv_ref, w_b_ref, q_g_ref,
                        w_uqn_ref, w_uqp_t_ref, w_uk_t_ref, kv_g_ref, conv_w_ref, w_oc_ref,
                        qt_ref, kc_ref, vt_ref, ckv_ref, kpet_ref, ga_ref, sa_ref, mb_ref, cst_ref,
                        ub_ref, carry_ref, *, tq):
    tm = x_ref.shape[1]
    h = _rms(x_ref[0], pre_g_ref[...]).astype(_BF16)
    cos_t = cos_ref[...]
    sin_t = sin_ref[...]

    qn = _rms(_dot(h, w_q_ref[...]), q_g_ref[...]).astype(_BF16)
    q_nope = _dot(qn, w_uqn_ref[...]).astype(_BF16)
    q_abs_t = _dot_nt(w_uk_t_ref[...], q_nope)
    q_pe_t = _dot_nt(w_uqp_t_ref[...], qn).reshape(N_HEADS, QK_ROPE, tm)
    x1 = q_pe_t[:, :HALF_ROPE, :]
    x2 = q_pe_t[:, HALF_ROPE:, :]
    r1 = (x1 * cos_t - x2 * sin_t) * Q_SCALE
    r2 = (x2 * cos_t + x1 * sin_t) * Q_SCALE
    groups = tq // COL_BLOCK
    for j in range(tm // tq):
        for hd in range(N_HEADS):
            for g in range(groups):
                tok = slice(j * tq + g * COL_BLOCK, j * tq + (g + 1) * COL_BLOCK)
                cb = hd * groups + g
                qt_ref[0, j, cb, 0:KV_LORA, :] = (
                    q_abs_t[hd * KV_LORA:(hd + 1) * KV_LORA, tok] * Q_SCALE).astype(_BF16)
                qt_ref[0, j, cb, KV_LORA:KV_LORA + HALF_ROPE, :] = r1[hd][:, tok].astype(_BF16)
                qt_ref[0, j, cb, KV_LORA + HALF_ROPE:KEY_DIM, :] = r2[hd][:, tok].astype(_BF16)

    kvr = _dot(h, w_kv_ref[...])
    ckv = _rms(kvr[:, :KV_LORA], kv_g_ref[...])
    ckv_ref[0] = ckv
    ckv_t = ckv.T.astype(_BF16)
    for u in range(tm // MXU_DIM):
        vt_ref[0, u] = ckv_t[:, u * MXU_DIM:(u + 1) * MXU_DIM]
    kr_t = kvr[:, KV_LORA:].T
    k1 = kr_t[0:HALF_ROPE]
    k2 = kr_t[HALF_ROPE:QK_ROPE]
    kpe_t = jnp.concatenate([k1 * cos_t - k2 * sin_t, k2 * cos_t + k1 * sin_t], axis=0)
    kpet_ref[0] = kpe_t
    kpe = jnp.concatenate([kpe_t, jnp.zeros((LANES - QK_ROPE, tm), _F32)], axis=0).T
    kc_ref[0, :, 0:KV_LORA] = ckv.astype(_BF16)
    kc_ref[0, :, KV_LORA:KEY_DIM] = kpe[:, :QK_ROPE].astype(_BF16)

    _gates_and_conv(h, w_b_ref, st_ref, conv_w_ref, w_oc_ref, ga_ref, sa_ref, mb_ref, cst_ref,
                    ub_ref, carry_ref, n_seq=1, carry_state=True)


def _project_prompt(x, cos_t, sin_t, state, w, *, tm, tq):
    b, t, _ = x.shape

    def rows(width, dtype):
        return (jax.ShapeDtypeStruct((b, t, width), dtype),
                pl.BlockSpec((1, tm, width), lambda bi, ti: (bi, ti, 0)))

    def full(a):
        return pl.BlockSpec(a.shape, lambda bi, ti: (0,) * a.ndim)

    outs = [
        (jax.ShapeDtypeStruct((b, t // tq, N_HEADS * tq // COL_BLOCK, KEY_DIM, COL_BLOCK), _BF16),
         pl.BlockSpec((1, tm // tq, N_HEADS * tq // COL_BLOCK, KEY_DIM, COL_BLOCK),
                      lambda bi, ti: (bi, ti, 0, 0, 0))),
        rows(KEY_DIM, _BF16),
        (jax.ShapeDtypeStruct((b, t // MXU_DIM, KV_LORA, MXU_DIM), _BF16),
         pl.BlockSpec((1, tm // MXU_DIM, KV_LORA, MXU_DIM), lambda bi, ti: (bi, ti, 0, 0))),
        rows(KV_LORA, _F32),
        (jax.ShapeDtypeStruct((b, QK_ROPE, t), _F32),
         pl.BlockSpec((1, QK_ROPE, tm), lambda bi, ti: (bi, 0, ti))),
        rows(MLA_WIDTH, _BF16),
        rows(D_MODEL, _BF16),
        rows(D_MODEL, _BF16),
        (jax.ShapeDtypeStruct((b, CONV_K - 1, CONV_WIDTH), _F32),
         pl.BlockSpec((1, CONV_K - 1, CONV_WIDTH), lambda bi, ti: (bi, 0, 0))),
    ]
    table_spec = pl.BlockSpec((HALF_ROPE, tm), lambda bi, ti: (0, ti))
    weights = (w["pre_g"], w["w_q"], w["w_kv"], w["w_b"], w["q_g"], w["w_uqn"], w["w_uqp_t"], w["w_uk_t"],
               w["kv_g"], w["conv_w"], w["w_oc"])
    return pl.pallas_call(
        functools.partial(_proj_prompt_kernel, tq=tq),
        grid=(b, t // tm),
        in_specs=[pl.BlockSpec((1, tm, D_MODEL), lambda bi, ti: (bi, ti, 0)),
                  table_spec, table_spec,
                  pl.BlockSpec((1, CONV_K - 1, CONV_WIDTH), lambda bi, ti: (bi, 0, 0))]
                 + [full(a) for a in weights],
        out_specs=[o[1] for o in outs],
        out_shape=[o[0] for o in outs],
        scratch_shapes=[pltpu.VMEM((1, tm + 8, CONV_WIDTH), _F32),
                        pltpu.VMEM((8, CONV_WIDTH), _F32)],
        compiler_params=pltpu.CompilerParams(
            dimension_semantics=("arbitrary", "arbitrary"), vmem_limit_bytes=VMEM_LIMIT_BYTES),
        name="proj_prompt",
    )(x, cos_t, sin_t, state, *weights)


def _proj_sample_kernel(x_ref, cos_ref, sn1_ref, sn2_ref, st_ref, pre_g_ref, w_q_ref, w_kv_ref, w_b_ref, q_g_ref,
                        w_uq_ref, w_uk_ref, kv_g_ref, conv_w_ref, w_oc_ref,
                        q_ref, kc_ref, ckv_ref, kpe_ref, ga_ref, sa_ref, mb_ref, cst_ref,
                        ub_ref, carry_ref, *, n_seq):
    h = _rms(x_ref[0], pre_g_ref[...]).astype(_BF16)
    cos = cos_ref[...]
    sn1 = sn1_ref[...]
    sn2 = sn2_ref[...]

    def rope(v):
        return (v * cos + pltpu.roll(v, HALF_ROPE, 1) * sn1
                + pltpu.roll(v, LANES - HALF_ROPE, 1) * sn2)

    qn = _rms(_dot(h, w_q_ref[...]), q_g_ref[...]).astype(_BF16)
    q = _dot(qn, w_uq_ref[...])
    q_abs = _dot(q[:, :N_HEADS * QK_NOPE].astype(_BF16), w_uk_ref[...])
    n_rope_blocks = N_HEADS // HEADS_PER_ROPE_BLOCK
    q_rot = [rope(q[:, N_HEADS * QK_NOPE + j * LANES:N_HEADS * QK_NOPE + (j + 1) * LANES]) * Q_SCALE
             for j in range(n_rope_blocks)]
    for hd in range(N_HEADS):
        q_ref[0, hd, :, 0:KV_LORA] = (q_abs[:, hd * LANES:(hd + 1) * LANES] * Q_SCALE).astype(_BF16)
        grp = hd % HEADS_PER_ROPE_BLOCK
        blk = q_rot[hd // HEADS_PER_ROPE_BLOCK]
        if grp:
            blk = pltpu.roll(blk, LANES - grp * QK_ROPE, 1)
        q_ref[0, hd, :, KV_LORA:KEY_DIM] = blk[:, :QK_ROPE].astype(_BF16)

    kvr = _dot(h, w_kv_ref[...])
    ckv = _rms(kvr[:, :KV_LORA], kv_g_ref[...])
    k_rot = rope(kvr[:, KV_LORA:])
    ckv_ref[0] = ckv
    kpe_ref[0] = k_rot[:, :QK_ROPE]
    kc_ref[0, :, 0:KV_LORA] = ckv.astype(_BF16)
    kc_ref[0, :, KV_LORA:KEY_DIM] = k_rot[:, :QK_ROPE].astype(_BF16)

    _gates_and_conv(h, w_b_ref, st_ref, conv_w_ref, w_oc_ref, ga_ref, sa_ref, mb_ref, cst_ref,
                    ub_ref, carry_ref, n_seq=n_seq, carry_state=False)


def _project_sample(x, cos, sn1, sn2, state, w, *, n_seq):
    _, tm, _ = x.shape

    def rows(width, dtype):
        return (jax.ShapeDtypeStruct((1, tm, width), dtype), pl.BlockSpec((1, tm, width), lambda i: (0, 0, 0)))

    def full(a):
        return pl.BlockSpec(a.shape, lambda i: (0,) * a.ndim)

    outs = [
        (jax.ShapeDtypeStruct((1, N_HEADS, tm, KEY_DIM), _BF16),
         pl.BlockSpec((1, N_HEADS, tm, KEY_DIM), lambda i: (0, 0, 0, 0))),
        rows(KEY_DIM, _BF16), rows(KV_LORA, _F32), rows(QK_ROPE, _F32),
        rows(MLA_WIDTH, _BF16), rows(D_MODEL, _BF16), rows(D_MODEL, _BF16),
        (jax.ShapeDtypeStruct((n_seq, CONV_K - 1, CONV_WIDTH), _F32),
         pl.BlockSpec((n_seq, CONV_K - 1, CONV_WIDTH), lambda i: (0, 0, 0))),
    ]
    weights = (w["pre_g"], w["w_q"], w["w_kv"], w["w_b"], w["q_g"], w["w_uq"], w["w_uk"],
               w["kv_g"], w["conv_w"], w["w_oc"])
    return pl.pallas_call(
        functools.partial(_proj_sample_kernel, n_seq=n_seq),
        grid=(1,),
        in_specs=[full(x), full(cos), full(sn1), full(sn2), full(state)] + [full(a) for a in weights],
        out_specs=[o[1] for o in outs],
        out_shape=[o[0] for o in outs],
        scratch_shapes=[pltpu.VMEM((n_seq, tm // n_seq + 8, CONV_WIDTH), _F32),
                        pltpu.VMEM((8, CONV_WIDTH), _F32)],
        compiler_params=pltpu.CompilerParams(
            dimension_semantics=("arbitrary",), vmem_limit_bytes=VMEM_LIMIT_BYTES),
        name="proj_sample",
    )(x, cos, sn1, sn2, state, *weights)


def _epilogue(o_lat, ga, sa, mb, x, w_uv_ref, w_om_ref, w_out_ref, post_g_ref):
    o = _dot(o_lat.astype(_BF16), w_uv_ref[...])
    branch_a = _dot((o * ga.astype(_F32)).astype(_BF16), w_om_ref[...])
    merged = sa.astype(_F32) * branch_a + mb.astype(_F32)
    z = _dot(merged.astype(_BF16), w_out_ref[...])
    return x + _rms(z, post_g_ref[...])


def _col_reduce(v, op):
    n_keys, cols = v.shape
    slabs = max(n_keys // REDUCE_SLAB, 1)
    part = op(v.reshape(slabs, n_keys // slabs, cols), axis=0)
    return op(part, axis=0, keepdims=True)


def _attn_prompt_kernel(qt_ref, qn_ref, kc_ref, vt_ref, bias_ref, ga_ref, sa_ref, mb_ref, x_ref,
                        w_uv_ref, w_om_ref, w_out_ref, post_g_ref, y_ref,
                        m_ref, l_ref, acc_ref, s_ref, mc_ref, *, tq, tk):
    i = pl.program_id(1)
    units = tk // MXU_DIM
    q_tiles_per_k_tile = tk // tq
    edge = i // q_tiles_per_k_tile
    groups = tq // COL_BLOCK
    n_cb = N_HEADS * groups

    m_ref[...] = jnp.full(m_ref.shape, NEG_INF, _F32)
    l_ref[...] = jnp.zeros(l_ref.shape, _F32)
    acc_ref[...] = jnp.zeros(acc_ref.shape, _F32)

    def keys_of(tile):
        return kc_ref[0, pl.ds(pl.multiple_of(tile * tk, tk), tk), :]

    def values_of(tile):
        return [vt_ref[0, tile * units + u] for u in range(units)]

    def stage1(k, cb, q_tile=None, q_ref=qt_ref):
        s = _dot(k, q_ref[0, 0, cb])
        if q_tile is not None:
            s = s + bias_ref[(q_tile % q_tiles_per_k_tile) * groups + cb % groups]
        s_ref[cb] = s
        mc_ref[cb] = _col_reduce(s, jnp.max)

    def stage2(vts, cb):
        m_prev = m_ref[cb]
        m_new = jnp.maximum(m_prev, mc_ref[cb])
        alpha = jnp.exp2(m_prev - m_new)
        p = jnp.exp2(s_ref[cb] - m_new)
        l_ref[cb] = alpha * l_ref[cb] + _col_reduce(p, jnp.sum)
        pb = p.astype(_BF16)
        pv = _dot(vts[0], pb[0:MXU_DIM])
        for u in range(1, units):
            pv = pv + _dot(vts[u], pb[u * MXU_DIM:(u + 1) * MXU_DIM])
        acc_ref[cb] = alpha * acc_ref[cb] + pv
        m_ref[cb] = m_new

    @pl.when(i == 0)
    def _():
        k_edge = keys_of(edge)
        for cb in range(n_cb):
            stage1(k_edge, cb, q_tile=i)

    def advance(j):
        vts = values_of(jnp.where(j == 0, edge, j - 1))
        k = keys_of(j)
        for cb in range(n_cb):
            stage2(vts, cb)
            stage1(k, cb)

    def single(j, carry):
        advance(j)
        return carry

    def double(pair, carry):
        advance(odd + 2 * pair)
        advance(odd + 2 * pair + 1)
        return carry

    odd = edge % 2
    lax.fori_loop(0, odd, single, 0)
    lax.fori_loop(0, edge // 2, double, 0)

    last_vts = values_of(jnp.where(edge == 0, edge, edge - 1))
    nxt = jnp.minimum(i + 1, pl.num_programs(1) - 1)
    k_next = keys_of(nxt // q_tiles_per_k_tile)
    for cb in range(n_cb):
        stage2(last_vts, cb)
        stage1(k_next, cb, q_tile=nxt, q_ref=qn_ref)

    def head_rows(hd):
        parts = [(acc_ref[hd * groups + g] / l_ref[hd * groups + g]).T for g in range(groups)]
        return parts[0] if groups == 1 else jnp.concatenate(parts, axis=0)

    o_lat = jnp.concatenate([head_rows(hd) for hd in range(N_HEADS)], axis=1)
    y_ref[0] = _epilogue(o_lat, ga_ref[0], sa_ref[0], mb_ref[0], x_ref[0],
                         w_uv_ref, w_om_ref, w_out_ref, post_g_ref)


def _attend_prompt(qt, kc, vt, ga, sa, mb, x, w, *, tq, tk):
    b, t, _ = x.shape
    n_q = t // tq

    def row_spec(width):
        return pl.BlockSpec((1, tq, width), lambda bi, qi: (bi, qi, 0))

    def full(a):
        return pl.BlockSpec(a.shape, lambda bi, qi: (0,) * a.ndim)

    k_chunk = np.arange(tk)[None, :, None] // CHUNK
    q_chunk = (np.arange(tk // COL_BLOCK)[:, None, None] * COL_BLOCK
               + np.arange(COL_BLOCK)[None, None, :]) // CHUNK
    bias = jnp.asarray(np.where(k_chunk <= q_chunk, 0.0, NEG_INF), _F32)
    n_cb = N_HEADS * tq // COL_BLOCK

    weights = (w["w_uv"], w["w_om"], w["w_out"], w["post_g"])
    return pl.pallas_call(
        functools.partial(_attn_prompt_kernel, tq=tq, tk=tk),
        grid=(b, t // tq),
        in_specs=[pl.BlockSpec((1, 1, n_cb, KEY_DIM, COL_BLOCK), lambda bi, qi: (bi, qi, 0, 0, 0)),
                  pl.BlockSpec((1, 1, n_cb, KEY_DIM, COL_BLOCK),
                               lambda bi, qi: (bi, jnp.minimum(qi + 1, n_q - 1), 0, 0, 0)),
                  pl.BlockSpec((1, t, KEY_DIM), lambda bi, qi: (bi, 0, 0)),
                  pl.BlockSpec((1, t // MXU_DIM, KV_LORA, MXU_DIM), lambda bi, qi: (bi, 0, 0, 0)),
                  full(bias),
                  row_spec(MLA_WIDTH), row_spec(D_MODEL), row_spec(D_MODEL), row_spec(D_MODEL)]
                 + [full(a) for a in weights],
        out_specs=row_spec(D_MODEL),
        out_shape=jax.ShapeDtypeStruct((b, t, D_MODEL), _F32),
        scratch_shapes=[pltpu.VMEM((n_cb, 1, COL_BLOCK), _F32), pltpu.VMEM((n_cb, 1, COL_BLOCK), _F32),
                        pltpu.VMEM((n_cb, KV_LORA, COL_BLOCK), _F32),
                        pltpu.VMEM((n_cb, tk, COL_BLOCK), _F32), pltpu.VMEM((n_cb, 1, COL_BLOCK), _F32)],
        compiler_params=pltpu.CompilerParams(
            dimension_semantics=("arbitrary", "arbitrary"), vmem_limit_bytes=VMEM_LIMIT_BYTES),
        name="attn_prompt",
    )(qt, qt, kc, vt, bias, ga, sa, mb, x, *weights)


def _attn_first(q, k, m_ref, l_ref, acc_ref, visible=None):
    s = _dot_nt(q, k)
    if visible is not None:
        s = jnp.where(visible, s, NEG_INF)
    m = jnp.max(s, axis=1, keepdims=True)
    p = jnp.exp2(s - m)
    m_ref[...] = jnp.broadcast_to(m, m_ref.shape)
    l_ref[...] = jnp.broadcast_to(jnp.sum(p, axis=1, keepdims=True), l_ref.shape)
    acc_ref[...] = _dot(p.astype(_BF16), k[:, :KV_LORA])


def _attn_update(q, k_lat, k_rope, m_ref, l_ref, acc_ref):
    s = _dot_nt(q[:, :KV_LORA], k_lat) + _dot_nt(q[:, KV_LORA:], k_rope)
    m_prev = m_ref[...]
    m_new = jnp.maximum(m_prev, jnp.max(s, axis=1, keepdims=True))
    alpha = jnp.exp2(m_prev - m_new)
    p = jnp.exp2(s - jnp.concatenate([m_new] * (s.shape[1] // LANES), axis=1))
    l_ref[...] = alpha * l_ref[...] + jnp.sum(p, axis=1, keepdims=True)
    acc_ref[...] = alpha * acc_ref[...] + _dot(p.astype(_BF16), k_lat)
    m_ref[...] = m_new


def _attn_sample_kernel(q_ref, past_lat_ref, past_rope_ref, knew_ref, ga_ref, sa_ref, mb_ref, x_ref,
                        w_uv_ref, w_om_ref, w_out_ref, post_g_ref, y_ref,
                        m_ref, l_ref, acc_ref, o_ref, *, new_visible):
    bi = pl.program_id(0)
    t_new = knew_ref.shape[1]
    rows = N_HEADS * t_new
    q = q_ref[0].reshape(rows, KEY_DIM)
    if new_visible is None:
        visible = None
    else:
        q_pos = lax.broadcasted_iota(jnp.int32, (rows, t_new), 0) % t_new
        k_pos = lax.broadcasted_iota(jnp.int32, (rows, t_new), 1)
        visible = (k_pos + new_visible[0]) // CHUNK <= (q_pos + new_visible[0]) // CHUNK
    _attn_first(q, knew_ref[0], m_ref, l_ref, acc_ref, visible=visible)
    _attn_update(q, past_lat_ref[0].astype(_BF16), past_rope_ref[0].astype(_BF16), m_ref, l_ref, acc_ref)
    o = acc_ref[...] / l_ref[...]
    r0 = pl.multiple_of(bi * t_new, t_new)
    o_ref[pl.ds(r0, t_new), :] = jnp.concatenate(
        [o[hd * t_new:(hd + 1) * t_new] for hd in range(N_HEADS)], axis=1)

    @pl.when(bi == pl.num_programs(0) - 1)
    def _():
        y_ref[...] = _epilogue(o_ref[...], ga_ref[...], sa_ref[...], mb_ref[...], x_ref[...],
                               w_uv_ref, w_om_ref, w_out_ref, post_g_ref)


def _attend_sample(q, past_lat, past_rope, k_new, ga, sa, mb, x, w):
    nb, t_new, _ = k_new.shape
    past_len = past_lat.shape[1]
    n_rows = nb * t_new
    rows = N_HEADS * t_new
    last_q, first_q = past_len + t_new - 1, past_len
    new_visible = None if last_q // CHUNK == first_q // CHUNK else (past_len,)

    def full(a):
        return pl.BlockSpec(a.shape, lambda bi: (0,) * a.ndim)

    weights = (w["w_uv"], w["w_om"], w["w_out"], w["post_g"])
    return pl.pallas_call(
        functools.partial(_attn_sample_kernel, new_visible=new_visible),
        grid=(nb,),
        in_specs=[pl.BlockSpec((1, N_HEADS, t_new, KEY_DIM), lambda bi: (0, 0, bi, 0)),
                  pl.BlockSpec((1, past_len, KV_LORA), lambda bi: (bi, 0, 0)),
                  pl.BlockSpec((1, past_len, QK_ROPE), lambda bi: (bi, 0, 0)),
                  pl.BlockSpec((1, t_new, KEY_DIM), lambda bi: (bi, 0, 0)),
                  full(ga), full(sa), full(mb), full(x)] + [full(a) for a in weights],
        out_specs=pl.BlockSpec((n_rows, D_MODEL), lambda bi: (0, 0)),
        out_shape=jax.ShapeDtypeStruct((n_rows, D_MODEL), _F32),
        scratch_shapes=[pltpu.VMEM((rows, LANES), _F32), pltpu.VMEM((rows, LANES), _F32),
                        pltpu.VMEM((rows, KV_LORA), _F32), pltpu.VMEM((n_rows, N_HEADS * KV_LORA), _F32)],
        compiler_params=pltpu.CompilerParams(
            dimension_semantics=("arbitrary",), vmem_limit_bytes=VMEM_LIMIT_BYTES),
        name="attn_sample",
    )(q, past_lat, past_rope, k_new, ga, sa, mb, x, *weights)


def _block_diag(blocks):
    n = len(blocks)
    r, c = blocks[0].shape
    rows = []
    for j, blk in enumerate(blocks):
        rows.append(jnp.pad(blk, ((0, 0), (j * c, (n - 1 - j) * c))))
    return jnp.concatenate(rows, axis=0)


def _prep_weights(pre_norm, w_in, q_norm, w_uq, kv_norm, w_uk, w_uv, w_o_mla, conv_w, w_o_conv, w_out, post_norm):
    o_kv = Q_LORA
    o_kr = o_kv + KV_LORA
    o_tail = o_kr + QK_ROPE
    assert w_in.shape[1] == o_tail + _TAIL_COLS
    w_kv_lat = w_in[:, o_kv:o_kr].astype(_BF16)
    w_kr = w_in[:, o_kr:o_tail].astype(_BF16)
    wq = w_uq.reshape(Q_LORA, N_HEADS, QK_NOPE + QK_ROPE)
    w_uqn = wq[:, :, :QK_NOPE].reshape(Q_LORA, N_HEADS * QK_NOPE).astype(_BF16)
    w_uqp = wq[:, :, QK_NOPE:].reshape(Q_LORA, N_HEADS * QK_ROPE).astype(_BF16)
    w_uk_bd = _block_diag([w_uk[:, hd, :].T.astype(_BF16) for hd in range(N_HEADS)])
    return {
        "pre_g": pre_norm.reshape(1, D_MODEL),
        "w_q": w_in[:, :o_kv].astype(_BF16),
        "w_kv": jnp.concatenate([w_kv_lat, w_kr, jnp.zeros((D_MODEL, LANES - QK_ROPE), _BF16)], axis=1),
        "w_b": w_in[:, o_tail:].astype(_BF16),
        "q_g": q_norm.reshape(1, Q_LORA),
        "w_uqn": w_uqn,
        "w_uqp_t": w_uqp.T,
        "w_uq": jnp.concatenate([w_uqn, w_uqp], axis=1),
        "w_uk": w_uk_bd,
        "w_uk_t": w_uk_bd.T,
        "kv_g": kv_norm.reshape(1, KV_LORA),
        "conv_w": conv_w,
        "w_oc": w_o_conv.astype(_BF16),
        "w_uv": _block_diag([w_uv[:, hd, :].astype(_BF16) for hd in range(N_HEADS)]),
        "w_om": w_o_mla.astype(_BF16),
        "w_out": w_out.astype(_BF16),
        "post_g": post_norm.reshape(1, D_MODEL),
    }


def _rope_angles(pos):
    inv = ROPE_BASE ** (-jnp.arange(HALF_ROPE, dtype=_F32) / HALF_ROPE)
    ang = pos.astype(_F32)[:, None] * inv[None, :]
    return jnp.cos(ang), jnp.sin(ang)


def _rope_tables_rows(pos):
    cos, sin = _rope_angles(pos)
    zero = jnp.zeros_like(sin)
    reps = LANES // QK_ROPE
    return (jnp.tile(jnp.concatenate([cos, cos], axis=1), (1, reps)),
            jnp.tile(jnp.concatenate([zero, sin], axis=1), (1, reps)),
            jnp.tile(jnp.concatenate([-sin, zero], axis=1), (1, reps)))


PROMPT_ROW_TILE = 1024
PROMPT_Q_TILE = 512
PROMPT_K_TILE = 512


def kernel(x_prompt, x_sample, cache_kv_latent, cache_k_rope, state_conv, pre_norm, w_in, q_norm, w_uq, kv_norm,
           w_uk, w_uv, w_o_mla, conv_w, w_o_conv, w_out, post_norm):
    depth = pre_norm.shape[0]
    assert depth == 1
    b, t, _ = x_prompt.shape
    nb, t_new, _ = x_sample.shape
    past_len = cache_kv_latent.shape[2]
    lyr = 0
    w = _prep_weights(pre_norm[lyr], w_in[lyr], q_norm[lyr], w_uq[lyr], kv_norm[lyr], w_uk[lyr], w_uv[lyr],
                      w_o_mla[lyr], conv_w[lyr], w_o_conv[lyr], w_out[lyr], post_norm[lyr])

    cos, sin = _rope_angles(jnp.arange(t, dtype=jnp.int32))
    zero_state = jnp.zeros((b, CONV_K - 1, CONV_WIDTH), _F32)
    qt, kc, vt, ckv_p, kpe_t, ga, sa, mb, cv_p = _project_prompt(
        x_prompt, cos.T, sin.T, zero_state, w, tm=PROMPT_ROW_TILE, tq=PROMPT_Q_TILE)
    y_p = _attend_prompt(qt, kc, vt, ga, sa, mb, x_prompt, w, tq=PROMPT_Q_TILE, tk=PROMPT_K_TILE)

    n_rows = nb * t_new
    pos_s = past_len + jnp.arange(t_new, dtype=jnp.int32)
    tabs_s = tuple(jnp.tile(tb, (nb, 1)) for tb in _rope_tables_rows(pos_s))
    xs = x_sample.reshape(1, n_rows, D_MODEL)
    q_s, kc_s, ckv_s, kpe_s, ga_s, sa_s, mb_s, cv_s = _project_sample(
        xs, *tabs_s, state_conv[lyr], w, n_seq=nb)
    y_s = _attend_sample(q_s, cache_kv_latent[lyr], cache_k_rope[lyr], kc_s.reshape(nb, t_new, KEY_DIM),
                         ga_s[0], sa_s[0], mb_s[0], xs[0], w)

    return (y_p, y_s.reshape(nb, t_new, D_MODEL),
            ckv_p[None], jnp.swapaxes(kpe_t, 1, 2)[None], cv_p[None],
            ckv_s.reshape(1, nb, t_new, KV_LORA), kpe_s.reshape(1, nb, t_new, QK_ROPE), cv_s[None])
```

```python
import functools

import numpy as np
import jax
import jax.numpy as jnp
from jax import lax
from jax.experimental import pallas as pl
from jax.experimental.pallas import tpu as pltpu

N_HEADS = 8
QK_NOPE = 64
QK_ROPE = 32
V_HEAD = 64
Q_LORA = 256
KV_LORA = 128
MLA_WIDTH = N_HEADS * V_HEAD
CONV_WIDTH = 512
CONV_K = 3
D_MODEL = 1024
CHUNK = 64
ROPE_BASE = 10000.0
EPS = 1e-6
SM_SCALE = (QK_NOPE + QK_ROPE) ** -0.5
NEG_INF = -1e30
LOG2E = 1.4426950408889634

LANES = 128
MXU_DIM = 256
COL_BLOCK = MXU_DIM
REDUCE_SLAB = 64
HALF_ROPE = QK_ROPE // 2
KEY_DIM = KV_LORA + QK_ROPE
HEADS_PER_ROPE_BLOCK = LANES // QK_ROPE
Q_SCALE = SM_SCALE * LOG2E
VMEM_LIMIT_BYTES = 56 * 1024 * 1024

_O_GM = 0
_O_CB = _O_GM + MLA_WIDTH
_O_CC = _O_CB + CONV_WIDTH
_O_CX = _O_CC + CONV_WIDTH
_O_GC = _O_CX + CONV_WIDTH
_O_MM = _O_GC + CONV_WIDTH
_O_MC = _O_MM + D_MODEL
_TAIL_COLS = _O_MC + D_MODEL

_F32 = jnp.float32
_BF16 = jnp.bfloat16
_NT = (((1,), (1,)), ((), ()))


def _rms(v, g):
    return v * lax.rsqrt(jnp.mean(v * v, axis=-1, keepdims=True) + EPS) * g


def _silu(v):
    return v * jax.nn.sigmoid(v)


def _dot(a, b):
    return jnp.dot(a, b, preferred_element_type=_F32)


def _dot_nt(a, b):
    return lax.dot_general(a, b, _NT, preferred_element_type=_F32)


def _gates_and_conv(h, w_b_ref, st_ref, conv_w_ref, w_oc_ref, ga_ref, sa_ref, mb_ref, cst_ref,
                    ub_ref, carry_ref, *, n_seq, carry_state):
    seq_len = h.shape[0] // n_seq

    def proj(off, n):
        return _dot(h, w_b_ref[:, off:off + n])

    ga_ref[0] = _silu(proj(_O_GM, MLA_WIDTH)).astype(_BF16)
    sa_ref[0] = jax.nn.sigmoid(proj(_O_MM, D_MODEL)).astype(_BF16)

    u = proj(_O_CC, CONV_WIDTH) * proj(_O_CX, CONV_WIDTH)
    w0 = conv_w_ref[0:1, :]
    w1 = conv_w_ref[1:2, :]
    w2 = conv_w_ref[2:3, :]
    if carry_state:
        @pl.when(pl.program_id(1) == 0)
        def _():
            carry_ref[0:CONV_K - 1, :] = st_ref[0]
    convs = []
    for j in range(n_seq):
        u_j = u[j * seq_len:(j + 1) * seq_len]
        prev = carry_ref[0:CONV_K - 1, :] if carry_state else st_ref[j]
        ub_ref[j, 8 - (CONV_K - 1):8, :] = prev
        ub_ref[j, 8:8 + seq_len, :] = u_j
        convs.append(w0 * ub_ref[j, 6:6 + seq_len, :] + w1 * ub_ref[j, 7:7 + seq_len, :] + w2 * u_j)
        new_state = u_j[seq_len - (CONV_K - 1):, :]
        cst_ref[j] = new_state
        if carry_state:
            carry_ref[0:CONV_K - 1, :] = new_state
    conv = convs[0] if n_seq == 1 else jnp.concatenate(convs, axis=0)
    bb = (proj(_O_CB, CONV_WIDTH) * conv * _silu(proj(_O_GC, CONV_WIDTH))).astype(_BF16)
    branch_b = _dot(bb, w_oc_ref[...])
    mb_ref[0] = (jax.nn.sigmoid(proj(_O_MC, D_MODEL)) * branch_b).astype(_BF16)


def _proj_prompt_kernel(x_ref, cos_ref, sin_ref, st_ref, pre_g_ref, w_qkv_ref, w_b_ref, q_g_ref,
                        w_uqn_ref, w_uqp_t_ref, w_uk_t_ref, kv_g_ref, conv_w_ref, w_oc_ref,
                        qt_ref, kc_ref, vt_ref, ckv_ref, kpet_ref, ga_ref, sa_ref, mb_ref, cst_ref,
                        ub_ref, carry_ref, *, tq):
    tm = x_ref.shape[1]
    h = _rms(x_ref[0], pre_g_ref[...]).astype(_BF16)
    cos_t = cos_ref[...]
    sin_t = sin_ref[...]

    qkv = _dot(h, w_qkv_ref[...])
    qn = _rms(qkv[:, :Q_LORA], q_g_ref[...]).astype(_BF16)
    q_nope = _dot(qn, w_uqn_ref[...]).astype(_BF16)
    q_abs_t = _dot_nt(w_uk_t_ref[...], q_nope)
    q_pe_t = _dot_nt(w_uqp_t_ref[...], qn).reshape(N_HEADS, QK_ROPE, tm)
    x1 = q_pe_t[:, :HALF_ROPE, :]
    x2 = q_pe_t[:, HALF_ROPE:, :]
    r1 = (x1 * cos_t - x2 * sin_t) * Q_SCALE
    r2 = (x2 * cos_t + x1 * sin_t) * Q_SCALE
    groups = tq // COL_BLOCK
    for j in range(tm // tq):
        for hd in range(N_HEADS):
            for g in range(groups):
                tok = slice(j * tq + g * COL_BLOCK, j * tq + (g + 1) * COL_BLOCK)
                cb = hd * groups + g
                qt_ref[0, j, cb, 0:KV_LORA, :] = (
                    q_abs_t[hd * KV_LORA:(hd + 1) * KV_LORA, tok] * Q_SCALE).astype(_BF16)
                qt_ref[0, j, cb, KV_LORA:KV_LORA + HALF_ROPE, :] = r1[hd][:, tok].astype(_BF16)
                qt_ref[0, j, cb, KV_LORA + HALF_ROPE:KEY_DIM, :] = r2[hd][:, tok].astype(_BF16)

    kvr = qkv[:, Q_LORA:]
    ckv = _rms(kvr[:, :KV_LORA], kv_g_ref[...])
    ckv_ref[0] = ckv
    ckv_t = ckv.T.astype(_BF16)
    for u in range(tm // MXU_DIM):
        vt_ref[0, u] = ckv_t[:, u * MXU_DIM:(u + 1) * MXU_DIM]
    kr_t = kvr[:, KV_LORA:].T
    k1 = kr_t[0:HALF_ROPE]
    k2 = kr_t[HALF_ROPE:QK_ROPE]
    kpe_t = jnp.concatenate([k1 * cos_t - k2 * sin_t, k2 * cos_t + k1 * sin_t], axis=0)
    kpet_ref[0] = kpe_t
    kpe = jnp.concatenate([kpe_t, jnp.zeros((LANES - QK_ROPE, tm), _F32)], axis=0).T
    kc_ref[0, :, 0:KV_LORA] = ckv.astype(_BF16)
    kc_ref[0, :, KV_LORA:KEY_DIM] = kpe[:, :QK_ROPE].astype(_BF16)

    _gates_and_conv(h, w_b_ref, st_ref, conv_w_ref, w_oc_ref, ga_ref, sa_ref, mb_ref, cst_ref,
                    ub_ref, carry_ref, n_seq=1, carry_state=True)


def _project_prompt(x, cos_t, sin_t, state, w, *, tm, tq):
    b, t, _ = x.shape

    def rows(width, dtype):
        return (jax.ShapeDtypeStruct((b, t, width), dtype),
                pl.BlockSpec((1, tm, width), lambda bi, ti: (bi, ti, 0)))

    def full(a):
        return pl.BlockSpec(a.shape, lambda bi, ti: (0,) * a.ndim)

    outs = [
        (jax.ShapeDtypeStruct((b, t // tq, N_HEADS * tq // COL_BLOCK, KEY_DIM, COL_BLOCK), _BF16),
         pl.BlockSpec((1, tm // tq, N_HEADS * tq // COL_BLOCK, KEY_DIM, COL_BLOCK),
                      lambda bi, ti: (bi, ti, 0, 0, 0))),
        rows(KEY_DIM, _BF16),
        (jax.ShapeDtypeStruct((b, t // MXU_DIM, KV_LORA, MXU_DIM), _BF16),
         pl.BlockSpec((1, tm // MXU_DIM, KV_LORA, MXU_DIM), lambda bi, ti: (bi, ti, 0, 0))),
        rows(KV_LORA, _F32),
        (jax.ShapeDtypeStruct((b, QK_ROPE, t), _F32),
         pl.BlockSpec((1, QK_ROPE, tm), lambda bi, ti: (bi, 0, ti))),
        rows(MLA_WIDTH, _BF16),
        rows(D_MODEL, _BF16),
        rows(D_MODEL, _BF16),
        (jax.ShapeDtypeStruct((b, CONV_K - 1, CONV_WIDTH), _F32),
         pl.BlockSpec((1, CONV_K - 1, CONV_WIDTH), lambda bi, ti: (bi, 0, 0))),
    ]
    table_spec = pl.BlockSpec((HALF_ROPE, tm), lambda bi, ti: (0, ti))
    weights = (w["pre_g"], w["w_qkv"], w["w_b"], w["q_g"], w["w_uqn"], w["w_uqp_t"], w["w_uk_t"],
               w["kv_g"], w["conv_w"], w["w_oc"])
    return pl.pallas_call(
        functools.partial(_proj_prompt_kernel, tq=tq),
        grid=(b, t // tm),
        in_specs=[pl.BlockSpec((1, tm, D_MODEL), lambda bi, ti: (bi, ti, 0)),
                  table_spec, table_spec,
                  pl.BlockSpec((1, CONV_K - 1, CONV_WIDTH), lambda bi, ti: (bi, 0, 0))]
                 + [full(a) for a in weights],
        out_specs=[o[1] for o in outs],
        out_shape=[o[0] for o in outs],
        scratch_shapes=[pltpu.VMEM((1, tm + 8, CONV_WIDTH), _F32),
                        pltpu.VMEM((8, CONV_WIDTH), _F32)],
        compiler_params=pltpu.CompilerParams(
            dimension_semantics=("arbitrary", "arbitrary"), vmem_limit_bytes=VMEM_LIMIT_BYTES),
        name="proj_prompt",
    )(x, cos_t, sin_t, state, *weights)


def _proj_sample_kernel(x_ref, cos_ref, sn1_ref, sn2_ref, st_ref, pre_g_ref, w_qkv_ref, w_b_ref, q_g_ref,
                        w_uq_ref, w_uk_ref, kv_g_ref, conv_w_ref, w_oc_ref,
                        q_ref, kc_ref, ckv_ref, kpe_ref, ga_ref, sa_ref, mb_ref, cst_ref,
                        ub_ref, carry_ref, *, n_seq):
    h = _rms(x_ref[0], pre_g_ref[...]).astype(_BF16)
    cos = cos_ref[...]
    sn1 = sn1_ref[...]
    sn2 = sn2_ref[...]

    def rope(v):
        return (v * cos + pltpu.roll(v, HALF_ROPE, 1) * sn1
                + pltpu.roll(v, LANES - HALF_ROPE, 1) * sn2)

    qkv = _dot(h, w_qkv_ref[...])
    qn = _rms(qkv[:, :Q_LORA], q_g_ref[...]).astype(_BF16)
    q = _dot(qn, w_uq_ref[...])
    q_abs = _dot(q[:, :N_HEADS * QK_NOPE].astype(_BF16), w_uk_ref[...])
    n_rope_blocks = N_HEADS // HEADS_PER_ROPE_BLOCK
    q_rot = [rope(q[:, N_HEADS * QK_NOPE + j * LANES:N_HEADS * QK_NOPE + (j + 1) * LANES]) * Q_SCALE
             for j in range(n_rope_blocks)]
    for hd in range(N_HEADS):
        q_ref[0, hd, :, 0:KV_LORA] = (q_abs[:, hd * LANES:(hd + 1) * LANES] * Q_SCALE).astype(_BF16)
        grp = hd % HEADS_PER_ROPE_BLOCK
        blk = q_rot[hd // HEADS_PER_ROPE_BLOCK]
        if grp:
            blk = pltpu.roll(blk, LANES - grp * QK_ROPE, 1)
        q_ref[0, hd, :, KV_LORA:KEY_DIM] = blk[:, :QK_ROPE].astype(_BF16)

    kvr = qkv[:, Q_LORA:]
    ckv = _rms(kvr[:, :KV_LORA], kv_g_ref[...])
    k_rot = rope(kvr[:, KV_LORA:])
    ckv_ref[0] = ckv
    kpe_ref[0] = k_rot[:, :QK_ROPE]
    kc_ref[0, :, 0:KV_LORA] = ckv.astype(_BF16)
    kc_ref[0, :, KV_LORA:KEY_DIM] = k_rot[:, :QK_ROPE].astype(_BF16)

    _gates_and_conv(h, w_b_ref, st_ref, conv_w_ref, w_oc_ref, ga_ref, sa_ref, mb_ref, cst_ref,
                    ub_ref, carry_ref, n_seq=n_seq, carry_state=False)


def _project_sample(x, cos, sn1, sn2, state, w, *, n_seq):
    _, tm, _ = x.shape

    def rows(width, dtype):
        return (jax.ShapeDtypeStruct((1, tm, width), dtype), pl.BlockSpec((1, tm, width), lambda i: (0, 0, 0)))

    def full(a):
        return pl.BlockSpec(a.shape, lambda i: (0,) * a.ndim)

    outs = [
        (jax.ShapeDtypeStruct((1, N_HEADS, tm, KEY_DIM), _BF16),
         pl.BlockSpec((1, N_HEADS, tm, KEY_DIM), lambda i: (0, 0, 0, 0))),
        rows(KEY_DIM, _BF16), rows(KV_LORA, _F32), rows(QK_ROPE, _F32),
        rows(MLA_WIDTH, _BF16), rows(D_MODEL, _BF16), rows(D_MODEL, _BF16),
        (jax.ShapeDtypeStruct((n_seq, CONV_K - 1, CONV_WIDTH), _F32),
         pl.BlockSpec((n_seq, CONV_K - 1, CONV_WIDTH), lambda i: (0, 0, 0))),
    ]
    weights = (w["pre_g"], w["w_qkv"], w["w_b"], w["q_g"], w["w_uq"], w["w_uk"],
               w["kv_g"], w["conv_w"], w["w_oc"])
    return pl.pallas_call(
        functools.partial(_proj_sample_kernel, n_seq=n_seq),
        grid=(1,),
        in_specs=[full(x), full(cos), full(sn1), full(sn2), full(state)] + [full(a) for a in weights],
        out_specs=[o[1] for o in outs],
        out_shape=[o[0] for o in outs],
        scratch_shapes=[pltpu.VMEM((n_seq, tm // n_seq + 8, CONV_WIDTH), _F32),
                        pltpu.VMEM((8, CONV_WIDTH), _F32)],
        compiler_params=pltpu.CompilerParams(
            dimension_semantics=("arbitrary",), vmem_limit_bytes=VMEM_LIMIT_BYTES),
        name="proj_sample",
    )(x, cos, sn1, sn2, state, *weights)


def _epilogue(o_lat, ga, sa, mb, x, w_uv_ref, w_om_ref, w_out_ref, post_g_ref):
    o = _dot(o_lat.astype(_BF16), w_uv_ref[...])
    branch_a = _dot((o * ga.astype(_F32)).astype(_BF16), w_om_ref[...])
    merged = sa.astype(_F32) * branch_a + mb.astype(_F32)
    z = _dot(merged.astype(_BF16), w_out_ref[...])
    return x + _rms(z, post_g_ref[...])


def _col_reduce(v, op):
    n_keys, cols = v.shape
    slabs = max(n_keys // REDUCE_SLAB, 1)
    part = op(v.reshape(slabs, n_keys // slabs, cols), axis=0)
    return op(part, axis=0, keepdims=True)


def _attn_prompt_kernel(qt_ref, qn_ref, kc_ref, vt_ref, bias_ref, ga_ref, sa_ref, mb_ref, x_ref,
                        w_uv_ref, w_om_ref, w_out_ref, post_g_ref, y_ref,
                        m_ref, l_ref, acc_ref, s_ref, mc_ref, *, tq, tk):
    i = pl.program_id(1)
    units = tk // MXU_DIM
    q_tiles_per_k_tile = tk // tq
    edge = i // q_tiles_per_k_tile
    groups = tq // COL_BLOCK
    n_cb = N_HEADS * groups

    m_ref[...] = jnp.full(m_ref.shape, NEG_INF, _F32)
    l_ref[...] = jnp.zeros(l_ref.shape, _F32)
    acc_ref[...] = jnp.zeros(acc_ref.shape, _F32)

    def keys_of(tile):
        return kc_ref[0, pl.ds(pl.multiple_of(tile * tk, tk), tk), :]

    def values_of(tile):
        return [vt_ref[0, tile * units + u] for u in range(units)]

    def stage1(k, cb, q_tile=None, q_ref=qt_ref):
        s = _dot(k, q_ref[0, 0, cb])
        if q_tile is not None:
            s = s + bias_ref[(q_tile % q_tiles_per_k_tile) * groups + cb % groups]
        s_ref[cb] = s
        mc_ref[cb] = _col_reduce(s, jnp.max)

    def stage2(vts, cb):
        m_prev = m_ref[cb]
        m_new = jnp.maximum(m_prev, mc_ref[cb])
        alpha = jnp.exp2(m_prev - m_new)
        p = jnp.exp2(s_ref[cb] - m_new)
        l_ref[cb] = alpha * l_ref[cb] + _col_reduce(p, jnp.sum)
        pb = p.astype(_BF16)
        pv = _dot(vts[0], pb[0:MXU_DIM])
        for u in range(1, units):
            pv = pv + _dot(vts[u], pb[u * MXU_DIM:(u + 1) * MXU_DIM])
        acc_ref[cb] = alpha * acc_ref[cb] + pv
        m_ref[cb] = m_new

    @pl.when(i == 0)
    def _():
        k_edge = keys_of(edge)
        for cb in range(n_cb):
            stage1(k_edge, cb, q_tile=i)

    def advance(j):
        vts = values_of(jnp.where(j == 0, edge, j - 1))
        k = keys_of(j)
        for cb in range(n_cb):
            stage2(vts, cb)
            stage1(k, cb)

    def single(j, carry):
        advance(j)
        return carry

    def double(pair, carry):
        advance(odd + 2 * pair)
        advance(odd + 2 * pair + 1)
        return carry

    odd = edge % 2
    lax.fori_loop(0, odd, single, 0)
    lax.fori_loop(0, edge // 2, double, 0)

    last_vts = values_of(jnp.where(edge == 0, edge, edge - 1))
    nxt = jnp.minimum(i + 1, pl.num_programs(1) - 1)
    k_next = keys_of(nxt // q_tiles_per_k_tile)
    for cb in range(n_cb):
        stage2(last_vts, cb)
        stage1(k_next, cb, q_tile=nxt, q_ref=qn_ref)

    def head_rows(hd):
        parts = [(acc_ref[hd * groups + g] / l_ref[hd * groups + g]).T for g in range(groups)]
        return parts[0] if groups == 1 else jnp.concatenate(parts, axis=0)

    o_lat = jnp.concatenate([head_rows(hd) for hd in range(N_HEADS)], axis=1)
    y_ref[0] = _epilogue(o_lat, ga_ref[0], sa_ref[0], mb_ref[0], x_ref[0],
                         w_uv_ref, w_om_ref, w_out_ref, post_g_ref)


def _attend_prompt(qt, kc, vt, ga, sa, mb, x, w, *, tq, tk):
    b, t, _ = x.shape
    n_q = t // tq

    def row_spec(width):
        return pl.BlockSpec((1, tq, width), lambda bi, qi: (bi, qi, 0))

    def full(a):
        return pl.BlockSpec(a.shape, lambda bi, qi: (0,) * a.ndim)

    k_chunk = np.arange(tk)[None, :, None] // CHUNK
    q_chunk = (np.arange(tk // COL_BLOCK)[:, None, None] * COL_BLOCK
               + np.arange(COL_BLOCK)[None, None, :]) // CHUNK
    bias = jnp.asarray(np.where(k_chunk <= q_chunk, 0.0, NEG_INF), _F32)
    n_cb = N_HEADS * tq // COL_BLOCK

    weights = (w["w_uv"], w["w_om"], w["w_out"], w["post_g"])
    return pl.pallas_call(
        functools.partial(_attn_prompt_kernel, tq=tq, tk=tk),
        grid=(b, t // tq),
        in_specs=[pl.BlockSpec((1, 1, n_cb, KEY_DIM, COL_BLOCK), lambda bi, qi: (bi, qi, 0, 0, 0)),
                  pl.BlockSpec((1, 1, n_cb, KEY_DIM, COL_BLOCK),
                               lambda bi, qi: (bi, jnp.minimum(qi + 1, n_q - 1), 0, 0, 0)),
                  pl.BlockSpec((1, t, KEY_DIM), lambda bi, qi: (bi, 0, 0)),
                  pl.BlockSpec((1, t // MXU_DIM, KV_LORA, MXU_DIM), lambda bi, qi: (bi, 0, 0, 0)),
                  full(bias),
                  row_spec(MLA_WIDTH), row_spec(D_MODEL), row_spec(D_MODEL), row_spec(D_MODEL)]
                 + [full(a) for a in weights],
        out_specs=row_spec(D_MODEL),
        out_shape=jax.ShapeDtypeStruct((b, t, D_MODEL), _F32),
        scratch_shapes=[pltpu.VMEM((n_cb, 1, COL_BLOCK), _F32), pltpu.VMEM((n_cb, 1, COL_BLOCK), _F32),
                        pltpu.VMEM((n_cb, KV_LORA, COL_BLOCK), _F32),
                        pltpu.VMEM((n_cb, tk, COL_BLOCK), _F32), pltpu.VMEM((n_cb, 1, COL_BLOCK), _F32)],
        compiler_params=pltpu.CompilerParams(
            dimension_semantics=("arbitrary", "arbitrary"), vmem_limit_bytes=VMEM_LIMIT_BYTES),
        name="attn_prompt",
    )(qt, qt, kc, vt, bias, ga, sa, mb, x, *weights)


def _attn_first(q, k, m_ref, l_ref, acc_ref, visible=None):
    s = _dot_nt(q, k)
    if visible is not None:
        s = jnp.where(visible, s, NEG_INF)
    m = jnp.max(s, axis=1, keepdims=True)
    p = jnp.exp2(s - m)
    m_ref[...] = jnp.broadcast_to(m, m_ref.shape)
    l_ref[...] = jnp.broadcast_to(jnp.sum(p, axis=1, keepdims=True), l_ref.shape)
    acc_ref[...] = _dot(p.astype(_BF16), k[:, :KV_LORA])


def _attn_update(q, k_lat, k_rope, m_ref, l_ref, acc_ref):
    s = _dot_nt(q[:, :KV_LORA], k_lat) + _dot_nt(q[:, KV_LORA:], k_rope)
    m_prev = m_ref[...]
    m_new = jnp.maximum(m_prev, jnp.max(s, axis=1, keepdims=True))
    alpha = jnp.exp2(m_prev - m_new)
    p = jnp.exp2(s - jnp.concatenate([m_new] * (s.shape[1] // LANES), axis=1))
    l_ref[...] = alpha * l_ref[...] + jnp.sum(p, axis=1, keepdims=True)
    acc_ref[...] = alpha * acc_ref[...] + _dot(p.astype(_BF16), k_lat)
    m_ref[...] = m_new


def _attn_sample_kernel(q_ref, past_lat_ref, past_rope_ref, knew_ref, ga_ref, sa_ref, mb_ref, x_ref,
                        w_uv_ref, w_om_ref, w_out_ref, post_g_ref, y_ref,
                        m_ref, l_ref, acc_ref, o_ref, *, new_visible):
    bi = pl.program_id(0)
    t_new = knew_ref.shape[1]
    rows = N_HEADS * t_new
    q = q_ref[0].reshape(rows, KEY_DIM)
    if new_visible is None:
        visible = None
    else:
        q_pos = lax.broadcasted_iota(jnp.int32, (rows, t_new), 0) % t_new
        k_pos = lax.broadcasted_iota(jnp.int32, (rows, t_new), 1)
        visible = (k_pos + new_visible[0]) // CHUNK <= (q_pos + new_visible[0]) // CHUNK
    _attn_first(q, knew_ref[0], m_ref, l_ref, acc_ref, visible=visible)
    _attn_update(q, past_lat_ref[0].astype(_BF16), past_rope_ref[0].astype(_BF16), m_ref, l_ref, acc_ref)
    o = acc_ref[...] / l_ref[...]
    r0 = pl.multiple_of(bi * t_new, t_new)
    o_ref[pl.ds(r0, t_new), :] = jnp.concatenate(
        [o[hd * t_new:(hd + 1) * t_new] for hd in range(N_HEADS)], axis=1)

    @pl.when(bi == pl.num_programs(0) - 1)
    def _():
        y_ref[...] = _epilogue(o_ref[...], ga_ref[...], sa_ref[...], mb_ref[...], x_ref[...],
                               w_uv_ref, w_om_ref, w_out_ref, post_g_ref)


def _attend_sample(q, past_lat, past_rope, k_new, ga, sa, mb, x, w):
    nb, t_new, _ = k_new.shape
    past_len = past_lat.shape[1]
    n_rows = nb * t_new
    rows = N_HEADS * t_new
    last_q, first_q = past_len + t_new - 1, past_len
    new_visible = None if last_q // CHUNK == first_q // CHUNK else (past_len,)

    def full(a):
        return pl.BlockSpec(a.shape, lambda bi: (0,) * a.ndim)

    weights = (w["w_uv"], w["w_om"], w["w_out"], w["post_g"])
    return pl.pallas_call(
        functools.partial(_attn_sample_kernel, new_visible=new_visible),
        grid=(nb,),
        in_specs=[pl.BlockSpec((1, N_HEADS, t_new, KEY_DIM), lambda bi: (0, 0, bi, 0)),
                  pl.BlockSpec((1, past_len, KV_LORA), lambda bi: (bi, 0, 0)),
                  pl.BlockSpec((1, past_len, QK_ROPE), lambda bi: (bi, 0, 0)),
                  pl.BlockSpec((1, t_new, KEY_DIM), lambda bi: (bi, 0, 0)),
                  full(ga), full(sa), full(mb), full(x)] + [full(a) for a in weights],
        out_specs=pl.BlockSpec((n_rows, D_MODEL), lambda bi: (0, 0)),
        out_shape=jax.ShapeDtypeStruct((n_rows, D_MODEL), _F32),
        scratch_shapes=[pltpu.VMEM((rows, LANES), _F32), pltpu.VMEM((rows, LANES), _F32),
                        pltpu.VMEM((rows, KV_LORA), _F32), pltpu.VMEM((n_rows, N_HEADS * KV_LORA), _F32)],
        compiler_params=pltpu.CompilerParams(
            dimension_semantics=("arbitrary",), vmem_limit_bytes=VMEM_LIMIT_BYTES),
        name="attn_sample",
    )(q, past_lat, past_rope, k_new, ga, sa, mb, x, *weights)


def _block_diag(blocks):
    n = len(blocks)
    r, c = blocks[0].shape
    rows = []
    for j, blk in enumerate(blocks):
        rows.append(jnp.pad(blk, ((0, 0), (j * c, (n - 1 - j) * c))))
    return jnp.concatenate(rows, axis=0)


def _shift_cast_kernel(*refs, shift):
    *in_refs, o_ref = refs
    parts = [r[...] for r in in_refs[:-1]] + [in_refs[-1][:, :shift]]
    x = jnp.concatenate(parts, axis=1)
    o_ref[...] = x[:, shift:].astype(o_ref.dtype)


def _tail_columns_bf16(w, start):
    rows, cols = w.shape
    width = cols - start
    block = 4 * LANES
    assert width % block == 0 and 0 < start % LANES
    first = start // LANES
    shift = start % LANES
    n_in = block // LANES + 1
    return pl.pallas_call(
        functools.partial(_shift_cast_kernel, shift=shift),
        grid=(width // block,),
        in_specs=[pl.BlockSpec((rows, LANES), lambda j, q=q: (0, first + (block // LANES) * j + q))
                  for q in range(n_in)],
        out_specs=pl.BlockSpec((rows, block), lambda j: (0, j)),
        out_shape=jax.ShapeDtypeStruct((rows, width), _BF16),
        compiler_params=pltpu.CompilerParams(dimension_semantics=("arbitrary",)),
        name="w_tail_prep",
    )(*([w] * n_in))


def _prep_weights(pre_norm, w_in, q_norm, w_uq, kv_norm, w_uk, w_uv, w_o_mla, conv_w, w_o_conv, w_out, post_norm):
    o_tail = Q_LORA + KV_LORA + QK_ROPE
    assert w_in.shape[1] == o_tail + _TAIL_COLS
    wq = w_uq.reshape(Q_LORA, N_HEADS, QK_NOPE + QK_ROPE)
    w_uqn = wq[:, :, :QK_NOPE].reshape(Q_LORA, N_HEADS * QK_NOPE).astype(_BF16)
    w_uqp = wq[:, :, QK_NOPE:].reshape(Q_LORA, N_HEADS * QK_ROPE).astype(_BF16)
    w_uk_bd = _block_diag([w_uk[:, hd, :].T.astype(_BF16) for hd in range(N_HEADS)])
    return {
        "pre_g": pre_norm.reshape(1, D_MODEL),
        "w_qkv": jnp.pad(w_in[:, :o_tail].astype(_BF16), ((0, 0), (0, -o_tail % MXU_DIM))),
        "w_b": _tail_columns_bf16(w_in, o_tail),
        "q_g": q_norm.reshape(1, Q_LORA),
        "w_uqn": w_uqn,
        "w_uqp_t": w_uqp.T,
        "w_uq": jnp.concatenate([w_uqn, w_uqp], axis=1),
        "w_uk": w_uk_bd,
        "w_uk_t": w_uk_bd.T,
        "kv_g": kv_norm.reshape(1, KV_LORA),
        "conv_w": conv_w,
        "w_oc": w_o_conv.astype(_BF16),
        "w_uv": _block_diag([w_uv[:, hd, :].astype(_BF16) for hd in range(N_HEADS)]),
        "w_om": w_o_mla.astype(_BF16),
        "w_out": w_out.astype(_BF16),
        "post_g": post_norm.reshape(1, D_MODEL),
    }


def _rope_angles(pos):
    inv = ROPE_BASE ** (-jnp.arange(HALF_ROPE, dtype=_F32) / HALF_ROPE)
    ang = pos.astype(_F32)[:, None] * inv[None, :]
    return jnp.cos(ang), jnp.sin(ang)


def _rope_tables_rows(pos):
    cos, sin = _rope_angles(pos)
    zero = jnp.zeros_like(sin)
    reps = LANES // QK_ROPE
    return (jnp.tile(jnp.concatenate([cos, cos], axis=1), (1, reps)),
            jnp.tile(jnp.concatenate([zero, sin], axis=1), (1, reps)),
            jnp.tile(jnp.concatenate([-sin, zero], axis=1), (1, reps)))


PROMPT_ROW_TILE = 1024
PROMPT_Q_TILE = 512
PROMPT_K_TILE = 512


def kernel(x_prompt, x_sample, cache_kv_latent, cache_k_rope, state_conv, pre_norm, w_in, q_norm, w_uq, kv_norm,
           w_uk, w_uv, w_o_mla, conv_w, w_o_conv, w_out, post_norm):
    depth = pre_norm.shape[0]
    assert depth == 1
    b, t, _ = x_prompt.shape
    nb, t_new, _ = x_sample.shape
    past_len = cache_kv_latent.shape[2]
    lyr = 0
    w = _prep_weights(pre_norm[lyr], w_in[lyr], q_norm[lyr], w_uq[lyr], kv_norm[lyr], w_uk[lyr], w_uv[lyr],
                      w_o_mla[lyr], conv_w[lyr], w_o_conv[lyr], w_out[lyr], post_norm[lyr])

    cos, sin = _rope_angles(jnp.arange(t, dtype=jnp.int32))
    zero_state = jnp.zeros((b, CONV_K - 1, CONV_WIDTH), _F32)
    qt, kc, vt, ckv_p, kpe_t, ga, sa, mb, cv_p = _project_prompt(
        x_prompt, cos.T, sin.T, zero_state, w, tm=PROMPT_ROW_TILE, tq=PROMPT_Q_TILE)
    y_p = _attend_prompt(qt, kc, vt, ga, sa, mb, x_prompt, w, tq=PROMPT_Q_TILE, tk=PROMPT_K_TILE)

    n_rows = nb * t_new
    pos_s = past_len + jnp.arange(t_new, dtype=jnp.int32)
    tabs_s = tuple(jnp.tile(tb, (nb, 1)) for tb in _rope_tables_rows(pos_s))
    xs = x_sample.reshape(1, n_rows, D_MODEL)
    q_s, kc_s, ckv_s, kpe_s, ga_s, sa_s, mb_s, cv_s = _project_sample(
        xs, *tabs_s, state_conv[lyr], w, n_seq=nb)
    y_s = _attend_sample(q_s, cache_kv_latent[lyr], cache_k_rope[lyr], kc_s.reshape(nb, t_new, KEY_DIM),
                         ga_s[0], sa_s[0], mb_s[0], xs[0], w)

    return (y_p, y_s.reshape(nb, t_new, D_MODEL),
            ckv_p[None], jnp.swapaxes(kpe_t, 1, 2)[None], cv_p[None],
            ckv_s.reshape(1, nb, t_new, KV_LORA), kpe_s.reshape(1, nb, t_new, QK_ROPE), cv_s[None])
```

```python
import functools

import numpy as np
import jax
import jax.numpy as jnp
from jax import lax
from jax.experimental import pallas as pl
from jax.experimental.pallas import tpu as pltpu

N_HEADS = 8
QK_NOPE = 64
QK_ROPE = 32
V_HEAD = 64
Q_LORA = 256
KV_LORA = 128
MLA_WIDTH = N_HEADS * V_HEAD
CONV_WIDTH = 512
CONV_K = 3
D_MODEL = 1024
CHUNK = 64
ROPE_BASE = 10000.0
EPS = 1e-6
SM_SCALE = (QK_NOPE + QK_ROPE) ** -0.5
NEG_INF = -1e30
LOG2E = 1.4426950408889634

LANES = 128
MXU_DIM = 256
COL_BLOCK = MXU_DIM
REDUCE_SLAB = 64
HALF_ROPE = QK_ROPE // 2
KEY_DIM = KV_LORA + QK_ROPE
HEADS_PER_ROPE_BLOCK = LANES // QK_ROPE
Q_SCALE = SM_SCALE * LOG2E
VMEM_LIMIT_BYTES = 56 * 1024 * 1024

_O_TAIL = Q_LORA + KV_LORA + QK_ROPE
_QKV_COLS = -(-_O_TAIL // MXU_DIM) * MXU_DIM
_O_GM = 0
_O_CB = _O_GM + MLA_WIDTH
_O_CC = _O_CB + CONV_WIDTH
_O_CX = _O_CC + CONV_WIDTH
_O_GC = _O_CX + CONV_WIDTH
_O_MM = _O_GC + CONV_WIDTH
_O_MC = _O_MM + D_MODEL
_TAIL_COLS = _O_MC + D_MODEL

_F32 = jnp.float32
_BF16 = jnp.bfloat16
_NT = (((1,), (1,)), ((), ()))


def _rms(v, g):
    return v * lax.rsqrt(jnp.mean(v * v, axis=-1, keepdims=True) + EPS) * g


def _silu(v):
    return v * jax.nn.sigmoid(v)


def _dot(a, b):
    return jnp.dot(a, b, preferred_element_type=_F32)


def _dot_nt(a, b):
    return lax.dot_general(a, b, _NT, preferred_element_type=_F32)


def _gates_and_conv(h, w_in_t_ref, st_ref, conv_w_ref, w_oc_ref, ga_ref, sa_ref, mb_ref, cst_ref,
                    ub_ref, carry_ref, *, n_seq, carry_state):
    seq_len = h.shape[0] // n_seq

    def proj(off, n):
        return _dot_nt(h, w_in_t_ref[_O_TAIL + off:_O_TAIL + off + n, :])

    ga_ref[0] = _silu(proj(_O_GM, MLA_WIDTH)).astype(_BF16)
    sa_ref[0] = jax.nn.sigmoid(proj(_O_MM, D_MODEL)).astype(_BF16)

    u = proj(_O_CC, CONV_WIDTH) * proj(_O_CX, CONV_WIDTH)
    w0 = conv_w_ref[0:1, :]
    w1 = conv_w_ref[1:2, :]
    w2 = conv_w_ref[2:3, :]
    if carry_state:
        @pl.when(pl.program_id(1) == 0)
        def _():
            carry_ref[0:CONV_K - 1, :] = st_ref[0]
    convs = []
    for j in range(n_seq):
        u_j = u[j * seq_len:(j + 1) * seq_len]
        prev = carry_ref[0:CONV_K - 1, :] if carry_state else st_ref[j]
        ub_ref[j, 8 - (CONV_K - 1):8, :] = prev
        ub_ref[j, 8:8 + seq_len, :] = u_j
        convs.append(w0 * ub_ref[j, 6:6 + seq_len, :] + w1 * ub_ref[j, 7:7 + seq_len, :] + w2 * u_j)
        new_state = u_j[seq_len - (CONV_K - 1):, :]
        cst_ref[j] = new_state
        if carry_state:
            carry_ref[0:CONV_K - 1, :] = new_state
    conv = convs[0] if n_seq == 1 else jnp.concatenate(convs, axis=0)
    bb = (proj(_O_CB, CONV_WIDTH) * conv * _silu(proj(_O_GC, CONV_WIDTH))).astype(_BF16)
    branch_b = _dot(bb, w_oc_ref[...])
    mb_ref[0] = (jax.nn.sigmoid(proj(_O_MC, D_MODEL)) * branch_b).astype(_BF16)


def _proj_prompt_kernel(x_ref, cos_ref, sin_ref, st_ref, pre_g_ref, w_in_t_ref, q_g_ref,
                        w_uqn_ref, w_uqp_t_ref, w_uk_t_ref, kv_g_ref, conv_w_ref, w_oc_ref,
                        qt_ref, kc_ref, vt_ref, ckv_ref, kpet_ref, ga_ref, sa_ref, mb_ref, cst_ref,
                        ub_ref, carry_ref, *, tq):
    tm = x_ref.shape[1]
    h = _rms(x_ref[0], pre_g_ref[...]).astype(_BF16)
    cos_t = cos_ref[...]
    sin_t = sin_ref[...]

    qkv = _dot_nt(h, w_in_t_ref[0:_QKV_COLS, :])
    qn = _rms(qkv[:, :Q_LORA], q_g_ref[...]).astype(_BF16)
    q_nope = _dot(qn, w_uqn_ref[...]).astype(_BF16)
    q_abs_t = _dot_nt(w_uk_t_ref[...], q_nope)
    q_pe_t = _dot_nt(w_uqp_t_ref[...], qn).reshape(N_HEADS, QK_ROPE, tm)
    x1 = q_pe_t[:, :HALF_ROPE, :]
    x2 = q_pe_t[:, HALF_ROPE:, :]
    r1 = (x1 * cos_t - x2 * sin_t) * Q_SCALE
    r2 = (x2 * cos_t + x1 * sin_t) * Q_SCALE
    groups = tq // COL_BLOCK
    for j in range(tm // tq):
        for hd in range(N_HEADS):
            for g in range(groups):
                tok = slice(j * tq + g * COL_BLOCK, j * tq + (g + 1) * COL_BLOCK)
                cb = hd * groups + g
                qt_ref[0, j, cb, 0:KV_LORA, :] = (
                    q_abs_t[hd * KV_LORA:(hd + 1) * KV_LORA, tok] * Q_SCALE).astype(_BF16)
                qt_ref[0, j, cb, KV_LORA:KV_LORA + HALF_ROPE, :] = r1[hd][:, tok].astype(_BF16)
                qt_ref[0, j, cb, KV_LORA + HALF_ROPE:KEY_DIM, :] = r2[hd][:, tok].astype(_BF16)

    kvr = qkv[:, Q_LORA:]
    ckv = _rms(kvr[:, :KV_LORA], kv_g_ref[...])
    ckv_ref[0] = ckv
    ckv_t = ckv.T.astype(_BF16)
    for u in range(tm // MXU_DIM):
        vt_ref[0, u] = ckv_t[:, u * MXU_DIM:(u + 1) * MXU_DIM]
    kr_t = kvr[:, KV_LORA:].T
    k1 = kr_t[0:HALF_ROPE]
    k2 = kr_t[HALF_ROPE:QK_ROPE]
    kpe_t = jnp.concatenate([k1 * cos_t - k2 * sin_t, k2 * cos_t + k1 * sin_t], axis=0)
    kpet_ref[0] = kpe_t
    kpe = jnp.concatenate([kpe_t, jnp.zeros((LANES - QK_ROPE, tm), _F32)], axis=0).T
    kc_ref[0, :, 0:KV_LORA] = ckv.astype(_BF16)
    kc_ref[0, :, KV_LORA:KEY_DIM] = kpe[:, :QK_ROPE].astype(_BF16)

    _gates_and_conv(h, w_in_t_ref, st_ref, conv_w_ref, w_oc_ref, ga_ref, sa_ref, mb_ref, cst_ref,
                    ub_ref, carry_ref, n_seq=1, carry_state=True)


def _project_prompt(x, cos_t, sin_t, state, w, *, tm, tq):
    b, t, _ = x.shape

    def rows(width, dtype):
        return (jax.ShapeDtypeStruct((b, t, width), dtype),
                pl.BlockSpec((1, tm, width), lambda bi, ti: (bi, ti, 0)))

    def full(a):
        return pl.BlockSpec(a.shape, lambda bi, ti: (0,) * a.ndim)

    outs = [
        (jax.ShapeDtypeStruct((b, t // tq, N_HEADS * tq // COL_BLOCK, KEY_DIM, COL_BLOCK), _BF16),
         pl.BlockSpec((1, tm // tq, N_HEADS * tq // COL_BLOCK, KEY_DIM, COL_BLOCK),
                      lambda bi, ti: (bi, ti, 0, 0, 0))),
        rows(KEY_DIM, _BF16),
        (jax.ShapeDtypeStruct((b, t // MXU_DIM, KV_LORA, MXU_DIM), _BF16),
         pl.BlockSpec((1, tm // MXU_DIM, KV_LORA, MXU_DIM), lambda bi, ti: (bi, ti, 0, 0))),
        rows(KV_LORA, _F32),
        (jax.ShapeDtypeStruct((b, QK_ROPE, t), _F32),
         pl.BlockSpec((1, QK_ROPE, tm), lambda bi, ti: (bi, 0, ti))),
        rows(MLA_WIDTH, _BF16),
        rows(D_MODEL, _BF16),
        rows(D_MODEL, _BF16),
        (jax.ShapeDtypeStruct((b, CONV_K - 1, CONV_WIDTH), _F32),
         pl.BlockSpec((1, CONV_K - 1, CONV_WIDTH), lambda bi, ti: (bi, 0, 0))),
    ]
    table_spec = pl.BlockSpec((HALF_ROPE, tm), lambda bi, ti: (0, ti))
    weights = (w["pre_g"], w["w_in_t"], w["q_g"], w["w_uqn"], w["w_uqp_t"], w["w_uk_t"],
               w["kv_g"], w["conv_w"], w["w_oc"])
    return pl.pallas_call(
        functools.partial(_proj_prompt_kernel, tq=tq),
        grid=(b, t // tm),
        in_specs=[pl.BlockSpec((1, tm, D_MODEL), lambda bi, ti: (bi, ti, 0)),
                  table_spec, table_spec,
                  pl.BlockSpec((1, CONV_K - 1, CONV_WIDTH), lambda bi, ti: (bi, 0, 0))]
                 + [full(a) for a in weights],
        out_specs=[o[1] for o in outs],
        out_shape=[o[0] for o in outs],
        scratch_shapes=[pltpu.VMEM((1, tm + 8, CONV_WIDTH), _F32),
                        pltpu.VMEM((8, CONV_WIDTH), _F32)],
        compiler_params=pltpu.CompilerParams(
            dimension_semantics=("arbitrary", "arbitrary"), vmem_limit_bytes=VMEM_LIMIT_BYTES),
        name="proj_prompt",
    )(x, cos_t, sin_t, state, *weights)


def _proj_sample_kernel(x_ref, cos_ref, sn1_ref, sn2_ref, st_ref, pre_g_ref, w_in_t_ref, q_g_ref,
                        w_uq_ref, w_uk_ref, kv_g_ref, conv_w_ref, w_oc_ref,
                        q_ref, kc_ref, ckv_ref, kpe_ref, ga_ref, sa_ref, mb_ref, cst_ref,
                        ub_ref, carry_ref, *, n_seq):
    h = _rms(x_ref[0], pre_g_ref[...]).astype(_BF16)
    cos = cos_ref[...]
    sn1 = sn1_ref[...]
    sn2 = sn2_ref[...]

    def rope(v):
        return (v * cos + pltpu.roll(v, HALF_ROPE, 1) * sn1
                + pltpu.roll(v, LANES - HALF_ROPE, 1) * sn2)

    qkv = _dot_nt(h, w_in_t_ref[0:_QKV_COLS, :])
    qn = _rms(qkv[:, :Q_LORA], q_g_ref[...]).astype(_BF16)
    q = _dot(qn, w_uq_ref[...])
    q_abs = _dot(q[:, :N_HEADS * QK_NOPE].astype(_BF16), w_uk_ref[...])
    n_rope_blocks = N_HEADS // HEADS_PER_ROPE_BLOCK
    q_rot = [rope(q[:, N_HEADS * QK_NOPE + j * LANES:N_HEADS * QK_NOPE + (j + 1) * LANES]) * Q_SCALE
             for j in range(n_rope_blocks)]
    for hd in range(N_HEADS):
        q_ref[0, hd, :, 0:KV_LORA] = (q_abs[:, hd * LANES:(hd + 1) * LANES] * Q_SCALE).astype(_BF16)
        grp = hd % HEADS_PER_ROPE_BLOCK
        blk = q_rot[hd // HEADS_PER_ROPE_BLOCK]
        if grp:
            blk = pltpu.roll(blk, LANES - grp * QK_ROPE, 1)
        q_ref[0, hd, :, KV_LORA:KEY_DIM] = blk[:, :QK_ROPE].astype(_BF16)

    kvr = qkv[:, Q_LORA:]
    ckv = _rms(kvr[:, :KV_LORA], kv_g_ref[...])
    k_rot = rope(kvr[:, KV_LORA:])
    ckv_ref[0] = ckv
    kpe_ref[0] = k_rot[:, :QK_ROPE]
    kc_ref[0, :, 0:KV_LORA] = ckv.astype(_BF16)
    kc_ref[0, :, KV_LORA:KEY_DIM] = k_rot[:, :QK_ROPE].astype(_BF16)

    _gates_and_conv(h, w_in_t_ref, st_ref, conv_w_ref, w_oc_ref, ga_ref, sa_ref, mb_ref, cst_ref,
                    ub_ref, carry_ref, n_seq=n_seq, carry_state=False)


def _project_sample(x, cos, sn1, sn2, state, w, *, n_seq):
    _, tm, _ = x.shape

    def rows(width, dtype):
        return (jax.ShapeDtypeStruct((1, tm, width), dtype), pl.BlockSpec((1, tm, width), lambda i: (0, 0, 0)))

    def full(a):
        return pl.BlockSpec(a.shape, lambda i: (0,) * a.ndim)

    outs = [
        (jax.ShapeDtypeStruct((1, N_HEADS, tm, KEY_DIM), _BF16),
         pl.BlockSpec((1, N_HEADS, tm, KEY_DIM), lambda i: (0, 0, 0, 0))),
        rows(KEY_DIM, _BF16), rows(KV_LORA, _F32), rows(QK_ROPE, _F32),
        rows(MLA_WIDTH, _BF16), rows(D_MODEL, _BF16), rows(D_MODEL, _BF16),
        (jax.ShapeDtypeStruct((n_seq, CONV_K - 1, CONV_WIDTH), _F32),
         pl.BlockSpec((n_seq, CONV_K - 1, CONV_WIDTH), lambda i: (0, 0, 0))),
    ]
    weights = (w["pre_g"], w["w_in_t"], w["q_g"], w["w_uq"], w["w_uk"],
               w["kv_g"], w["conv_w"], w["w_oc"])
    return pl.pallas_call(
        functools.partial(_proj_sample_kernel, n_seq=n_seq),
        grid=(1,),
        in_specs=[full(x), full(cos), full(sn1), full(sn2), full(state)] + [full(a) for a in weights],
        out_specs=[o[1] for o in outs],
        out_shape=[o[0] for o in outs],
        scratch_shapes=[pltpu.VMEM((n_seq, tm // n_seq + 8, CONV_WIDTH), _F32),
                        pltpu.VMEM((8, CONV_WIDTH), _F32)],
        compiler_params=pltpu.CompilerParams(
            dimension_semantics=("arbitrary",), vmem_limit_bytes=VMEM_LIMIT_BYTES),
        name="proj_sample",
    )(x, cos, sn1, sn2, state, *weights)


def _epilogue(o_lat, ga, sa, mb, x, w_uv_ref, w_om_ref, w_out_ref, post_g_ref):
    o = _dot(o_lat.astype(_BF16), w_uv_ref[...])
    branch_a = _dot((o * ga.astype(_F32)).astype(_BF16), w_om_ref[...])
    merged = sa.astype(_F32) * branch_a + mb.astype(_F32)
    z = _dot(merged.astype(_BF16), w_out_ref[...])
    return x + _rms(z, post_g_ref[...])


def _col_reduce(v, op):
    n_keys, cols = v.shape
    slabs = max(n_keys // REDUCE_SLAB, 1)
    part = op(v.reshape(slabs, n_keys // slabs, cols), axis=0)
    return op(part, axis=0, keepdims=True)


def _attn_prompt_kernel(qt_ref, qn_ref, kc_ref, vt_ref, bias_ref, ga_ref, sa_ref, mb_ref, x_ref,
                        w_uv_ref, w_om_ref, w_out_ref, post_g_ref, y_ref,
                        m_ref, l_ref, acc_ref, s_ref, mc_ref, *, tq, tk):
    i = pl.program_id(1)
    units = tk // MXU_DIM
    q_tiles_per_k_tile = tk // tq
    edge = i // q_tiles_per_k_tile
    groups = tq // COL_BLOCK
    n_cb = N_HEADS * groups

    m_ref[...] = jnp.full(m_ref.shape, NEG_INF, _F32)
    l_ref[...] = jnp.zeros(l_ref.shape, _F32)
    acc_ref[...] = jnp.zeros(acc_ref.shape, _F32)

    def keys_of(tile):
        return kc_ref[0, pl.ds(pl.multiple_of(tile * tk, tk), tk), :]

    def values_of(tile):
        return [vt_ref[0, tile * units + u] for u in range(units)]

    def stage1(k, cb, q_tile=None, q_ref=qt_ref):
        s = _dot(k, q_ref[0, 0, cb])
        if q_tile is not None:
            s = s + bias_ref[(q_tile % q_tiles_per_k_tile) * groups + cb % groups]
        s_ref[cb] = s
        mc_ref[cb] = _col_reduce(s, jnp.max)

    def stage2(vts, cb):
        m_prev = m_ref[cb]
        m_new = jnp.maximum(m_prev, mc_ref[cb])
        alpha = jnp.exp2(m_prev - m_new)
        p = jnp.exp2(s_ref[cb] - m_new)
        l_ref[cb] = alpha * l_ref[cb] + _col_reduce(p, jnp.sum)
        pb = p.astype(_BF16)
        pv = _dot(vts[0], pb[0:MXU_DIM])
        for u in range(1, units):
            pv = pv + _dot(vts[u], pb[u * MXU_DIM:(u + 1) * MXU_DIM])
        acc_ref[cb] = alpha * acc_ref[cb] + pv
        m_ref[cb] = m_new

    @pl.when(i == 0)
    def _():
        k_edge = keys_of(edge)
        for cb in range(n_cb):
            stage1(k_edge, cb, q_tile=i)

    def advance(j):
        vts = values_of(jnp.where(j == 0, edge, j - 1))
        k = keys_of(j)
        for cb in range(n_cb):
            stage2(vts, cb)
            stage1(k, cb)

    def single(j, carry):
        advance(j)
        return carry

    def double(pair, carry):
        advance(odd + 2 * pair)
        advance(odd + 2 * pair + 1)
        return carry

    odd = edge % 2
    lax.fori_loop(0, odd, single, 0)
    lax.fori_loop(0, edge // 2, double, 0)

    last_vts = values_of(jnp.where(edge == 0, edge, edge - 1))
    nxt = jnp.minimum(i + 1, pl.num_programs(1) - 1)
    k_next = keys_of(nxt // q_tiles_per_k_tile)
    for cb in range(n_cb):
        stage2(last_vts, cb)
        stage1(k_next, cb, q_tile=nxt, q_ref=qn_ref)

    def head_rows(hd):
        parts = [(acc_ref[hd * groups + g] / l_ref[hd * groups + g]).T for g in range(groups)]
        return parts[0] if groups == 1 else jnp.concatenate(parts, axis=0)

    o_lat = jnp.concatenate([head_rows(hd) for hd in range(N_HEADS)], axis=1)
    y_ref[0] = _epilogue(o_lat, ga_ref[0], sa_ref[0], mb_ref[0], x_ref[0],
                         w_uv_ref, w_om_ref, w_out_ref, post_g_ref)


def _attend_prompt(qt, kc, vt, ga, sa, mb, x, w, *, tq, tk):
    b, t, _ = x.shape
    n_q = t // tq

    def row_spec(width):
        return pl.BlockSpec((1, tq, width), lambda bi, qi: (bi, qi, 0))

    def full(a):
        return pl.BlockSpec(a.shape, lambda bi, qi: (0,) * a.ndim)

    k_chunk = np.arange(tk)[None, :, None] // CHUNK
    q_chunk = (np.arange(tk // COL_BLOCK)[:, None, None] * COL_BLOCK
               + np.arange(COL_BLOCK)[None, None, :]) // CHUNK
    bias = jnp.asarray(np.where(k_chunk <= q_chunk, 0.0, NEG_INF), _F32)
    n_cb = N_HEADS * tq // COL_BLOCK

    weights = (w["w_uv"], w["w_om"], w["w_out"], w["post_g"])
    return pl.pallas_call(
        functools.partial(_attn_prompt_kernel, tq=tq, tk=tk),
        grid=(b, t // tq),
        in_specs=[pl.BlockSpec((1, 1, n_cb, KEY_DIM, COL_BLOCK), lambda bi, qi: (bi, qi, 0, 0, 0)),
                  pl.BlockSpec((1, 1, n_cb, KEY_DIM, COL_BLOCK),
                               lambda bi, qi: (bi, jnp.minimum(qi + 1, n_q - 1), 0, 0, 0)),
                  pl.BlockSpec((1, t, KEY_DIM), lambda bi, qi: (bi, 0, 0)),
                  pl.BlockSpec((1, t // MXU_DIM, KV_LORA, MXU_DIM), lambda bi, qi: (bi, 0, 0, 0)),
                  full(bias),
                  row_spec(MLA_WIDTH), row_spec(D_MODEL), row_spec(D_MODEL), row_spec(D_MODEL)]
                 + [full(a) for a in weights],
        out_specs=row_spec(D_MODEL),
        out_shape=jax.ShapeDtypeStruct((b, t, D_MODEL), _F32),
        scratch_shapes=[pltpu.VMEM((n_cb, 1, COL_BLOCK), _F32), pltpu.VMEM((n_cb, 1, COL_BLOCK), _F32),
                        pltpu.VMEM((n_cb, KV_LORA, COL_BLOCK), _F32),
                        pltpu.VMEM((n_cb, tk, COL_BLOCK), _F32), pltpu.VMEM((n_cb, 1, COL_BLOCK), _F32)],
        compiler_params=pltpu.CompilerParams(
            dimension_semantics=("arbitrary", "arbitrary"), vmem_limit_bytes=VMEM_LIMIT_BYTES),
        name="attn_prompt",
    )(qt, qt, kc, vt, bias, ga, sa, mb, x, *weights)


def _attn_first(q, k, m_ref, l_ref, acc_ref, visible=None):
    s = _dot_nt(q, k)
    if visible is not None:
        s = jnp.where(visible, s, NEG_INF)
    m = jnp.max(s, axis=1, keepdims=True)
    p = jnp.exp2(s - m)
    m_ref[...] = jnp.broadcast_to(m, m_ref.shape)
    l_ref[...] = jnp.broadcast_to(jnp.sum(p, axis=1, keepdims=True), l_ref.shape)
    acc_ref[...] = _dot(p.astype(_BF16), k[:, :KV_LORA])


def _attn_update(q, k_lat, k_rope_t, m_ref, l_ref, acc_ref):
    s = _dot_nt(q[:, :KV_LORA], k_lat) + _dot(q[:, KV_LORA:], k_rope_t)
    m_prev = m_ref[...]
    m_new = jnp.maximum(m_prev, jnp.max(s, axis=1, keepdims=True))
    alpha = jnp.exp2(m_prev - m_new)
    p = jnp.exp2(s - jnp.concatenate([m_new] * (s.shape[1] // LANES), axis=1))
    l_ref[...] = alpha * l_ref[...] + jnp.sum(p, axis=1, keepdims=True)
    acc_ref[...] = alpha * acc_ref[...] + _dot(p.astype(_BF16), k_lat)
    m_ref[...] = m_new


def _attn_sample_kernel(q_ref, past_lat_ref, past_rope_ref, knew_ref, ga_ref, sa_ref, mb_ref, x_ref,
                        w_uv_ref, w_om_ref, w_out_ref, post_g_ref, y_ref,
                        m_ref, l_ref, acc_ref, o_ref, *, new_visible):
    bi = pl.program_id(0)
    t_new = knew_ref.shape[1]
    rows = N_HEADS * t_new
    q = q_ref[0].reshape(rows, KEY_DIM)
    if new_visible is None:
        visible = None
    else:
        q_pos = lax.broadcasted_iota(jnp.int32, (rows, t_new), 0) % t_new
        k_pos = lax.broadcasted_iota(jnp.int32, (rows, t_new), 1)
        visible = (k_pos + new_visible[0]) // CHUNK <= (q_pos + new_visible[0]) // CHUNK
    _attn_first(q, knew_ref[0], m_ref, l_ref, acc_ref, visible=visible)
    _attn_update(q, past_lat_ref[0].astype(_BF16), past_rope_ref[0].astype(_BF16), m_ref, l_ref, acc_ref)
    o = acc_ref[...] / l_ref[...]
    r0 = pl.multiple_of(bi * t_new, t_new)
    o_ref[pl.ds(r0, t_new), :] = jnp.concatenate(
        [o[hd * t_new:(hd + 1) * t_new] for hd in range(N_HEADS)], axis=1)

    @pl.when(bi == pl.num_programs(0) - 1)
    def _():
        y_ref[...] = _epilogue(o_ref[...], ga_ref[...], sa_ref[...], mb_ref[...], x_ref[...],
                               w_uv_ref, w_om_ref, w_out_ref, post_g_ref)


def _attend_sample(q, past_lat, past_rope, k_new, ga, sa, mb, x, w):
    nb, t_new, _ = k_new.shape
    past_len = past_lat.shape[1]
    n_rows = nb * t_new
    rows = N_HEADS * t_new
    last_q, first_q = past_len + t_new - 1, past_len
    new_visible = None if last_q // CHUNK == first_q // CHUNK else (past_len,)

    def full(a):
        return pl.BlockSpec(a.shape, lambda bi: (0,) * a.ndim)

    weights = (w["w_uv"], w["w_om"], w["w_out"], w["post_g"])
    return pl.pallas_call(
        functools.partial(_attn_sample_kernel, new_visible=new_visible),
        grid=(nb,),
        in_specs=[pl.BlockSpec((1, N_HEADS, t_new, KEY_DIM), lambda bi: (0, 0, bi, 0)),
                  pl.BlockSpec((1, past_len, KV_LORA), lambda bi: (bi, 0, 0)),
                  pl.BlockSpec((1, QK_ROPE, past_len), lambda bi: (bi, 0, 0)),
                  pl.BlockSpec((1, t_new, KEY_DIM), lambda bi: (bi, 0, 0)),
                  full(ga), full(sa), full(mb), full(x)] + [full(a) for a in weights],
        out_specs=pl.BlockSpec((n_rows, D_MODEL), lambda bi: (0, 0)),
        out_shape=jax.ShapeDtypeStruct((n_rows, D_MODEL), _F32),
        scratch_shapes=[pltpu.VMEM((rows, LANES), _F32), pltpu.VMEM((rows, LANES), _F32),
                        pltpu.VMEM((rows, KV_LORA), _F32), pltpu.VMEM((n_rows, N_HEADS * KV_LORA), _F32)],
        compiler_params=pltpu.CompilerParams(
            dimension_semantics=("arbitrary",), vmem_limit_bytes=VMEM_LIMIT_BYTES),
        name="attn_sample",
    )(q, past_lat, past_rope, k_new, ga, sa, mb, x, *weights)


def _block_diag(blocks):
    n = len(blocks)
    r, c = blocks[0].shape
    rows = []
    for j, blk in enumerate(blocks):
        rows.append(jnp.pad(blk, ((0, 0), (j * c, (n - 1 - j) * c))))
    return jnp.concatenate(rows, axis=0)


def _prep_weights(pre_norm, w_in, q_norm, w_uq, kv_norm, w_uk, w_uv, w_o_mla, conv_w, w_o_conv, w_out, post_norm):
    assert w_in.shape[1] == _O_TAIL + _TAIL_COLS
    wq = w_uq.reshape(Q_LORA, N_HEADS, QK_NOPE + QK_ROPE)
    w_uqn = wq[:, :, :QK_NOPE].reshape(Q_LORA, N_HEADS * QK_NOPE).astype(_BF16)
    w_uqp = wq[:, :, QK_NOPE:].reshape(Q_LORA, N_HEADS * QK_ROPE).astype(_BF16)
    w_uk_bd = _block_diag([w_uk[:, hd, :].T.astype(_BF16) for hd in range(N_HEADS)])
    return {
        "pre_g": pre_norm.reshape(1, D_MODEL),
        "w_in_t": w_in.T.astype(_BF16),
        "q_g": q_norm.reshape(1, Q_LORA),
        "w_uqn": w_uqn,
        "w_uqp_t": w_uqp.T,
        "w_uq": jnp.concatenate([w_uqn, w_uqp], axis=1),
        "w_uk": w_uk_bd,
        "w_uk_t": w_uk_bd.T,
        "kv_g": kv_norm.reshape(1, KV_LORA),
        "conv_w": conv_w,
        "w_oc": w_o_conv.astype(_BF16),
        "w_uv": _block_diag([w_uv[:, hd, :].astype(_BF16) for hd in range(N_HEADS)]),
        "w_om": w_o_mla.astype(_BF16),
        "w_out": w_out.astype(_BF16),
        "post_g": post_norm.reshape(1, D_MODEL),
    }


def _rope_angles(pos):
    inv = ROPE_BASE ** (-jnp.arange(HALF_ROPE, dtype=_F32) / HALF_ROPE)
    ang = pos.astype(_F32)[:, None] * inv[None, :]
    return jnp.cos(ang), jnp.sin(ang)


def _rope_tables_rows(pos):
    cos, sin = _rope_angles(pos)
    zero = jnp.zeros_like(sin)
    reps = LANES // QK_ROPE
    return (jnp.tile(jnp.concatenate([cos, cos], axis=1), (1, reps)),
            jnp.tile(jnp.concatenate([zero, sin], axis=1), (1, reps)),
            jnp.tile(jnp.concatenate([-sin, zero], axis=1), (1, reps)))


PROMPT_ROW_TILE = 1024
PROMPT_Q_TILE = 512
PROMPT_K_TILE = 512


def kernel(x_prompt, x_sample, cache_kv_latent, cache_k_rope, state_conv, pre_norm, w_in, q_norm, w_uq, kv_norm,
           w_uk, w_uv, w_o_mla, conv_w, w_o_conv, w_out, post_norm):
    depth = pre_norm.shape[0]
    assert depth == 1
    b, t, _ = x_prompt.shape
    nb, t_new, _ = x_sample.shape
    past_len = cache_kv_latent.shape[2]
    lyr = 0
    w = _prep_weights(pre_norm[lyr], w_in[lyr], q_norm[lyr], w_uq[lyr], kv_norm[lyr], w_uk[lyr], w_uv[lyr],
                      w_o_mla[lyr], conv_w[lyr], w_o_conv[lyr], w_out[lyr], post_norm[lyr])

    cos, sin = _rope_angles(jnp.arange(t, dtype=jnp.int32))
    zero_state = jnp.zeros((b, CONV_K - 1, CONV_WIDTH), _F32)
    qt, kc, vt, ckv_p, kpe_t, ga, sa, mb, cv_p = _project_prompt(
        x_prompt, cos.T, sin.T, zero_state, w, tm=PROMPT_ROW_TILE, tq=PROMPT_Q_TILE)
    y_p = _attend_prompt(qt, kc, vt, ga, sa, mb, x_prompt, w, tq=PROMPT_Q_TILE, tk=PROMPT_K_TILE)

    n_rows = nb * t_new
    pos_s = past_len + jnp.arange(t_new, dtype=jnp.int32)
    tabs_s = tuple(jnp.tile(tb, (nb, 1)) for tb in _rope_tables_rows(pos_s))
    xs = x_sample.reshape(1, n_rows, D_MODEL)
    q_s, kc_s, ckv_s, kpe_s, ga_s, sa_s, mb_s, cv_s = _project_sample(
        xs, *tabs_s, state_conv[lyr], w, n_seq=nb)
    y_s = _attend_sample(q_s, cache_kv_latent[lyr], jnp.swapaxes(cache_k_rope[lyr], 1, 2),
                         kc_s.reshape(nb, t_new, KEY_DIM),
                         ga_s[0], sa_s[0], mb_s[0], xs[0], w)

    return (y_p, y_s.reshape(nb, t_new, D_MODEL),
            ckv_p[None], jnp.swapaxes(kpe_t, 1, 2)[None], cv_p[None],
            ckv_s.reshape(1, nb, t_new, KV_LORA), kpe_s.reshape(1, nb, t_new, QK_ROPE), cv_s[None])
```

```python
import functools

import numpy as np
import jax
import jax.numpy as jnp
from jax import lax
from jax.experimental import pallas as pl
from jax.experimental.pallas import tpu as pltpu

N_HEADS = 8
QK_NOPE = 64
QK_ROPE = 32
V_HEAD = 64
Q_LORA = 256
KV_LORA = 128
MLA_WIDTH = N_HEADS * V_HEAD
CONV_WIDTH = 512
CONV_K = 3
D_MODEL = 1024
CHUNK = 64
ROPE_BASE = 10000.0
EPS = 1e-6
SM_SCALE = (QK_NOPE + QK_ROPE) ** -0.5
NEG_INF = -1e30
LOG2E = 1.4426950408889634

LANES = 128
MXU_DIM = 256
COL_BLOCK = MXU_DIM
REDUCE_SLAB = 64
STAGE1_LEAD = 1
HALF_ROPE = QK_ROPE // 2
KEY_DIM = KV_LORA + QK_ROPE
HEADS_PER_ROPE_BLOCK = LANES // QK_ROPE
Q_SCALE = SM_SCALE * LOG2E
VMEM_LIMIT_BYTES = 56 * 1024 * 1024

_O_TAIL = Q_LORA + KV_LORA + QK_ROPE
_QKV_COLS = -(-_O_TAIL // MXU_DIM) * MXU_DIM
_O_GM = 0
_O_CB = _O_GM + MLA_WIDTH
_O_CC = _O_CB + CONV_WIDTH
_O_CX = _O_CC + CONV_WIDTH
_O_GC = _O_CX + CONV_WIDTH
_O_MM = _O_GC + CONV_WIDTH
_O_MC = _O_MM + D_MODEL
_TAIL_COLS = _O_MC + D_MODEL

_F32 = jnp.float32
_BF16 = jnp.bfloat16
_NT = (((1,), (1,)), ((), ()))


def _rms(v, g):
    return v * lax.rsqrt(jnp.mean(v * v, axis=-1, keepdims=True) + EPS) * g


def _silu(v):
    return v * jax.nn.sigmoid(v)


def _dot(a, b):
    return jnp.dot(a, b, preferred_element_type=_F32)


def _dot_nt(a, b):
    return lax.dot_general(a, b, _NT, preferred_element_type=_F32)


def _gates_and_conv(h, w_in_t_ref, st_ref, conv_w_ref, w_oc_ref, ga_ref, sa_ref, mb_ref, cst_ref,
                    ub_ref, carry_ref, *, n_seq, carry_state):
    seq_len = h.shape[0] // n_seq

    def proj(off, n):
        return _dot_nt(h, w_in_t_ref[_O_TAIL + off:_O_TAIL + off + n, :])

    ga_ref[0] = _silu(proj(_O_GM, MLA_WIDTH)).astype(_BF16)
    sa_ref[0] = jax.nn.sigmoid(proj(_O_MM, D_MODEL)).astype(_BF16)

    u = proj(_O_CC, CONV_WIDTH) * proj(_O_CX, CONV_WIDTH)
    w0 = conv_w_ref[0:1, :]
    w1 = conv_w_ref[1:2, :]
    w2 = conv_w_ref[2:3, :]
    if carry_state:
        @pl.when(pl.program_id(1) == 0)
        def _():
            carry_ref[0:CONV_K - 1, :] = st_ref[0]
    convs = []
    for j in range(n_seq):
        u_j = u[j * seq_len:(j + 1) * seq_len]
        prev = carry_ref[0:CONV_K - 1, :] if carry_state else st_ref[j]
        ub_ref[j, 8 - (CONV_K - 1):8, :] = prev
        ub_ref[j, 8:8 + seq_len, :] = u_j
        convs.append(w0 * ub_ref[j, 6:6 + seq_len, :] + w1 * ub_ref[j, 7:7 + seq_len, :] + w2 * u_j)
        new_state = u_j[seq_len - (CONV_K - 1):, :]
        cst_ref[j] = new_state
        if carry_state:
            carry_ref[0:CONV_K - 1, :] = new_state
    conv = convs[0] if n_seq == 1 else jnp.concatenate(convs, axis=0)
    bb = (proj(_O_CB, CONV_WIDTH) * conv * _silu(proj(_O_GC, CONV_WIDTH))).astype(_BF16)
    branch_b = _dot(bb, w_oc_ref[...])
    mb_ref[0] = (jax.nn.sigmoid(proj(_O_MC, D_MODEL)) * branch_b).astype(_BF16)


def _proj_prompt_kernel(x_ref, cos_ref, sin_ref, st_ref, pre_g_ref, w_in_t_ref, q_g_ref,
                        w_uqn_ref, w_uqp_t_ref, w_uk_t_ref, kv_g_ref, conv_w_ref, w_oc_ref,
                        qt_ref, kc_ref, vt_ref, ckv_ref, kpet_ref, ga_ref, sa_ref, mb_ref, cst_ref,
                        ub_ref, carry_ref, *, tq):
    tm = x_ref.shape[1]
    h = _rms(x_ref[0], pre_g_ref[...]).astype(_BF16)
    cos_t = cos_ref[...]
    sin_t = sin_ref[...]

    qkv = _dot_nt(h, w_in_t_ref[0:_QKV_COLS, :])
    qn = _rms(qkv[:, :Q_LORA], q_g_ref[...]).astype(_BF16)
    q_nope = _dot(qn, w_uqn_ref[...]).astype(_BF16)
    q_abs_t = _dot_nt(w_uk_t_ref[...], q_nope)
    q_pe_t = _dot_nt(w_uqp_t_ref[...], qn).reshape(N_HEADS, QK_ROPE, tm)
    x1 = q_pe_t[:, :HALF_ROPE, :]
    x2 = q_pe_t[:, HALF_ROPE:, :]
    r1 = (x1 * cos_t - x2 * sin_t) * Q_SCALE
    r2 = (x2 * cos_t + x1 * sin_t) * Q_SCALE
    groups = tq // COL_BLOCK
    for j in range(tm // tq):
        for hd in range(N_HEADS):
            for g in range(groups):
                tok = slice(j * tq + g * COL_BLOCK, j * tq + (g + 1) * COL_BLOCK)
                cb = hd * groups + g
                qt_ref[0, j, cb, 0:KV_LORA, :] = (
                    q_abs_t[hd * KV_LORA:(hd + 1) * KV_LORA, tok] * Q_SCALE).astype(_BF16)
                qt_ref[0, j, cb, KV_LORA:KV_LORA + HALF_ROPE, :] = r1[hd][:, tok].astype(_BF16)
                qt_ref[0, j, cb, KV_LORA + HALF_ROPE:KEY_DIM, :] = r2[hd][:, tok].astype(_BF16)

    kvr = qkv[:, Q_LORA:]
    ckv = _rms(kvr[:, :KV_LORA], kv_g_ref[...])
    ckv_ref[0] = ckv
    ckv_t = ckv.T.astype(_BF16)
    for u in range(tm // MXU_DIM):
        vt_ref[0, u] = ckv_t[:, u * MXU_DIM:(u + 1) * MXU_DIM]
    kr_t = kvr[:, KV_LORA:].T
    k1 = kr_t[0:HALF_ROPE]
    k2 = kr_t[HALF_ROPE:QK_ROPE]
    kpe_t = jnp.concatenate([k1 * cos_t - k2 * sin_t, k2 * cos_t + k1 * sin_t], axis=0)
    kpet_ref[0] = kpe_t
    kpe = jnp.concatenate([kpe_t, jnp.zeros((LANES - QK_ROPE, tm), _F32)], axis=0).T
    kc_ref[0, :, 0:KV_LORA] = ckv.astype(_BF16)
    kc_ref[0, :, KV_LORA:KEY_DIM] = kpe[:, :QK_ROPE].astype(_BF16)

    _gates_and_conv(h, w_in_t_ref, st_ref, conv_w_ref, w_oc_ref, ga_ref, sa_ref, mb_ref, cst_ref,
                    ub_ref, carry_ref, n_seq=1, carry_state=True)


def _project_prompt(x, cos_t, sin_t, state, w, *, tm, tq):
    b, t, _ = x.shape

    def rows(width, dtype):
        return (jax.ShapeDtypeStruct((b, t, width), dtype),
                pl.BlockSpec((1, tm, width), lambda bi, ti: (bi, ti, 0)))

    def full(a):
        return pl.BlockSpec(a.shape, lambda bi, ti: (0,) * a.ndim)

    outs = [
        (jax.ShapeDtypeStruct((b, t // tq, N_HEADS * tq // COL_BLOCK, KEY_DIM, COL_BLOCK), _BF16),
         pl.BlockSpec((1, tm // tq, N_HEADS * tq // COL_BLOCK, KEY_DIM, COL_BLOCK),
                      lambda bi, ti: (bi, ti, 0, 0, 0))),
        rows(KEY_DIM, _BF16),
        (jax.ShapeDtypeStruct((b, t // MXU_DIM, KV_LORA, MXU_DIM), _BF16),
         pl.BlockSpec((1, tm // MXU_DIM, KV_LORA, MXU_DIM), lambda bi, ti: (bi, ti, 0, 0))),
        rows(KV_LORA, _F32),
        (jax.ShapeDtypeStruct((b, QK_ROPE, t), _F32),
         pl.BlockSpec((1, QK_ROPE, tm), lambda bi, ti: (bi, 0, ti))),
        rows(MLA_WIDTH, _BF16),
        rows(D_MODEL, _BF16),
        rows(D_MODEL, _BF16),
        (jax.ShapeDtypeStruct((b, CONV_K - 1, CONV_WIDTH), _F32),
         pl.BlockSpec((1, CONV_K - 1, CONV_WIDTH), lambda bi, ti: (bi, 0, 0))),
    ]
    table_spec = pl.BlockSpec((HALF_ROPE, tm), lambda bi, ti: (0, ti))
    weights = (w["pre_g"], w["w_in_t"], w["q_g"], w["w_uqn"], w["w_uqp_t"], w["w_uk_t"],
               w["kv_g"], w["conv_w"], w["w_oc"])
    return pl.pallas_call(
        functools.partial(_proj_prompt_kernel, tq=tq),
        grid=(b, t // tm),
        in_specs=[pl.BlockSpec((1, tm, D_MODEL), lambda bi, ti: (bi, ti, 0)),
                  table_spec, table_spec,
                  pl.BlockSpec((1, CONV_K - 1, CONV_WIDTH), lambda bi, ti: (bi, 0, 0))]
                 + [full(a) for a in weights],
        out_specs=[o[1] for o in outs],
        out_shape=[o[0] for o in outs],
        scratch_shapes=[pltpu.VMEM((1, tm + 8, CONV_WIDTH), _F32),
                        pltpu.VMEM((8, CONV_WIDTH), _F32)],
        compiler_params=pltpu.CompilerParams(
            dimension_semantics=("arbitrary", "arbitrary"), vmem_limit_bytes=VMEM_LIMIT_BYTES),
        name="proj_prompt",
    )(x, cos_t, sin_t, state, *weights)


def _proj_sample_kernel(x_ref, cos_ref, sn1_ref, sn2_ref, st_ref, pre_g_ref, w_in_t_ref, q_g_ref,
                        w_uq_ref, w_uk_ref, kv_g_ref, conv_w_ref, w_oc_ref,
                        q_ref, kc_ref, ckv_ref, kpe_ref, ga_ref, sa_ref, mb_ref, cst_ref,
                        ub_ref, carry_ref, *, n_seq):
    h = _rms(x_ref[0], pre_g_ref[...]).astype(_BF16)
    cos = cos_ref[...]
    sn1 = sn1_ref[...]
    sn2 = sn2_ref[...]

    def rope(v):
        return (v * cos + pltpu.roll(v, HALF_ROPE, 1) * sn1
                + pltpu.roll(v, LANES - HALF_ROPE, 1) * sn2)

    qkv = _dot_nt(h, w_in_t_ref[0:_QKV_COLS, :])
    qn = _rms(qkv[:, :Q_LORA], q_g_ref[...]).astype(_BF16)
    q = _dot(qn, w_uq_ref[...])
    q_abs = _dot(q[:, :N_HEADS * QK_NOPE].astype(_BF16), w_uk_ref[...])
    n_rope_blocks = N_HEADS // HEADS_PER_ROPE_BLOCK
    q_rot = [rope(q[:, N_HEADS * QK_NOPE + j * LANES:N_HEADS * QK_NOPE + (j + 1) * LANES]) * Q_SCALE
             for j in range(n_rope_blocks)]
    for hd in range(N_HEADS):
        q_ref[0, hd, :, 0:KV_LORA] = (q_abs[:, hd * LANES:(hd + 1) * LANES] * Q_SCALE).astype(_BF16)
        grp = hd % HEADS_PER_ROPE_BLOCK
        blk = q_rot[hd // HEADS_PER_ROPE_BLOCK]
        if grp:
            blk = pltpu.roll(blk, LANES - grp * QK_ROPE, 1)
        q_ref[0, hd, :, KV_LORA:KEY_DIM] = blk[:, :QK_ROPE].astype(_BF16)

    kvr = qkv[:, Q_LORA:]
    ckv = _rms(kvr[:, :KV_LORA], kv_g_ref[...])
    k_rot = rope(kvr[:, KV_LORA:])
    ckv_ref[0] = ckv
    kpe_ref[0] = k_rot[:, :QK_ROPE]
    kc_ref[0, :, 0:KV_LORA] = ckv.astype(_BF16)
    kc_ref[0, :, KV_LORA:KEY_DIM] = k_rot[:, :QK_ROPE].astype(_BF16)

    _gates_and_conv(h, w_in_t_ref, st_ref, conv_w_ref, w_oc_ref, ga_ref, sa_ref, mb_ref, cst_ref,
                    ub_ref, carry_ref, n_seq=n_seq, carry_state=False)


def _project_sample(x, cos, sn1, sn2, state, w, *, n_seq):
    _, tm, _ = x.shape

    def rows(width, dtype):
        return (jax.ShapeDtypeStruct((1, tm, width), dtype), pl.BlockSpec((1, tm, width), lambda i: (0, 0, 0)))

    def full(a):
        return pl.BlockSpec(a.shape, lambda i: (0,) * a.ndim)

    outs = [
        (jax.ShapeDtypeStruct((1, N_HEADS, tm, KEY_DIM), _BF16),
         pl.BlockSpec((1, N_HEADS, tm, KEY_DIM), lambda i: (0, 0, 0, 0))),
        rows(KEY_DIM, _BF16), rows(KV_LORA, _F32), rows(QK_ROPE, _F32),
        rows(MLA_WIDTH, _BF16), rows(D_MODEL, _BF16), rows(D_MODEL, _BF16),
        (jax.ShapeDtypeStruct((n_seq, CONV_K - 1, CONV_WIDTH), _F32),
         pl.BlockSpec((n_seq, CONV_K - 1, CONV_WIDTH), lambda i: (0, 0, 0))),
    ]
    weights = (w["pre_g"], w["w_in_t"], w["q_g"], w["w_uq"], w["w_uk"],
               w["kv_g"], w["conv_w"], w["w_oc"])
    return pl.pallas_call(
        functools.partial(_proj_sample_kernel, n_seq=n_seq),
        grid=(1,),
        in_specs=[full(x), full(cos), full(sn1), full(sn2), full(state)] + [full(a) for a in weights],
        out_specs=[o[1] for o in outs],
        out_shape=[o[0] for o in outs],
        scratch_shapes=[pltpu.VMEM((n_seq, tm // n_seq + 8, CONV_WIDTH), _F32),
                        pltpu.VMEM((8, CONV_WIDTH), _F32)],
        compiler_params=pltpu.CompilerParams(
            dimension_semantics=("arbitrary",), vmem_limit_bytes=VMEM_LIMIT_BYTES),
        name="proj_sample",
    )(x, cos, sn1, sn2, state, *weights)


def _epilogue(o_lat, ga, sa, mb, x, w_uv_ref, w_om_ref, w_out_ref, post_g_ref):
    o = _dot(o_lat.astype(_BF16), w_uv_ref[...])
    branch_a = _dot((o * ga.astype(_F32)).astype(_BF16), w_om_ref[...])
    merged = sa.astype(_F32) * branch_a + mb.astype(_F32)
    z = _dot(merged.astype(_BF16), w_out_ref[...])
    return x + _rms(z, post_g_ref[...])


def _col_reduce(v, op):
    n_keys, cols = v.shape
    slabs = max(n_keys // REDUCE_SLAB, 1)
    part = op(v.reshape(slabs, n_keys // slabs, cols), axis=0)
    return op(part, axis=0, keepdims=True)


def _attn_prompt_kernel(qt_ref, qn_ref, kc_ref, vt_ref, bias_ref, ga_ref, sa_ref, mb_ref, x_ref,
                        w_uv_ref, w_om_ref, w_out_ref, post_g_ref, y_ref,
                        m_ref, l_ref, acc_ref, s_ref, mc_ref, *, tq, tk):
    i = pl.program_id(1)
    units = tk // MXU_DIM
    q_tiles_per_k_tile = tk // tq
    edge = i // q_tiles_per_k_tile
    groups = tq // COL_BLOCK
    n_cb = N_HEADS * groups

    m_ref[...] = jnp.full(m_ref.shape, NEG_INF, _F32)
    l_ref[...] = jnp.zeros(l_ref.shape, _F32)
    acc_ref[...] = jnp.zeros(acc_ref.shape, _F32)

    def keys_of(tile):
        return kc_ref[0, pl.ds(pl.multiple_of(tile * tk, tk), tk), :]

    def values_of(tile):
        return [vt_ref[0, tile * units + u] for u in range(units)]

    def stage1(k, cb, slot=0, q_tile=None, q_ref=qt_ref):
        s = _dot(k, q_ref[0, 0, cb])
        if q_tile is not None:
            s = s + bias_ref[(q_tile % q_tiles_per_k_tile) * groups + cb % groups]
        s_ref[slot, cb] = s
        mc_ref[slot, cb] = _col_reduce(s, jnp.max)

    def stage2(vts, cb, slot=0):
        m_prev = m_ref[cb]
        m_new = jnp.maximum(m_prev, mc_ref[slot, cb])
        alpha = jnp.exp2(m_prev - m_new)
        p = jnp.exp2(s_ref[slot, cb] - m_new)
        l_ref[cb] = alpha * l_ref[cb] + _col_reduce(p, jnp.sum)
        pb = p.astype(_BF16)
        pv = _dot(vts[0], pb[0:MXU_DIM])
        for u in range(1, units):
            pv = pv + _dot(vts[u], pb[u * MXU_DIM:(u + 1) * MXU_DIM])
        acc_ref[cb] = alpha * acc_ref[cb] + pv
        m_ref[cb] = m_new

    @pl.when(i == 0)
    def _():
        k_edge = keys_of(edge)
        for cb in range(n_cb):
            stage1(k_edge, cb, q_tile=i)

    def advance(j, rd, wr):
        vts = values_of(jnp.where(j == 0, edge, j - 1))
        k = keys_of(j)
        lead = 0 if rd == wr else STAGE1_LEAD
        for cb in range(lead):
            stage1(k, cb, wr)
        for cb in range(n_cb):
            stage2(vts, cb, rd)
            if cb + lead < n_cb:
                stage1(k, cb + lead, wr)

    def single(j, carry):
        advance(j, 0, 0)
        return carry

    def double(pair, carry):
        advance(odd + 2 * pair, 0, 1)
        advance(odd + 2 * pair + 1, 1, 0)
        return carry

    odd = edge % 2
    lax.fori_loop(0, odd, single, 0)
    lax.fori_loop(0, edge // 2, double, 0)

    last_vts = values_of(jnp.where(edge == 0, edge, edge - 1))
    nxt = jnp.minimum(i + 1, pl.num_programs(1) - 1)
    k_next = keys_of(nxt // q_tiles_per_k_tile)
    for cb in range(n_cb):
        stage2(last_vts, cb)
        stage1(k_next, cb, 0, q_tile=nxt, q_ref=qn_ref)

    def head_rows(hd):
        parts = [(acc_ref[hd * groups + g] / l_ref[hd * groups + g]).T for g in range(groups)]
        return parts[0] if groups == 1 else jnp.concatenate(parts, axis=0)

    o_lat = jnp.concatenate([head_rows(hd) for hd in range(N_HEADS)], axis=1)
    y_ref[0] = _epilogue(o_lat, ga_ref[0], sa_ref[0], mb_ref[0], x_ref[0],
                         w_uv_ref, w_om_ref, w_out_ref, post_g_ref)


def _attend_prompt(qt, kc, vt, ga, sa, mb, x, w, *, tq, tk):
    b, t, _ = x.shape
    n_q = t // tq

    def row_spec(width):
        return pl.BlockSpec((1, tq, width), lambda bi, qi: (bi, qi, 0))

    def full(a):
        return pl.BlockSpec(a.shape, lambda bi, qi: (0,) * a.ndim)

    k_chunk = np.arange(tk)[None, :, None] // CHUNK
    q_chunk = (np.arange(tk // COL_BLOCK)[:, None, None] * COL_BLOCK
               + np.arange(COL_BLOCK)[None, None, :]) // CHUNK
    bias = jnp.asarray(np.where(k_chunk <= q_chunk, 0.0, NEG_INF), _F32)
    n_cb = N_HEADS * tq // COL_BLOCK

    weights = (w["w_uv"], w["w_om"], w["w_out"], w["post_g"])
    return pl.pallas_call(
        functools.partial(_attn_prompt_kernel, tq=tq, tk=tk),
        grid=(b, t // tq),
        in_specs=[pl.BlockSpec((1, 1, n_cb, KEY_DIM, COL_BLOCK), lambda bi, qi: (bi, qi, 0, 0, 0)),
                  pl.BlockSpec((1, 1, n_cb, KEY_DIM, COL_BLOCK),
                               lambda bi, qi: (bi, jnp.minimum(qi + 1, n_q - 1), 0, 0, 0)),
                  pl.BlockSpec((1, t, KEY_DIM), lambda bi, qi: (bi, 0, 0), pipeline_mode=pl.Buffered(1)),
                  pl.BlockSpec((1, t // MXU_DIM, KV_LORA, MXU_DIM), lambda bi, qi: (bi, 0, 0, 0),
                               pipeline_mode=pl.Buffered(1)),
                  full(bias),
                  row_spec(MLA_WIDTH), row_spec(D_MODEL), row_spec(D_MODEL), row_spec(D_MODEL)]
                 + [full(a) for a in weights],
        out_specs=row_spec(D_MODEL),
        out_shape=jax.ShapeDtypeStruct((b, t, D_MODEL), _F32),
        scratch_shapes=[pltpu.VMEM((n_cb, 1, COL_BLOCK), _F32), pltpu.VMEM((n_cb, 1, COL_BLOCK), _F32),
                        pltpu.VMEM((n_cb, KV_LORA, COL_BLOCK), _F32),
                        pltpu.VMEM((2, n_cb, tk, COL_BLOCK), _F32), pltpu.VMEM((2, n_cb, 1, COL_BLOCK), _F32)],
        compiler_params=pltpu.CompilerParams(
            dimension_semantics=("arbitrary", "arbitrary"), vmem_limit_bytes=VMEM_LIMIT_BYTES),
        name="attn_prompt",
    )(qt, qt, kc, vt, bias, ga, sa, mb, x, *weights)


def _attn_first(q, k, m_ref, l_ref, acc_ref, visible=None):
    s = _dot_nt(q, k)
    if visible is not None:
        s = jnp.where(visible, s, NEG_INF)
    m = jnp.max(s, axis=1, keepdims=True)
    p = jnp.exp2(s - m)
    m_ref[...] = jnp.broadcast_to(m, m_ref.shape)
    l_ref[...] = jnp.broadcast_to(jnp.sum(p, axis=1, keepdims=True), l_ref.shape)
    acc_ref[...] = _dot(p.astype(_BF16), k[:, :KV_LORA])


def _attn_update(q, k_lat, k_rope_t, m_ref, l_ref, acc_ref):
    s = _dot_nt(q[:, :KV_LORA], k_lat) + _dot(q[:, KV_LORA:], k_rope_t)
    m_prev = m_ref[...]
    m_new = jnp.maximum(m_prev, jnp.max(s, axis=1, keepdims=True))
    alpha = jnp.exp2(m_prev - m_new)
    p = jnp.exp2(s - jnp.concatenate([m_new] * (s.shape[1] // LANES), axis=1))
    l_ref[...] = alpha * l_ref[...] + jnp.sum(p, axis=1, keepdims=True)
    acc_ref[...] = alpha * acc_ref[...] + _dot(p.astype(_BF16), k_lat)
    m_ref[...] = m_new


def _attn_sample_kernel(q_ref, past_lat_ref, past_rope_ref, knew_ref, ga_ref, sa_ref, mb_ref, x_ref,
                        w_uv_ref, w_om_ref, w_out_ref, post_g_ref, y_ref,
                        m_ref, l_ref, acc_ref, o_ref, *, new_visible):
    bi = pl.program_id(0)
    t_new = knew_ref.shape[1]
    rows = N_HEADS * t_new
    q = q_ref[0].reshape(rows, KEY_DIM)
    if new_visible is None:
        visible = None
    else:
        q_pos = lax.broadcasted_iota(jnp.int32, (rows, t_new), 0) % t_new
        k_pos = lax.broadcasted_iota(jnp.int32, (rows, t_new), 1)
        visible = (k_pos + new_visible[0]) // CHUNK <= (q_pos + new_visible[0]) // CHUNK
    _attn_first(q, knew_ref[0], m_ref, l_ref, acc_ref, visible=visible)
    _attn_update(q, past_lat_ref[0].astype(_BF16), past_rope_ref[0].astype(_BF16), m_ref, l_ref, acc_ref)
    o = acc_ref[...] / l_ref[...]
    r0 = pl.multiple_of(bi * t_new, t_new)
    o_ref[pl.ds(r0, t_new), :] = jnp.concatenate(
        [o[hd * t_new:(hd + 1) * t_new] for hd in range(N_HEADS)], axis=1)

    @pl.when(bi == pl.num_programs(0) - 1)
    def _():
        y_ref[...] = _epilogue(o_ref[...], ga_ref[...], sa_ref[...], mb_ref[...], x_ref[...],
                               w_uv_ref, w_om_ref, w_out_ref, post_g_ref)


def _attend_sample(q, past_lat, past_rope, k_new, ga, sa, mb, x, w):
    nb, t_new, _ = k_new.shape
    past_len = past_lat.shape[1]
    n_rows = nb * t_new
    rows = N_HEADS * t_new
    last_q, first_q = past_len + t_new - 1, past_len
    new_visible = None if last_q // CHUNK == first_q // CHUNK else (past_len,)

    def full(a):
        return pl.BlockSpec(a.shape, lambda bi: (0,) * a.ndim)

    weights = (w["w_uv"], w["w_om"], w["w_out"], w["post_g"])
    return pl.pallas_call(
        functools.partial(_attn_sample_kernel, new_visible=new_visible),
        grid=(nb,),
        in_specs=[pl.BlockSpec((1, N_HEADS, t_new, KEY_DIM), lambda bi: (0, 0, bi, 0)),
                  pl.BlockSpec((1, past_len, KV_LORA), lambda bi: (bi, 0, 0)),
                  pl.BlockSpec((1, QK_ROPE, past_len), lambda bi: (bi, 0, 0)),
                  pl.BlockSpec((1, t_new, KEY_DIM), lambda bi: (bi, 0, 0)),
                  full(ga), full(sa), full(mb), full(x)] + [full(a) for a in weights],
        out_specs=pl.BlockSpec((n_rows, D_MODEL), lambda bi: (0, 0)),
        out_shape=jax.ShapeDtypeStruct((n_rows, D_MODEL), _F32),
        scratch_shapes=[pltpu.VMEM((rows, LANES), _F32), pltpu.VMEM((rows, LANES), _F32),
                        pltpu.VMEM((rows, KV_LORA), _F32), pltpu.VMEM((n_rows, N_HEADS * KV_LORA), _F32)],
        compiler_params=pltpu.CompilerParams(
            dimension_semantics=("arbitrary",), vmem_limit_bytes=VMEM_LIMIT_BYTES),
        name="attn_sample",
    )(q, past_lat, past_rope, k_new, ga, sa, mb, x, *weights)


def _block_diag(blocks):
    n = len(blocks)
    r, c = blocks[0].shape
    rows = []
    for j, blk in enumerate(blocks):
        rows.append(jnp.pad(blk, ((0, 0), (j * c, (n - 1 - j) * c))))
    return jnp.concatenate(rows, axis=0)


def _prep_weights(pre_norm, w_in, q_norm, w_uq, kv_norm, w_uk, w_uv, w_o_mla, conv_w, w_o_conv, w_out, post_norm):
    assert w_in.shape[1] == _O_TAIL + _TAIL_COLS
    wq = w_uq.reshape(Q_LORA, N_HEADS, QK_NOPE + QK_ROPE)
    w_uqn = wq[:, :, :QK_NOPE].reshape(Q_LORA, N_HEADS * QK_NOPE).astype(_BF16)
    w_uqp = wq[:, :, QK_NOPE:].reshape(Q_LORA, N_HEADS * QK_ROPE).astype(_BF16)
    w_uk_bd = _block_diag([w_uk[:, hd, :].T.astype(_BF16) for hd in range(N_HEADS)])
    return {
        "pre_g": pre_norm.reshape(1, D_MODEL),
        "w_in_t": w_in.T.astype(_BF16),
        "q_g": q_norm.reshape(1, Q_LORA),
        "w_uqn": w_uqn,
        "w_uqp_t": w_uqp.T,
        "w_uq": jnp.concatenate([w_uqn, w_uqp], axis=1),
        "w_uk": w_uk_bd,
        "w_uk_t": w_uk_bd.T,
        "kv_g": kv_norm.reshape(1, KV_LORA),
        "conv_w": conv_w,
        "w_oc": w_o_conv.astype(_BF16),
        "w_uv": _block_diag([w_uv[:, hd, :].astype(_BF16) for hd in range(N_HEADS)]),
        "w_om": w_o_mla.astype(_BF16),
        "w_out": w_out.astype(_BF16),
        "post_g": post_norm.reshape(1, D_MODEL),
    }


def _rope_angles(pos):
    inv = ROPE_BASE ** (-jnp.arange(HALF_ROPE, dtype=_F32) / HALF_ROPE)
    ang = pos.astype(_F32)[:, None] * inv[None, :]
    return jnp.cos(ang), jnp.sin(ang)


def _rope_tables_rows(pos):
    cos, sin = _rope_angles(pos)
    zero = jnp.zeros_like(sin)
    reps = LANES // QK_ROPE
    return (jnp.tile(jnp.concatenate([cos, cos], axis=1), (1, reps)),
            jnp.tile(jnp.concatenate([zero, sin], axis=1), (1, reps)),
            jnp.tile(jnp.concatenate([-sin, zero], axis=1), (1, reps)))


PROMPT_ROW_TILE = 1024
PROMPT_Q_TILE = 512
PROMPT_K_TILE = 512


def kernel(x_prompt, x_sample, cache_kv_latent, cache_k_rope, state_conv, pre_norm, w_in, q_norm, w_uq, kv_norm,
           w_uk, w_uv, w_o_mla, conv_w, w_o_conv, w_out, post_norm):
    depth = pre_norm.shape[0]
    assert depth == 1
    b, t, _ = x_prompt.shape
    nb, t_new, _ = x_sample.shape
    past_len = cache_kv_latent.shape[2]
    lyr = 0
    w = _prep_weights(pre_norm[lyr], w_in[lyr], q_norm[lyr], w_uq[lyr], kv_norm[lyr], w_uk[lyr], w_uv[lyr],
                      w_o_mla[lyr], conv_w[lyr], w_o_conv[lyr], w_out[lyr], post_norm[lyr])

    cos, sin = _rope_angles(jnp.arange(t, dtype=jnp.int32))
    zero_state = jnp.zeros((b, CONV_K - 1, CONV_WIDTH), _F32)
    qt, kc, vt, ckv_p, kpe_t, ga, sa, mb, cv_p = _project_prompt(
        x_prompt, cos.T, sin.T, zero_state, w, tm=PROMPT_ROW_TILE, tq=PROMPT_Q_TILE)
    y_p = _attend_prompt(qt, kc, vt, ga, sa, mb, x_prompt, w, tq=PROMPT_Q_TILE, tk=PROMPT_K_TILE)

    n_rows = nb * t_new
    pos_s = past_len + jnp.arange(t_new, dtype=jnp.int32)
    tabs_s = tuple(jnp.tile(tb, (nb, 1)) for tb in _rope_tables_rows(pos_s))
    xs = x_sample.reshape(1, n_rows, D_MODEL)
    q_s, kc_s, ckv_s, kpe_s, ga_s, sa_s, mb_s, cv_s = _project_sample(
        xs, *tabs_s, state_conv[lyr], w, n_seq=nb)
    y_s = _attend_sample(q_s, cache_kv_latent[lyr], jnp.swapaxes(cache_k_rope[lyr], 1, 2),
                         kc_s.reshape(nb, t_new, KEY_DIM),
                         ga_s[0], sa_s[0], mb_s[0], xs[0], w)

    return (y_p, y_s.reshape(nb, t_new, D_MODEL),
            ckv_p[None], jnp.swapaxes(kpe_t, 1, 2)[None], cv_p[None],
            ckv_s.reshape(1, nb, t_new, KV_LORA), kpe_s.reshape(1, nb, t_new, QK_ROPE), cv_s[None])
```

```python
import functools

import numpy as np
import jax
import jax.numpy as jnp
from jax import lax
from jax.experimental import pallas as pl
from jax.experimental.pallas import tpu as pltpu

N_HEADS = 8
QK_NOPE = 64
QK_ROPE = 32
V_HEAD = 64
Q_LORA = 256
KV_LORA = 128
MLA_WIDTH = N_HEADS * V_HEAD
CONV_WIDTH = 512
CONV_K = 3
D_MODEL = 1024
CHUNK = 64
ROPE_BASE = 10000.0
EPS = 1e-6
SM_SCALE = (QK_NOPE + QK_ROPE) ** -0.5
NEG_INF = -1e30
LOG2E = 1.4426950408889634

LANES = 128
MXU_DIM = 256
COL_BLOCK = MXU_DIM
REDUCE_SLAB = 64
STAGE1_LEAD = 1
HALF_ROPE = QK_ROPE // 2
KEY_DIM = KV_LORA + QK_ROPE
HEADS_PER_ROPE_BLOCK = LANES // QK_ROPE
Q_SCALE = SM_SCALE * LOG2E
VMEM_LIMIT_BYTES = 56 * 1024 * 1024

_O_TAIL = Q_LORA + KV_LORA + QK_ROPE
_QKV_COLS = -(-_O_TAIL // MXU_DIM) * MXU_DIM
_O_GM = 0
_O_CB = _O_GM + MLA_WIDTH
_O_CC = _O_CB + CONV_WIDTH
_O_CX = _O_CC + CONV_WIDTH
_O_GC = _O_CX + CONV_WIDTH
_O_MM = _O_GC + CONV_WIDTH
_O_MC = _O_MM + D_MODEL
_TAIL_COLS = _O_MC + D_MODEL

_F32 = jnp.float32
_BF16 = jnp.bfloat16
_NT = (((1,), (1,)), ((), ()))


def _rms(v, g):
    return v * lax.rsqrt(jnp.mean(v * v, axis=-1, keepdims=True) + EPS) * g


def _silu(v):
    return v * jax.nn.sigmoid(v)


def _dot(a, b):
    return jnp.dot(a, b, preferred_element_type=_F32)


def _dot_nt(a, b):
    return lax.dot_general(a, b, _NT, preferred_element_type=_F32)


def _gates_and_conv(h, w_in_t_ref, st_ref, conv_w_ref, w_oc_ref, ga_ref, sa_ref, mb_ref, cst_ref,
                    ub_ref, carry_ref, *, n_seq, carry_state):
    seq_len = h.shape[0] // n_seq

    def proj(off, n):
        return _dot_nt(h, w_in_t_ref[_O_TAIL + off:_O_TAIL + off + n, :])

    ga_ref[0] = _silu(proj(_O_GM, MLA_WIDTH)).astype(_BF16)
    sa_ref[0] = jax.nn.sigmoid(proj(_O_MM, D_MODEL)).astype(_BF16)

    u = proj(_O_CC, CONV_WIDTH) * proj(_O_CX, CONV_WIDTH)
    w0 = conv_w_ref[0:1, :]
    w1 = conv_w_ref[1:2, :]
    w2 = conv_w_ref[2:3, :]
    if carry_state:
        @pl.when(pl.program_id(1) == 0)
        def _():
            carry_ref[0:CONV_K - 1, :] = st_ref[0]
    convs = []
    for j in range(n_seq):
        u_j = u[j * seq_len:(j + 1) * seq_len]
        prev = carry_ref[0:CONV_K - 1, :] if carry_state else st_ref[j]
        ub_ref[j, 8 - (CONV_K - 1):8, :] = prev
        ub_ref[j, 8:8 + seq_len, :] = u_j
        convs.append(w0 * ub_ref[j, 6:6 + seq_len, :] + w1 * ub_ref[j, 7:7 + seq_len, :] + w2 * u_j)
        new_state = u_j[seq_len - (CONV_K - 1):, :]
        cst_ref[j] = new_state
        if carry_state:
            carry_ref[0:CONV_K - 1, :] = new_state
    conv = convs[0] if n_seq == 1 else jnp.concatenate(convs, axis=0)
    bb = (proj(_O_CB, CONV_WIDTH) * conv * _silu(proj(_O_GC, CONV_WIDTH))).astype(_BF16)
    branch_b = _dot(bb, w_oc_ref[...])
    mb_ref[0] = (jax.nn.sigmoid(proj(_O_MC, D_MODEL)) * branch_b).astype(_BF16)


def _proj_prompt_kernel(x_ref, cos_ref, sin_ref, st_ref, pre_g_ref, w_in_t_ref, q_g_ref,
                        w_qabs_ref, w_uqp_t_ref, kv_g_ref, conv_w_ref, w_oc_ref,
                        qt_ref, kc_ref, vt_ref, ckv_ref, kpet_ref, ga_ref, sa_ref, mb_ref, cst_ref,
                        ub_ref, carry_ref, *, tq):
    tm = x_ref.shape[1]
    h = _rms(x_ref[0], pre_g_ref[...]).astype(_BF16)
    cos_t = cos_ref[...]
    sin_t = sin_ref[...]

    qkv = _dot_nt(h, w_in_t_ref[0:_QKV_COLS, :])
    qn = _rms(qkv[:, :Q_LORA], q_g_ref[...]).astype(_BF16)
    q_abs_t = _dot_nt(w_qabs_ref[...], qn)
    q_pe_t = _dot_nt(w_uqp_t_ref[...], qn).reshape(N_HEADS, QK_ROPE, tm)
    x1 = q_pe_t[:, :HALF_ROPE, :]
    x2 = q_pe_t[:, HALF_ROPE:, :]
    r1 = (x1 * cos_t - x2 * sin_t) * Q_SCALE
    r2 = (x2 * cos_t + x1 * sin_t) * Q_SCALE
    groups = tq // COL_BLOCK
    for j in range(tm // tq):
        for hd in range(N_HEADS):
            for g in range(groups):
                tok = slice(j * tq + g * COL_BLOCK, j * tq + (g + 1) * COL_BLOCK)
                cb = hd * groups + g
                qt_ref[0, j, cb, 0:KV_LORA, :] = (
                    q_abs_t[hd * KV_LORA:(hd + 1) * KV_LORA, tok] * Q_SCALE).astype(_BF16)
                qt_ref[0, j, cb, KV_LORA:KV_LORA + HALF_ROPE, :] = r1[hd][:, tok].astype(_BF16)
                qt_ref[0, j, cb, KV_LORA + HALF_ROPE:KEY_DIM, :] = r2[hd][:, tok].astype(_BF16)

    kvr = qkv[:, Q_LORA:]
    ckv = _rms(kvr[:, :KV_LORA], kv_g_ref[...])
    ckv_ref[0] = ckv
    ckv_t = ckv.T.astype(_BF16)
    for u in range(tm // MXU_DIM):
        vt_ref[0, u] = ckv_t[:, u * MXU_DIM:(u + 1) * MXU_DIM]
    kr_t = kvr[:, KV_LORA:].T
    k1 = kr_t[0:HALF_ROPE]
    k2 = kr_t[HALF_ROPE:QK_ROPE]
    kpe_t = jnp.concatenate([k1 * cos_t - k2 * sin_t, k2 * cos_t + k1 * sin_t], axis=0)
    kpet_ref[0] = kpe_t
    kpe = jnp.concatenate([kpe_t, jnp.zeros((LANES - QK_ROPE, tm), _F32)], axis=0).T
    kc_ref[0, :, 0:KV_LORA] = ckv.astype(_BF16)
    kc_ref[0, :, KV_LORA:KEY_DIM] = kpe[:, :QK_ROPE].astype(_BF16)

    _gates_and_conv(h, w_in_t_ref, st_ref, conv_w_ref, w_oc_ref, ga_ref, sa_ref, mb_ref, cst_ref,
                    ub_ref, carry_ref, n_seq=1, carry_state=True)


def _project_prompt(x, cos_t, sin_t, state, w, *, tm, tq):
    b, t, _ = x.shape

    def rows(width, dtype):
        return (jax.ShapeDtypeStruct((b, t, width), dtype),
                pl.BlockSpec((1, tm, width), lambda bi, ti: (bi, ti, 0)))

    def full(a):
        return pl.BlockSpec(a.shape, lambda bi, ti: (0,) * a.ndim)

    outs = [
        (jax.ShapeDtypeStruct((b, t // tq, N_HEADS * tq // COL_BLOCK, KEY_DIM, COL_BLOCK), _BF16),
         pl.BlockSpec((1, tm // tq, N_HEADS * tq // COL_BLOCK, KEY_DIM, COL_BLOCK),
                      lambda bi, ti: (bi, ti, 0, 0, 0))),
        rows(KEY_DIM, _BF16),
        (jax.ShapeDtypeStruct((b, t // MXU_DIM, KV_LORA, MXU_DIM), _BF16),
         pl.BlockSpec((1, tm // MXU_DIM, KV_LORA, MXU_DIM), lambda bi, ti: (bi, ti, 0, 0))),
        rows(KV_LORA, _F32),
        (jax.ShapeDtypeStruct((b, QK_ROPE, t), _F32),
         pl.BlockSpec((1, QK_ROPE, tm), lambda bi, ti: (bi, 0, ti))),
        rows(MLA_WIDTH, _BF16),
        rows(D_MODEL, _BF16),
        rows(D_MODEL, _BF16),
        (jax.ShapeDtypeStruct((b, CONV_K - 1, CONV_WIDTH), _F32),
         pl.BlockSpec((1, CONV_K - 1, CONV_WIDTH), lambda bi, ti: (bi, 0, 0))),
    ]
    table_spec = pl.BlockSpec((HALF_ROPE, tm), lambda bi, ti: (0, ti))
    weights = (w["pre_g"], w["w_in_t"], w["q_g"], w["w_qabs"], w["w_uqp_t"],
               w["kv_g"], w["conv_w"], w["w_oc"])
    return pl.pallas_call(
        functools.partial(_proj_prompt_kernel, tq=tq),
        grid=(b, t // tm),
        in_specs=[pl.BlockSpec((1, tm, D_MODEL), lambda bi, ti: (bi, ti, 0)),
                  table_spec, table_spec,
                  pl.BlockSpec((1, CONV_K - 1, CONV_WIDTH), lambda bi, ti: (bi, 0, 0))]
                 + [full(a) for a in weights],
        out_specs=[o[1] for o in outs],
        out_shape=[o[0] for o in outs],
        scratch_shapes=[pltpu.VMEM((1, tm + 8, CONV_WIDTH), _F32),
                        pltpu.VMEM((8, CONV_WIDTH), _F32)],
        compiler_params=pltpu.CompilerParams(
            dimension_semantics=("arbitrary", "arbitrary"), vmem_limit_bytes=VMEM_LIMIT_BYTES),
        name="proj_prompt",
    )(x, cos_t, sin_t, state, *weights)


def _proj_sample_kernel(x_ref, cos_ref, sn1_ref, sn2_ref, st_ref, pre_g_ref, w_in_t_ref, q_g_ref,
                        w_qabs_ref, w_uqp_ref, kv_g_ref, conv_w_ref, w_oc_ref,
                        q_ref, kc_ref, ckv_ref, kpe_ref, ga_ref, sa_ref, mb_ref, cst_ref,
                        ub_ref, carry_ref, *, n_seq):
    h = _rms(x_ref[0], pre_g_ref[...]).astype(_BF16)
    cos = cos_ref[...]
    sn1 = sn1_ref[...]
    sn2 = sn2_ref[...]

    def rope(v):
        return (v * cos + pltpu.roll(v, HALF_ROPE, 1) * sn1
                + pltpu.roll(v, LANES - HALF_ROPE, 1) * sn2)

    qkv = _dot_nt(h, w_in_t_ref[0:_QKV_COLS, :])
    qn = _rms(qkv[:, :Q_LORA], q_g_ref[...]).astype(_BF16)
    q_abs = _dot_nt(qn, w_qabs_ref[...])
    q_pe = _dot(qn, w_uqp_ref[...])
    n_rope_blocks = N_HEADS // HEADS_PER_ROPE_BLOCK
    q_rot = [rope(q_pe[:, j * LANES:(j + 1) * LANES]) * Q_SCALE for j in range(n_rope_blocks)]
    for hd in range(N_HEADS):
        q_ref[0, hd, :, 0:KV_LORA] = (q_abs[:, hd * LANES:(hd + 1) * LANES] * Q_SCALE).astype(_BF16)
        grp = hd % HEADS_PER_ROPE_BLOCK
        blk = q_rot[hd // HEADS_PER_ROPE_BLOCK]
        if grp:
            blk = pltpu.roll(blk, LANES - grp * QK_ROPE, 1)
        q_ref[0, hd, :, KV_LORA:KEY_DIM] = blk[:, :QK_ROPE].astype(_BF16)

    kvr = qkv[:, Q_LORA:]
    ckv = _rms(kvr[:, :KV_LORA], kv_g_ref[...])
    k_rot = rope(kvr[:, KV_LORA:])
    ckv_ref[0] = ckv
    kpe_ref[0] = k_rot[:, :QK_ROPE]
    kc_ref[0, :, 0:KV_LORA] = ckv.astype(_BF16)
    kc_ref[0, :, KV_LORA:KEY_DIM] = k_rot[:, :QK_ROPE].astype(_BF16)

    _gates_and_conv(h, w_in_t_ref, st_ref, conv_w_ref, w_oc_ref, ga_ref, sa_ref, mb_ref, cst_ref,
                    ub_ref, carry_ref, n_seq=n_seq, carry_state=False)


def _project_sample(x, cos, sn1, sn2, state, w, *, n_seq):
    _, tm, _ = x.shape

    def rows(width, dtype):
        return (jax.ShapeDtypeStruct((1, tm, width), dtype), pl.BlockSpec((1, tm, width), lambda i: (0, 0, 0)))

    def full(a):
        return pl.BlockSpec(a.shape, lambda i: (0,) * a.ndim)

    outs = [
        (jax.ShapeDtypeStruct((1, N_HEADS, tm, KEY_DIM), _BF16),
         pl.BlockSpec((1, N_HEADS, tm, KEY_DIM), lambda i: (0, 0, 0, 0))),
        rows(KEY_DIM, _BF16), rows(KV_LORA, _F32), rows(QK_ROPE, _F32),
        rows(MLA_WIDTH, _BF16), rows(D_MODEL, _BF16), rows(D_MODEL, _BF16),
        (jax.ShapeDtypeStruct((n_seq, CONV_K - 1, CONV_WIDTH), _F32),
         pl.BlockSpec((n_seq, CONV_K - 1, CONV_WIDTH), lambda i: (0, 0, 0))),
    ]
    weights = (w["pre_g"], w["w_in_t"], w["q_g"], w["w_qabs"], w["w_uqp"],
               w["kv_g"], w["conv_w"], w["w_oc"])
    return pl.pallas_call(
        functools.partial(_proj_sample_kernel, n_seq=n_seq),
        grid=(1,),
        in_specs=[full(x), full(cos), full(sn1), full(sn2), full(state)] + [full(a) for a in weights],
        out_specs=[o[1] for o in outs],
        out_shape=[o[0] for o in outs],
        scratch_shapes=[pltpu.VMEM((n_seq, tm // n_seq + 8, CONV_WIDTH), _F32),
                        pltpu.VMEM((8, CONV_WIDTH), _F32)],
        compiler_params=pltpu.CompilerParams(
            dimension_semantics=("arbitrary",), vmem_limit_bytes=VMEM_LIMIT_BYTES),
        name="proj_sample",
    )(x, cos, sn1, sn2, state, *weights)


def _epilogue(o_lat, ga, sa, mb, x, w_uv_ref, w_om_ref, w_out_ref, post_g_ref):
    o = _dot(o_lat.astype(_BF16), w_uv_ref[...])
    branch_a = _dot((o * ga.astype(_F32)).astype(_BF16), w_om_ref[...])
    merged = sa.astype(_F32) * branch_a + mb.astype(_F32)
    z = _dot(merged.astype(_BF16), w_out_ref[...])
    return x + _rms(z, post_g_ref[...])


def _col_reduce(v, op):
    n_keys, cols = v.shape
    slabs = max(n_keys // REDUCE_SLAB, 1)
    part = op(v.reshape(slabs, n_keys // slabs, cols), axis=0)
    return op(part, axis=0, keepdims=True)


def _attn_prompt_kernel(qt_ref, qn_ref, kc_ref, vt_ref, bias_ref, ga_ref, sa_ref, mb_ref, x_ref,
                        w_uv_ref, w_om_ref, w_out_ref, post_g_ref, y_ref,
                        m_ref, l_ref, acc_ref, s_ref, mc_ref, *, tq, tk):
    i = pl.program_id(1)
    units = tk // MXU_DIM
    q_tiles_per_k_tile = tk // tq
    edge = i // q_tiles_per_k_tile
    groups = tq // COL_BLOCK
    n_cb = N_HEADS * groups

    m_ref[...] = jnp.full(m_ref.shape, NEG_INF, _F32)
    l_ref[...] = jnp.zeros(l_ref.shape, _F32)
    acc_ref[...] = jnp.zeros(acc_ref.shape, _F32)

    def keys_of(tile):
        return kc_ref[0, pl.ds(pl.multiple_of(tile * tk, tk), tk), :]

    def values_of(tile):
        return [vt_ref[0, tile * units + u] for u in range(units)]

    def stage1(k, cb, slot=0, q_tile=None, q_ref=qt_ref, split=True):
        qb = q_ref[0, 0, cb]
        if split:
            s = jnp.concatenate([_dot(k[u * MXU_DIM:(u + 1) * MXU_DIM], qb) for u in range(units)], axis=0)
        else:
            s = _dot(k, qb)
        if q_tile is not None:
            s = s + bias_ref[(q_tile % q_tiles_per_k_tile) * groups + cb % groups]
        s_ref[slot, cb] = s
        mc_ref[slot, cb] = _col_reduce(s, jnp.max)

    def stage2(vts, cb, slot=0):
        m_prev = m_ref[cb]
        m_new = jnp.maximum(m_prev, mc_ref[slot, cb])
        alpha = jnp.exp2(m_prev - m_new)
        p = jnp.exp2(s_ref[slot, cb] - m_new)
        l_ref[cb] = alpha * l_ref[cb] + _col_reduce(p, jnp.sum)
        pb = p.astype(_BF16)
        pv = _dot(vts[0], pb[0:MXU_DIM])
        for u in range(1, units):
            pv = pv + _dot(vts[u], pb[u * MXU_DIM:(u + 1) * MXU_DIM])
        acc_ref[cb] = alpha * acc_ref[cb] + pv
        m_ref[cb] = m_new

    @pl.when(i == 0)
    def _():
        k_edge = keys_of(edge)
        for cb in range(n_cb):
            stage1(k_edge, cb, q_tile=i)

    def advance(j, rd, wr):
        vts = values_of(jnp.where(j == 0, edge, j - 1))
        k = keys_of(j)
        in_place = rd == wr
        lead = 0 if in_place else STAGE1_LEAD
        for cb in range(lead):
            stage1(k, cb, wr, split=in_place)
        for cb in range(n_cb):
            stage2(vts, cb, rd)
            if cb + lead < n_cb:
                stage1(k, cb + lead, wr, split=in_place)

    def single(j, carry):
        advance(j, 0, 0)
        return carry

    def double(pair, carry):
        advance(odd + 2 * pair, 0, 1)
        advance(odd + 2 * pair + 1, 1, 0)
        return carry

    odd = edge % 2
    lax.fori_loop(0, odd, single, 0)
    lax.fori_loop(0, edge // 2, double, 0)

    last_vts = values_of(jnp.where(edge == 0, edge, edge - 1))
    nxt = jnp.minimum(i + 1, pl.num_programs(1) - 1)
    k_next = keys_of(nxt // q_tiles_per_k_tile)
    for cb in range(n_cb):
        stage2(last_vts, cb)
        stage1(k_next, cb, 0, q_tile=nxt, q_ref=qn_ref)

    def head_rows(hd):
        parts = [(acc_ref[hd * groups + g] / l_ref[hd * groups + g]).T for g in range(groups)]
        return parts[0] if groups == 1 else jnp.concatenate(parts, axis=0)

    o_lat = jnp.concatenate([head_rows(hd) for hd in range(N_HEADS)], axis=1)
    y_ref[0] = _epilogue(o_lat, ga_ref[0], sa_ref[0], mb_ref[0], x_ref[0],
                         w_uv_ref, w_om_ref, w_out_ref, post_g_ref)


def _attend_prompt(qt, kc, vt, ga, sa, mb, x, w, *, tq, tk):
    b, t, _ = x.shape
    n_q = t // tq

    def row_spec(width):
        return pl.BlockSpec((1, tq, width), lambda bi, qi: (bi, qi, 0))

    def full(a):
        return pl.BlockSpec(a.shape, lambda bi, qi: (0,) * a.ndim)

    k_chunk = np.arange(tk)[None, :, None] // CHUNK
    q_chunk = (np.arange(tk // COL_BLOCK)[:, None, None] * COL_BLOCK
               + np.arange(COL_BLOCK)[None, None, :]) // CHUNK
    bias = jnp.asarray(np.where(k_chunk <= q_chunk, 0.0, NEG_INF), _F32)
    n_cb = N_HEADS * tq // COL_BLOCK

    weights = (w["w_uv"], w["w_om"], w["w_out"], w["post_g"])
    return pl.pallas_call(
        functools.partial(_attn_prompt_kernel, tq=tq, tk=tk),
        grid=(b, t // tq),
        in_specs=[pl.BlockSpec((1, 1, n_cb, KEY_DIM, COL_BLOCK), lambda bi, qi: (bi, qi, 0, 0, 0)),
                  pl.BlockSpec((1, 1, n_cb, KEY_DIM, COL_BLOCK),
                               lambda bi, qi: (bi, jnp.minimum(qi + 1, n_q - 1), 0, 0, 0)),
                  pl.BlockSpec((1, t, KEY_DIM), lambda bi, qi: (bi, 0, 0), pipeline_mode=pl.Buffered(1)),
                  pl.BlockSpec((1, t // MXU_DIM, KV_LORA, MXU_DIM), lambda bi, qi: (bi, 0, 0, 0),
                               pipeline_mode=pl.Buffered(1)),
                  full(bias),
                  row_spec(MLA_WIDTH), row_spec(D_MODEL), row_spec(D_MODEL), row_spec(D_MODEL)]
                 + [full(a) for a in weights],
        out_specs=row_spec(D_MODEL),
        out_shape=jax.ShapeDtypeStruct((b, t, D_MODEL), _F32),
        scratch_shapes=[pltpu.VMEM((n_cb, 1, COL_BLOCK), _F32), pltpu.VMEM((n_cb, 1, COL_BLOCK), _F32),
                        pltpu.VMEM((n_cb, KV_LORA, COL_BLOCK), _F32),
                        pltpu.VMEM((2, n_cb, tk, COL_BLOCK), _F32), pltpu.VMEM((2, n_cb, 1, COL_BLOCK), _F32)],
        compiler_params=pltpu.CompilerParams(
            dimension_semantics=("arbitrary", "arbitrary"), vmem_limit_bytes=VMEM_LIMIT_BYTES),
        name="attn_prompt",
    )(qt, qt, kc, vt, bias, ga, sa, mb, x, *weights)


def _attn_first(q, k, m_ref, l_ref, acc_ref, visible=None):
    s = _dot_nt(q, k)
    if visible is not None:
        s = jnp.where(visible, s, NEG_INF)
    m = jnp.max(s, axis=1, keepdims=True)
    p = jnp.exp2(s - m)
    m_ref[...] = jnp.broadcast_to(m, m_ref.shape)
    l_ref[...] = jnp.broadcast_to(jnp.sum(p, axis=1, keepdims=True), l_ref.shape)
    acc_ref[...] = _dot(p.astype(_BF16), k[:, :KV_LORA])


def _attn_update(q, k_lat, k_rope_t, m_ref, l_ref, acc_ref):
    s = _dot_nt(q[:, :KV_LORA], k_lat) + _dot(q[:, KV_LORA:], k_rope_t)
    m_prev = m_ref[...]
    m_new = jnp.maximum(m_prev, jnp.max(s, axis=1, keepdims=True))
    alpha = jnp.exp2(m_prev - m_new)
    p = jnp.exp2(s - jnp.concatenate([m_new] * (s.shape[1] // LANES), axis=1))
    l_ref[...] = alpha * l_ref[...] + jnp.sum(p, axis=1, keepdims=True)
    acc_ref[...] = alpha * acc_ref[...] + _dot(p.astype(_BF16), k_lat)
    m_ref[...] = m_new


def _attn_sample_kernel(q_ref, past_lat_ref, past_rope_ref, knew_ref, ga_ref, sa_ref, mb_ref, x_ref,
                        w_uv_ref, w_om_ref, w_out_ref, post_g_ref, y_ref,
                        m_ref, l_ref, acc_ref, o_ref, *, new_visible):
    bi = pl.program_id(0)
    t_new = knew_ref.shape[1]
    rows = N_HEADS * t_new
    q = q_ref[0].reshape(rows, KEY_DIM)
    if new_visible is None:
        visible = None
    else:
        q_pos = lax.broadcasted_iota(jnp.int32, (rows, t_new), 0) % t_new
        k_pos = lax.broadcasted_iota(jnp.int32, (rows, t_new), 1)
        visible = (k_pos + new_visible[0]) // CHUNK <= (q_pos + new_visible[0]) // CHUNK
    _attn_first(q, knew_ref[0], m_ref, l_ref, acc_ref, visible=visible)
    _attn_update(q, past_lat_ref[0].astype(_BF16), past_rope_ref[0].astype(_BF16), m_ref, l_ref, acc_ref)
    o = acc_ref[...] / l_ref[...]
    r0 = pl.multiple_of(bi * t_new, t_new)
    o_ref[pl.ds(r0, t_new), :] = jnp.concatenate(
        [o[hd * t_new:(hd + 1) * t_new] for hd in range(N_HEADS)], axis=1)

    @pl.when(bi == pl.num_programs(0) - 1)
    def _():
        y_ref[...] = _epilogue(o_ref[...], ga_ref[...], sa_ref[...], mb_ref[...], x_ref[...],
                               w_uv_ref, w_om_ref, w_out_ref, post_g_ref)


def _attend_sample(q, past_lat, past_rope, k_new, ga, sa, mb, x, w):
    nb, t_new, _ = k_new.shape
    past_len = past_lat.shape[1]
    n_rows = nb * t_new
    rows = N_HEADS * t_new
    last_q, first_q = past_len + t_new - 1, past_len
    new_visible = None if last_q // CHUNK == first_q // CHUNK else (past_len,)

    def full(a):
        return pl.BlockSpec(a.shape, lambda bi: (0,) * a.ndim)

    weights = (w["w_uv"], w["w_om"], w["w_out"], w["post_g"])
    return pl.pallas_call(
        functools.partial(_attn_sample_kernel, new_visible=new_visible),
        grid=(nb,),
        in_specs=[pl.BlockSpec((1, N_HEADS, t_new, KEY_DIM), lambda bi: (0, 0, bi, 0)),
                  pl.BlockSpec((1, past_len, KV_LORA), lambda bi: (bi, 0, 0)),
                  pl.BlockSpec((1, QK_ROPE, past_len), lambda bi: (bi, 0, 0)),
                  pl.BlockSpec((1, t_new, KEY_DIM), lambda bi: (bi, 0, 0)),
                  full(ga), full(sa), full(mb), full(x)] + [full(a) for a in weights],
        out_specs=pl.BlockSpec((n_rows, D_MODEL), lambda bi: (0, 0)),
        out_shape=jax.ShapeDtypeStruct((n_rows, D_MODEL), _F32),
        scratch_shapes=[pltpu.VMEM((rows, LANES), _F32), pltpu.VMEM((rows, LANES), _F32),
                        pltpu.VMEM((rows, KV_LORA), _F32), pltpu.VMEM((n_rows, N_HEADS * KV_LORA), _F32)],
        compiler_params=pltpu.CompilerParams(
            dimension_semantics=("arbitrary",), vmem_limit_bytes=VMEM_LIMIT_BYTES),
        name="attn_sample",
    )(q, past_lat, past_rope, k_new, ga, sa, mb, x, *weights)


def _block_diag(blocks):
    n = len(blocks)
    r, c = blocks[0].shape
    rows = []
    for j, blk in enumerate(blocks):
        rows.append(jnp.pad(blk, ((0, 0), (j * c, (n - 1 - j) * c))))
    return jnp.concatenate(rows, axis=0)


def _absorb_kernel(w_uk_ref, w_uqn_ref, o_ref):
    for hd in range(N_HEADS):
        o_ref[hd] = lax.dot_general(w_uk_ref[hd], w_uqn_ref[hd], _NT, precision=lax.Precision.HIGHEST,
                                    preferred_element_type=_F32).astype(o_ref.dtype)


def _absorbed_query_weight(w_uk, w_uqn):
    out = pl.pallas_call(
        _absorb_kernel,
        out_shape=jax.ShapeDtypeStruct((N_HEADS, KV_LORA, Q_LORA), _BF16),
        name="absorb_q_weight",
    )(jnp.transpose(w_uk, (1, 0, 2)), jnp.transpose(w_uqn, (1, 0, 2)))
    return out.reshape(N_HEADS * KV_LORA, Q_LORA)


def _prep_weights(pre_norm, w_in, q_norm, w_uq, kv_norm, w_uk, w_uv, w_o_mla, conv_w, w_o_conv, w_out, post_norm):
    assert w_in.shape[1] == _O_TAIL + _TAIL_COLS
    wq = w_uq.reshape(Q_LORA, N_HEADS, QK_NOPE + QK_ROPE)
    w_uqp = wq[:, :, QK_NOPE:].reshape(Q_LORA, N_HEADS * QK_ROPE).astype(_BF16)
    return {
        "pre_g": pre_norm.reshape(1, D_MODEL),
        "w_in_t": w_in.T.astype(_BF16),
        "q_g": q_norm.reshape(1, Q_LORA),
        "w_qabs": _absorbed_query_weight(w_uk, wq[:, :, :QK_NOPE]),
        "w_uqp": w_uqp,
        "w_uqp_t": w_uqp.T,
        "kv_g": kv_norm.reshape(1, KV_LORA),
        "conv_w": conv_w,
        "w_oc": w_o_conv.astype(_BF16),
        "w_uv": _block_diag([w_uv[:, hd, :].astype(_BF16) for hd in range(N_HEADS)]),
        "w_om": w_o_mla.astype(_BF16),
        "w_out": w_out.astype(_BF16),
        "post_g": post_norm.reshape(1, D_MODEL),
    }


def _rope_angles(pos):
    inv = ROPE_BASE ** (-jnp.arange(HALF_ROPE, dtype=_F32) / HALF_ROPE)
    ang = pos.astype(_F32)[:, None] * inv[None, :]
    return jnp.cos(ang), jnp.sin(ang)


def _rope_tables_rows(pos):
    cos, sin = _rope_angles(pos)
    zero = jnp.zeros_like(sin)
    reps = LANES // QK_ROPE
    return (jnp.tile(jnp.concatenate([cos, cos], axis=1), (1, reps)),
            jnp.tile(jnp.concatenate([zero, sin], axis=1), (1, reps)),
            jnp.tile(jnp.concatenate([-sin, zero], axis=1), (1, reps)))


PROMPT_ROW_TILE = 1024
PROMPT_Q_TILE = 512
PROMPT_K_TILE = 512


def kernel(x_prompt, x_sample, cache_kv_latent, cache_k_rope, state_conv, pre_norm, w_in, q_norm, w_uq, kv_norm,
           w_uk, w_uv, w_o_mla, conv_w, w_o_conv, w_out, post_norm):
    depth = pre_norm.shape[0]
    assert depth == 1
    b, t, _ = x_prompt.shape
    nb, t_new, _ = x_sample.shape
    past_len = cache_kv_latent.shape[2]
    lyr = 0
    w = _prep_weights(pre_norm[lyr], w_in[lyr], q_norm[lyr], w_uq[lyr], kv_norm[lyr], w_uk[lyr], w_uv[lyr],
                      w_o_mla[lyr], conv_w[lyr], w_o_conv[lyr], w_out[lyr], post_norm[lyr])

    cos, sin = _rope_angles(jnp.arange(t, dtype=jnp.int32))
    zero_state = jnp.zeros((b, CONV_K - 1, CONV_WIDTH), _F32)
    qt, kc, vt, ckv_p, kpe_t, ga, sa, mb, cv_p = _project_prompt(
        x_prompt, cos.T, sin.T, zero_state, w, tm=PROMPT_ROW_TILE, tq=PROMPT_Q_TILE)
    y_p = _attend_prompt(qt, kc, vt, ga, sa, mb, x_prompt, w, tq=PROMPT_Q_TILE, tk=PROMPT_K_TILE)

    n_rows = nb * t_new
    pos_s = past_len + jnp.arange(t_new, dtype=jnp.int32)
    tabs_s = tuple(jnp.tile(tb, (nb, 1)) for tb in _rope_tables_rows(pos_s))
    xs = x_sample.reshape(1, n_rows, D_MODEL)
    q_s, kc_s, ckv_s, kpe_s, ga_s, sa_s, mb_s, cv_s = _project_sample(
        xs, *tabs_s, state_conv[lyr], w, n_seq=nb)
    y_s = _attend_sample(q_s, cache_kv_latent[lyr], jnp.swapaxes(cache_k_rope[lyr], 1, 2),
                         kc_s.reshape(nb, t_new, KEY_DIM),
                         ga_s[0], sa_s[0], mb_s[0], xs[0], w)

    return (y_p, y_s.reshape(nb, t_new, D_MODEL),
            ckv_p[None], jnp.swapaxes(kpe_t, 1, 2)[None], cv_p[None],
            ckv_s.reshape(1, nb, t_new, KV_LORA), kpe_s.reshape(1, nb, t_new, QK_ROPE), cv_s[None])
```

```python
import functools

import numpy as np
import jax
import jax.numpy as jnp
from jax import lax
from jax.experimental import pallas as pl
from jax.experimental.pallas import tpu as pltpu

N_HEADS = 8
QK_NOPE = 64
QK_ROPE = 32
V_HEAD = 64
Q_LORA = 256
KV_LORA = 128
MLA_WIDTH = N_HEADS * V_HEAD
CONV_WIDTH = 512
CONV_K = 3
D_MODEL = 1024
CHUNK = 64
ROPE_BASE = 10000.0
EPS = 1e-6
SM_SCALE = (QK_NOPE + QK_ROPE) ** -0.5
NEG_INF = -1e30
LOG2E = 1.4426950408889634

LANES = 128
MXU_DIM = 256
COL_BLOCK = MXU_DIM
REDUCE_SLAB = 64
STAGE1_LEAD = 1
HALF_ROPE = QK_ROPE // 2
KEY_DIM = KV_LORA + QK_ROPE
HEADS_PER_ROPE_BLOCK = LANES // QK_ROPE
Q_SCALE = SM_SCALE * LOG2E
VMEM_LIMIT_BYTES = 56 * 1024 * 1024

_O_TAIL = Q_LORA + KV_LORA + QK_ROPE
_QKV_COLS = -(-_O_TAIL // MXU_DIM) * MXU_DIM
_O_GM = 0
_O_CB = _O_GM + MLA_WIDTH
_O_CC = _O_CB + CONV_WIDTH
_O_CX = _O_CC + CONV_WIDTH
_O_GC = _O_CX + CONV_WIDTH
_O_MM = _O_GC + CONV_WIDTH
_O_MC = _O_MM + D_MODEL
_TAIL_COLS = _O_MC + D_MODEL

_F32 = jnp.float32
_BF16 = jnp.bfloat16
_NT = (((1,), (1,)), ((), ()))


def _rms(v, g):
    return v * lax.rsqrt(jnp.mean(v * v, axis=-1, keepdims=True) + EPS) * g


def _silu(v):
    return v * jax.nn.sigmoid(v)


def _dot(a, b):
    return jnp.dot(a, b, preferred_element_type=_F32)


def _dot_nt(a, b):
    return lax.dot_general(a, b, _NT, preferred_element_type=_F32)


def _tail_proj(h, w_in_t_ref):
    def proj(off, n):
        return _dot_nt(h, w_in_t_ref[_O_TAIL + off:_O_TAIL + off + n, :])
    return proj


def _conv_branch(proj, st_ref, conv_w_ref, w_oc_ref, mb_ref, cst_ref, ub_ref, carry_ref, *,
                 tm, n_seq, carry_state):
    seq_len = tm // n_seq
    u = proj(_O_CC, CONV_WIDTH) * proj(_O_CX, CONV_WIDTH)
    w0 = conv_w_ref[0:1, :]
    w1 = conv_w_ref[1:2, :]
    w2 = conv_w_ref[2:3, :]
    if carry_state:
        @pl.when(pl.program_id(1) == 0)
        def _():
            carry_ref[0:CONV_K - 1, :] = st_ref[0]
    convs = []
    for j in range(n_seq):
        u_j = u[j * seq_len:(j + 1) * seq_len]
        prev = carry_ref[0:CONV_K - 1, :] if carry_state else st_ref[j]
        ub_ref[j, 8 - (CONV_K - 1):8, :] = prev
        ub_ref[j, 8:8 + seq_len, :] = u_j
        convs.append(w0 * ub_ref[j, 6:6 + seq_len, :] + w1 * ub_ref[j, 7:7 + seq_len, :] + w2 * u_j)
        new_state = u_j[seq_len - (CONV_K - 1):, :]
        cst_ref[j] = new_state
        if carry_state:
            carry_ref[0:CONV_K - 1, :] = new_state
    conv = convs[0] if n_seq == 1 else jnp.concatenate(convs, axis=0)
    c_b = proj(_O_CB, CONV_WIDTH)
    g_conv = proj(_O_GC, CONV_WIDTH)
    merge_gate = jax.nn.sigmoid(proj(_O_MC, D_MODEL))
    bb = (c_b * conv * _silu(g_conv)).astype(_BF16)
    branch_b = _dot(bb, w_oc_ref[...])
    mb_ref[0] = (merge_gate * branch_b).astype(_BF16)


def _proj_prompt_kernel(x_ref, cos_ref, sin_ref, st_ref, pre_g_ref, w_in_t_ref, q_g_ref,
                        w_qabs_ref, w_uqp_t_ref, kv_g_ref, conv_w_ref, w_oc_ref,
                        qt_ref, kc_ref, vt_ref, ckv_ref, kpet_ref, ga_ref, sa_ref, mb_ref, cst_ref,
                        ub_ref, carry_ref, *, tq):
    tm = x_ref.shape[1]
    h = _rms(x_ref[0], pre_g_ref[...]).astype(_BF16)
    cos_t = cos_ref[...]
    sin_t = sin_ref[...]

    qkv = _dot_nt(h, w_in_t_ref[0:_QKV_COLS, :])
    proj = _tail_proj(h, w_in_t_ref)
    sa_ref[0] = jax.nn.sigmoid(proj(_O_MM, D_MODEL)).astype(_BF16)
    ga_ref[0] = _silu(proj(_O_GM, MLA_WIDTH)).astype(_BF16)
    qn = _rms(qkv[:, :Q_LORA], q_g_ref[...]).astype(_BF16)
    q_abs_t = _dot_nt(w_qabs_ref[...], qn)
    q_pe_t = _dot_nt(w_uqp_t_ref[...], qn).reshape(N_HEADS, QK_ROPE, tm)
    x1 = q_pe_t[:, :HALF_ROPE, :]
    x2 = q_pe_t[:, HALF_ROPE:, :]
    r1 = (x1 * cos_t - x2 * sin_t) * Q_SCALE
    r2 = (x2 * cos_t + x1 * sin_t) * Q_SCALE
    groups = tq // COL_BLOCK
    for j in range(tm // tq):
        for hd in range(N_HEADS):
            for g in range(groups):
                tok = slice(j * tq + g * COL_BLOCK, j * tq + (g + 1) * COL_BLOCK)
                cb = hd * groups + g
                qt_ref[0, j, cb, 0:KV_LORA, :] = (
                    q_abs_t[hd * KV_LORA:(hd + 1) * KV_LORA, tok] * Q_SCALE).astype(_BF16)
                qt_ref[0, j, cb, KV_LORA:KV_LORA + HALF_ROPE, :] = r1[hd][:, tok].astype(_BF16)
                qt_ref[0, j, cb, KV_LORA + HALF_ROPE:KEY_DIM, :] = r2[hd][:, tok].astype(_BF16)

    kvr = qkv[:, Q_LORA:]
    ckv = _rms(kvr[:, :KV_LORA], kv_g_ref[...])
    ckv_ref[0] = ckv
    ckv_t = ckv.T.astype(_BF16)
    for u in range(tm // MXU_DIM):
        vt_ref[0, u] = ckv_t[:, u * MXU_DIM:(u + 1) * MXU_DIM]
    kr_t = kvr[:, KV_LORA:].T
    k1 = kr_t[0:HALF_ROPE]
    k2 = kr_t[HALF_ROPE:QK_ROPE]
    kpe_t = jnp.concatenate([k1 * cos_t - k2 * sin_t, k2 * cos_t + k1 * sin_t], axis=0)
    kpet_ref[0] = kpe_t
    kpe = jnp.concatenate([kpe_t, jnp.zeros((LANES - QK_ROPE, tm), _F32)], axis=0).T
    kc_ref[0, :, 0:KV_LORA] = ckv.astype(_BF16)
    kc_ref[0, :, KV_LORA:KEY_DIM] = kpe[:, :QK_ROPE].astype(_BF16)

    _conv_branch(proj, st_ref, conv_w_ref, w_oc_ref, mb_ref, cst_ref, ub_ref, carry_ref,
                 tm=tm, n_seq=1, carry_state=True)


def _project_prompt(x, cos_t, sin_t, state, w, *, tm, tq):
    b, t, _ = x.shape

    def rows(width, dtype):
        return (jax.ShapeDtypeStruct((b, t, width), dtype),
                pl.BlockSpec((1, tm, width), lambda bi, ti: (bi, ti, 0)))

    def full(a):
        return pl.BlockSpec(a.shape, lambda bi, ti: (0,) * a.ndim)

    outs = [
        (jax.ShapeDtypeStruct((b, t // tq, N_HEADS * tq // COL_BLOCK, KEY_DIM, COL_BLOCK), _BF16),
         pl.BlockSpec((1, tm // tq, N_HEADS * tq // COL_BLOCK, KEY_DIM, COL_BLOCK),
                      lambda bi, ti: (bi, ti, 0, 0, 0))),
        rows(KEY_DIM, _BF16),
        (jax.ShapeDtypeStruct((b, t // MXU_DIM, KV_LORA, MXU_DIM), _BF16),
         pl.BlockSpec((1, tm // MXU_DIM, KV_LORA, MXU_DIM), lambda bi, ti: (bi, ti, 0, 0))),
        rows(KV_LORA, _F32),
        (jax.ShapeDtypeStruct((b, QK_ROPE, t), _F32),
         pl.BlockSpec((1, QK_ROPE, tm), lambda bi, ti: (bi, 0, ti))),
        rows(MLA_WIDTH, _BF16),
        rows(D_MODEL, _BF16),
        rows(D_MODEL, _BF16),
        (jax.ShapeDtypeStruct((b, CONV_K - 1, CONV_WIDTH), _F32),
         pl.BlockSpec((1, CONV_K - 1, CONV_WIDTH), lambda bi, ti: (bi, 0, 0))),
    ]
    table_spec = pl.BlockSpec((HALF_ROPE, tm), lambda bi, ti: (0, ti))
    weights = (w["pre_g"], w["w_in_t"], w["q_g"], w["w_qabs"], w["w_uqp_t"],
               w["kv_g"], w["conv_w"], w["w_oc"])
    return pl.pallas_call(
        functools.partial(_proj_prompt_kernel, tq=tq),
        grid=(b, t // tm),
        in_specs=[pl.BlockSpec((1, tm, D_MODEL), lambda bi, ti: (bi, ti, 0)),
                  table_spec, table_spec,
                  pl.BlockSpec((1, CONV_K - 1, CONV_WIDTH), lambda bi, ti: (bi, 0, 0))]
                 + [full(a) for a in weights],
        out_specs=[o[1] for o in outs],
        out_shape=[o[0] for o in outs],
        scratch_shapes=[pltpu.VMEM((1, tm + 8, CONV_WIDTH), _F32),
                        pltpu.VMEM((8, CONV_WIDTH), _F32)],
        compiler_params=pltpu.CompilerParams(
            dimension_semantics=("arbitrary", "arbitrary"), vmem_limit_bytes=VMEM_LIMIT_BYTES),
        name="proj_prompt",
    )(x, cos_t, sin_t, state, *weights)


def _proj_sample_kernel(x_ref, cos_ref, sn1_ref, sn2_ref, st_ref, pre_g_ref, w_in_t_ref, q_g_ref,
                        w_qabs_ref, w_uqp_ref, kv_g_ref, conv_w_ref, w_oc_ref,
                        q_ref, kc_ref, ckv_ref, kpe_ref, ga_ref, sa_ref, mb_ref, cst_ref,
                        ub_ref, carry_ref, *, n_seq):
    h = _rms(x_ref[0], pre_g_ref[...]).astype(_BF16)
    cos = cos_ref[...]
    sn1 = sn1_ref[...]
    sn2 = sn2_ref[...]

    def rope(v):
        return (v * cos + pltpu.roll(v, HALF_ROPE, 1) * sn1
                + pltpu.roll(v, LANES - HALF_ROPE, 1) * sn2)

    qkv = _dot_nt(h, w_in_t_ref[0:_QKV_COLS, :])
    qn = _rms(qkv[:, :Q_LORA], q_g_ref[...]).astype(_BF16)
    q_abs = _dot_nt(qn, w_qabs_ref[...])
    q_pe = _dot(qn, w_uqp_ref[...])
    n_rope_blocks = N_HEADS // HEADS_PER_ROPE_BLOCK
    q_rot = [rope(q_pe[:, j * LANES:(j + 1) * LANES]) * Q_SCALE for j in range(n_rope_blocks)]
    for hd in range(N_HEADS):
        q_ref[0, hd, :, 0:KV_LORA] = (q_abs[:, hd * LANES:(hd + 1) * LANES] * Q_SCALE).astype(_BF16)
        grp = hd % HEADS_PER_ROPE_BLOCK
        blk = q_rot[hd // HEADS_PER_ROPE_BLOCK]
        if grp:
            blk = pltpu.roll(blk, LANES - grp * QK_ROPE, 1)
        q_ref[0, hd, :, KV_LORA:KEY_DIM] = blk[:, :QK_ROPE].astype(_BF16)

    kvr = qkv[:, Q_LORA:]
    ckv = _rms(kvr[:, :KV_LORA], kv_g_ref[...])
    k_rot = rope(kvr[:, KV_LORA:])
    ckv_ref[0] = ckv
    kpe_ref[0] = k_rot[:, :QK_ROPE]
    kc_ref[0, :, 0:KV_LORA] = ckv.astype(_BF16)
    kc_ref[0, :, KV_LORA:KEY_DIM] = k_rot[:, :QK_ROPE].astype(_BF16)

    proj = _tail_proj(h, w_in_t_ref)
    sa_ref[0] = jax.nn.sigmoid(proj(_O_MM, D_MODEL)).astype(_BF16)
    ga_ref[0] = _silu(proj(_O_GM, MLA_WIDTH)).astype(_BF16)
    _conv_branch(proj, st_ref, conv_w_ref, w_oc_ref, mb_ref, cst_ref, ub_ref, carry_ref,
                 tm=x_ref.shape[1], n_seq=n_seq, carry_state=False)


def _project_sample(x, cos, sn1, sn2, state, w, *, n_seq):
    _, tm, _ = x.shape

    def rows(width, dtype):
        return (jax.ShapeDtypeStruct((1, tm, width), dtype), pl.BlockSpec((1, tm, width), lambda i: (0, 0, 0)))

    def full(a):
        return pl.BlockSpec(a.shape, lambda i: (0,) * a.ndim)

    outs = [
        (jax.ShapeDtypeStruct((1, N_HEADS, tm, KEY_DIM), _BF16),
         pl.BlockSpec((1, N_HEADS, tm, KEY_DIM), lambda i: (0, 0, 0, 0))),
        rows(KEY_DIM, _BF16), rows(KV_LORA, _F32), rows(QK_ROPE, _F32),
        rows(MLA_WIDTH, _BF16), rows(D_MODEL, _BF16), rows(D_MODEL, _BF16),
        (jax.ShapeDtypeStruct((n_seq, CONV_K - 1, CONV_WIDTH), _F32),
         pl.BlockSpec((n_seq, CONV_K - 1, CONV_WIDTH), lambda i: (0, 0, 0))),
    ]
    weights = (w["pre_g"], w["w_in_t"], w["q_g"], w["w_qabs"], w["w_uqp"],
               w["kv_g"], w["conv_w"], w["w_oc"])
    return pl.pallas_call(
        functools.partial(_proj_sample_kernel, n_seq=n_seq),
        grid=(1,),
        in_specs=[full(x), full(cos), full(sn1), full(sn2), full(state)] + [full(a) for a in weights],
        out_specs=[o[1] for o in outs],
        out_shape=[o[0] for o in outs],
        scratch_shapes=[pltpu.VMEM((n_seq, tm // n_seq + 8, CONV_WIDTH), _F32),
                        pltpu.VMEM((8, CONV_WIDTH), _F32)],
        compiler_params=pltpu.CompilerParams(
            dimension_semantics=("arbitrary",), vmem_limit_bytes=VMEM_LIMIT_BYTES),
        name="proj_sample",
    )(x, cos, sn1, sn2, state, *weights)


def _epilogue(o_lat, ga, sa, mb, x, w_uv_ref, w_om_ref, w_out_ref, post_g_ref):
    o = _dot(o_lat.astype(_BF16), w_uv_ref[...])
    branch_a = _dot((o * ga.astype(_F32)).astype(_BF16), w_om_ref[...])
    merged = sa.astype(_F32) * branch_a + mb.astype(_F32)
    z = _dot(merged.astype(_BF16), w_out_ref[...])
    return x + _rms(z, post_g_ref[...])


def _col_reduce(v, op):
    n_keys, cols = v.shape
    slabs = max(n_keys // REDUCE_SLAB, 1)
    part = op(v.reshape(slabs, n_keys // slabs, cols), axis=0)
    return op(part, axis=0, keepdims=True)


def _attn_prompt_kernel(qt_ref, qn_ref, kc_ref, vt_ref, bias_ref, ga_ref, sa_ref, mb_ref, x_ref,
                        w_uv_ref, w_om_ref, w_out_ref, post_g_ref, y_ref,
                        m_ref, l_ref, acc_ref, s_ref, mc_ref, *, tq, tk):
    i = pl.program_id(1)
    units = tk // MXU_DIM
    q_tiles_per_k_tile = tk // tq
    edge = i // q_tiles_per_k_tile
    groups = tq // COL_BLOCK
    n_cb = N_HEADS * groups

    m_ref[...] = jnp.full(m_ref.shape, NEG_INF, _F32)
    l_ref[...] = jnp.zeros(l_ref.shape, _F32)
    acc_ref[...] = jnp.zeros(acc_ref.shape, _F32)

    def keys_of(tile):
        return kc_ref[0, pl.ds(pl.multiple_of(tile * tk, tk), tk), :]

    def values_of(tile):
        return [vt_ref[0, tile * units + u] for u in range(units)]

    def stage1(k, cb, slot=0, q_tile=None, q_ref=qt_ref, split=True):
        qb = q_ref[0, 0, cb]
        if split:
            s = jnp.concatenate([_dot(k[u * MXU_DIM:(u + 1) * MXU_DIM], qb) for u in range(units)], axis=0)
        else:
            s = _dot(k, qb)
        if q_tile is not None:
            s = s + bias_ref[(q_tile % q_tiles_per_k_tile) * groups + cb % groups]
        s_ref[slot, cb] = s
        mc_ref[slot, cb] = _col_reduce(s, jnp.max)

    def stage2(vts, cb, slot=0):
        m_prev = m_ref[cb]
        m_new = jnp.maximum(m_prev, mc_ref[slot, cb])
        alpha = jnp.exp2(m_prev - m_new)
        p = jnp.exp2(s_ref[slot, cb] - m_new)
        l_ref[cb] = alpha * l_ref[cb] + _col_reduce(p, jnp.sum)
        pb = p.astype(_BF16)
        pv = _dot(vts[0], pb[0:MXU_DIM])
        for u in range(1, units):
            pv = pv + _dot(vts[u], pb[u * MXU_DIM:(u + 1) * MXU_DIM])
        acc_ref[cb] = alpha * acc_ref[cb] + pv
        m_ref[cb] = m_new

    @pl.when(i == 0)
    def _():
        k_edge = keys_of(edge)
        for cb in range(n_cb):
            stage1(k_edge, cb, q_tile=i)

    def advance(j, rd, wr):
        vts = values_of(jnp.where(j == 0, edge, j - 1))
        k = keys_of(j)
        in_place = rd == wr
        lead = 0 if in_place else STAGE1_LEAD
        for cb in range(lead):
            stage1(k, cb, wr, split=in_place)
        for cb in range(n_cb):
            stage2(vts, cb, rd)
            if cb + lead < n_cb:
                stage1(k, cb + lead, wr, split=in_place)

    def single(j, carry):
        advance(j, 0, 0)
        return carry

    def double(pair, carry):
        advance(odd + 2 * pair, 0, 1)
        advance(odd + 2 * pair + 1, 1, 0)
        return carry

    odd = edge % 2
    lax.fori_loop(0, odd, single, 0)
    lax.fori_loop(0, edge // 2, double, 0)

    last_vts = values_of(jnp.where(edge == 0, edge, edge - 1))
    nxt = jnp.minimum(i + 1, pl.num_programs(1) - 1)
    k_next = keys_of(nxt // q_tiles_per_k_tile)
    for cb in range(n_cb):
        stage2(last_vts, cb)
        stage1(k_next, cb, 0, q_tile=nxt, q_ref=qn_ref)

    def head_rows(hd):
        parts = [(acc_ref[hd * groups + g] / l_ref[hd * groups + g]).T for g in range(groups)]
        return parts[0] if groups == 1 else jnp.concatenate(parts, axis=0)

    o_lat = jnp.concatenate([head_rows(hd) for hd in range(N_HEADS)], axis=1)
    y_ref[0] = _epilogue(o_lat, ga_ref[0], sa_ref[0], mb_ref[0], x_ref[0],
                         w_uv_ref, w_om_ref, w_out_ref, post_g_ref)


def _attend_prompt(qt, kc, vt, ga, sa, mb, x, w, *, tq, tk):
    b, t, _ = x.shape
    n_q = t // tq

    def row_spec(width):
        return pl.BlockSpec((1, tq, width), lambda bi, qi: (bi, qi, 0))

    def full(a):
        return pl.BlockSpec(a.shape, lambda bi, qi: (0,) * a.ndim)

    k_chunk = np.arange(tk)[None, :, None] // CHUNK
    q_chunk = (np.arange(tk // COL_BLOCK)[:, None, None] * COL_BLOCK
               + np.arange(COL_BLOCK)[None, None, :]) // CHUNK
    bias = jnp.asarray(np.where(k_chunk <= q_chunk, 0.0, NEG_INF), _F32)
    n_cb = N_HEADS * tq // COL_BLOCK

    weights = (w["w_uv"], w["w_om"], w["w_out"], w["post_g"])
    return pl.pallas_call(
        functools.partial(_attn_prompt_kernel, tq=tq, tk=tk),
        grid=(b, t // tq),
        in_specs=[pl.BlockSpec((1, 1, n_cb, KEY_DIM, COL_BLOCK), lambda bi, qi: (bi, qi, 0, 0, 0)),
                  pl.BlockSpec((1, 1, n_cb, KEY_DIM, COL_BLOCK),
                               lambda bi, qi: (bi, jnp.minimum(qi + 1, n_q - 1), 0, 0, 0)),
                  pl.BlockSpec((1, t, KEY_DIM), lambda bi, qi: (bi, 0, 0), pipeline_mode=pl.Buffered(1)),
                  pl.BlockSpec((1, t // MXU_DIM, KV_LORA, MXU_DIM), lambda bi, qi: (bi, 0, 0, 0),
                               pipeline_mode=pl.Buffered(1)),
                  full(bias),
                  row_spec(MLA_WIDTH), row_spec(D_MODEL), row_spec(D_MODEL), row_spec(D_MODEL)]
                 + [full(a) for a in weights],
        out_specs=row_spec(D_MODEL),
        out_shape=jax.ShapeDtypeStruct((b, t, D_MODEL), _F32),
        scratch_shapes=[pltpu.VMEM((n_cb, 1, COL_BLOCK), _F32), pltpu.VMEM((n_cb, 1, COL_BLOCK), _F32),
                        pltpu.VMEM((n_cb, KV_LORA, COL_BLOCK), _F32),
                        pltpu.VMEM((2, n_cb, tk, COL_BLOCK), _F32), pltpu.VMEM((2, n_cb, 1, COL_BLOCK), _F32)],
        compiler_params=pltpu.CompilerParams(
            dimension_semantics=("arbitrary", "arbitrary"), vmem_limit_bytes=VMEM_LIMIT_BYTES),
        name="attn_prompt",
    )(qt, qt, kc, vt, bias, ga, sa, mb, x, *weights)


def _attn_first(q, k, m_ref, l_ref, acc_ref, visible=None):
    s = _dot_nt(q, k)
    if visible is not None:
        s = jnp.where(visible, s, NEG_INF)
    m = jnp.max(s, axis=1, keepdims=True)
    p = jnp.exp2(s - m)
    m_ref[...] = jnp.broadcast_to(m, m_ref.shape)
    l_ref[...] = jnp.broadcast_to(jnp.sum(p, axis=1, keepdims=True), l_ref.shape)
    acc_ref[...] = _dot(p.astype(_BF16), k[:, :KV_LORA])


def _attn_update(q, k_lat, k_rope_t, m_ref, l_ref, acc_ref):
    s = _dot_nt(q[:, :KV_LORA], k_lat) + _dot(q[:, KV_LORA:], k_rope_t)
    m_prev = m_ref[...]
    m_new = jnp.maximum(m_prev, jnp.max(s, axis=1, keepdims=True))
    alpha = jnp.exp2(m_prev - m_new)
    p = jnp.exp2(s - jnp.concatenate([m_new] * (s.shape[1] // LANES), axis=1))
    l_ref[...] = alpha * l_ref[...] + jnp.sum(p, axis=1, keepdims=True)
    acc_ref[...] = alpha * acc_ref[...] + _dot(p.astype(_BF16), k_lat)
    m_ref[...] = m_new


def _attn_sample_kernel(q_ref, past_lat_ref, past_rope_ref, knew_ref, ga_ref, sa_ref, mb_ref, x_ref,
                        w_uv_ref, w_om_ref, w_out_ref, post_g_ref, y_ref,
                        m_ref, l_ref, acc_ref, o_ref, *, new_visible):
    bi = pl.program_id(0)
    t_new = knew_ref.shape[1]
    rows = N_HEADS * t_new
    q = q_ref[0].reshape(rows, KEY_DIM)
    if new_visible is None:
        visible = None
    else:
        q_pos = lax.broadcasted_iota(jnp.int32, (rows, t_new), 0) % t_new
        k_pos = lax.broadcasted_iota(jnp.int32, (rows, t_new), 1)
        visible = (k_pos + new_visible[0]) // CHUNK <= (q_pos + new_visible[0]) // CHUNK
    _attn_first(q, knew_ref[0], m_ref, l_ref, acc_ref, visible=visible)
    _attn_update(q, past_lat_ref[0].astype(_BF16), past_rope_ref[0].astype(_BF16), m_ref, l_ref, acc_ref)
    o = acc_ref[...] / l_ref[...]
    r0 = pl.multiple_of(bi * t_new, t_new)
    o_ref[pl.ds(r0, t_new), :] = jnp.concatenate(
        [o[hd * t_new:(hd + 1) * t_new] for hd in range(N_HEADS)], axis=1)

    @pl.when(bi == pl.num_programs(0) - 1)
    def _():
        y_ref[...] = _epilogue(o_ref[...], ga_ref[...], sa_ref[...], mb_ref[...], x_ref[...],
                               w_uv_ref, w_om_ref, w_out_ref, post_g_ref)


def _attend_sample(q, past_lat, past_rope, k_new, ga, sa, mb, x, w):
    nb, t_new, _ = k_new.shape
    past_len = past_lat.shape[1]
    n_rows = nb * t_new
    rows = N_HEADS * t_new
    last_q, first_q = past_len + t_new - 1, past_len
    new_visible = None if last_q // CHUNK == first_q // CHUNK else (past_len,)

    def full(a):
        return pl.BlockSpec(a.shape, lambda bi: (0,) * a.ndim)

    weights = (w["w_uv"], w["w_om"], w["w_out"], w["post_g"])
    return pl.pallas_call(
        functools.partial(_attn_sample_kernel, new_visible=new_visible),
        grid=(nb,),
        in_specs=[pl.BlockSpec((1, N_HEADS, t_new, KEY_DIM), lambda bi: (0, 0, bi, 0)),
                  pl.BlockSpec((1, past_len, KV_LORA), lambda bi: (bi, 0, 0)),
                  pl.BlockSpec((1, QK_ROPE, past_len), lambda bi: (bi, 0, 0)),
                  pl.BlockSpec((1, t_new, KEY_DIM), lambda bi: (bi, 0, 0)),
                  full(ga), full(sa), full(mb), full(x)] + [full(a) for a in weights],
        out_specs=pl.BlockSpec((n_rows, D_MODEL), lambda bi: (0, 0)),
        out_shape=jax.ShapeDtypeStruct((n_rows, D_MODEL), _F32),
        scratch_shapes=[pltpu.VMEM((rows, LANES), _F32), pltpu.VMEM((rows, LANES), _F32),
                        pltpu.VMEM((rows, KV_LORA), _F32), pltpu.VMEM((n_rows, N_HEADS * KV_LORA), _F32)],
        compiler_params=pltpu.CompilerParams(
            dimension_semantics=("arbitrary",), vmem_limit_bytes=VMEM_LIMIT_BYTES),
        name="attn_sample",
    )(q, past_lat, past_rope, k_new, ga, sa, mb, x, *weights)


def _block_diag(blocks):
    n = len(blocks)
    r, c = blocks[0].shape
    rows = []
    for j, blk in enumerate(blocks):
        rows.append(jnp.pad(blk, ((0, 0), (j * c, (n - 1 - j) * c))))
    return jnp.concatenate(rows, axis=0)


def _absorb_kernel(w_uk_ref, w_uqn_ref, o_ref):
    for hd in range(N_HEADS):
        o_ref[hd] = lax.dot_general(w_uk_ref[hd], w_uqn_ref[hd], _NT, precision=lax.Precision.HIGHEST,
                                    preferred_element_type=_F32).astype(o_ref.dtype)


def _absorbed_query_weight(w_uk, w_uqn):
    out = pl.pallas_call(
        _absorb_kernel,
        out_shape=jax.ShapeDtypeStruct((N_HEADS, KV_LORA, Q_LORA), _BF16),
        name="absorb_q_weight",
    )(jnp.transpose(w_uk, (1, 0, 2)), jnp.transpose(w_uqn, (1, 0, 2)))
    return out.reshape(N_HEADS * KV_LORA, Q_LORA)


def _prep_weights(pre_norm, w_in, q_norm, w_uq, kv_norm, w_uk, w_uv, w_o_mla, conv_w, w_o_conv, w_out, post_norm):
    assert w_in.shape[1] == _O_TAIL + _TAIL_COLS
    wq = w_uq.reshape(Q_LORA, N_HEADS, QK_NOPE + QK_ROPE)
    w_uqp = wq[:, :, QK_NOPE:].reshape(Q_LORA, N_HEADS * QK_ROPE).astype(_BF16)
    return {
        "pre_g": pre_norm.reshape(1, D_MODEL),
        "w_in_t": w_in.T.astype(_BF16),
        "q_g": q_norm.reshape(1, Q_LORA),
        "w_qabs": _absorbed_query_weight(w_uk, wq[:, :, :QK_NOPE]),
        "w_uqp": w_uqp,
        "w_uqp_t": w_uqp.T,
        "kv_g": kv_norm.reshape(1, KV_LORA),
        "conv_w": conv_w,
        "w_oc": w_o_conv.astype(_BF16),
        "w_uv": _block_diag([w_uv[:, hd, :].astype(_BF16) for hd in range(N_HEADS)]),
        "w_om": w_o_mla.astype(_BF16),
        "w_out": w_out.astype(_BF16),
        "post_g": post_norm.reshape(1, D_MODEL),
    }


def _rope_angles(pos, frequency_major=False):
    inv = ROPE_BASE ** (-jnp.arange(HALF_ROPE, dtype=_F32) / HALF_ROPE)
    if frequency_major:
        ang = inv[:, None] * pos.astype(_F32)[None, :]
    else:
        ang = pos.astype(_F32)[:, None] * inv[None, :]
    return jnp.cos(ang), jnp.sin(ang)


def _rope_tables_rows(pos):
    cos, sin = _rope_angles(pos)
    zero = jnp.zeros_like(sin)
    reps = LANES // QK_ROPE
    return (jnp.tile(jnp.concatenate([cos, cos], axis=1), (1, reps)),
            jnp.tile(jnp.concatenate([zero, sin], axis=1), (1, reps)),
            jnp.tile(jnp.concatenate([-sin, zero], axis=1), (1, reps)))


PROMPT_ROW_TILE = 1024
PROMPT_Q_TILE = 512
PROMPT_K_TILE = 512


def kernel(x_prompt, x_sample, cache_kv_latent, cache_k_rope, state_conv, pre_norm, w_in, q_norm, w_uq, kv_norm,
           w_uk, w_uv, w_o_mla, conv_w, w_o_conv, w_out, post_norm):
    depth = pre_norm.shape[0]
    assert depth == 1
    b, t, _ = x_prompt.shape
    nb, t_new, _ = x_sample.shape
    past_len = cache_kv_latent.shape[2]
    lyr = 0
    w = _prep_weights(pre_norm[lyr], w_in[lyr], q_norm[lyr], w_uq[lyr], kv_norm[lyr], w_uk[lyr], w_uv[lyr],
                      w_o_mla[lyr], conv_w[lyr], w_o_conv[lyr], w_out[lyr], post_norm[lyr])

    cos_t, sin_t = _rope_angles(jnp.arange(t, dtype=jnp.int32), frequency_major=True)
    zero_state = jnp.zeros((b, CONV_K - 1, CONV_WIDTH), _F32)
    qt, kc, vt, ckv_p, kpe_t, ga, sa, mb, cv_p = _project_prompt(
        x_prompt, cos_t, sin_t, zero_state, w, tm=PROMPT_ROW_TILE, tq=PROMPT_Q_TILE)
    y_p = _attend_prompt(qt, kc, vt, ga, sa, mb, x_prompt, w, tq=PROMPT_Q_TILE, tk=PROMPT_K_TILE)

    n_rows = nb * t_new
    pos_s = past_len + jnp.arange(t_new, dtype=jnp.int32)
    tabs_s = tuple(jnp.tile(tb, (nb, 1)) for tb in _rope_tables_rows(pos_s))
    xs = x_sample.reshape(1, n_rows, D_MODEL)
    q_s, kc_s, ckv_s, kpe_s, ga_s, sa_s, mb_s, cv_s = _project_sample(
        xs, *tabs_s, state_conv[lyr], w, n_seq=nb)
    y_s = _attend_sample(q_s, cache_kv_latent[lyr], jnp.swapaxes(cache_k_rope[lyr], 1, 2),
                         kc_s.reshape(nb, t_new, KEY_DIM),
                         ga_s[0], sa_s[0], mb_s[0], xs[0], w)

    return (y_p, y_s.reshape(nb, t_new, D_MODEL),
            ckv_p[None], jnp.swapaxes(kpe_t, 1, 2)[None], cv_p[None],
            ckv_s.reshape(1, nb, t_new, KV_LORA), kpe_s.reshape(1, nb, t_new, QK_ROPE), cv_s[None])
```

```python
import functools

import numpy as np
import jax
import jax.numpy as jnp
from jax import lax
from jax.experimental import pallas as pl
from jax.experimental.pallas import tpu as pltpu

N_HEADS = 8
QK_NOPE = 64
QK_ROPE = 32
V_HEAD = 64
Q_LORA = 256
KV_LORA = 128
MLA_WIDTH = N_HEADS * V_HEAD
CONV_WIDTH = 512
CONV_K = 3
D_MODEL = 1024
CHUNK = 64
ROPE_BASE = 10000.0
EPS = 1e-6
SM_SCALE = (QK_NOPE + QK_ROPE) ** -0.5
NEG_INF = -1e30
LOG2E = 1.4426950408889634

LANES = 128
MXU_DIM = 256
COL_BLOCK = MXU_DIM
REDUCE_SLAB = 64
STAGE1_LEAD = 1
HALF_ROPE = QK_ROPE // 2
KEY_DIM = KV_LORA + QK_ROPE
HEADS_PER_ROPE_BLOCK = LANES // QK_ROPE
Q_SCALE = SM_SCALE * LOG2E
VMEM_LIMIT_BYTES = 56 * 1024 * 1024

_O_TAIL = Q_LORA + KV_LORA + QK_ROPE
_QKV_COLS = -(-_O_TAIL // MXU_DIM) * MXU_DIM
_O_GM = 0
_O_CB = _O_GM + MLA_WIDTH
_O_CC = _O_CB + CONV_WIDTH
_O_CX = _O_CC + CONV_WIDTH
_O_GC = _O_CX + CONV_WIDTH
_O_MM = _O_GC + CONV_WIDTH
_O_MC = _O_MM + D_MODEL
_TAIL_COLS = _O_MC + D_MODEL

_F32 = jnp.float32
_BF16 = jnp.bfloat16
_NT = (((1,), (1,)), ((), ()))


def _rms(v, g):
    return v * lax.rsqrt(jnp.mean(v * v, axis=-1, keepdims=True) + EPS) * g


def _silu(v):
    return v * jax.nn.sigmoid(v)


def _dot(a, b):
    return jnp.dot(a, b, preferred_element_type=_F32)


def _dot_nt(a, b):
    return lax.dot_general(a, b, _NT, preferred_element_type=_F32)


def _tail_proj(h, w_in_t_ref):
    def proj(off, n):
        return _dot_nt(h, w_in_t_ref[_O_TAIL + off:_O_TAIL + off + n, :])
    return proj


def _conv_branch(proj, st_ref, conv_w_ref, w_oc_ref, mb_ref, cst_ref, ub_ref, carry_ref, *,
                 tm, n_seq, carry_state):
    seq_len = tm // n_seq
    u = proj(_O_CC, CONV_WIDTH) * proj(_O_CX, CONV_WIDTH)
    w0 = conv_w_ref[0:1, :]
    w1 = conv_w_ref[1:2, :]
    w2 = conv_w_ref[2:3, :]
    if carry_state:
        @pl.when(pl.program_id(1) == 0)
        def _():
            carry_ref[0:CONV_K - 1, :] = st_ref[0]
    convs = []
    for j in range(n_seq):
        u_j = u[j * seq_len:(j + 1) * seq_len]
        prev = carry_ref[0:CONV_K - 1, :] if carry_state else st_ref[j]
        ub_ref[j, 8 - (CONV_K - 1):8, :] = prev
        ub_ref[j, 8:8 + seq_len, :] = u_j
        convs.append(w0 * ub_ref[j, 6:6 + seq_len, :] + w1 * ub_ref[j, 7:7 + seq_len, :] + w2 * u_j)
        new_state = u_j[seq_len - (CONV_K - 1):, :]
        cst_ref[j] = new_state
        if carry_state:
            carry_ref[0:CONV_K - 1, :] = new_state
    conv = convs[0] if n_seq == 1 else jnp.concatenate(convs, axis=0)
    c_b = proj(_O_CB, CONV_WIDTH)
    g_conv = proj(_O_GC, CONV_WIDTH)
    merge_gate = jax.nn.sigmoid(proj(_O_MC, D_MODEL))
    bb = (c_b * conv * _silu(g_conv)).astype(_BF16)
    branch_b = _dot(bb, w_oc_ref[...])
    mb_ref[0] = (merge_gate * branch_b).astype(_BF16)


def _proj_prompt_kernel(x_ref, cos_ref, sin_ref, st_ref, pre_g_ref, w_in_t_ref, q_g_ref,
                        w_qabs_ref, w_uqp_t_ref, kv_g_ref, conv_w_ref, w_oc_ref,
                        qt_ref, kc_ref, vt_ref, ckv_ref, kpet_ref, ga_ref, sa_ref, mb_ref, cst_ref,
                        ub_ref, carry_ref, *, tq):
    tm = x_ref.shape[1]
    h = _rms(x_ref[0], pre_g_ref[...]).astype(_BF16)
    cos_t = cos_ref[...]
    sin_t = sin_ref[...]

    qkv = _dot_nt(h, w_in_t_ref[0:_QKV_COLS, :])
    proj = _tail_proj(h, w_in_t_ref)
    sa_ref[0] = jax.nn.sigmoid(proj(_O_MM, D_MODEL)).astype(_BF16)
    ga_ref[0] = _silu(proj(_O_GM, MLA_WIDTH)).astype(_BF16)
    qn = _rms(qkv[:, :Q_LORA], q_g_ref[...]).astype(_BF16)
    q_abs_t = _dot_nt(w_qabs_ref[...], qn)
    q_pe_t = _dot_nt(w_uqp_t_ref[...], qn).reshape(N_HEADS, QK_ROPE, tm)
    x1 = q_pe_t[:, :HALF_ROPE, :]
    x2 = q_pe_t[:, HALF_ROPE:, :]
    r1 = (x1 * cos_t - x2 * sin_t) * Q_SCALE
    r2 = (x2 * cos_t + x1 * sin_t) * Q_SCALE
    groups = tq // COL_BLOCK
    for j in range(tm // tq):
        for hd in range(N_HEADS):
            for g in range(groups):
                tok = slice(j * tq + g * COL_BLOCK, j * tq + (g + 1) * COL_BLOCK)
                cb = hd * groups + g
                qt_ref[0, j, cb, 0:KV_LORA, :] = (
                    q_abs_t[hd * KV_LORA:(hd + 1) * KV_LORA, tok] * Q_SCALE).astype(_BF16)
                qt_ref[0, j, cb, KV_LORA:KV_LORA + HALF_ROPE, :] = r1[hd][:, tok].astype(_BF16)
                qt_ref[0, j, cb, KV_LORA + HALF_ROPE:KEY_DIM, :] = r2[hd][:, tok].astype(_BF16)

    kvr = qkv[:, Q_LORA:]
    ckv = _rms(kvr[:, :KV_LORA], kv_g_ref[...])
    ckv_ref[0] = ckv
    ckv_t = ckv.T.astype(_BF16)
    for u in range(tm // MXU_DIM):
        vt_ref[0, u] = ckv_t[:, u * MXU_DIM:(u + 1) * MXU_DIM]
    kr_t = kvr[:, KV_LORA:].T
    k1 = kr_t[0:HALF_ROPE]
    k2 = kr_t[HALF_ROPE:QK_ROPE]
    kpe_t = jnp.concatenate([k1 * cos_t - k2 * sin_t, k2 * cos_t + k1 * sin_t], axis=0)
    kpet_ref[0] = kpe_t
    kpe = jnp.concatenate([kpe_t, jnp.zeros((LANES - QK_ROPE, tm), _F32)], axis=0).T
    kc_ref[0, :, 0:KV_LORA] = ckv.astype(_BF16)
    kc_ref[0, :, KV_LORA:KEY_DIM] = kpe[:, :QK_ROPE].astype(_BF16)

    _conv_branch(proj, st_ref, conv_w_ref, w_oc_ref, mb_ref, cst_ref, ub_ref, carry_ref,
                 tm=tm, n_seq=1, carry_state=True)


def _project_prompt(x, cos_t, sin_t, state, w, *, tm, tq):
    b, t, _ = x.shape

    def rows(width, dtype):
        return (jax.ShapeDtypeStruct((b, t, width), dtype),
                pl.BlockSpec((1, tm, width), lambda bi, ti: (bi, ti, 0)))

    def full(a):
        return pl.BlockSpec(a.shape, lambda bi, ti: (0,) * a.ndim)

    outs = [
        (jax.ShapeDtypeStruct((b, t // tq, N_HEADS * tq // COL_BLOCK, KEY_DIM, COL_BLOCK), _BF16),
         pl.BlockSpec((1, tm // tq, N_HEADS * tq // COL_BLOCK, KEY_DIM, COL_BLOCK),
                      lambda bi, ti: (bi, ti, 0, 0, 0))),
        rows(KEY_DIM, _BF16),
        (jax.ShapeDtypeStruct((b, t // MXU_DIM, KV_LORA, MXU_DIM), _BF16),
         pl.BlockSpec((1, tm // MXU_DIM, KV_LORA, MXU_DIM), lambda bi, ti: (bi, ti, 0, 0))),
        rows(KV_LORA, _F32),
        (jax.ShapeDtypeStruct((b, QK_ROPE, t), _F32),
         pl.BlockSpec((1, QK_ROPE, tm), lambda bi, ti: (bi, 0, ti))),
        rows(MLA_WIDTH, _BF16),
        rows(D_MODEL, _BF16),
        rows(D_MODEL, _BF16),
        (jax.ShapeDtypeStruct((b, CONV_K - 1, CONV_WIDTH), _F32),
         pl.BlockSpec((1, CONV_K - 1, CONV_WIDTH), lambda bi, ti: (bi, 0, 0))),
    ]
    table_spec = pl.BlockSpec((HALF_ROPE, tm), lambda bi, ti: (0, ti))
    weights = (w["pre_g"], w["w_in_t"], w["q_g"], w["w_qabs"], w["w_uqp_t"],
               w["kv_g"], w["conv_w"], w["w_oc"])
    return pl.pallas_call(
        functools.partial(_proj_prompt_kernel, tq=tq),
        grid=(b, t // tm),
        in_specs=[pl.BlockSpec((1, tm, D_MODEL), lambda bi, ti: (bi, ti, 0)),
                  table_spec, table_spec,
                  pl.BlockSpec((1, CONV_K - 1, CONV_WIDTH), lambda bi, ti: (bi, 0, 0))]
                 + [full(a) for a in weights],
        out_specs=[o[1] for o in outs],
        out_shape=[o[0] for o in outs],
        scratch_shapes=[pltpu.VMEM((1, tm + 8, CONV_WIDTH), _F32),
                        pltpu.VMEM((8, CONV_WIDTH), _F32)],
        compiler_params=pltpu.CompilerParams(
            dimension_semantics=("arbitrary", "arbitrary"), vmem_limit_bytes=VMEM_LIMIT_BYTES),
        name="proj_prompt",
    )(x, cos_t, sin_t, state, *weights)


def _proj_sample_kernel(x_ref, cos_ref, sn1_ref, sn2_ref, st_ref, pre_g_ref, w_in_t_ref, q_g_ref,
                        w_qabs_ref, w_uqp_ref, kv_g_ref, conv_w_ref, w_oc_ref,
                        q_ref, kc_ref, ckv_ref, kpe_ref, ga_ref, sa_ref, mb_ref, cst_ref,
                        ub_ref, carry_ref, *, n_seq):
    h = _rms(x_ref[0], pre_g_ref[...]).astype(_BF16)
    cos = cos_ref[...]
    sn1 = sn1_ref[...]
    sn2 = sn2_ref[...]

    def rope(v):
        return (v * cos + pltpu.roll(v, HALF_ROPE, 1) * sn1
                + pltpu.roll(v, LANES - HALF_ROPE, 1) * sn2)

    qkv = _dot_nt(h, w_in_t_ref[0:_QKV_COLS, :])
    qn = _rms(qkv[:, :Q_LORA], q_g_ref[...]).astype(_BF16)
    q_abs = _dot_nt(qn, w_qabs_ref[...])
    q_pe = _dot(qn, w_uqp_ref[...])
    n_rope_blocks = N_HEADS // HEADS_PER_ROPE_BLOCK
    q_rot = [rope(q_pe[:, j * LANES:(j + 1) * LANES]) * Q_SCALE for j in range(n_rope_blocks)]
    for hd in range(N_HEADS):
        q_ref[0, hd, :, 0:KV_LORA] = (q_abs[:, hd * LANES:(hd + 1) * LANES] * Q_SCALE).astype(_BF16)
        grp = hd % HEADS_PER_ROPE_BLOCK
        blk = q_rot[hd // HEADS_PER_ROPE_BLOCK]
        if grp:
            blk = pltpu.roll(blk, LANES - grp * QK_ROPE, 1)
        q_ref[0, hd, :, KV_LORA:KEY_DIM] = blk[:, :QK_ROPE].astype(_BF16)

    kvr = qkv[:, Q_LORA:]
    ckv = _rms(kvr[:, :KV_LORA], kv_g_ref[...])
    k_rot = rope(kvr[:, KV_LORA:])
    ckv_ref[0] = ckv
    kpe_ref[0] = k_rot[:, :QK_ROPE]
    kc_ref[0, :, 0:KV_LORA] = ckv.astype(_BF16)
    kc_ref[0, :, KV_LORA:KEY_DIM] = k_rot[:, :QK_ROPE].astype(_BF16)

    proj = _tail_proj(h, w_in_t_ref)
    sa_ref[0] = jax.nn.sigmoid(proj(_O_MM, D_MODEL)).astype(_BF16)
    ga_ref[0] = _silu(proj(_O_GM, MLA_WIDTH)).astype(_BF16)
    _conv_branch(proj, st_ref, conv_w_ref, w_oc_ref, mb_ref, cst_ref, ub_ref, carry_ref,
                 tm=x_ref.shape[1], n_seq=n_seq, carry_state=False)


def _project_sample(x, cos, sn1, sn2, state, w, *, n_seq):
    _, tm, _ = x.shape

    def rows(width, dtype):
        return (jax.ShapeDtypeStruct((1, tm, width), dtype), pl.BlockSpec((1, tm, width), lambda i: (0, 0, 0)))

    def full(a):
        return pl.BlockSpec(a.shape, lambda i: (0,) * a.ndim)

    outs = [
        (jax.ShapeDtypeStruct((1, N_HEADS, tm, KEY_DIM), _BF16),
         pl.BlockSpec((1, N_HEADS, tm, KEY_DIM), lambda i: (0, 0, 0, 0))),
        rows(KEY_DIM, _BF16), rows(KV_LORA, _F32), rows(QK_ROPE, _F32),
        rows(MLA_WIDTH, _BF16), rows(D_MODEL, _BF16), rows(D_MODEL, _BF16),
        (jax.ShapeDtypeStruct((n_seq, CONV_K - 1, CONV_WIDTH), _F32),
         pl.BlockSpec((n_seq, CONV_K - 1, CONV_WIDTH), lambda i: (0, 0, 0))),
    ]
    weights = (w["pre_g"], w["w_in_t"], w["q_g"], w["w_qabs"], w["w_uqp"],
               w["kv_g"], w["conv_w"], w["w_oc"])
    return pl.pallas_call(
        functools.partial(_proj_sample_kernel, n_seq=n_seq),
        grid=(1,),
        in_specs=[full(x), full(cos), full(sn1), full(sn2), full(state)] + [full(a) for a in weights],
        out_specs=[o[1] for o in outs],
        out_shape=[o[0] for o in outs],
        scratch_shapes=[pltpu.VMEM((n_seq, tm // n_seq + 8, CONV_WIDTH), _F32),
                        pltpu.VMEM((8, CONV_WIDTH), _F32)],
        compiler_params=pltpu.CompilerParams(
            dimension_semantics=("arbitrary",), vmem_limit_bytes=VMEM_LIMIT_BYTES),
        name="proj_sample",
    )(x, cos, sn1, sn2, state, *weights)


def _epilogue(o_lat, ga, sa, mb, x, w_uv_ref, w_om_ref, w_out_ref, post_g_ref):
    o = _dot(o_lat.astype(_BF16), w_uv_ref[...])
    branch_a = _dot((o * ga.astype(_F32)).astype(_BF16), w_om_ref[...])
    merged = sa.astype(_F32) * branch_a + mb.astype(_F32)
    z = _dot(merged.astype(_BF16), w_out_ref[...])
    return x + _rms(z, post_g_ref[...])


def _col_reduce(v, op):
    n_keys, cols = v.shape
    slabs = max(n_keys // REDUCE_SLAB, 1)
    part = op(v.reshape(slabs, n_keys // slabs, cols), axis=0)
    return op(part, axis=0, keepdims=True)


def _attn_prompt_kernel(q0_ref, qn_ref, kc_ref, vt_ref, bias_ref, ga_ref, sa_ref, mb_ref, x_ref,
                        w_uv_ref, w_om_ref, w_out_ref, post_g_ref, y_ref,
                        m_ref, l_ref, acc_ref, s_ref, mc_ref, qc_ref, *, tq, tk):
    i = pl.program_id(1)
    units = tk // MXU_DIM
    q_tiles_per_k_tile = tk // tq
    edge = i // q_tiles_per_k_tile
    groups = tq // COL_BLOCK
    n_cb = N_HEADS * groups

    m_ref[...] = jnp.full(m_ref.shape, NEG_INF, _F32)
    l_ref[...] = jnp.zeros(l_ref.shape, _F32)
    acc_ref[...] = jnp.zeros(acc_ref.shape, _F32)

    def keys_of(tile):
        return kc_ref[0, pl.ds(pl.multiple_of(tile * tk, tk), tk), :]

    def values_of(tile):
        return [vt_ref[0, tile * units + u] for u in range(units)]

    def stage1(k, cb, slot=0, q_tile=None, next_queries=False, split=True):
        qb = qn_ref[0, 0, cb] if next_queries else qc_ref[cb]
        if split:
            s = jnp.concatenate([_dot(k[u * MXU_DIM:(u + 1) * MXU_DIM], qb) for u in range(units)], axis=0)
        else:
            s = _dot(k, qb)
        if q_tile is not None:
            s = s + bias_ref[(q_tile % q_tiles_per_k_tile) * groups + cb % groups]
        s_ref[slot, cb] = s
        mc_ref[slot, cb] = _col_reduce(s, jnp.max)

    def stage2(vts, cb, slot=0):
        m_prev = m_ref[cb]
        m_new = jnp.maximum(m_prev, mc_ref[slot, cb])
        alpha = jnp.exp2(m_prev - m_new)
        p = jnp.exp2(s_ref[slot, cb] - m_new)
        l_ref[cb] = alpha * l_ref[cb] + _col_reduce(p, jnp.sum)
        pb = p.astype(_BF16)
        pv = _dot(vts[0], pb[0:MXU_DIM])
        for u in range(1, units):
            pv = pv + _dot(vts[u], pb[u * MXU_DIM:(u + 1) * MXU_DIM])
        acc_ref[cb] = alpha * acc_ref[cb] + pv
        m_ref[cb] = m_new

    @pl.when(i == 0)
    def _():
        qc_ref[...] = q0_ref[0, 0]
        k_edge = keys_of(edge)
        for cb in range(n_cb):
            stage1(k_edge, cb, q_tile=i)

    def advance(j, rd, wr):
        vts = values_of(jnp.where(j == 0, edge, j - 1))
        k = keys_of(j)
        in_place = rd == wr
        lead = 0 if in_place else STAGE1_LEAD
        for cb in range(lead):
            stage1(k, cb, wr, split=in_place)
        for cb in range(n_cb):
            stage2(vts, cb, rd)
            if cb + lead < n_cb:
                stage1(k, cb + lead, wr, split=in_place)

    def single(j, carry):
        advance(j, 0, 0)
        return carry

    def double(pair, carry):
        advance(odd + 2 * pair, 0, 1)
        advance(odd + 2 * pair + 1, 1, 0)
        return carry

    odd = edge % 2
    lax.fori_loop(0, odd, single, 0)
    lax.fori_loop(0, edge // 2, double, 0)

    last_vts = values_of(jnp.where(edge == 0, edge, edge - 1))
    nxt = jnp.minimum(i + 1, pl.num_programs(1) - 1)
    k_next = keys_of(nxt // q_tiles_per_k_tile)
    for cb in range(n_cb):
        stage2(last_vts, cb)
        stage1(k_next, cb, 0, q_tile=nxt, next_queries=True)
    qc_ref[...] = qn_ref[0, 0]

    row_groups = [slice(g * COL_BLOCK, (g + 1) * COL_BLOCK) for g in range(groups)]
    o_lat = [jnp.concatenate([(acc_ref[hd * groups + g] / l_ref[hd * groups + g]).T
                              for hd in range(N_HEADS)], axis=1) for g in range(groups)]
    o = [_dot(v.astype(_BF16), w_uv_ref[...]) for v in o_lat]
    branch_a = [_dot((o[g] * ga_ref[0, rows, :].astype(_F32)).astype(_BF16), w_om_ref[...])
                for g, rows in enumerate(row_groups)]
    z = [_dot((sa_ref[0, rows, :].astype(_F32) * branch_a[g] + mb_ref[0, rows, :].astype(_F32)).astype(_BF16),
              w_out_ref[...]) for g, rows in enumerate(row_groups)]
    for g, rows in enumerate(row_groups):
        y_ref[0, rows, :] = x_ref[0, rows, :] + _rms(z[g], post_g_ref[...])


def _attend_prompt(qt, kc, vt, ga, sa, mb, x, w, *, tq, tk):
    b, t, _ = x.shape
    n_q = t // tq

    def row_spec(width):
        return pl.BlockSpec((1, tq, width), lambda bi, qi: (bi, qi, 0))

    def full(a):
        return pl.BlockSpec(a.shape, lambda bi, qi: (0,) * a.ndim)

    k_chunk = np.arange(tk)[None, :, None] // CHUNK
    q_chunk = (np.arange(tk // COL_BLOCK)[:, None, None] * COL_BLOCK
               + np.arange(COL_BLOCK)[None, None, :]) // CHUNK
    bias = jnp.asarray(np.where(k_chunk <= q_chunk, 0.0, NEG_INF), _F32)
    n_cb = N_HEADS * tq // COL_BLOCK

    weights = (w["w_uv"], w["w_om"], w["w_out"], w["post_g"])
    return pl.pallas_call(
        functools.partial(_attn_prompt_kernel, tq=tq, tk=tk),
        grid=(b, t // tq),
        in_specs=[pl.BlockSpec((1, 1, n_cb, KEY_DIM, COL_BLOCK), lambda bi, qi: (bi, 0, 0, 0, 0),
                               pipeline_mode=pl.Buffered(1)),
                  pl.BlockSpec((1, 1, n_cb, KEY_DIM, COL_BLOCK),
                               lambda bi, qi: (bi, jnp.minimum(qi + 1, n_q - 1), 0, 0, 0)),
                  pl.BlockSpec((1, t, KEY_DIM), lambda bi, qi: (bi, 0, 0), pipeline_mode=pl.Buffered(1)),
                  pl.BlockSpec((1, t // MXU_DIM, KV_LORA, MXU_DIM), lambda bi, qi: (bi, 0, 0, 0),
                               pipeline_mode=pl.Buffered(1)),
                  full(bias),
                  row_spec(MLA_WIDTH), row_spec(D_MODEL), row_spec(D_MODEL), row_spec(D_MODEL)]
                 + [full(a) for a in weights],
        out_specs=row_spec(D_MODEL),
        out_shape=jax.ShapeDtypeStruct((b, t, D_MODEL), _F32),
        scratch_shapes=[pltpu.VMEM((n_cb, 1, COL_BLOCK), _F32), pltpu.VMEM((n_cb, 1, COL_BLOCK), _F32),
                        pltpu.VMEM((n_cb, KV_LORA, COL_BLOCK), _F32),
                        pltpu.VMEM((2, n_cb, tk, COL_BLOCK), _F32), pltpu.VMEM((2, n_cb, 1, COL_BLOCK), _F32),
                        pltpu.VMEM((n_cb, KEY_DIM, COL_BLOCK), _BF16)],
        compiler_params=pltpu.CompilerParams(
            dimension_semantics=("arbitrary", "arbitrary"), vmem_limit_bytes=VMEM_LIMIT_BYTES),
        name="attn_prompt",
    )(qt, qt, kc, vt, bias, ga, sa, mb, x, *weights)


def _attn_first(q, k, m_ref, l_ref, acc_ref, visible=None):
    s = _dot_nt(q, k)
    if visible is not None:
        s = jnp.where(visible, s, NEG_INF)
    m = jnp.max(s, axis=1, keepdims=True)
    p = jnp.exp2(s - m)
    m_ref[...] = jnp.broadcast_to(m, m_ref.shape)
    l_ref[...] = jnp.broadcast_to(jnp.sum(p, axis=1, keepdims=True), l_ref.shape)
    acc_ref[...] = _dot(p.astype(_BF16), k[:, :KV_LORA])


def _attn_update(q, k_lat, k_rope_t, m_ref, l_ref, acc_ref):
    s = _dot_nt(q[:, :KV_LORA], k_lat) + _dot(q[:, KV_LORA:], k_rope_t)
    m_prev = m_ref[...]
    m_new = jnp.maximum(m_prev, jnp.max(s, axis=1, keepdims=True))
    alpha = jnp.exp2(m_prev - m_new)
    p = jnp.exp2(s - jnp.concatenate([m_new] * (s.shape[1] // LANES), axis=1))
    l_ref[...] = alpha * l_ref[...] + jnp.sum(p, axis=1, keepdims=True)
    acc_ref[...] = alpha * acc_ref[...] + _dot(p.astype(_BF16), k_lat)
    m_ref[...] = m_new


def _attn_sample_kernel(q_ref, past_lat_ref, past_rope_ref, knew_ref, ga_ref, sa_ref, mb_ref, x_ref,
                        w_uv_ref, w_om_ref, w_out_ref, post_g_ref, y_ref,
                        m_ref, l_ref, acc_ref, o_ref, *, new_visible):
    bi = pl.program_id(0)
    t_new = knew_ref.shape[1]
    rows = N_HEADS * t_new
    q = q_ref[0].reshape(rows, KEY_DIM)
    if new_visible is None:
        visible = None
    else:
        q_pos = lax.broadcasted_iota(jnp.int32, (rows, t_new), 0) % t_new
        k_pos = lax.broadcasted_iota(jnp.int32, (rows, t_new), 1)
        visible = (k_pos + new_visible[0]) // CHUNK <= (q_pos + new_visible[0]) // CHUNK
    _attn_first(q, knew_ref[0], m_ref, l_ref, acc_ref, visible=visible)
    _attn_update(q, past_lat_ref[0].astype(_BF16), past_rope_ref[0].astype(_BF16), m_ref, l_ref, acc_ref)
    o = acc_ref[...] / l_ref[...]
    r0 = pl.multiple_of(bi * t_new, t_new)
    o_ref[pl.ds(r0, t_new), :] = jnp.concatenate(
        [o[hd * t_new:(hd + 1) * t_new] for hd in range(N_HEADS)], axis=1)

    @pl.when(bi == pl.num_programs(0) - 1)
    def _():
        y_ref[...] = _epilogue(o_ref[...], ga_ref[...], sa_ref[...], mb_ref[...], x_ref[...],
                               w_uv_ref, w_om_ref, w_out_ref, post_g_ref)


def _attend_sample(q, past_lat, past_rope, k_new, ga, sa, mb, x, w):
    nb, t_new, _ = k_new.shape
    past_len = past_lat.shape[1]
    n_rows = nb * t_new
    rows = N_HEADS * t_new
    last_q, first_q = past_len + t_new - 1, past_len
    new_visible = None if last_q // CHUNK == first_q // CHUNK else (past_len,)

    def full(a):
        return pl.BlockSpec(a.shape, lambda bi: (0,) * a.ndim)

    weights = (w["w_uv"], w["w_om"], w["w_out"], w["post_g"])
    return pl.pallas_call(
        functools.partial(_attn_sample_kernel, new_visible=new_visible),
        grid=(nb,),
        in_specs=[pl.BlockSpec((1, N_HEADS, t_new, KEY_DIM), lambda bi: (0, 0, bi, 0)),
                  pl.BlockSpec((1, past_len, KV_LORA), lambda bi: (bi, 0, 0)),
                  pl.BlockSpec((1, QK_ROPE, past_len), lambda bi: (bi, 0, 0)),
                  pl.BlockSpec((1, t_new, KEY_DIM), lambda bi: (bi, 0, 0)),
                  full(ga), full(sa), full(mb), full(x)] + [full(a) for a in weights],
        out_specs=pl.BlockSpec((n_rows, D_MODEL), lambda bi: (0, 0)),
        out_shape=jax.ShapeDtypeStruct((n_rows, D_MODEL), _F32),
        scratch_shapes=[pltpu.VMEM((rows, LANES), _F32), pltpu.VMEM((rows, LANES), _F32),
                        pltpu.VMEM((rows, KV_LORA), _F32), pltpu.VMEM((n_rows, N_HEADS * KV_LORA), _F32)],
        compiler_params=pltpu.CompilerParams(
            dimension_semantics=("arbitrary",), vmem_limit_bytes=VMEM_LIMIT_BYTES),
        name="attn_sample",
    )(q, past_lat, past_rope, k_new, ga, sa, mb, x, *weights)


def _block_diag(blocks):
    n = len(blocks)
    r, c = blocks[0].shape
    rows = []
    for j, blk in enumerate(blocks):
        rows.append(jnp.pad(blk, ((0, 0), (j * c, (n - 1 - j) * c))))
    return jnp.concatenate(rows, axis=0)


def _absorb_kernel(w_uk_ref, w_uqn_ref, o_ref):
    for hd in range(N_HEADS):
        o_ref[hd] = lax.dot_general(w_uk_ref[hd], w_uqn_ref[hd], _NT, precision=lax.Precision.HIGHEST,
                                    preferred_element_type=_F32).astype(o_ref.dtype)


def _absorbed_query_weight(w_uk, w_uqn):
    out = pl.pallas_call(
        _absorb_kernel,
        out_shape=jax.ShapeDtypeStruct((N_HEADS, KV_LORA, Q_LORA), _BF16),
        name="absorb_q_weight",
    )(jnp.transpose(w_uk, (1, 0, 2)), jnp.transpose(w_uqn, (1, 0, 2)))
    return out.reshape(N_HEADS * KV_LORA, Q_LORA)


def _prep_weights(pre_norm, w_in, q_norm, w_uq, kv_norm, w_uk, w_uv, w_o_mla, conv_w, w_o_conv, w_out, post_norm):
    assert w_in.shape[1] == _O_TAIL + _TAIL_COLS
    wq = w_uq.reshape(Q_LORA, N_HEADS, QK_NOPE + QK_ROPE)
    w_uqp = wq[:, :, QK_NOPE:].reshape(Q_LORA, N_HEADS * QK_ROPE).astype(_BF16)
    return {
        "pre_g": pre_norm.reshape(1, D_MODEL),
        "w_in_t": w_in.T.astype(_BF16),
        "q_g": q_norm.reshape(1, Q_LORA),
        "w_qabs": _absorbed_query_weight(w_uk, wq[:, :, :QK_NOPE]),
        "w_uqp": w_uqp,
        "w_uqp_t": w_uqp.T,
        "kv_g": kv_norm.reshape(1, KV_LORA),
        "conv_w": conv_w,
        "w_oc": w_o_conv.astype(_BF16),
        "w_uv": _block_diag([w_uv[:, hd, :].astype(_BF16) for hd in range(N_HEADS)]),
        "w_om": w_o_mla.astype(_BF16),
        "w_out": w_out.astype(_BF16),
        "post_g": post_norm.reshape(1, D_MODEL),
    }


def _rope_angles(pos, frequency_major=False):
    inv = ROPE_BASE ** (-jnp.arange(HALF_ROPE, dtype=_F32) / HALF_ROPE)
    if frequency_major:
        ang = inv[:, None] * pos.astype(_F32)[None, :]
    else:
        ang = pos.astype(_F32)[:, None] * inv[None, :]
    return jnp.cos(ang), jnp.sin(ang)


def _rope_tables_rows(pos):
    cos, sin = _rope_angles(pos)
    zero = jnp.zeros_like(sin)
    reps = LANES // QK_ROPE
    return (jnp.tile(jnp.concatenate([cos, cos], axis=1), (1, reps)),
            jnp.tile(jnp.concatenate([zero, sin], axis=1), (1, reps)),
            jnp.tile(jnp.concatenate([-sin, zero], axis=1), (1, reps)))


PROMPT_ROW_TILE = 1024
PROMPT_Q_TILE = 512
PROMPT_K_TILE = 512


def kernel(x_prompt, x_sample, cache_kv_latent, cache_k_rope, state_conv, pre_norm, w_in, q_norm, w_uq, kv_norm,
           w_uk, w_uv, w_o_mla, conv_w, w_o_conv, w_out, post_norm):
    depth = pre_norm.shape[0]
    assert depth == 1
    b, t, _ = x_prompt.shape
    nb, t_new, _ = x_sample.shape
    past_len = cache_kv_latent.shape[2]
    lyr = 0
    w = _prep_weights(pre_norm[lyr], w_in[lyr], q_norm[lyr], w_uq[lyr], kv_norm[lyr], w_uk[lyr], w_uv[lyr],
                      w_o_mla[lyr], conv_w[lyr], w_o_conv[lyr], w_out[lyr], post_norm[lyr])

    cos_t, sin_t = _rope_angles(jnp.arange(t, dtype=jnp.int32), frequency_major=True)
    zero_state = jnp.zeros((b, CONV_K - 1, CONV_WIDTH), _F32)
    qt, kc, vt, ckv_p, kpe_t, ga, sa, mb, cv_p = _project_prompt(
        x_prompt, cos_t, sin_t, zero_state, w, tm=PROMPT_ROW_TILE, tq=PROMPT_Q_TILE)
    y_p = _attend_prompt(qt, kc, vt, ga, sa, mb, x_prompt, w, tq=PROMPT_Q_TILE, tk=PROMPT_K_TILE)

    n_rows = nb * t_new
    pos_s = past_len + jnp.arange(t_new, dtype=jnp.int32)
    tabs_s = tuple(jnp.tile(tb, (nb, 1)) for tb in _rope_tables_rows(pos_s))
    xs = x_sample.reshape(1, n_rows, D_MODEL)
    q_s, kc_s, ckv_s, kpe_s, ga_s, sa_s, mb_s, cv_s = _project_sample(
        xs, *tabs_s, state_conv[lyr], w, n_seq=nb)
    y_s = _attend_sample(q_s, cache_kv_latent[lyr], jnp.swapaxes(cache_k_rope[lyr], 1, 2),
                         kc_s.reshape(nb, t_new, KEY_DIM),
                         ga_s[0], sa_s[0], mb_s[0], xs[0], w)

    return (y_p, y_s.reshape(nb, t_new, D_MODEL),
            ckv_p[None], jnp.swapaxes(kpe_t, 1, 2)[None], cv_p[None],
            ckv_s.reshape(1, nb, t_new, KV_LORA), kpe_s.reshape(1, nb, t_new, QK_ROPE), cv_s[None])
```

```python
import functools

import numpy as np
import jax
import jax.numpy as jnp
from jax import lax
from jax.experimental import pallas as pl
from jax.experimental.pallas import tpu as pltpu

N_HEADS = 8
QK_NOPE = 64
QK_ROPE = 32
V_HEAD = 64
Q_LORA = 256
KV_LORA = 128
MLA_WIDTH = N_HEADS * V_HEAD
CONV_WIDTH = 512
CONV_K = 3
D_MODEL = 1024
CHUNK = 64
ROPE_BASE = 10000.0
EPS = 1e-6
SM_SCALE = (QK_NOPE + QK_ROPE) ** -0.5
NEG_INF = -1e30
LOG2E = 1.4426950408889634

LANES = 128
MXU_DIM = 256
COL_BLOCK = MXU_DIM
REDUCE_SLAB = 64
STAGE1_LEAD = 1
HALF_ROPE = QK_ROPE // 2
KEY_DIM = KV_LORA + QK_ROPE
HEADS_PER_ROPE_BLOCK = LANES // QK_ROPE
Q_SCALE = SM_SCALE * LOG2E
VMEM_LIMIT_BYTES = 56 * 1024 * 1024

_O_TAIL = Q_LORA + KV_LORA + QK_ROPE
_QKV_COLS = -(-_O_TAIL // MXU_DIM) * MXU_DIM
_O_GM = 0
_O_CB = _O_GM + MLA_WIDTH
_O_CC = _O_CB + CONV_WIDTH
_O_CX = _O_CC + CONV_WIDTH
_O_GC = _O_CX + CONV_WIDTH
_O_MM = _O_GC + CONV_WIDTH
_O_MC = _O_MM + D_MODEL
_TAIL_COLS = _O_MC + D_MODEL

_F32 = jnp.float32
_BF16 = jnp.bfloat16
_NT = (((1,), (1,)), ((), ()))


def _rms(v, g):
    return v * lax.rsqrt(jnp.mean(v * v, axis=-1, keepdims=True) + EPS) * g


def _silu(v):
    return v * jax.nn.sigmoid(v)


def _dot(a, b):
    return jnp.dot(a, b, preferred_element_type=_F32)


def _dot_nt(a, b):
    return lax.dot_general(a, b, _NT, preferred_element_type=_F32)


def _tail_proj(h, w_in_t_ref):
    def proj(off, n):
        return _dot_nt(h, w_in_t_ref[_O_TAIL + off:_O_TAIL + off + n, :])
    return proj


def _conv_branch(proj, st_ref, conv_w_ref, w_oc_ref, mb_ref, cst_ref, ub_ref, carry_ref, *,
                 tm, n_seq, carry_state):
    seq_len = tm // n_seq
    u = proj(_O_CC, CONV_WIDTH) * proj(_O_CX, CONV_WIDTH)
    w0 = conv_w_ref[0:1, :]
    w1 = conv_w_ref[1:2, :]
    w2 = conv_w_ref[2:3, :]
    if carry_state:
        @pl.when(pl.program_id(1) == 0)
        def _():
            carry_ref[0:CONV_K - 1, :] = st_ref[0]
    convs = []
    for j in range(n_seq):
        u_j = u[j * seq_len:(j + 1) * seq_len]
        prev = carry_ref[0:CONV_K - 1, :] if carry_state else st_ref[j]
        ub_ref[j, 8 - (CONV_K - 1):8, :] = prev
        ub_ref[j, 8:8 + seq_len, :] = u_j
        convs.append(w0 * ub_ref[j, 6:6 + seq_len, :] + w1 * ub_ref[j, 7:7 + seq_len, :] + w2 * u_j)
        new_state = u_j[seq_len - (CONV_K - 1):, :]
        cst_ref[j] = new_state
        if carry_state:
            carry_ref[0:CONV_K - 1, :] = new_state
    conv = convs[0] if n_seq == 1 else jnp.concatenate(convs, axis=0)
    c_b = proj(_O_CB, CONV_WIDTH)
    g_conv = proj(_O_GC, CONV_WIDTH)
    merge_gate = jax.nn.sigmoid(proj(_O_MC, D_MODEL))
    bb = (c_b * conv * _silu(g_conv)).astype(_BF16)
    branch_b = _dot(bb, w_oc_ref[...])
    mb_ref[0] = (merge_gate * branch_b).astype(_BF16)


def _proj_prompt_kernel(x_ref, cos_ref, sin_ref, st_ref, pre_g_ref, w_in_t_ref, q_g_ref,
                        w_qabs_ref, w_uqp_t_ref, kv_g_ref, conv_w_ref, w_oc_ref,
                        qt_ref, kc_ref, vt_ref, ckv_ref, kpet_ref, ga_ref, sa_ref, mb_ref, cst_ref,
                        ub_ref, carry_ref, *, tq):
    tm = x_ref.shape[1]
    h = _rms(x_ref[0], pre_g_ref[...]).astype(_BF16)
    cos_t = cos_ref[...]
    sin_t = sin_ref[...]

    qkv = _dot_nt(h, w_in_t_ref[0:_QKV_COLS, :])
    proj = _tail_proj(h, w_in_t_ref)
    sa_ref[0] = jax.nn.sigmoid(proj(_O_MM, D_MODEL)).astype(_BF16)
    ga_ref[0] = _silu(proj(_O_GM, MLA_WIDTH)).astype(_BF16)
    qn = _rms(qkv[:, :Q_LORA], q_g_ref[...]).astype(_BF16)
    q_abs_t = _dot_nt(w_qabs_ref[...], qn)
    q_pe_t = _dot_nt(w_uqp_t_ref[...], qn).reshape(N_HEADS, QK_ROPE, tm)
    x1 = q_pe_t[:, :HALF_ROPE, :]
    x2 = q_pe_t[:, HALF_ROPE:, :]
    r1 = (x1 * cos_t - x2 * sin_t) * Q_SCALE
    r2 = (x2 * cos_t + x1 * sin_t) * Q_SCALE
    groups = tq // COL_BLOCK
    for j in range(tm // tq):
        for hd in range(N_HEADS):
            for g in range(groups):
                tok = slice(j * tq + g * COL_BLOCK, j * tq + (g + 1) * COL_BLOCK)
                cb = hd * groups + g
                qt_ref[0, j, cb, 0:KV_LORA, :] = (
                    q_abs_t[hd * KV_LORA:(hd + 1) * KV_LORA, tok] * Q_SCALE).astype(_BF16)
                qt_ref[0, j, cb, KV_LORA:KV_LORA + HALF_ROPE, :] = r1[hd][:, tok].astype(_BF16)
                qt_ref[0, j, cb, KV_LORA + HALF_ROPE:KEY_DIM, :] = r2[hd][:, tok].astype(_BF16)

    kvr = qkv[:, Q_LORA:]
    ckv = _rms(kvr[:, :KV_LORA], kv_g_ref[...])
    ckv_ref[0] = ckv
    ckv_t = ckv.T.astype(_BF16)
    for u in range(tm // MXU_DIM):
        vt_ref[0, u] = ckv_t[:, u * MXU_DIM:(u + 1) * MXU_DIM]
    kr_t = kvr[:, KV_LORA:].T
    k1 = kr_t[0:HALF_ROPE]
    k2 = kr_t[HALF_ROPE:QK_ROPE]
    kpe_t = jnp.concatenate([k1 * cos_t - k2 * sin_t, k2 * cos_t + k1 * sin_t], axis=0)
    kpet_ref[0] = kpe_t
    kpe = jnp.concatenate([kpe_t, jnp.zeros((LANES - QK_ROPE, tm), _F32)], axis=0).T
    kc_ref[0, :, 0:KV_LORA] = ckv.astype(_BF16)
    kc_ref[0, :, KV_LORA:KEY_DIM] = kpe[:, :QK_ROPE].astype(_BF16)

    _conv_branch(proj, st_ref, conv_w_ref, w_oc_ref, mb_ref, cst_ref, ub_ref, carry_ref,
                 tm=tm, n_seq=1, carry_state=True)


def _project_prompt(x, cos_t, sin_t, state, w, *, tm, tq):
    b, t, _ = x.shape

    def rows(width, dtype):
        return (jax.ShapeDtypeStruct((b, t, width), dtype),
                pl.BlockSpec((1, tm, width), lambda bi, ti: (bi, ti, 0)))

    def full(a):
        return pl.BlockSpec(a.shape, lambda bi, ti: (0,) * a.ndim)

    outs = [
        (jax.ShapeDtypeStruct((b, t // tq, N_HEADS * tq // COL_BLOCK, KEY_DIM, COL_BLOCK), _BF16),
         pl.BlockSpec((1, tm // tq, N_HEADS * tq // COL_BLOCK, KEY_DIM, COL_BLOCK),
                      lambda bi, ti: (bi, ti, 0, 0, 0))),
        rows(KEY_DIM, _BF16),
        (jax.ShapeDtypeStruct((b, t // MXU_DIM, KV_LORA, MXU_DIM), _BF16),
         pl.BlockSpec((1, tm // MXU_DIM, KV_LORA, MXU_DIM), lambda bi, ti: (bi, ti, 0, 0))),
        rows(KV_LORA, _F32),
        (jax.ShapeDtypeStruct((b, QK_ROPE, t), _F32),
         pl.BlockSpec((1, QK_ROPE, tm), lambda bi, ti: (bi, 0, ti))),
        rows(MLA_WIDTH, _BF16),
        rows(D_MODEL, _BF16),
        rows(D_MODEL, _BF16),
        (jax.ShapeDtypeStruct((b, CONV_K - 1, CONV_WIDTH), _F32),
         pl.BlockSpec((1, CONV_K - 1, CONV_WIDTH), lambda bi, ti: (bi, 0, 0))),
    ]
    table_spec = pl.BlockSpec((HALF_ROPE, tm), lambda bi, ti: (0, ti))
    weights = (w["pre_g"], w["w_in_t"], w["q_g"], w["w_qabs"], w["w_uqp_t"],
               w["kv_g"], w["conv_w"], w["w_oc"])
    return pl.pallas_call(
        functools.partial(_proj_prompt_kernel, tq=tq),
        grid=(b, t // tm),
        in_specs=[pl.BlockSpec((1, tm, D_MODEL), lambda bi, ti: (bi, ti, 0)),
                  table_spec, table_spec,
                  pl.BlockSpec((1, CONV_K - 1, CONV_WIDTH), lambda bi, ti: (bi, 0, 0))]
                 + [full(a) for a in weights],
        out_specs=[o[1] for o in outs],
        out_shape=[o[0] for o in outs],
        scratch_shapes=[pltpu.VMEM((1, tm + 8, CONV_WIDTH), _F32),
                        pltpu.VMEM((8, CONV_WIDTH), _F32)],
        compiler_params=pltpu.CompilerParams(
            dimension_semantics=("arbitrary", "arbitrary"), vmem_limit_bytes=VMEM_LIMIT_BYTES),
        name="proj_prompt",
    )(x, cos_t, sin_t, state, *weights)


def _proj_sample_kernel(x_ref, cos_ref, sn1_ref, sn2_ref, st_ref, pre_g_ref, w_in_t_ref, q_g_ref,
                        w_qabs_ref, w_uqp_ref, kv_g_ref, conv_w_ref, w_oc_ref,
                        q_ref, kc_ref, ckv_ref, kpe_ref, ga_ref, sa_ref, mb_ref, cst_ref,
                        ub_ref, carry_ref, *, n_seq):
    h = _rms(x_ref[0], pre_g_ref[...]).astype(_BF16)
    cos = cos_ref[...]
    sn1 = sn1_ref[...]
    sn2 = sn2_ref[...]

    def rope(v):
        return (v * cos + pltpu.roll(v, HALF_ROPE, 1) * sn1
                + pltpu.roll(v, LANES - HALF_ROPE, 1) * sn2)

    qkv = _dot_nt(h, w_in_t_ref[0:_QKV_COLS, :])
    qn = _rms(qkv[:, :Q_LORA], q_g_ref[...]).astype(_BF16)
    q_abs = _dot_nt(qn, w_qabs_ref[...])
    q_pe = _dot(qn, w_uqp_ref[...])
    n_rope_blocks = N_HEADS // HEADS_PER_ROPE_BLOCK
    q_rot = [rope(q_pe[:, j * LANES:(j + 1) * LANES]) * Q_SCALE for j in range(n_rope_blocks)]
    for hd in range(N_HEADS):
        q_ref[0, hd, :, 0:KV_LORA] = (q_abs[:, hd * LANES:(hd + 1) * LANES] * Q_SCALE).astype(_BF16)
        grp = hd % HEADS_PER_ROPE_BLOCK
        blk = q_rot[hd // HEADS_PER_ROPE_BLOCK]
        if grp:
            blk = pltpu.roll(blk, LANES - grp * QK_ROPE, 1)
        q_ref[0, hd, :, KV_LORA:KEY_DIM] = blk[:, :QK_ROPE].astype(_BF16)

    kvr = qkv[:, Q_LORA:]
    ckv = _rms(kvr[:, :KV_LORA], kv_g_ref[...])
    k_rot = rope(kvr[:, KV_LORA:])
    ckv_ref[0] = ckv
    kpe_ref[0] = k_rot[:, :QK_ROPE]
    kc_ref[0, :, 0:KV_LORA] = ckv.astype(_BF16)
    kc_ref[0, :, KV_LORA:KEY_DIM] = k_rot[:, :QK_ROPE].astype(_BF16)

    proj = _tail_proj(h, w_in_t_ref)
    sa_ref[0] = jax.nn.sigmoid(proj(_O_MM, D_MODEL)).astype(_BF16)
    ga_ref[0] = _silu(proj(_O_GM, MLA_WIDTH)).astype(_BF16)
    _conv_branch(proj, st_ref, conv_w_ref, w_oc_ref, mb_ref, cst_ref, ub_ref, carry_ref,
                 tm=x_ref.shape[1], n_seq=n_seq, carry_state=False)


def _project_sample(x, cos, sn1, sn2, state, w, *, n_seq):
    _, tm, _ = x.shape

    def rows(width, dtype):
        return (jax.ShapeDtypeStruct((1, tm, width), dtype), pl.BlockSpec((1, tm, width), lambda i: (0, 0, 0)))

    def full(a):
        return pl.BlockSpec(a.shape, lambda i: (0,) * a.ndim)

    outs = [
        (jax.ShapeDtypeStruct((1, N_HEADS, tm, KEY_DIM), _BF16),
         pl.BlockSpec((1, N_HEADS, tm, KEY_DIM), lambda i: (0, 0, 0, 0))),
        rows(KEY_DIM, _BF16), rows(KV_LORA, _F32), rows(QK_ROPE, _F32),
        rows(MLA_WIDTH, _BF16), rows(D_MODEL, _BF16), rows(D_MODEL, _BF16),
        (jax.ShapeDtypeStruct((n_seq, CONV_K - 1, CONV_WIDTH), _F32),
         pl.BlockSpec((n_seq, CONV_K - 1, CONV_WIDTH), lambda i: (0, 0, 0))),
    ]
    weights = (w["pre_g"], w["w_in_t"], w["q_g"], w["w_qabs"], w["w_uqp"],
               w["kv_g"], w["conv_w"], w["w_oc"])
    return pl.pallas_call(
        functools.partial(_proj_sample_kernel, n_seq=n_seq),
        grid=(1,),
        in_specs=[full(x), full(cos), full(sn1), full(sn2), full(state)] + [full(a) for a in weights],
        out_specs=[o[1] for o in outs],
        out_shape=[o[0] for o in outs],
        scratch_shapes=[pltpu.VMEM((n_seq, tm // n_seq + 8, CONV_WIDTH), _F32),
                        pltpu.VMEM((8, CONV_WIDTH), _F32)],
        compiler_params=pltpu.CompilerParams(
            dimension_semantics=("arbitrary",), vmem_limit_bytes=VMEM_LIMIT_BYTES),
        name="proj_sample",
    )(x, cos, sn1, sn2, state, *weights)


def _epilogue(o_lat, ga, sa, mb, x, w_uv_ref, w_om_ref, w_out_ref, post_g_ref):
    o = _dot(o_lat.astype(_BF16), w_uv_ref[...])
    branch_a = _dot((o * ga.astype(_F32)).astype(_BF16), w_om_ref[...])
    merged = sa.astype(_F32) * branch_a + mb.astype(_F32)
    z = _dot(merged.astype(_BF16), w_out_ref[...])
    return x + _rms(z, post_g_ref[...])


def _col_reduce(v, op):
    n_keys, cols = v.shape
    slabs = max(n_keys // REDUCE_SLAB, 1)
    part = op(v.reshape(slabs, n_keys // slabs, cols), axis=0)
    return op(part, axis=0, keepdims=True)


def _attn_prompt_kernel(qt_ref, qn_ref, kc_ref, vt_ref, bias_ref, ga_ref, sa_ref, mb_ref, x_ref,
                        w_uv_ref, w_om_ref, w_out_ref, post_g_ref, y_ref,
                        m_ref, l_ref, acc_ref, s_ref, mc_ref, *, tq, tk):
    i = pl.program_id(1)
    units = tk // MXU_DIM
    q_tiles_per_k_tile = tk // tq
    edge = i // q_tiles_per_k_tile
    groups = tq // COL_BLOCK
    n_cb = N_HEADS * groups

    m_ref[...] = jnp.full(m_ref.shape, NEG_INF, _F32)
    l_ref[...] = jnp.zeros(l_ref.shape, _F32)
    acc_ref[...] = jnp.zeros(acc_ref.shape, _F32)

    def keys_of(tile):
        return kc_ref[0, pl.ds(pl.multiple_of(tile * tk, tk), tk), :]

    def values_of(tile):
        return [vt_ref[0, tile * units + u] for u in range(units)]

    def stage1(k, cb, slot=0, q_tile=None, q_ref=qt_ref, split=True):
        qb = q_ref[0, 0, cb]
        if split:
            s = jnp.concatenate([_dot(k[u * MXU_DIM:(u + 1) * MXU_DIM], qb) for u in range(units)], axis=0)
        else:
            s = _dot(k, qb)
        if q_tile is not None:
            s = s + bias_ref[(q_tile % q_tiles_per_k_tile) * groups + cb % groups]
        s_ref[slot, cb] = s
        mc_ref[slot, cb] = _col_reduce(s, jnp.max)

    def stage2(vts, cb, slot=0):
        m_prev = m_ref[cb]
        m_new = jnp.maximum(m_prev, mc_ref[slot, cb])
        alpha = jnp.exp2(m_prev - m_new)
        p = jnp.exp2(s_ref[slot, cb] - m_new)
        l_ref[cb] = alpha * l_ref[cb] + _col_reduce(p, jnp.sum)
        pb = p.astype(_BF16)
        pv = _dot(vts[0], pb[0:MXU_DIM])
        for u in range(1, units):
            pv = pv + _dot(vts[u], pb[u * MXU_DIM:(u + 1) * MXU_DIM])
        acc_ref[cb] = alpha * acc_ref[cb] + pv
        m_ref[cb] = m_new

    @pl.when(i == 0)
    def _():
        k_edge = keys_of(edge)
        for cb in range(n_cb):
            stage1(k_edge, cb, q_tile=i)

    def advance(j, rd, wr):
        vts = values_of(jnp.where(j == 0, edge, j - 1))
        k = keys_of(j)
        in_place = rd == wr
        lead = 0 if in_place else STAGE1_LEAD
        for cb in range(lead):
            stage1(k, cb, wr, split=in_place)
        for cb in range(n_cb):
            stage2(vts, cb, rd)
            if cb + lead < n_cb:
                stage1(k, cb + lead, wr, split=in_place)

    def single(j, carry):
        advance(j, 0, 0)
        return carry

    def double(pair, carry):
        advance(odd + 2 * pair, 0, 1)
        advance(odd + 2 * pair + 1, 1, 0)
        return carry

    def quad(trip, carry):
        j0 = odd + 2 * odd_pair + 4 * trip
        advance(j0, 0, 1)
        advance(j0 + 1, 1, 0)
        advance(j0 + 2, 0, 1)
        advance(j0 + 3, 1, 0)
        return carry

    odd = edge % 2
    odd_pair = (edge // 2) % 2
    lax.fori_loop(0, odd, single, 0)
    lax.fori_loop(0, odd_pair, double, 0)
    lax.fori_loop(0, edge // 4, quad, 0)

    last_vts = values_of(jnp.where(edge == 0, edge, edge - 1))
    nxt = jnp.minimum(i + 1, pl.num_programs(1) - 1)
    k_next = keys_of(nxt // q_tiles_per_k_tile)
    for cb in range(n_cb):
        stage2(last_vts, cb)
        stage1(k_next, cb, 0, q_tile=nxt, q_ref=qn_ref)

    def head_rows(hd):
        parts = [(acc_ref[hd * groups + g] / l_ref[hd * groups + g]).T for g in range(groups)]
        return parts[0] if groups == 1 else jnp.concatenate(parts, axis=0)

    o_lat = jnp.concatenate([head_rows(hd) for hd in range(N_HEADS)], axis=1)
    y_ref[0] = _epilogue(o_lat, ga_ref[0], sa_ref[0], mb_ref[0], x_ref[0],
                         w_uv_ref, w_om_ref, w_out_ref, post_g_ref)


def _attend_prompt(qt, kc, vt, ga, sa, mb, x, w, *, tq, tk):
    b, t, _ = x.shape
    n_q = t // tq

    def row_spec(width):
        return pl.BlockSpec((1, tq, width), lambda bi, qi: (bi, qi, 0))

    def full(a):
        return pl.BlockSpec(a.shape, lambda bi, qi: (0,) * a.ndim)

    k_chunk = np.arange(tk)[None, :, None] // CHUNK
    q_chunk = (np.arange(tk // COL_BLOCK)[:, None, None] * COL_BLOCK
               + np.arange(COL_BLOCK)[None, None, :]) // CHUNK
    bias = jnp.asarray(np.where(k_chunk <= q_chunk, 0.0, NEG_INF), _F32)
    n_cb = N_HEADS * tq // COL_BLOCK

    weights = (w["w_uv"], w["w_om"], w["w_out"], w["post_g"])
    return pl.pallas_call(
        functools.partial(_attn_prompt_kernel, tq=tq, tk=tk),
        grid=(b, t // tq),
        in_specs=[pl.BlockSpec((1, 1, n_cb, KEY_DIM, COL_BLOCK), lambda bi, qi: (bi, qi, 0, 0, 0)),
                  pl.BlockSpec((1, 1, n_cb, KEY_DIM, COL_BLOCK),
                               lambda bi, qi: (bi, jnp.minimum(qi + 1, n_q - 1), 0, 0, 0)),
                  pl.BlockSpec((1, t, KEY_DIM), lambda bi, qi: (bi, 0, 0), pipeline_mode=pl.Buffered(1)),
                  pl.BlockSpec((1, t // MXU_DIM, KV_LORA, MXU_DIM), lambda bi, qi: (bi, 0, 0, 0),
                               pipeline_mode=pl.Buffered(1)),
                  full(bias),
                  row_spec(MLA_WIDTH), row_spec(D_MODEL), row_spec(D_MODEL), row_spec(D_MODEL)]
                 + [full(a) for a in weights],
        out_specs=row_spec(D_MODEL),
        out_shape=jax.ShapeDtypeStruct((b, t, D_MODEL), _F32),
        scratch_shapes=[pltpu.VMEM((n_cb, 1, COL_BLOCK), _F32), pltpu.VMEM((n_cb, 1, COL_BLOCK), _F32),
                        pltpu.VMEM((n_cb, KV_LORA, COL_BLOCK), _F32),
                        pltpu.VMEM((2, n_cb, tk, COL_BLOCK), _F32), pltpu.VMEM((2, n_cb, 1, COL_BLOCK), _F32)],
        compiler_params=pltpu.CompilerParams(
            dimension_semantics=("arbitrary", "arbitrary"), vmem_limit_bytes=VMEM_LIMIT_BYTES),
        name="attn_prompt",
    )(qt, qt, kc, vt, bias, ga, sa, mb, x, *weights)


def _attn_first(q, k, m_ref, l_ref, acc_ref, visible=None):
    s = _dot_nt(q, k)
    if visible is not None:
        s = jnp.where(visible, s, NEG_INF)
    m = jnp.max(s, axis=1, keepdims=True)
    p = jnp.exp2(s - m)
    m_ref[...] = jnp.broadcast_to(m, m_ref.shape)
    l_ref[...] = jnp.broadcast_to(jnp.sum(p, axis=1, keepdims=True), l_ref.shape)
    acc_ref[...] = _dot(p.astype(_BF16), k[:, :KV_LORA])


def _attn_update(q, k_lat, k_rope_t, m_ref, l_ref, acc_ref):
    s = _dot_nt(q[:, :KV_LORA], k_lat) + _dot(q[:, KV_LORA:], k_rope_t)
    m_prev = m_ref[...]
    m_new = jnp.maximum(m_prev, jnp.max(s, axis=1, keepdims=True))
    alpha = jnp.exp2(m_prev - m_new)
    p = jnp.exp2(s - jnp.concatenate([m_new] * (s.shape[1] // LANES), axis=1))
    l_ref[...] = alpha * l_ref[...] + jnp.sum(p, axis=1, keepdims=True)
    acc_ref[...] = alpha * acc_ref[...] + _dot(p.astype(_BF16), k_lat)
    m_ref[...] = m_new


def _attn_sample_kernel(q_ref, past_lat_ref, past_rope_ref, knew_ref, ga_ref, sa_ref, mb_ref, x_ref,
                        w_uv_ref, w_om_ref, w_out_ref, post_g_ref, y_ref,
                        m_ref, l_ref, acc_ref, o_ref, *, new_visible):
    bi = pl.program_id(0)
    per_step, t_new, _ = knew_ref.shape
    rows = N_HEADS * t_new
    if new_visible is None:
        visible = None
    else:
        q_pos = lax.broadcasted_iota(jnp.int32, (rows, t_new), 0) % t_new
        k_pos = lax.broadcasted_iota(jnp.int32, (rows, t_new), 1)
        visible = (k_pos + new_visible[0]) // CHUNK <= (q_pos + new_visible[0]) // CHUNK
    for g in range(per_step):
        q = q_ref[0, :, g * t_new:(g + 1) * t_new, :].reshape(rows, KEY_DIM)
        m_g, l_g, acc_g = m_ref.at[g], l_ref.at[g], acc_ref.at[g]
        _attn_first(q, knew_ref[g], m_g, l_g, acc_g, visible=visible)
        _attn_update(q, past_lat_ref[g].astype(_BF16), past_rope_ref[g].astype(_BF16), m_g, l_g, acc_g)
        o = acc_g[...] / l_g[...]
        r0 = pl.multiple_of((bi * per_step + g) * t_new, t_new)
        o_ref[pl.ds(r0, t_new), :] = jnp.concatenate(
            [o[hd * t_new:(hd + 1) * t_new] for hd in range(N_HEADS)], axis=1)

    @pl.when(bi == pl.num_programs(0) - 1)
    def _():
        y_ref[...] = _epilogue(o_ref[...], ga_ref[...], sa_ref[...], mb_ref[...], x_ref[...],
                               w_uv_ref, w_om_ref, w_out_ref, post_g_ref)


def _attend_sample(q, past_lat, past_rope, k_new, ga, sa, mb, x, w):
    nb, t_new, _ = k_new.shape
    past_len = past_lat.shape[1]
    n_rows = nb * t_new
    rows = N_HEADS * t_new
    per_step = SAMPLE_STREAMS_PER_STEP if nb % SAMPLE_STREAMS_PER_STEP == 0 else 1
    last_q, first_q = past_len + t_new - 1, past_len
    new_visible = None if last_q // CHUNK == first_q // CHUNK else (past_len,)

    def full(a):
        return pl.BlockSpec(a.shape, lambda bi: (0,) * a.ndim)

    weights = (w["w_uv"], w["w_om"], w["w_out"], w["post_g"])
    return pl.pallas_call(
        functools.partial(_attn_sample_kernel, new_visible=new_visible),
        grid=(nb // per_step,),
        in_specs=[pl.BlockSpec((1, N_HEADS, per_step * t_new, KEY_DIM), lambda bi: (0, 0, bi, 0)),
                  pl.BlockSpec((per_step, past_len, KV_LORA), lambda bi: (bi, 0, 0)),
                  pl.BlockSpec((per_step, QK_ROPE, past_len), lambda bi: (bi, 0, 0)),
                  pl.BlockSpec((per_step, t_new, KEY_DIM), lambda bi: (bi, 0, 0)),
                  full(ga), full(sa), full(mb), full(x)] + [full(a) for a in weights],
        out_specs=pl.BlockSpec((n_rows, D_MODEL), lambda bi: (0, 0)),
        out_shape=jax.ShapeDtypeStruct((n_rows, D_MODEL), _F32),
        scratch_shapes=[pltpu.VMEM((per_step, rows, LANES), _F32), pltpu.VMEM((per_step, rows, LANES), _F32),
                        pltpu.VMEM((per_step, rows, KV_LORA), _F32),
                        pltpu.VMEM((n_rows, N_HEADS * KV_LORA), _F32)],
        compiler_params=pltpu.CompilerParams(
            dimension_semantics=("arbitrary",), vmem_limit_bytes=VMEM_LIMIT_BYTES),
        name="attn_sample",
    )(q, past_lat, past_rope, k_new, ga, sa, mb, x, *weights)


def _block_diag(blocks):
    n = len(blocks)
    r, c = blocks[0].shape
    rows = []
    for j, blk in enumerate(blocks):
        rows.append(jnp.pad(blk, ((0, 0), (j * c, (n - 1 - j) * c))))
    return jnp.concatenate(rows, axis=0)


def _absorb_kernel(w_uk_ref, w_uqn_ref, o_ref):
    for hd in range(N_HEADS):
        o_ref[hd] = lax.dot_general(w_uk_ref[hd], w_uqn_ref[hd], _NT, precision=lax.Precision.HIGHEST,
                                    preferred_element_type=_F32).astype(o_ref.dtype)


def _absorbed_query_weight(w_uk, w_uqn):
    out = pl.pallas_call(
        _absorb_kernel,
        out_shape=jax.ShapeDtypeStruct((N_HEADS, KV_LORA, Q_LORA), _BF16),
        name="absorb_q_weight",
    )(jnp.transpose(w_uk, (1, 0, 2)), jnp.transpose(w_uqn, (1, 0, 2)))
    return out.reshape(N_HEADS * KV_LORA, Q_LORA)


def _prep_weights(pre_norm, w_in, q_norm, w_uq, kv_norm, w_uk, w_uv, w_o_mla, conv_w, w_o_conv, w_out, post_norm):
    assert w_in.shape[1] == _O_TAIL + _TAIL_COLS
    wq = w_uq.reshape(Q_LORA, N_HEADS, QK_NOPE + QK_ROPE)
    w_uqp = wq[:, :, QK_NOPE:].reshape(Q_LORA, N_HEADS * QK_ROPE).astype(_BF16)
    return {
        "pre_g": pre_norm.reshape(1, D_MODEL),
        "w_in_t": w_in.T.astype(_BF16),
        "q_g": q_norm.reshape(1, Q_LORA),
        "w_qabs": _absorbed_query_weight(w_uk, wq[:, :, :QK_NOPE]),
        "w_uqp": w_uqp,
        "w_uqp_t": w_uqp.T,
        "kv_g": kv_norm.reshape(1, KV_LORA),
        "conv_w": conv_w,
        "w_oc": w_o_conv.astype(_BF16),
        "w_uv": _block_diag([w_uv[:, hd, :].astype(_BF16) for hd in range(N_HEADS)]),
        "w_om": w_o_mla.astype(_BF16),
        "w_out": w_out.astype(_BF16),
        "post_g": post_norm.reshape(1, D_MODEL),
    }


def _rope_angles(pos, frequency_major=False):
    inv = ROPE_BASE ** (-jnp.arange(HALF_ROPE, dtype=_F32) / HALF_ROPE)
    if frequency_major:
        ang = inv[:, None] * pos.astype(_F32)[None, :]
    else:
        ang = pos.astype(_F32)[:, None] * inv[None, :]
    return jnp.cos(ang), jnp.sin(ang)


def _rope_tables_rows(pos):
    cos, sin = _rope_angles(pos)
    zero = jnp.zeros_like(sin)
    reps = LANES // QK_ROPE
    return (jnp.tile(jnp.concatenate([cos, cos], axis=1), (1, reps)),
            jnp.tile(jnp.concatenate([zero, sin], axis=1), (1, reps)),
            jnp.tile(jnp.concatenate([-sin, zero], axis=1), (1, reps)))


SAMPLE_STREAMS_PER_STEP = 4
PROMPT_ROW_TILE = 1024
PROMPT_Q_TILE = 512
PROMPT_K_TILE = 512


def kernel(x_prompt, x_sample, cache_kv_latent, cache_k_rope, state_conv, pre_norm, w_in, q_norm, w_uq, kv_norm,
           w_uk, w_uv, w_o_mla, conv_w, w_o_conv, w_out, post_norm):
    depth = pre_norm.shape[0]
    assert depth == 1
    b, t, _ = x_prompt.shape
    nb, t_new, _ = x_sample.shape
    past_len = cache_kv_latent.shape[2]
    lyr = 0
    w = _prep_weights(pre_norm[lyr], w_in[lyr], q_norm[lyr], w_uq[lyr], kv_norm[lyr], w_uk[lyr], w_uv[lyr],
                      w_o_mla[lyr], conv_w[lyr], w_o_conv[lyr], w_out[lyr], post_norm[lyr])

    cos_t, sin_t = _rope_angles(jnp.arange(t, dtype=jnp.int32), frequency_major=True)
    zero_state = jnp.zeros((b, CONV_K - 1, CONV_WIDTH), _F32)
    qt, kc, vt, ckv_p, kpe_t, ga, sa, mb, cv_p = _project_prompt(
        x_prompt, cos_t, sin_t, zero_state, w, tm=PROMPT_ROW_TILE, tq=PROMPT_Q_TILE)
    y_p = _attend_prompt(qt, kc, vt, ga, sa, mb, x_prompt, w, tq=PROMPT_Q_TILE, tk=PROMPT_K_TILE)

    n_rows = nb * t_new
    pos_s = past_len + jnp.arange(t_new, dtype=jnp.int32)
    tabs_s = tuple(jnp.tile(tb, (nb, 1)) for tb in _rope_tables_rows(pos_s))
    xs = x_sample.reshape(1, n_rows, D_MODEL)
    q_s, kc_s, ckv_s, kpe_s, ga_s, sa_s, mb_s, cv_s = _project_sample(
        xs, *tabs_s, state_conv[lyr], w, n_seq=nb)
    y_s = _attend_sample(q_s, cache_kv_latent[lyr], jnp.swapaxes(cache_k_rope[lyr], 1, 2),
                         kc_s.reshape(nb, t_new, KEY_DIM),
                         ga_s[0], sa_s[0], mb_s[0], xs[0], w)

    return (y_p, y_s.reshape(nb, t_new, D_MODEL),
            ckv_p[None], jnp.swapaxes(kpe_t, 1, 2)[None], cv_p[None],
            ckv_s.reshape(1, nb, t_new, KV_LORA), kpe_s.reshape(1, nb, t_new, QK_ROPE), cv_s[None])
```

```python
import functools

import numpy as np
import jax
import jax.numpy as jnp
from jax import lax
from jax.experimental import pallas as pl
from jax.experimental.pallas import tpu as pltpu

N_HEADS = 8
QK_NOPE = 64
QK_ROPE = 32
V_HEAD = 64
Q_LORA = 256
KV_LORA = 128
MLA_WIDTH = N_HEADS * V_HEAD
CONV_WIDTH = 512
CONV_K = 3
D_MODEL = 1024
CHUNK = 64
ROPE_BASE = 10000.0
EPS = 1e-6
SM_SCALE = (QK_NOPE + QK_ROPE) ** -0.5
NEG_INF = -1e30
LOG2E = 1.4426950408889634

LANES = 128
MXU_DIM = 256
COL_BLOCK = MXU_DIM
REDUCE_SLAB = 256
STAGE1_LEAD = 1
HALF_ROPE = QK_ROPE // 2
KEY_DIM = KV_LORA + QK_ROPE
HEADS_PER_ROPE_BLOCK = LANES // QK_ROPE
Q_SCALE = SM_SCALE * LOG2E
VMEM_LIMIT_BYTES = 56 * 1024 * 1024

_O_TAIL = Q_LORA + KV_LORA + QK_ROPE
_QKV_COLS = -(-_O_TAIL // MXU_DIM) * MXU_DIM
_O_GM = 0
_O_CB = _O_GM + MLA_WIDTH
_O_CC = _O_CB + CONV_WIDTH
_O_CX = _O_CC + CONV_WIDTH
_O_GC = _O_CX + CONV_WIDTH
_O_MM = _O_GC + CONV_WIDTH
_O_MC = _O_MM + D_MODEL
_TAIL_COLS = _O_MC + D_MODEL

_F32 = jnp.float32
_BF16 = jnp.bfloat16
_NT = (((1,), (1,)), ((), ()))


def _rms(v, g):
    return v * lax.rsqrt(jnp.mean(v * v, axis=-1, keepdims=True) + EPS) * g


def _silu(v):
    return v * jax.nn.sigmoid(v)


def _dot(a, b):
    return jnp.dot(a, b, preferred_element_type=_F32)


def _dot_nt(a, b):
    return lax.dot_general(a, b, _NT, preferred_element_type=_F32)


def _tail_proj(h, w_in_t_ref):
    def proj(off, n):
        return _dot_nt(h, w_in_t_ref[_O_TAIL + off:_O_TAIL + off + n, :])
    return proj


def _conv_branch(proj, st_ref, conv_w_ref, w_oc_ref, mb_ref, cst_ref, ub_ref, carry_ref, *,
                 tm, n_seq, carry_state):
    seq_len = tm // n_seq
    u = proj(_O_CC, CONV_WIDTH) * proj(_O_CX, CONV_WIDTH)
    w0 = conv_w_ref[0:1, :]
    w1 = conv_w_ref[1:2, :]
    w2 = conv_w_ref[2:3, :]
    if carry_state:
        @pl.when(pl.program_id(1) == 0)
        def _():
            carry_ref[0:CONV_K - 1, :] = st_ref[0]
    convs = []
    for j in range(n_seq):
        u_j = u[j * seq_len:(j + 1) * seq_len]
        prev = carry_ref[0:CONV_K - 1, :] if carry_state else st_ref[j]
        ub_ref[j, 8 - (CONV_K - 1):8, :] = prev
        ub_ref[j, 8:8 + seq_len, :] = u_j
        convs.append(w0 * ub_ref[j, 6:6 + seq_len, :] + w1 * ub_ref[j, 7:7 + seq_len, :] + w2 * u_j)
        new_state = u_j[seq_len - (CONV_K - 1):, :]
        cst_ref[j] = new_state
        if carry_state:
            carry_ref[0:CONV_K - 1, :] = new_state
    conv = convs[0] if n_seq == 1 else jnp.concatenate(convs, axis=0)
    c_b = proj(_O_CB, CONV_WIDTH)
    g_conv = proj(_O_GC, CONV_WIDTH)
    merge_gate = jax.nn.sigmoid(proj(_O_MC, D_MODEL))
    bb = (c_b * conv * _silu(g_conv)).astype(_BF16)
    branch_b = _dot(bb, w_oc_ref[...])
    mb_ref[0] = (merge_gate * branch_b).astype(_BF16)


def _proj_prompt_kernel(x_ref, cos_ref, sin_ref, st_ref, pre_g_ref, w_in_t_ref, q_g_ref,
                        w_qabs_ref, w_uqp_t_ref, kv_g_ref, conv_w_ref, w_oc_ref,
                        qt_ref, kc_ref, vt_ref, ckv_ref, kpet_ref, ga_ref, sa_ref, mb_ref, cst_ref,
                        ub_ref, carry_ref, *, tq):
    tm = x_ref.shape[1]
    h = _rms(x_ref[0], pre_g_ref[...]).astype(_BF16)
    cos_t = cos_ref[...]
    sin_t = sin_ref[...]

    qkv = _dot_nt(h, w_in_t_ref[0:_QKV_COLS, :])
    proj = _tail_proj(h, w_in_t_ref)
    sa_ref[0] = jax.nn.sigmoid(proj(_O_MM, D_MODEL)).astype(_BF16)
    ga_ref[0] = _silu(proj(_O_GM, MLA_WIDTH)).astype(_BF16)
    qn = _rms(qkv[:, :Q_LORA], q_g_ref[...]).astype(_BF16)
    q_abs_t = _dot_nt(w_qabs_ref[...], qn)
    q_pe_t = _dot_nt(w_uqp_t_ref[...], qn).reshape(N_HEADS, QK_ROPE, tm)
    x1 = q_pe_t[:, :HALF_ROPE, :]
    x2 = q_pe_t[:, HALF_ROPE:, :]
    r1 = (x1 * cos_t - x2 * sin_t) * Q_SCALE
    r2 = (x2 * cos_t + x1 * sin_t) * Q_SCALE
    groups = tq // COL_BLOCK
    for j in range(tm // tq):
        for hd in range(N_HEADS):
            for g in range(groups):
                tok = slice(j * tq + g * COL_BLOCK, j * tq + (g + 1) * COL_BLOCK)
                cb = hd * groups + g
                qt_ref[0, j, cb, 0:KV_LORA, :] = (
                    q_abs_t[hd * KV_LORA:(hd + 1) * KV_LORA, tok] * Q_SCALE).astype(_BF16)
                qt_ref[0, j, cb, KV_LORA:KV_LORA + HALF_ROPE, :] = r1[hd][:, tok].astype(_BF16)
                qt_ref[0, j, cb, KV_LORA + HALF_ROPE:KEY_DIM, :] = r2[hd][:, tok].astype(_BF16)

    kvr = qkv[:, Q_LORA:]
    ckv = _rms(kvr[:, :KV_LORA], kv_g_ref[...])
    ckv_ref[0] = ckv
    ckv_t = ckv.T.astype(_BF16)
    for u in range(tm // MXU_DIM):
        vt_ref[0, u] = ckv_t[:, u * MXU_DIM:(u + 1) * MXU_DIM]
    kr_t = kvr[:, KV_LORA:].T
    k1 = kr_t[0:HALF_ROPE]
    k2 = kr_t[HALF_ROPE:QK_ROPE]
    kpe_t = jnp.concatenate([k1 * cos_t - k2 * sin_t, k2 * cos_t + k1 * sin_t], axis=0)
    kpet_ref[0] = kpe_t
    kpe = jnp.concatenate([kpe_t, jnp.zeros((LANES - QK_ROPE, tm), _F32)], axis=0).T
    kc_ref[0, :, 0:KV_LORA] = ckv.astype(_BF16)
    kc_ref[0, :, KV_LORA:KEY_DIM] = kpe[:, :QK_ROPE].astype(_BF16)

    _conv_branch(proj, st_ref, conv_w_ref, w_oc_ref, mb_ref, cst_ref, ub_ref, carry_ref,
                 tm=tm, n_seq=1, carry_state=True)


def _project_prompt(x, cos_t, sin_t, state, w, *, tm, tq):
    b, t, _ = x.shape

    def rows(width, dtype):
        return (jax.ShapeDtypeStruct((b, t, width), dtype),
                pl.BlockSpec((1, tm, width), lambda bi, ti: (bi, ti, 0)))

    def full(a):
        return pl.BlockSpec(a.shape, lambda bi, ti: (0,) * a.ndim)

    outs = [
        (jax.ShapeDtypeStruct((b, t // tq, N_HEADS * tq // COL_BLOCK, KEY_DIM, COL_BLOCK), _BF16),
         pl.BlockSpec((1, tm // tq, N_HEADS * tq // COL_BLOCK, KEY_DIM, COL_BLOCK),
                      lambda bi, ti: (bi, ti, 0, 0, 0))),
        rows(KEY_DIM, _BF16),
        (jax.ShapeDtypeStruct((b, t // MXU_DIM, KV_LORA, MXU_DIM), _BF16),
         pl.BlockSpec((1, tm // MXU_DIM, KV_LORA, MXU_DIM), lambda bi, ti: (bi, ti, 0, 0))),
        rows(KV_LORA, _F32),
        (jax.ShapeDtypeStruct((b, QK_ROPE, t), _F32),
         pl.BlockSpec((1, QK_ROPE, tm), lambda bi, ti: (bi, 0, ti))),
        rows(MLA_WIDTH, _BF16),
        rows(D_MODEL, _BF16),
        rows(D_MODEL, _BF16),
        (jax.ShapeDtypeStruct((b, CONV_K - 1, CONV_WIDTH), _F32),
         pl.BlockSpec((1, CONV_K - 1, CONV_WIDTH), lambda bi, ti: (bi, 0, 0))),
    ]
    table_spec = pl.BlockSpec((HALF_ROPE, tm), lambda bi, ti: (0, ti))
    weights = (w["pre_g"], w["w_in_t"], w["q_g"], w["w_qabs"], w["w_uqp_t"],
               w["kv_g"], w["conv_w"], w["w_oc"])
    return pl.pallas_call(
        functools.partial(_proj_prompt_kernel, tq=tq),
        grid=(b, t // tm),
        in_specs=[pl.BlockSpec((1, tm, D_MODEL), lambda bi, ti: (bi, ti, 0)),
                  table_spec, table_spec,
                  pl.BlockSpec((1, CONV_K - 1, CONV_WIDTH), lambda bi, ti: (bi, 0, 0))]
                 + [full(a) for a in weights],
        out_specs=[o[1] for o in outs],
        out_shape=[o[0] for o in outs],
        scratch_shapes=[pltpu.VMEM((1, tm + 8, CONV_WIDTH), _F32),
                        pltpu.VMEM((8, CONV_WIDTH), _F32)],
        compiler_params=pltpu.CompilerParams(
            dimension_semantics=("arbitrary", "arbitrary"), vmem_limit_bytes=VMEM_LIMIT_BYTES),
        name="proj_prompt",
    )(x, cos_t, sin_t, state, *weights)


def _proj_sample_kernel(x_ref, cos_ref, sn1_ref, sn2_ref, st_ref, pre_g_ref, w_in_t_ref, q_g_ref,
                        w_qabs_ref, w_uqp_ref, kv_g_ref, conv_w_ref, w_oc_ref,
                        q_ref, kc_ref, ckv_ref, kpe_ref, ga_ref, sa_ref, mb_ref, cst_ref,
                        ub_ref, carry_ref, *, n_seq):
    h = _rms(x_ref[0], pre_g_ref[...]).astype(_BF16)
    cos = cos_ref[...]
    sn1 = sn1_ref[...]
    sn2 = sn2_ref[...]

    def rope(v):
        return (v * cos + pltpu.roll(v, HALF_ROPE, 1) * sn1
                + pltpu.roll(v, LANES - HALF_ROPE, 1) * sn2)

    qkv = _dot_nt(h, w_in_t_ref[0:_QKV_COLS, :])
    qn = _rms(qkv[:, :Q_LORA], q_g_ref[...]).astype(_BF16)
    q_abs = _dot_nt(qn, w_qabs_ref[...])
    q_pe = _dot(qn, w_uqp_ref[...])
    n_rope_blocks = N_HEADS // HEADS_PER_ROPE_BLOCK
    q_rot = [rope(q_pe[:, j * LANES:(j + 1) * LANES]) * Q_SCALE for j in range(n_rope_blocks)]
    for hd in range(N_HEADS):
        q_ref[0, hd, :, 0:KV_LORA] = (q_abs[:, hd * LANES:(hd + 1) * LANES] * Q_SCALE).astype(_BF16)
        grp = hd % HEADS_PER_ROPE_BLOCK
        blk = q_rot[hd // HEADS_PER_ROPE_BLOCK]
        if grp:
            blk = pltpu.roll(blk, LANES - grp * QK_ROPE, 1)
        q_ref[0, hd, :, KV_LORA:KEY_DIM] = blk[:, :QK_ROPE].astype(_BF16)

    kvr = qkv[:, Q_LORA:]
    ckv = _rms(kvr[:, :KV_LORA], kv_g_ref[...])
    k_rot = rope(kvr[:, KV_LORA:])
    ckv_ref[0] = ckv
    kpe_ref[0] = k_rot[:, :QK_ROPE]
    kc_ref[0, :, 0:KV_LORA] = ckv.astype(_BF16)
    kc_ref[0, :, KV_LORA:KEY_DIM] = k_rot[:, :QK_ROPE].astype(_BF16)

    proj = _tail_proj(h, w_in_t_ref)
    sa_ref[0] = jax.nn.sigmoid(proj(_O_MM, D_MODEL)).astype(_BF16)
    ga_ref[0] = _silu(proj(_O_GM, MLA_WIDTH)).astype(_BF16)
    _conv_branch(proj, st_ref, conv_w_ref, w_oc_ref, mb_ref, cst_ref, ub_ref, carry_ref,
                 tm=x_ref.shape[1], n_seq=n_seq, carry_state=False)


def _project_sample(x, cos, sn1, sn2, state, w, *, n_seq):
    _, tm, _ = x.shape

    def rows(width, dtype):
        return (jax.ShapeDtypeStruct((1, tm, width), dtype), pl.BlockSpec((1, tm, width), lambda i: (0, 0, 0)))

    def full(a):
        return pl.BlockSpec(a.shape, lambda i: (0,) * a.ndim)

    outs = [
        (jax.ShapeDtypeStruct((1, N_HEADS, tm, KEY_DIM), _BF16),
         pl.BlockSpec((1, N_HEADS, tm, KEY_DIM), lambda i: (0, 0, 0, 0))),
        rows(KEY_DIM, _BF16), rows(KV_LORA, _F32), rows(QK_ROPE, _F32),
        rows(MLA_WIDTH, _BF16), rows(D_MODEL, _BF16), rows(D_MODEL, _BF16),
        (jax.ShapeDtypeStruct((n_seq, CONV_K - 1, CONV_WIDTH), _F32),
         pl.BlockSpec((n_seq, CONV_K - 1, CONV_WIDTH), lambda i: (0, 0, 0))),
    ]
    weights = (w["pre_g"], w["w_in_t"], w["q_g"], w["w_qabs"], w["w_uqp"],
               w["kv_g"], w["conv_w"], w["w_oc"])
    return pl.pallas_call(
        functools.partial(_proj_sample_kernel, n_seq=n_seq),
        grid=(1,),
        in_specs=[full(x), full(cos), full(sn1), full(sn2), full(state)] + [full(a) for a in weights],
        out_specs=[o[1] for o in outs],
        out_shape=[o[0] for o in outs],
        scratch_shapes=[pltpu.VMEM((n_seq, tm // n_seq + 8, CONV_WIDTH), _F32),
                        pltpu.VMEM((8, CONV_WIDTH), _F32)],
        compiler_params=pltpu.CompilerParams(
            dimension_semantics=("arbitrary",), vmem_limit_bytes=VMEM_LIMIT_BYTES),
        name="proj_sample",
    )(x, cos, sn1, sn2, state, *weights)


def _epilogue(o_lat, ga, sa, mb, x, w_uv_ref, w_om_ref, w_out_ref, post_g_ref):
    o = _dot(o_lat.astype(_BF16), w_uv_ref[...])
    branch_a = _dot((o * ga.astype(_F32)).astype(_BF16), w_om_ref[...])
    merged = sa.astype(_F32) * branch_a + mb.astype(_F32)
    z = _dot(merged.astype(_BF16), w_out_ref[...])
    return x + _rms(z, post_g_ref[...])


def _col_reduce(v, op):
    n_keys, cols = v.shape
    slabs = max(n_keys // REDUCE_SLAB, 1)
    part = op(v.reshape(slabs, n_keys // slabs, cols), axis=0)
    return op(part, axis=0, keepdims=True)


def _attn_prompt_kernel(qt_ref, qn_ref, kc_ref, vt_ref, bias_ref, ga_ref, sa_ref, mb_ref, x_ref,
                        w_uv_ref, w_om_ref, w_out_ref, post_g_ref, y_ref,
                        m_ref, l_ref, acc_ref, s_ref, mc_ref, *, tq, tk):
    i = pl.program_id(1)
    units = tk // MXU_DIM
    q_tiles_per_k_tile = tk // tq
    edge = i // q_tiles_per_k_tile
    groups = tq // COL_BLOCK
    n_cb = N_HEADS * groups

    m_ref[...] = jnp.full(m_ref.shape, NEG_INF, _F32)
    l_ref[...] = jnp.zeros(l_ref.shape, _F32)
    acc_ref[...] = jnp.zeros(acc_ref.shape, _F32)

    def keys_of(tile):
        return kc_ref[0, pl.ds(pl.multiple_of(tile * tk, tk), tk), :]

    def values_of(tile):
        return [vt_ref[0, tile * units + u] for u in range(units)]

    def stage1(k, cb, slot=0, q_tile=None, q_ref=qt_ref, split=True):
        qb = q_ref[0, 0, cb]
        if split:
            s = jnp.concatenate([_dot(k[u * MXU_DIM:(u + 1) * MXU_DIM], qb) for u in range(units)], axis=0)
        else:
            s = _dot(k, qb)
        if q_tile is not None:
            s = s + bias_ref[(q_tile % q_tiles_per_k_tile) * groups + cb % groups]
        s_ref[slot, cb] = s
        mc_ref[slot, cb] = _col_reduce(s, jnp.max)

    def stage2(vts, cb, slot=0):
        m_prev = m_ref[cb]
        m_new = jnp.maximum(m_prev, mc_ref[slot, cb])
        alpha = jnp.exp2(m_prev - m_new)
        p = jnp.exp2(s_ref[slot, cb] - m_new)
        l_ref[cb] = alpha * l_ref[cb] + _col_reduce(p, jnp.sum)
        pb = p.astype(_BF16)
        pv = _dot(vts[0], pb[0:MXU_DIM])
        for u in range(1, units):
            pv = pv + _dot(vts[u], pb[u * MXU_DIM:(u + 1) * MXU_DIM])
        acc_ref[cb] = alpha * acc_ref[cb] + pv
        m_ref[cb] = m_new

    @pl.when(i == 0)
    def _():
        k_edge = keys_of(edge)
        for cb in range(n_cb):
            stage1(k_edge, cb, q_tile=i)

    def advance(j, rd, wr):
        vts = values_of(jnp.where(j == 0, edge, j - 1))
        k = keys_of(j)
        in_place = rd == wr
        lead = 0 if in_place else STAGE1_LEAD
        for cb in range(lead):
            stage1(k, cb, wr, split=in_place)
        for cb in range(n_cb):
            stage2(vts, cb, rd)
            if cb + lead < n_cb:
                stage1(k, cb + lead, wr, split=in_place)

    def single(j, carry):
        advance(j, 0, 0)
        return carry

    def double(pair, carry):
        advance(odd + 2 * pair, 0, 1)
        advance(odd + 2 * pair + 1, 1, 0)
        return carry

    odd = edge % 2
    lax.fori_loop(0, odd, single, 0)
    lax.fori_loop(0, edge // 2, double, 0)

    last_vts = values_of(jnp.where(edge == 0, edge, edge - 1))
    nxt = jnp.minimum(i + 1, pl.num_programs(1) - 1)
    k_next = keys_of(nxt // q_tiles_per_k_tile)
    for cb in range(n_cb):
        stage2(last_vts, cb)
        stage1(k_next, cb, 0, q_tile=nxt, q_ref=qn_ref)

    def head_rows(hd):
        parts = [(acc_ref[hd * groups + g] / l_ref[hd * groups + g]).T for g in range(groups)]
        return parts[0] if groups == 1 else jnp.concatenate(parts, axis=0)

    o_lat = jnp.concatenate([head_rows(hd) for hd in range(N_HEADS)], axis=1)
    y_ref[0] = _epilogue(o_lat, ga_ref[0], sa_ref[0], mb_ref[0], x_ref[0],
                         w_uv_ref, w_om_ref, w_out_ref, post_g_ref)


def _attend_prompt(qt, kc, vt, ga, sa, mb, x, w, *, tq, tk):
    b, t, _ = x.shape
    n_q = t // tq

    def row_spec(width):
        return pl.BlockSpec((1, tq, width), lambda bi, qi: (bi, qi, 0))

    def full(a):
        return pl.BlockSpec(a.shape, lambda bi, qi: (0,) * a.ndim)

    k_chunk = np.arange(tk)[None, :, None] // CHUNK
    q_chunk = (np.arange(tk // COL_BLOCK)[:, None, None] * COL_BLOCK
               + np.arange(COL_BLOCK)[None, None, :]) // CHUNK
    bias = jnp.asarray(np.where(k_chunk <= q_chunk, 0.0, NEG_INF), _F32)
    n_cb = N_HEADS * tq // COL_BLOCK

    weights = (w["w_uv"], w["w_om"], w["w_out"], w["post_g"])
    return pl.pallas_call(
        functools.partial(_attn_prompt_kernel, tq=tq, tk=tk),
        grid=(b, t // tq),
        in_specs=[pl.BlockSpec((1, 1, n_cb, KEY_DIM, COL_BLOCK), lambda bi, qi: (bi, qi, 0, 0, 0)),
                  pl.BlockSpec((1, 1, n_cb, KEY_DIM, COL_BLOCK),
                               lambda bi, qi: (bi, jnp.minimum(qi + 1, n_q - 1), 0, 0, 0)),
                  pl.BlockSpec((1, t, KEY_DIM), lambda bi, qi: (bi, 0, 0), pipeline_mode=pl.Buffered(1)),
                  pl.BlockSpec((1, t // MXU_DIM, KV_LORA, MXU_DIM), lambda bi, qi: (bi, 0, 0, 0),
                               pipeline_mode=pl.Buffered(1)),
                  full(bias),
                  row_spec(MLA_WIDTH), row_spec(D_MODEL), row_spec(D_MODEL), row_spec(D_MODEL)]
                 + [full(a) for a in weights],
        out_specs=row_spec(D_MODEL),
        out_shape=jax.ShapeDtypeStruct((b, t, D_MODEL), _F32),
        scratch_shapes=[pltpu.VMEM((n_cb, 1, COL_BLOCK), _F32), pltpu.VMEM((n_cb, 1, COL_BLOCK), _F32),
                        pltpu.VMEM((n_cb, KV_LORA, COL_BLOCK), _F32),
                        pltpu.VMEM((2, n_cb, tk, COL_BLOCK), _F32), pltpu.VMEM((2, n_cb, 1, COL_BLOCK), _F32)],
        compiler_params=pltpu.CompilerParams(
            dimension_semantics=("arbitrary", "arbitrary"), vmem_limit_bytes=VMEM_LIMIT_BYTES),
        name="attn_prompt",
    )(qt, qt, kc, vt, bias, ga, sa, mb, x, *weights)


def _attn_first(q, k, m_ref, l_ref, acc_ref, visible=None):
    s = _dot_nt(q, k)
    if visible is not None:
        s = jnp.where(visible, s, NEG_INF)
    m = jnp.max(s, axis=1, keepdims=True)
    p = jnp.exp2(s - m)
    m_ref[...] = jnp.broadcast_to(m, m_ref.shape)
    l_ref[...] = jnp.broadcast_to(jnp.sum(p, axis=1, keepdims=True), l_ref.shape)
    acc_ref[...] = _dot(p.astype(_BF16), k[:, :KV_LORA])


def _attn_update(q, k_lat, k_rope_t, m_ref, l_ref, acc_ref):
    s = _dot_nt(q[:, :KV_LORA], k_lat) + _dot(q[:, KV_LORA:], k_rope_t)
    m_prev = m_ref[...]
    m_new = jnp.maximum(m_prev, jnp.max(s, axis=1, keepdims=True))
    alpha = jnp.exp2(m_prev - m_new)
    p = jnp.exp2(s - jnp.concatenate([m_new] * (s.shape[1] // LANES), axis=1))
    l_ref[...] = alpha * l_ref[...] + jnp.sum(p, axis=1, keepdims=True)
    acc_ref[...] = alpha * acc_ref[...] + _dot(p.astype(_BF16), k_lat)
    m_ref[...] = m_new


def _attn_sample_kernel(q_ref, past_lat_ref, past_rope_ref, knew_ref, ga_ref, sa_ref, mb_ref, x_ref,
                        w_uv_ref, w_om_ref, w_out_ref, post_g_ref, y_ref,
                        m_ref, l_ref, acc_ref, o_ref, *, new_visible):
    bi = pl.program_id(0)
    per_step, t_new, _ = knew_ref.shape
    rows = N_HEADS * t_new
    if new_visible is None:
        visible = None
    else:
        q_pos = lax.broadcasted_iota(jnp.int32, (rows, t_new), 0) % t_new
        k_pos = lax.broadcasted_iota(jnp.int32, (rows, t_new), 1)
        visible = (k_pos + new_visible[0]) // CHUNK <= (q_pos + new_visible[0]) // CHUNK
    for g in range(per_step):
        q = q_ref[0, :, g * t_new:(g + 1) * t_new, :].reshape(rows, KEY_DIM)
        m_g, l_g, acc_g = m_ref.at[g], l_ref.at[g], acc_ref.at[g]
        _attn_first(q, knew_ref[g], m_g, l_g, acc_g, visible=visible)
        _attn_update(q, past_lat_ref[g].astype(_BF16), past_rope_ref[g].astype(_BF16), m_g, l_g, acc_g)
        o = acc_g[...] / l_g[...]
        r0 = pl.multiple_of((bi * per_step + g) * t_new, t_new)
        o_ref[pl.ds(r0, t_new), :] = jnp.concatenate(
            [o[hd * t_new:(hd + 1) * t_new] for hd in range(N_HEADS)], axis=1)

    @pl.when(bi == pl.num_programs(0) - 1)
    def _():
        y_ref[...] = _epilogue(o_ref[...], ga_ref[...], sa_ref[...], mb_ref[...], x_ref[...],
                               w_uv_ref, w_om_ref, w_out_ref, post_g_ref)


def _attend_sample(q, past_lat, past_rope, k_new, ga, sa, mb, x, w):
    nb, t_new, _ = k_new.shape
    past_len = past_lat.shape[1]
    n_rows = nb * t_new
    rows = N_HEADS * t_new
    per_step = SAMPLE_STREAMS_PER_STEP if nb % SAMPLE_STREAMS_PER_STEP == 0 else 1
    last_q, first_q = past_len + t_new - 1, past_len
    new_visible = None if last_q // CHUNK == first_q // CHUNK else (past_len,)

    def full(a):
        return pl.BlockSpec(a.shape, lambda bi: (0,) * a.ndim)

    weights = (w["w_uv"], w["w_om"], w["w_out"], w["post_g"])
    return pl.pallas_call(
        functools.partial(_attn_sample_kernel, new_visible=new_visible),
        grid=(nb // per_step,),
        in_specs=[pl.BlockSpec((1, N_HEADS, per_step * t_new, KEY_DIM), lambda bi: (0, 0, bi, 0)),
                  pl.BlockSpec((per_step, past_len, KV_LORA), lambda bi: (bi, 0, 0)),
                  pl.BlockSpec((per_step, QK_ROPE, past_len), lambda bi: (bi, 0, 0)),
                  pl.BlockSpec((per_step, t_new, KEY_DIM), lambda bi: (bi, 0, 0)),
                  full(ga), full(sa), full(mb), full(x)] + [full(a) for a in weights],
        out_specs=pl.BlockSpec((n_rows, D_MODEL), lambda bi: (0, 0)),
        out_shape=jax.ShapeDtypeStruct((n_rows, D_MODEL), _F32),
        scratch_shapes=[pltpu.VMEM((per_step, rows, LANES), _F32), pltpu.VMEM((per_step, rows, LANES), _F32),
                        pltpu.VMEM((per_step, rows, KV_LORA), _F32),
                        pltpu.VMEM((n_rows, N_HEADS * KV_LORA), _F32)],
        compiler_params=pltpu.CompilerParams(
            dimension_semantics=("arbitrary",), vmem_limit_bytes=VMEM_LIMIT_BYTES),
        name="attn_sample",
    )(q, past_lat, past_rope, k_new, ga, sa, mb, x, *weights)


def _absorb_kernel(w_uk_ref, w_uqn_ref, o_ref):
    for hd in range(N_HEADS):
        o_ref[hd] = lax.dot_general(w_uk_ref[hd], w_uqn_ref[hd], _NT, precision=lax.Precision.HIGHEST,
                                    preferred_element_type=_F32).astype(o_ref.dtype)


def _absorbed_query_weight(w_uk, w_uqn):
    out = pl.pallas_call(
        _absorb_kernel,
        out_shape=jax.ShapeDtypeStruct((N_HEADS, KV_LORA, Q_LORA), _BF16),
        name="absorb_q_weight",
    )(jnp.transpose(w_uk, (1, 0, 2)), jnp.transpose(w_uqn, (1, 0, 2)))
    return out.reshape(N_HEADS * KV_LORA, Q_LORA)


def _prep_weights(pre_norm, w_in, q_norm, w_uq, kv_norm, w_uk, w_uv, w_o_mla, conv_w, w_o_conv, w_out, post_norm):
    assert w_in.shape[1] == _O_TAIL + _TAIL_COLS
    wq = w_uq.reshape(Q_LORA, N_HEADS, QK_NOPE + QK_ROPE)
    w_uqp = wq[:, :, QK_NOPE:].reshape(Q_LORA, N_HEADS * QK_ROPE).astype(_BF16)
    return {
        "pre_g": pre_norm.reshape(1, D_MODEL),
        "w_in_t": w_in.T.astype(_BF16),
        "q_g": q_norm.reshape(1, Q_LORA),
        "w_qabs": _absorbed_query_weight(w_uk, wq[:, :, :QK_NOPE]),
        "w_uqp": w_uqp,
        "w_uqp_t": w_uqp.T,
        "kv_g": kv_norm.reshape(1, KV_LORA),
        "conv_w": conv_w,
        "w_oc": w_o_conv.astype(_BF16),
        "w_uv": (jnp.transpose(w_uv, (1, 0, 2))[:, :, None, :]
                 * jnp.eye(N_HEADS, dtype=_F32)[:, None, :, None]
                 ).reshape(N_HEADS * KV_LORA, N_HEADS * V_HEAD).astype(_BF16),
        "w_om": w_o_mla.astype(_BF16),
        "w_out": w_out.astype(_BF16),
        "post_g": post_norm.reshape(1, D_MODEL),
    }


def _rope_angles(pos, frequency_major=False):
    inv = ROPE_BASE ** (-jnp.arange(HALF_ROPE, dtype=_F32) / HALF_ROPE)
    if frequency_major:
        ang = inv[:, None] * pos.astype(_F32)[None, :]
    else:
        ang = pos.astype(_F32)[:, None] * inv[None, :]
    return jnp.cos(ang), jnp.sin(ang)


def _rope_tables_rows(pos):
    cos, sin = _rope_angles(pos)
    zero = jnp.zeros_like(sin)
    reps = LANES // QK_ROPE
    return (jnp.tile(jnp.concatenate([cos, cos], axis=1), (1, reps)),
            jnp.tile(jnp.concatenate([zero, sin], axis=1), (1, reps)),
            jnp.tile(jnp.concatenate([-sin, zero], axis=1), (1, reps)))


SAMPLE_STREAMS_PER_STEP = 4
PROMPT_ROW_TILE = 1024
PROMPT_Q_TILE = 512
PROMPT_K_TILE = 512


def kernel(x_prompt, x_sample, cache_kv_latent, cache_k_rope, state_conv, pre_norm, w_in, q_norm, w_uq, kv_norm,
           w_uk, w_uv, w_o_mla, conv_w, w_o_conv, w_out, post_norm):
    depth = pre_norm.shape[0]
    assert depth == 1
    b, t, _ = x_prompt.shape
    nb, t_new, _ = x_sample.shape
    past_len = cache_kv_latent.shape[2]
    lyr = 0
    w = _prep_weights(pre_norm[lyr], w_in[lyr], q_norm[lyr], w_uq[lyr], kv_norm[lyr], w_uk[lyr], w_uv[lyr],
                      w_o_mla[lyr], conv_w[lyr], w_o_conv[lyr], w_out[lyr], post_norm[lyr])

    cos_t, sin_t = _rope_angles(jnp.arange(t, dtype=jnp.int32), frequency_major=True)
    zero_state = jnp.zeros((b, CONV_K - 1, CONV_WIDTH), _F32)
    qt, kc, vt, ckv_p, kpe_t, ga, sa, mb, cv_p = _project_prompt(
        x_prompt, cos_t, sin_t, zero_state, w, tm=PROMPT_ROW_TILE, tq=PROMPT_Q_TILE)
    y_p = _attend_prompt(qt, kc, vt, ga, sa, mb, x_prompt, w, tq=PROMPT_Q_TILE, tk=PROMPT_K_TILE)

    n_rows = nb * t_new
    tabs_s = _rope_tables_rows(past_len + jnp.arange(n_rows, dtype=jnp.int32) % t_new)
    xs = x_sample.reshape(1, n_rows, D_MODEL)
    q_s, kc_s, ckv_s, kpe_s, ga_s, sa_s, mb_s, cv_s = _project_sample(
        xs, *tabs_s, state_conv[lyr], w, n_seq=nb)
    y_s = _attend_sample(q_s, cache_kv_latent[lyr], jnp.swapaxes(cache_k_rope[lyr], 1, 2),
                         kc_s.reshape(nb, t_new, KEY_DIM),
                         ga_s[0], sa_s[0], mb_s[0], xs[0], w)

    return (y_p, y_s.reshape(nb, t_new, D_MODEL),
            ckv_p[None], jnp.swapaxes(kpe_t, 1, 2)[None], cv_p[None],
            ckv_s.reshape(1, nb, t_new, KV_LORA), kpe_s.reshape(1, nb, t_new, QK_ROPE), cv_s[None])
```

```python
import functools

import numpy as np
import jax
import jax.numpy as jnp
from jax import lax
from jax.experimental import pallas as pl
from jax.experimental.pallas import tpu as pltpu

N_HEADS = 8
QK_NOPE = 64
QK_ROPE = 32
V_HEAD = 64
Q_LORA = 256
KV_LORA = 128
MLA_WIDTH = N_HEADS * V_HEAD
CONV_WIDTH = 512
CONV_K = 3
D_MODEL = 1024
CHUNK = 64
ROPE_BASE = 10000.0
EPS = 1e-6
SM_SCALE = (QK_NOPE + QK_ROPE) ** -0.5
NEG_INF = -1e30
LOG2E = 1.4426950408889634

LANES = 128
MXU_DIM = 256
COL_BLOCK = MXU_DIM
REDUCE_SLAB = 256
EPILOGUE_STAGES = 4
STAGE1_LEAD = 1
HALF_ROPE = QK_ROPE // 2
KEY_DIM = KV_LORA + QK_ROPE
HEADS_PER_ROPE_BLOCK = LANES // QK_ROPE
Q_SCALE = SM_SCALE * LOG2E
VMEM_LIMIT_BYTES = 56 * 1024 * 1024

_O_TAIL = Q_LORA + KV_LORA + QK_ROPE
_QKV_COLS = -(-_O_TAIL // MXU_DIM) * MXU_DIM
_O_GM = 0
_O_CB = _O_GM + MLA_WIDTH
_O_CC = _O_CB + CONV_WIDTH
_O_CX = _O_CC + CONV_WIDTH
_O_GC = _O_CX + CONV_WIDTH
_O_MM = _O_GC + CONV_WIDTH
_O_MC = _O_MM + D_MODEL
_TAIL_COLS = _O_MC + D_MODEL

_F32 = jnp.float32
_BF16 = jnp.bfloat16
_NT = (((1,), (1,)), ((), ()))


def _rms(v, g):
    return v * lax.rsqrt(jnp.mean(v * v, axis=-1, keepdims=True) + EPS) * g


def _silu(v):
    return v * jax.nn.sigmoid(v)


def _dot(a, b):
    return jnp.dot(a, b, preferred_element_type=_F32)


def _dot_nt(a, b):
    return lax.dot_general(a, b, _NT, preferred_element_type=_F32)


def _tail_proj(h, w_in_t_ref):
    def proj(off, n):
        return _dot_nt(h, w_in_t_ref[_O_TAIL + off:_O_TAIL + off + n, :])
    return proj


def _conv_branch(proj, st_ref, conv_w_ref, w_oc_ref, mb_ref, cst_ref, ub_ref, carry_ref, *,
                 tm, n_seq, carry_state):
    seq_len = tm // n_seq
    u = proj(_O_CC, CONV_WIDTH) * proj(_O_CX, CONV_WIDTH)
    w0 = conv_w_ref[0:1, :]
    w1 = conv_w_ref[1:2, :]
    w2 = conv_w_ref[2:3, :]
    if carry_state:
        @pl.when(pl.program_id(1) == 0)
        def _():
            carry_ref[0:CONV_K - 1, :] = st_ref[0]
    convs = []
    for j in range(n_seq):
        u_j = u[j * seq_len:(j + 1) * seq_len]
        prev = carry_ref[0:CONV_K - 1, :] if carry_state else st_ref[j]
        ub_ref[j, 8 - (CONV_K - 1):8, :] = prev
        ub_ref[j, 8:8 + seq_len, :] = u_j
        convs.append(w0 * ub_ref[j, 6:6 + seq_len, :] + w1 * ub_ref[j, 7:7 + seq_len, :] + w2 * u_j)
        new_state = u_j[seq_len - (CONV_K - 1):, :]
        cst_ref[j] = new_state
        if carry_state:
            carry_ref[0:CONV_K - 1, :] = new_state
    conv = convs[0] if n_seq == 1 else jnp.concatenate(convs, axis=0)
    c_b = proj(_O_CB, CONV_WIDTH)
    g_conv = proj(_O_GC, CONV_WIDTH)
    merge_gate = jax.nn.sigmoid(proj(_O_MC, D_MODEL))
    bb = (c_b * conv * _silu(g_conv)).astype(_BF16)
    branch_b = _dot(bb, w_oc_ref[...])
    mb_ref[0] = (merge_gate * branch_b).astype(_BF16)


def _proj_prompt_kernel(x_ref, cos_ref, sin_ref, st_ref, pre_g_ref, w_in_t_ref, q_g_ref,
                        w_qabs_ref, w_uqp_t_ref, kv_g_ref, conv_w_ref, w_oc_ref,
                        qt_ref, kc_ref, vt_ref, ckv_ref, kpet_ref, ga_ref, sa_ref, mb_ref, cst_ref,
                        ub_ref, carry_ref, *, tq):
    tm = x_ref.shape[1]
    h = _rms(x_ref[0], pre_g_ref[...]).astype(_BF16)
    cos_t = cos_ref[...]
    sin_t = sin_ref[...]

    qkv = _dot_nt(h, w_in_t_ref[0:_QKV_COLS, :])
    proj = _tail_proj(h, w_in_t_ref)
    sa_ref[0] = jax.nn.sigmoid(proj(_O_MM, D_MODEL)).astype(_BF16)
    ga_ref[0] = _silu(proj(_O_GM, MLA_WIDTH)).astype(_BF16)
    qn = _rms(qkv[:, :Q_LORA], q_g_ref[...]).astype(_BF16)
    q_abs_t = _dot_nt(w_qabs_ref[...], qn)
    q_pe_t = _dot_nt(w_uqp_t_ref[...], qn).reshape(N_HEADS, QK_ROPE, tm)
    x1 = q_pe_t[:, :HALF_ROPE, :]
    x2 = q_pe_t[:, HALF_ROPE:, :]
    r1 = (x1 * cos_t - x2 * sin_t) * Q_SCALE
    r2 = (x2 * cos_t + x1 * sin_t) * Q_SCALE
    groups = tq // COL_BLOCK
    for j in range(tm // tq):
        for hd in range(N_HEADS):
            for g in range(groups):
                tok = slice(j * tq + g * COL_BLOCK, j * tq + (g + 1) * COL_BLOCK)
                cb = hd * groups + g
                qt_ref[0, j, cb, 0:KV_LORA, :] = (
                    q_abs_t[hd * KV_LORA:(hd + 1) * KV_LORA, tok] * Q_SCALE).astype(_BF16)
                qt_ref[0, j, cb, KV_LORA:KV_LORA + HALF_ROPE, :] = r1[hd][:, tok].astype(_BF16)
                qt_ref[0, j, cb, KV_LORA + HALF_ROPE:KEY_DIM, :] = r2[hd][:, tok].astype(_BF16)

    kvr = qkv[:, Q_LORA:]
    ckv = _rms(kvr[:, :KV_LORA], kv_g_ref[...])
    ckv_ref[0] = ckv
    ckv_t = ckv.T.astype(_BF16)
    for u in range(tm // MXU_DIM):
        vt_ref[0, u] = ckv_t[:, u * MXU_DIM:(u + 1) * MXU_DIM]
    kr_t = kvr[:, KV_LORA:].T
    k1 = kr_t[0:HALF_ROPE]
    k2 = kr_t[HALF_ROPE:QK_ROPE]
    kpe_t = jnp.concatenate([k1 * cos_t - k2 * sin_t, k2 * cos_t + k1 * sin_t], axis=0)
    kpet_ref[0] = kpe_t
    kpe = jnp.concatenate([kpe_t, jnp.zeros((LANES - QK_ROPE, tm), _F32)], axis=0).T
    kc_ref[0, :, 0:KV_LORA] = ckv.astype(_BF16)
    kc_ref[0, :, KV_LORA:KEY_DIM] = kpe[:, :QK_ROPE].astype(_BF16)

    _conv_branch(proj, st_ref, conv_w_ref, w_oc_ref, mb_ref, cst_ref, ub_ref, carry_ref,
                 tm=tm, n_seq=1, carry_state=True)


def _project_prompt(x, cos_t, sin_t, state, w, *, tm, tq):
    b, t, _ = x.shape

    def rows(width, dtype):
        return (jax.ShapeDtypeStruct((b, t, width), dtype),
                pl.BlockSpec((1, tm, width), lambda bi, ti: (bi, ti, 0)))

    def full(a):
        return pl.BlockSpec(a.shape, lambda bi, ti: (0,) * a.ndim)

    outs = [
        (jax.ShapeDtypeStruct((b, t // tq, N_HEADS * tq // COL_BLOCK, KEY_DIM, COL_BLOCK), _BF16),
         pl.BlockSpec((1, tm // tq, N_HEADS * tq // COL_BLOCK, KEY_DIM, COL_BLOCK),
                      lambda bi, ti: (bi, ti, 0, 0, 0))),
        rows(KEY_DIM, _BF16),
        (jax.ShapeDtypeStruct((b, t // MXU_DIM, KV_LORA, MXU_DIM), _BF16),
         pl.BlockSpec((1, tm // MXU_DIM, KV_LORA, MXU_DIM), lambda bi, ti: (bi, ti, 0, 0))),
        rows(KV_LORA, _F32),
        (jax.ShapeDtypeStruct((b, QK_ROPE, t), _F32),
         pl.BlockSpec((1, QK_ROPE, tm), lambda bi, ti: (bi, 0, ti))),
        rows(MLA_WIDTH, _BF16),
        rows(D_MODEL, _BF16),
        rows(D_MODEL, _BF16),
        (jax.ShapeDtypeStruct((b, CONV_K - 1, CONV_WIDTH), _F32),
         pl.BlockSpec((1, CONV_K - 1, CONV_WIDTH), lambda bi, ti: (bi, 0, 0))),
    ]
    table_spec = pl.BlockSpec((HALF_ROPE, tm), lambda bi, ti: (0, ti))
    weights = (w["pre_g"], w["w_in_t"], w["q_g"], w["w_qabs"], w["w_uqp_t"],
               w["kv_g"], w["conv_w"], w["w_oc"])
    return pl.pallas_call(
        functools.partial(_proj_prompt_kernel, tq=tq),
        grid=(b, t // tm),
        in_specs=[pl.BlockSpec((1, tm, D_MODEL), lambda bi, ti: (bi, ti, 0)),
                  table_spec, table_spec,
                  pl.BlockSpec((1, CONV_K - 1, CONV_WIDTH), lambda bi, ti: (bi, 0, 0))]
                 + [full(a) for a in weights],
        out_specs=[o[1] for o in outs],
        out_shape=[o[0] for o in outs],
        scratch_shapes=[pltpu.VMEM((1, tm + 8, CONV_WIDTH), _F32),
                        pltpu.VMEM((8, CONV_WIDTH), _F32)],
        compiler_params=pltpu.CompilerParams(
            dimension_semantics=("arbitrary", "arbitrary"), vmem_limit_bytes=VMEM_LIMIT_BYTES),
        name="proj_prompt",
    )(x, cos_t, sin_t, state, *weights)


def _proj_sample_kernel(x_ref, cos_ref, sn1_ref, sn2_ref, st_ref, pre_g_ref, w_in_t_ref, q_g_ref,
                        w_qabs_ref, w_uqp_ref, kv_g_ref, conv_w_ref, w_oc_ref,
                        q_ref, kc_ref, ckv_ref, kpe_ref, ga_ref, sa_ref, mb_ref, cst_ref,
                        ub_ref, carry_ref, *, n_seq):
    h = _rms(x_ref[0], pre_g_ref[...]).astype(_BF16)
    cos = cos_ref[...]
    sn1 = sn1_ref[...]
    sn2 = sn2_ref[...]

    def rope(v):
        return (v * cos + pltpu.roll(v, HALF_ROPE, 1) * sn1
                + pltpu.roll(v, LANES - HALF_ROPE, 1) * sn2)

    qkv = _dot_nt(h, w_in_t_ref[0:_QKV_COLS, :])
    qn = _rms(qkv[:, :Q_LORA], q_g_ref[...]).astype(_BF16)
    q_abs = _dot_nt(qn, w_qabs_ref[...])
    q_pe = _dot(qn, w_uqp_ref[...])
    n_rope_blocks = N_HEADS // HEADS_PER_ROPE_BLOCK
    q_rot = [rope(q_pe[:, j * LANES:(j + 1) * LANES]) * Q_SCALE for j in range(n_rope_blocks)]
    for hd in range(N_HEADS):
        q_ref[0, hd, :, 0:KV_LORA] = (q_abs[:, hd * LANES:(hd + 1) * LANES] * Q_SCALE).astype(_BF16)
        grp = hd % HEADS_PER_ROPE_BLOCK
        blk = q_rot[hd // HEADS_PER_ROPE_BLOCK]
        if grp:
            blk = pltpu.roll(blk, LANES - grp * QK_ROPE, 1)
        q_ref[0, hd, :, KV_LORA:KEY_DIM] = blk[:, :QK_ROPE].astype(_BF16)

    kvr = qkv[:, Q_LORA:]
    ckv = _rms(kvr[:, :KV_LORA], kv_g_ref[...])
    k_rot = rope(kvr[:, KV_LORA:])
    ckv_ref[0] = ckv
    kpe_ref[0] = k_rot[:, :QK_ROPE]
    kc_ref[0, :, 0:KV_LORA] = ckv.astype(_BF16)
    kc_ref[0, :, KV_LORA:KEY_DIM] = k_rot[:, :QK_ROPE].astype(_BF16)

    proj = _tail_proj(h, w_in_t_ref)
    sa_ref[0] = jax.nn.sigmoid(proj(_O_MM, D_MODEL)).astype(_BF16)
    ga_ref[0] = _silu(proj(_O_GM, MLA_WIDTH)).astype(_BF16)
    _conv_branch(proj, st_ref, conv_w_ref, w_oc_ref, mb_ref, cst_ref, ub_ref, carry_ref,
                 tm=x_ref.shape[1], n_seq=n_seq, carry_state=False)


def _project_sample(x, cos, sn1, sn2, state, w, *, n_seq):
    _, tm, _ = x.shape

    def rows(width, dtype):
        return (jax.ShapeDtypeStruct((1, tm, width), dtype), pl.BlockSpec((1, tm, width), lambda i: (0, 0, 0)))

    def full(a):
        return pl.BlockSpec(a.shape, lambda i: (0,) * a.ndim)

    outs = [
        (jax.ShapeDtypeStruct((1, N_HEADS, tm, KEY_DIM), _BF16),
         pl.BlockSpec((1, N_HEADS, tm, KEY_DIM), lambda i: (0, 0, 0, 0))),
        rows(KEY_DIM, _BF16), rows(KV_LORA, _F32), rows(QK_ROPE, _F32),
        rows(MLA_WIDTH, _BF16), rows(D_MODEL, _BF16), rows(D_MODEL, _BF16),
        (jax.ShapeDtypeStruct((n_seq, CONV_K - 1, CONV_WIDTH), _F32),
         pl.BlockSpec((n_seq, CONV_K - 1, CONV_WIDTH), lambda i: (0, 0, 0))),
    ]
    weights = (w["pre_g"], w["w_in_t"], w["q_g"], w["w_qabs"], w["w_uqp"],
               w["kv_g"], w["conv_w"], w["w_oc"])
    return pl.pallas_call(
        functools.partial(_proj_sample_kernel, n_seq=n_seq),
        grid=(1,),
        in_specs=[full(x), full(cos), full(sn1), full(sn2), full(state)] + [full(a) for a in weights],
        out_specs=[o[1] for o in outs],
        out_shape=[o[0] for o in outs],
        scratch_shapes=[pltpu.VMEM((n_seq, tm // n_seq + 8, CONV_WIDTH), _F32),
                        pltpu.VMEM((8, CONV_WIDTH), _F32)],
        compiler_params=pltpu.CompilerParams(
            dimension_semantics=("arbitrary",), vmem_limit_bytes=VMEM_LIMIT_BYTES),
        name="proj_sample",
    )(x, cos, sn1, sn2, state, *weights)


def _epilogue(o_lat, ga, sa, mb, x, w_uv_ref, w_om_ref, w_out_ref, post_g_ref):
    o = _dot(o_lat.astype(_BF16), w_uv_ref[...])
    branch_a = _dot((o * ga.astype(_F32)).astype(_BF16), w_om_ref[...])
    merged = sa.astype(_F32) * branch_a + mb.astype(_F32)
    z = _dot(merged.astype(_BF16), w_out_ref[...])
    return x + _rms(z, post_g_ref[...])


def _col_reduce(v, op):
    n_keys, cols = v.shape
    slabs = max(n_keys // REDUCE_SLAB, 1)
    part = op(v.reshape(slabs, n_keys // slabs, cols), axis=0)
    return op(part, axis=0, keepdims=True)


def _attn_prompt_kernel(qt_ref, qn_ref, kc_ref, vt_ref, bias_ref, ga_ref, sa_ref, mb_ref, x_ref,
                        w_uv_ref, w_om_ref, w_out_ref, post_g_ref, y_ref,
                        m_ref, l_ref, acc_ref, s_ref, mc_ref, *, tq, tk):
    i = pl.program_id(1)
    units = tk // MXU_DIM
    q_tiles_per_k_tile = tk // tq
    edge = i // q_tiles_per_k_tile
    groups = tq // COL_BLOCK
    n_cb = N_HEADS * groups

    m_ref[...] = jnp.full(m_ref.shape, NEG_INF, _F32)
    l_ref[...] = jnp.zeros(l_ref.shape, _F32)
    acc_ref[...] = jnp.zeros(acc_ref.shape, _F32)

    def keys_of(tile):
        return kc_ref[0, pl.ds(pl.multiple_of(tile * tk, tk), tk), :]

    def values_of(tile):
        return [vt_ref[0, tile * units + u] for u in range(units)]

    def stage1(k, cb, slot=0, q_tile=None, q_ref=qt_ref, split=True):
        qb = q_ref[0, 0, cb]
        if split:
            s = jnp.concatenate([_dot(k[u * MXU_DIM:(u + 1) * MXU_DIM], qb) for u in range(units)], axis=0)
        else:
            s = _dot(k, qb)
        if q_tile is not None:
            s = s + bias_ref[(q_tile % q_tiles_per_k_tile) * groups + cb % groups]
        s_ref[slot, cb] = s
        mc_ref[slot, cb] = _col_reduce(s, jnp.max)

    def stage2(vts, cb, slot=0):
        m_prev = m_ref[cb]
        m_new = jnp.maximum(m_prev, mc_ref[slot, cb])
        alpha = jnp.exp2(m_prev - m_new)
        p = jnp.exp2(s_ref[slot, cb] - m_new)
        l_ref[cb] = alpha * l_ref[cb] + _col_reduce(p, jnp.sum)
        pb = p.astype(_BF16)
        pv = _dot(vts[0], pb[0:MXU_DIM])
        for u in range(1, units):
            pv = pv + _dot(vts[u], pb[u * MXU_DIM:(u + 1) * MXU_DIM])
        acc_ref[cb] = alpha * acc_ref[cb] + pv
        m_ref[cb] = m_new

    @pl.when(i == 0)
    def _():
        k_edge = keys_of(edge)
        for cb in range(n_cb):
            stage1(k_edge, cb, q_tile=i)

    def advance(j, rd, wr):
        vts = values_of(jnp.where(j == 0, edge, j - 1))
        k = keys_of(j)
        in_place = rd == wr
        lead = 0 if in_place else STAGE1_LEAD
        for cb in range(lead):
            stage1(k, cb, wr, split=in_place)
        for cb in range(n_cb):
            stage2(vts, cb, rd)
            if cb + lead < n_cb:
                stage1(k, cb + lead, wr, split=in_place)

    def single(j, carry):
        advance(j, 0, 0)
        return carry

    def double(pair, carry):
        advance(odd + 2 * pair, 0, 1)
        advance(odd + 2 * pair + 1, 1, 0)
        return carry

    odd = edge % 2
    lax.fori_loop(0, odd, single, 0)
    lax.fori_loop(0, edge // 2, double, 0)

    last_vts = values_of(jnp.where(edge == 0, edge, edge - 1))
    nxt = jnp.minimum(i + 1, pl.num_programs(1) - 1)
    k_next = keys_of(nxt // q_tiles_per_k_tile)
    def drain(cb):
        stage2(last_vts, cb)
        stage1(k_next, cb, 0, q_tile=nxt, q_ref=qn_ref)

    def epilogue_stages(g):
        rows = slice(g * COL_BLOCK, (g + 1) * COL_BLOCK)
        o_lat = jnp.concatenate([(acc_ref[hd * groups + g] / l_ref[hd * groups + g]).T
                                 for hd in range(N_HEADS)], axis=1)
        o = _dot(o_lat.astype(_BF16), w_uv_ref[...])
        yield
        branch_a = _dot((o * ga_ref[0, rows, :].astype(_F32)).astype(_BF16), w_om_ref[...])
        yield
        merged = sa_ref[0, rows, :].astype(_F32) * branch_a + mb_ref[0, rows, :].astype(_F32)
        z = _dot(merged.astype(_BF16), w_out_ref[...])
        yield
        y_ref[0, rows, :] = x_ref[0, rows, :] + _rms(z, post_g_ref[...])
        yield

    pending = iter(())
    for g in range(groups):
        blocks = [hd * groups + g for hd in range(N_HEADS)]
        chunk = -(-len(blocks) // EPILOGUE_STAGES)
        for start in range(0, len(blocks), chunk):
            next(pending, None)
            for cb in blocks[start:start + chunk]:
                drain(cb)
        for _ in pending:
            pass
        pending = epilogue_stages(g)
    for _ in pending:
        pass


def _attend_prompt(qt, kc, vt, ga, sa, mb, x, w, *, tq, tk):
    b, t, _ = x.shape
    n_q = t // tq

    def row_spec(width):
        return pl.BlockSpec((1, tq, width), lambda bi, qi: (bi, qi, 0))

    def full(a):
        return pl.BlockSpec(a.shape, lambda bi, qi: (0,) * a.ndim)

    k_chunk = np.arange(tk)[None, :, None] // CHUNK
    q_chunk = (np.arange(tk // COL_BLOCK)[:, None, None] * COL_BLOCK
               + np.arange(COL_BLOCK)[None, None, :]) // CHUNK
    bias = jnp.asarray(np.where(k_chunk <= q_chunk, 0.0, NEG_INF), _F32)
    n_cb = N_HEADS * tq // COL_BLOCK

    weights = (w["w_uv"], w["w_om"], w["w_out"], w["post_g"])
    return pl.pallas_call(
        functools.partial(_attn_prompt_kernel, tq=tq, tk=tk),
        grid=(b, t // tq),
        in_specs=[pl.BlockSpec((1, 1, n_cb, KEY_DIM, COL_BLOCK), lambda bi, qi: (bi, qi, 0, 0, 0)),
                  pl.BlockSpec((1, 1, n_cb, KEY_DIM, COL_BLOCK),
                               lambda bi, qi: (bi, jnp.minimum(qi + 1, n_q - 1), 0, 0, 0)),
                  pl.BlockSpec((1, t, KEY_DIM), lambda bi, qi: (bi, 0, 0), pipeline_mode=pl.Buffered(1)),
                  pl.BlockSpec((1, t // MXU_DIM, KV_LORA, MXU_DIM), lambda bi, qi: (bi, 0, 0, 0),
                               pipeline_mode=pl.Buffered(1)),
                  full(bias),
                  row_spec(MLA_WIDTH), row_spec(D_MODEL), row_spec(D_MODEL), row_spec(D_MODEL)]
                 + [full(a) for a in weights],
        out_specs=row_spec(D_MODEL),
        out_shape=jax.ShapeDtypeStruct((b, t, D_MODEL), _F32),
        scratch_shapes=[pltpu.VMEM((n_cb, 1, COL_BLOCK), _F32), pltpu.VMEM((n_cb, 1, COL_BLOCK), _F32),
                        pltpu.VMEM((n_cb, KV_LORA, COL_BLOCK), _F32),
                        pltpu.VMEM((2, n_cb, tk, COL_BLOCK), _F32), pltpu.VMEM((2, n_cb, 1, COL_BLOCK), _F32)],
        compiler_params=pltpu.CompilerParams(
            dimension_semantics=("arbitrary", "arbitrary"), vmem_limit_bytes=VMEM_LIMIT_BYTES),
        name="attn_prompt",
    )(qt, qt, kc, vt, bias, ga, sa, mb, x, *weights)


def _attn_first(q, k, m_ref, l_ref, acc_ref, visible=None):
    s = _dot_nt(q, k)
    if visible is not None:
        s = jnp.where(visible, s, NEG_INF)
    m = jnp.max(s, axis=1, keepdims=True)
    p = jnp.exp2(s - m)
    m_ref[...] = jnp.broadcast_to(m, m_ref.shape)
    l_ref[...] = jnp.broadcast_to(jnp.sum(p, axis=1, keepdims=True), l_ref.shape)
    acc_ref[...] = _dot(p.astype(_BF16), k[:, :KV_LORA])


def _attn_update(q, k_lat, k_rope_t, m_ref, l_ref, acc_ref):
    s = _dot_nt(q[:, :KV_LORA], k_lat) + _dot(q[:, KV_LORA:], k_rope_t)
    m_prev = m_ref[...]
    m_new = jnp.maximum(m_prev, jnp.max(s, axis=1, keepdims=True))
    alpha = jnp.exp2(m_prev - m_new)
    p = jnp.exp2(s - jnp.concatenate([m_new] * (s.shape[1] // LANES), axis=1))
    l_ref[...] = alpha * l_ref[...] + jnp.sum(p, axis=1, keepdims=True)
    acc_ref[...] = alpha * acc_ref[...] + _dot(p.astype(_BF16), k_lat)
    m_ref[...] = m_new


def _attn_sample_kernel(q_ref, past_lat_ref, past_rope_ref, knew_ref, ga_ref, sa_ref, mb_ref, x_ref,
                        w_uv_ref, w_om_ref, w_out_ref, post_g_ref, y_ref,
                        m_ref, l_ref, acc_ref, o_ref, *, new_visible):
    bi = pl.program_id(0)
    per_step, t_new, _ = knew_ref.shape
    rows = N_HEADS * t_new
    if new_visible is None:
        visible = None
    else:
        q_pos = lax.broadcasted_iota(jnp.int32, (rows, t_new), 0) % t_new
        k_pos = lax.broadcasted_iota(jnp.int32, (rows, t_new), 1)
        visible = (k_pos + new_visible[0]) // CHUNK <= (q_pos + new_visible[0]) // CHUNK
    for g in range(per_step):
        q = q_ref[0, :, g * t_new:(g + 1) * t_new, :].reshape(rows, KEY_DIM)
        m_g, l_g, acc_g = m_ref.at[g], l_ref.at[g], acc_ref.at[g]
        _attn_first(q, knew_ref[g], m_g, l_g, acc_g, visible=visible)
        _attn_update(q, past_lat_ref[g].astype(_BF16), past_rope_ref[g].astype(_BF16), m_g, l_g, acc_g)
        o = acc_g[...] / l_g[...]
        r0 = pl.multiple_of((bi * per_step + g) * t_new, t_new)
        o_ref[pl.ds(r0, t_new), :] = jnp.concatenate(
            [o[hd * t_new:(hd + 1) * t_new] for hd in range(N_HEADS)], axis=1)

    @pl.when(bi == pl.num_programs(0) - 1)
    def _():
        y_ref[...] = _epilogue(o_ref[...], ga_ref[...], sa_ref[...], mb_ref[...], x_ref[...],
                               w_uv_ref, w_om_ref, w_out_ref, post_g_ref)


def _attend_sample(q, past_lat, past_rope, k_new, ga, sa, mb, x, w):
    nb, t_new, _ = k_new.shape
    past_len = past_lat.shape[1]
    n_rows = nb * t_new
    rows = N_HEADS * t_new
    per_step = SAMPLE_STREAMS_PER_STEP if nb % SAMPLE_STREAMS_PER_STEP == 0 else 1
    last_q, first_q = past_len + t_new - 1, past_len
    new_visible = None if last_q // CHUNK == first_q // CHUNK else (past_len,)

    def full(a):
        return pl.BlockSpec(a.shape, lambda bi: (0,) * a.ndim)

    weights = (w["w_uv"], w["w_om"], w["w_out"], w["post_g"])
    return pl.pallas_call(
        functools.partial(_attn_sample_kernel, new_visible=new_visible),
        grid=(nb // per_step,),
        in_specs=[pl.BlockSpec((1, N_HEADS, per_step * t_new, KEY_DIM), lambda bi: (0, 0, bi, 0)),
                  pl.BlockSpec((per_step, past_len, KV_LORA), lambda bi: (bi, 0, 0)),
                  pl.BlockSpec((per_step, QK_ROPE, past_len), lambda bi: (bi, 0, 0)),
                  pl.BlockSpec((per_step, t_new, KEY_DIM), lambda bi: (bi, 0, 0)),
                  full(ga), full(sa), full(mb), full(x)] + [full(a) for a in weights],
        out_specs=pl.BlockSpec((n_rows, D_MODEL), lambda bi: (0, 0)),
        out_shape=jax.ShapeDtypeStruct((n_rows, D_MODEL), _F32),
        scratch_shapes=[pltpu.VMEM((per_step, rows, LANES), _F32), pltpu.VMEM((per_step, rows, LANES), _F32),
                        pltpu.VMEM((per_step, rows, KV_LORA), _F32),
                        pltpu.VMEM((n_rows, N_HEADS * KV_LORA), _F32)],
        compiler_params=pltpu.CompilerParams(
            dimension_semantics=("arbitrary",), vmem_limit_bytes=VMEM_LIMIT_BYTES),
        name="attn_sample",
    )(q, past_lat, past_rope, k_new, ga, sa, mb, x, *weights)


def _absorb_kernel(w_uk_ref, w_uqn_ref, o_ref):
    for hd in range(N_HEADS):
        o_ref[hd] = lax.dot_general(w_uk_ref[hd], w_uqn_ref[hd], _NT, precision=lax.Precision.HIGHEST,
                                    preferred_element_type=_F32).astype(o_ref.dtype)


def _absorbed_query_weight(w_uk, w_uqn):
    out = pl.pallas_call(
        _absorb_kernel,
        out_shape=jax.ShapeDtypeStruct((N_HEADS, KV_LORA, Q_LORA), _BF16),
        name="absorb_q_weight",
    )(jnp.transpose(w_uk, (1, 0, 2)), jnp.transpose(w_uqn, (1, 0, 2)))
    return out.reshape(N_HEADS * KV_LORA, Q_LORA)


def _prep_weights(pre_norm, w_in, q_norm, w_uq, kv_norm, w_uk, w_uv, w_o_mla, conv_w, w_o_conv, w_out, post_norm):
    assert w_in.shape[1] == _O_TAIL + _TAIL_COLS
    wq = w_uq.reshape(Q_LORA, N_HEADS, QK_NOPE + QK_ROPE)
    w_uqp = wq[:, :, QK_NOPE:].reshape(Q_LORA, N_HEADS * QK_ROPE).astype(_BF16)
    return {
        "pre_g": pre_norm.reshape(1, D_MODEL),
        "w_in_t": w_in.T.astype(_BF16),
        "q_g": q_norm.reshape(1, Q_LORA),
        "w_qabs": _absorbed_query_weight(w_uk, wq[:, :, :QK_NOPE]),
        "w_uqp": w_uqp,
        "w_uqp_t": w_uqp.T,
        "kv_g": kv_norm.reshape(1, KV_LORA),
        "conv_w": conv_w,
        "w_oc": w_o_conv.astype(_BF16),
        "w_uv": (jnp.transpose(w_uv, (1, 0, 2))[:, :, None, :]
                 * jnp.eye(N_HEADS, dtype=_F32)[:, None, :, None]
                 ).reshape(N_HEADS * KV_LORA, N_HEADS * V_HEAD).astype(_BF16),
        "w_om": w_o_mla.astype(_BF16),
        "w_out": w_out.astype(_BF16),
        "post_g": post_norm.reshape(1, D_MODEL),
    }


def _rope_angles(pos, frequency_major=False):
    inv = ROPE_BASE ** (-jnp.arange(HALF_ROPE, dtype=_F32) / HALF_ROPE)
    if frequency_major:
        ang = inv[:, None] * pos.astype(_F32)[None, :]
    else:
        ang = pos.astype(_F32)[:, None] * inv[None, :]
    return jnp.cos(ang), jnp.sin(ang)


def _rope_tables_rows(pos):
    cos, sin = _rope_angles(pos)
    zero = jnp.zeros_like(sin)
    reps = LANES // QK_ROPE
    return (jnp.tile(jnp.concatenate([cos, cos], axis=1), (1, reps)),
            jnp.tile(jnp.concatenate([zero, sin], axis=1), (1, reps)),
            jnp.tile(jnp.concatenate([-sin, zero], axis=1), (1, reps)))


SAMPLE_STREAMS_PER_STEP = 4
PROMPT_ROW_TILE = 1024
PROMPT_Q_TILE = 512
PROMPT_K_TILE = 512


def kernel(x_prompt, x_sample, cache_kv_latent, cache_k_rope, state_conv, pre_norm, w_in, q_norm, w_uq, kv_norm,
           w_uk, w_uv, w_o_mla, conv_w, w_o_conv, w_out, post_norm):
    depth = pre_norm.shape[0]
    assert depth == 1
    b, t, _ = x_prompt.shape
    nb, t_new, _ = x_sample.shape
    past_len = cache_kv_latent.shape[2]
    lyr = 0
    w = _prep_weights(pre_norm[lyr], w_in[lyr], q_norm[lyr], w_uq[lyr], kv_norm[lyr], w_uk[lyr], w_uv[lyr],
                      w_o_mla[lyr], conv_w[lyr], w_o_conv[lyr], w_out[lyr], post_norm[lyr])

    cos_t, sin_t = _rope_angles(jnp.arange(t, dtype=jnp.int32), frequency_major=True)
    zero_state = jnp.zeros((b, CONV_K - 1, CONV_WIDTH), _F32)
    qt, kc, vt, ckv_p, kpe_t, ga, sa, mb, cv_p = _project_prompt(
        x_prompt, cos_t, sin_t, zero_state, w, tm=PROMPT_ROW_TILE, tq=PROMPT_Q_TILE)
    y_p = _attend_prompt(qt, kc, vt, ga, sa, mb, x_prompt, w, tq=PROMPT_Q_TILE, tk=PROMPT_K_TILE)

    n_rows = nb * t_new
    tabs_s = _rope_tables_rows(past_len + jnp.arange(n_rows, dtype=jnp.int32) % t_new)
    xs = x_sample.reshape(1, n_rows, D_MODEL)
    q_s, kc_s, ckv_s, kpe_s, ga_s, sa_s, mb_s, cv_s = _project_sample(
        xs, *tabs_s, state_conv[lyr], w, n_seq=nb)
    y_s = _attend_sample(q_s, cache_kv_latent[lyr], jnp.swapaxes(cache_k_rope[lyr], 1, 2),
                         kc_s.reshape(nb, t_new, KEY_DIM),
                         ga_s[0], sa_s[0], mb_s[0], xs[0], w)

    return (y_p, y_s.reshape(nb, t_new, D_MODEL),
            ckv_p[None], jnp.swapaxes(kpe_t, 1, 2)[None], cv_p[None],
            ckv_s.reshape(1, nb, t_new, KV_LORA), kpe_s.reshape(1, nb, t_new, QK_ROPE), cv_s[None])
```

```python
import functools

import numpy as np
import jax
import jax.numpy as jnp
from jax import lax
from jax.experimental import pallas as pl
from jax.experimental.pallas import tpu as pltpu

N_HEADS = 8
QK_NOPE = 64
QK_ROPE = 32
V_HEAD = 64
Q_LORA = 256
KV_LORA = 128
MLA_WIDTH = N_HEADS * V_HEAD
CONV_WIDTH = 512
CONV_K = 3
D_MODEL = 1024
CHUNK = 64
ROPE_BASE = 10000.0
EPS = 1e-6
SM_SCALE = (QK_NOPE + QK_ROPE) ** -0.5
NEG_INF = -1e30
LOG2E = 1.4426950408889634

LANES = 128
MXU_DIM = 256
COL_BLOCK = MXU_DIM
REDUCE_SLAB = 256
EPILOGUE_STAGES = 4
STAGE1_LEAD = 1
HALF_ROPE = QK_ROPE // 2
KEY_DIM = KV_LORA + QK_ROPE
HEADS_PER_ROPE_BLOCK = LANES // QK_ROPE
Q_SCALE = SM_SCALE * LOG2E
VMEM_LIMIT_BYTES = 56 * 1024 * 1024

_O_TAIL = Q_LORA + KV_LORA + QK_ROPE
_QKV_COLS = -(-_O_TAIL // MXU_DIM) * MXU_DIM
_O_GM = 0
_O_CB = _O_GM + MLA_WIDTH
_O_CC = _O_CB + CONV_WIDTH
_O_CX = _O_CC + CONV_WIDTH
_O_GC = _O_CX + CONV_WIDTH
_O_MM = _O_GC + CONV_WIDTH
_O_MC = _O_MM + D_MODEL
_TAIL_COLS = _O_MC + D_MODEL

_F32 = jnp.float32
_BF16 = jnp.bfloat16
_NT = (((1,), (1,)), ((), ()))


def _rms(v, g):
    return v * lax.rsqrt(jnp.mean(v * v, axis=-1, keepdims=True) + EPS) * g


def _sigmoid(v):
    return 0.5 * jnp.tanh(0.5 * v) + 0.5


def _silu(v):
    return v * _sigmoid(v)


def _dot(a, b):
    return jnp.dot(a, b, preferred_element_type=_F32)


def _dot_nt(a, b):
    return lax.dot_general(a, b, _NT, preferred_element_type=_F32)


def _tail_proj(h, w_in_t_ref):
    def proj(off, n):
        return _dot_nt(h, w_in_t_ref[_O_TAIL + off:_O_TAIL + off + n, :])
    return proj


def _conv_branch(proj, st_ref, conv_w_ref, w_oc_ref, mb_ref, cst_ref, ub_ref, carry_ref, *,
                 tm, n_seq, carry_state):
    seq_len = tm // n_seq
    u = proj(_O_CC, CONV_WIDTH) * proj(_O_CX, CONV_WIDTH)
    w0 = conv_w_ref[0:1, :]
    w1 = conv_w_ref[1:2, :]
    w2 = conv_w_ref[2:3, :]
    if carry_state:
        @pl.when(pl.program_id(1) == 0)
        def _():
            carry_ref[0:CONV_K - 1, :] = st_ref[0]
    convs = []
    for j in range(n_seq):
        u_j = u[j * seq_len:(j + 1) * seq_len]
        prev = carry_ref[0:CONV_K - 1, :] if carry_state else st_ref[j]
        ub_ref[j, 8 - (CONV_K - 1):8, :] = prev
        ub_ref[j, 8:8 + seq_len, :] = u_j
        convs.append(w0 * ub_ref[j, 6:6 + seq_len, :] + w1 * ub_ref[j, 7:7 + seq_len, :] + w2 * u_j)
        new_state = u_j[seq_len - (CONV_K - 1):, :]
        cst_ref[j] = new_state
        if carry_state:
            carry_ref[0:CONV_K - 1, :] = new_state
    conv = convs[0] if n_seq == 1 else jnp.concatenate(convs, axis=0)
    c_b = proj(_O_CB, CONV_WIDTH)
    g_conv = proj(_O_GC, CONV_WIDTH)
    merge_gate = _sigmoid(proj(_O_MC, D_MODEL))
    bb = (c_b * conv * _silu(g_conv)).astype(_BF16)
    branch_b = _dot(bb, w_oc_ref[...])
    mb_ref[0] = (merge_gate * branch_b).astype(_BF16)


def _proj_prompt_kernel(x_ref, cos_ref, sin_ref, st_ref, pre_g_ref, w_in_t_ref, q_g_ref,
                        w_qabs_ref, w_uqp_t_ref, kv_g_ref, conv_w_ref, w_oc_ref,
                        qt_ref, kc_ref, vt_ref, ckv_ref, kpet_ref, ga_ref, sa_ref, mb_ref, cst_ref,
                        ub_ref, carry_ref, *, tq):
    tm = x_ref.shape[1]
    h = _rms(x_ref[0], pre_g_ref[...]).astype(_BF16)
    cos_t = cos_ref[...]
    sin_t = sin_ref[...]

    qkv = _dot_nt(h, w_in_t_ref[0:_QKV_COLS, :])
    proj = _tail_proj(h, w_in_t_ref)
    sa_ref[0] = _sigmoid(proj(_O_MM, D_MODEL)).astype(_BF16)
    ga_ref[0] = _silu(proj(_O_GM, MLA_WIDTH)).astype(_BF16)
    qn = _rms(qkv[:, :Q_LORA], q_g_ref[...]).astype(_BF16)
    q_abs_t = _dot_nt(w_qabs_ref[...], qn)
    q_pe_t = _dot_nt(w_uqp_t_ref[...], qn).reshape(N_HEADS, QK_ROPE, tm)
    x1 = q_pe_t[:, :HALF_ROPE, :]
    x2 = q_pe_t[:, HALF_ROPE:, :]
    r1 = (x1 * cos_t - x2 * sin_t) * Q_SCALE
    r2 = (x2 * cos_t + x1 * sin_t) * Q_SCALE
    groups = tq // COL_BLOCK
    for j in range(tm // tq):
        for hd in range(N_HEADS):
            for g in range(groups):
                tok = slice(j * tq + g * COL_BLOCK, j * tq + (g + 1) * COL_BLOCK)
                cb = hd * groups + g
                qt_ref[0, j, cb, 0:KV_LORA, :] = (
                    q_abs_t[hd * KV_LORA:(hd + 1) * KV_LORA, tok] * Q_SCALE).astype(_BF16)
                qt_ref[0, j, cb, KV_LORA:KV_LORA + HALF_ROPE, :] = r1[hd][:, tok].astype(_BF16)
                qt_ref[0, j, cb, KV_LORA + HALF_ROPE:KEY_DIM, :] = r2[hd][:, tok].astype(_BF16)

    kvr = qkv[:, Q_LORA:]
    ckv = _rms(kvr[:, :KV_LORA], kv_g_ref[...])
    ckv_ref[0] = ckv
    ckv_t = ckv.T.astype(_BF16)
    for u in range(tm // MXU_DIM):
        vt_ref[0, u] = ckv_t[:, u * MXU_DIM:(u + 1) * MXU_DIM]
    kr_t = kvr[:, KV_LORA:].T
    k1 = kr_t[0:HALF_ROPE]
    k2 = kr_t[HALF_ROPE:QK_ROPE]
    kpe_t = jnp.concatenate([k1 * cos_t - k2 * sin_t, k2 * cos_t + k1 * sin_t], axis=0)
    kpet_ref[0] = kpe_t
    kpe = jnp.concatenate([kpe_t, jnp.zeros((LANES - QK_ROPE, tm), _F32)], axis=0).T
    kc_ref[0, :, 0:KV_LORA] = ckv.astype(_BF16)
    kc_ref[0, :, KV_LORA:KEY_DIM] = kpe[:, :QK_ROPE].astype(_BF16)

    _conv_branch(proj, st_ref, conv_w_ref, w_oc_ref, mb_ref, cst_ref, ub_ref, carry_ref,
                 tm=tm, n_seq=1, carry_state=True)


def _project_prompt(x, cos_t, sin_t, state, w, *, tm, tq):
    b, t, _ = x.shape

    def rows(width, dtype):
        return (jax.ShapeDtypeStruct((b, t, width), dtype),
                pl.BlockSpec((1, tm, width), lambda bi, ti: (bi, ti, 0)))

    def full(a):
        return pl.BlockSpec(a.shape, lambda bi, ti: (0,) * a.ndim)

    outs = [
        (jax.ShapeDtypeStruct((b, t // tq, N_HEADS * tq // COL_BLOCK, KEY_DIM, COL_BLOCK), _BF16),
         pl.BlockSpec((1, tm // tq, N_HEADS * tq // COL_BLOCK, KEY_DIM, COL_BLOCK),
                      lambda bi, ti: (bi, ti, 0, 0, 0))),
        rows(KEY_DIM, _BF16),
        (jax.ShapeDtypeStruct((b, t // MXU_DIM, KV_LORA, MXU_DIM), _BF16),
         pl.BlockSpec((1, tm // MXU_DIM, KV_LORA, MXU_DIM), lambda bi, ti: (bi, ti, 0, 0))),
        rows(KV_LORA, _F32),
        (jax.ShapeDtypeStruct((b, QK_ROPE, t), _F32),
         pl.BlockSpec((1, QK_ROPE, tm), lambda bi, ti: (bi, 0, ti))),
        rows(MLA_WIDTH, _BF16),
        rows(D_MODEL, _BF16),
        rows(D_MODEL, _BF16),
        (jax.ShapeDtypeStruct((b, CONV_K - 1, CONV_WIDTH), _F32),
         pl.BlockSpec((1, CONV_K - 1, CONV_WIDTH), lambda bi, ti: (bi, 0, 0))),
    ]
    table_spec = pl.BlockSpec((HALF_ROPE, tm), lambda bi, ti: (0, ti))
    weights = (w["pre_g"], w["w_in_t"], w["q_g"], w["w_qabs"], w["w_uqp_t"],
               w["kv_g"], w["conv_w"], w["w_oc"])
    return pl.pallas_call(
        functools.partial(_proj_prompt_kernel, tq=tq),
        grid=(b, t // tm),
        in_specs=[pl.BlockSpec((1, tm, D_MODEL), lambda bi, ti: (bi, ti, 0)),
                  table_spec, table_spec,
                  pl.BlockSpec((1, CONV_K - 1, CONV_WIDTH), lambda bi, ti: (bi, 0, 0))]
                 + [full(a) for a in weights],
        out_specs=[o[1] for o in outs],
        out_shape=[o[0] for o in outs],
        scratch_shapes=[pltpu.VMEM((1, tm + 8, CONV_WIDTH), _F32),
                        pltpu.VMEM((8, CONV_WIDTH), _F32)],
        compiler_params=pltpu.CompilerParams(
            dimension_semantics=("arbitrary", "arbitrary"), vmem_limit_bytes=VMEM_LIMIT_BYTES),
        name="proj_prompt",
    )(x, cos_t, sin_t, state, *weights)


def _proj_sample_kernel(x_ref, cos_ref, sn1_ref, sn2_ref, st_ref, pre_g_ref, w_in_t_ref, q_g_ref,
                        w_qabs_ref, w_uqp_ref, kv_g_ref, conv_w_ref, w_oc_ref,
                        q_ref, kc_ref, ckv_ref, kpe_ref, ga_ref, sa_ref, mb_ref, cst_ref,
                        ub_ref, carry_ref, *, n_seq):
    h = _rms(x_ref[0], pre_g_ref[...]).astype(_BF16)
    cos = cos_ref[...]
    sn1 = sn1_ref[...]
    sn2 = sn2_ref[...]

    def rope(v):
        return (v * cos + pltpu.roll(v, HALF_ROPE, 1) * sn1
                + pltpu.roll(v, LANES - HALF_ROPE, 1) * sn2)

    qkv = _dot_nt(h, w_in_t_ref[0:_QKV_COLS, :])
    qn = _rms(qkv[:, :Q_LORA], q_g_ref[...]).astype(_BF16)
    q_abs = _dot_nt(qn, w_qabs_ref[...])
    q_pe = _dot(qn, w_uqp_ref[...])
    n_rope_blocks = N_HEADS // HEADS_PER_ROPE_BLOCK
    q_rot = [rope(q_pe[:, j * LANES:(j + 1) * LANES]) * Q_SCALE for j in range(n_rope_blocks)]
    for hd in range(N_HEADS):
        q_ref[0, hd, :, 0:KV_LORA] = (q_abs[:, hd * LANES:(hd + 1) * LANES] * Q_SCALE).astype(_BF16)
        grp = hd % HEADS_PER_ROPE_BLOCK
        blk = q_rot[hd // HEADS_PER_ROPE_BLOCK]
        if grp:
            blk = pltpu.roll(blk, LANES - grp * QK_ROPE, 1)
        q_ref[0, hd, :, KV_LORA:KEY_DIM] = blk[:, :QK_ROPE].astype(_BF16)

    kvr = qkv[:, Q_LORA:]
    ckv = _rms(kvr[:, :KV_LORA], kv_g_ref[...])
    k_rot = rope(kvr[:, KV_LORA:])
    ckv_ref[0] = ckv
    kpe_ref[0] = k_rot[:, :QK_ROPE]
    kc_ref[0, :, 0:KV_LORA] = ckv.astype(_BF16)
    kc_ref[0, :, KV_LORA:KEY_DIM] = k_rot[:, :QK_ROPE].astype(_BF16)

    proj = _tail_proj(h, w_in_t_ref)
    sa_ref[0] = _sigmoid(proj(_O_MM, D_MODEL)).astype(_BF16)
    ga_ref[0] = _silu(proj(_O_GM, MLA_WIDTH)).astype(_BF16)
    _conv_branch(proj, st_ref, conv_w_ref, w_oc_ref, mb_ref, cst_ref, ub_ref, carry_ref,
                 tm=x_ref.shape[1], n_seq=n_seq, carry_state=False)


def _project_sample(x, cos, sn1, sn2, state, w, *, n_seq):
    _, tm, _ = x.shape

    def rows(width, dtype):
        return (jax.ShapeDtypeStruct((1, tm, width), dtype), pl.BlockSpec((1, tm, width), lambda i: (0, 0, 0)))

    def full(a):
        return pl.BlockSpec(a.shape, lambda i: (0,) * a.ndim)

    outs = [
        (jax.ShapeDtypeStruct((1, N_HEADS, tm, KEY_DIM), _BF16),
         pl.BlockSpec((1, N_HEADS, tm, KEY_DIM), lambda i: (0, 0, 0, 0))),
        rows(KEY_DIM, _BF16), rows(KV_LORA, _F32), rows(QK_ROPE, _F32),
        rows(MLA_WIDTH, _BF16), rows(D_MODEL, _BF16), rows(D_MODEL, _BF16),
        (jax.ShapeDtypeStruct((n_seq, CONV_K - 1, CONV_WIDTH), _F32),
         pl.BlockSpec((n_seq, CONV_K - 1, CONV_WIDTH), lambda i: (0, 0, 0))),
    ]
    weights = (w["pre_g"], w["w_in_t"], w["q_g"], w["w_qabs"], w["w_uqp"],
               w["kv_g"], w["conv_w"], w["w_oc"])
    return pl.pallas_call(
        functools.partial(_proj_sample_kernel, n_seq=n_seq),
        grid=(1,),
        in_specs=[full(x), full(cos), full(sn1), full(sn2), full(state)] + [full(a) for a in weights],
        out_specs=[o[1] for o in outs],
        out_shape=[o[0] for o in outs],
        scratch_shapes=[pltpu.VMEM((n_seq, tm // n_seq + 8, CONV_WIDTH), _F32),
                        pltpu.VMEM((8, CONV_WIDTH), _F32)],
        compiler_params=pltpu.CompilerParams(
            dimension_semantics=("arbitrary",), vmem_limit_bytes=VMEM_LIMIT_BYTES),
        name="proj_sample",
    )(x, cos, sn1, sn2, state, *weights)


def _epilogue(o_lat, ga, sa, mb, x, w_uv_ref, w_om_ref, w_out_ref, post_g_ref):
    o = _dot(o_lat.astype(_BF16), w_uv_ref[...])
    branch_a = _dot((o * ga.astype(_F32)).astype(_BF16), w_om_ref[...])
    merged = sa.astype(_F32) * branch_a + mb.astype(_F32)
    z = _dot(merged.astype(_BF16), w_out_ref[...])
    return x + _rms(z, post_g_ref[...])


def _col_reduce(v, op):
    n_keys, cols = v.shape
    slabs = max(n_keys // REDUCE_SLAB, 1)
    part = op(v.reshape(slabs, n_keys // slabs, cols), axis=0)
    return op(part, axis=0, keepdims=True)


def _attn_prompt_kernel(qt_ref, qn_ref, kc_ref, vt_ref, bias_ref, ga_ref, sa_ref, mb_ref, x_ref,
                        w_uv_ref, w_om_ref, w_out_ref, post_g_ref, y_ref,
                        m_ref, l_ref, acc_ref, s_ref, mc_ref, *, tq, tk):
    i = pl.program_id(1)
    units = tk // MXU_DIM
    q_tiles_per_k_tile = tk // tq
    edge = i // q_tiles_per_k_tile
    groups = tq // COL_BLOCK
    n_cb = N_HEADS * groups

    m_ref[...] = jnp.full(m_ref.shape, NEG_INF, _F32)
    l_ref[...] = jnp.zeros(l_ref.shape, _F32)
    acc_ref[...] = jnp.zeros(acc_ref.shape, _F32)

    def keys_of(tile):
        return kc_ref[0, pl.ds(pl.multiple_of(tile * tk, tk), tk), :]

    def values_of(tile):
        return jnp.concatenate([vt_ref[0, tile * units + u] for u in range(units)], axis=1)

    def stage1(k, cb, slot=0, q_tile=None, q_ref=qt_ref, split=True):
        qb = q_ref[0, 0, cb]
        if split:
            s = jnp.concatenate([_dot(k[u * MXU_DIM:(u + 1) * MXU_DIM], qb) for u in range(units)], axis=0)
        else:
            s = _dot(k, qb)
        if q_tile is not None:
            s = s + bias_ref[(q_tile % q_tiles_per_k_tile) * groups + cb % groups]
        s_ref[slot, cb] = s
        mc_ref[slot, cb] = _col_reduce(s, jnp.max)

    def stage2(vts, cb, slot=0):
        m_prev = m_ref[cb]
        m_new = jnp.maximum(m_prev, mc_ref[slot, cb])
        alpha = jnp.exp2(m_prev - m_new)
        p = jnp.exp2(s_ref[slot, cb] - m_new)
        l_ref[cb] = alpha * l_ref[cb] + _col_reduce(p, jnp.sum)
        pv = _dot(vts, p.astype(_BF16))
        acc_ref[cb] = alpha * acc_ref[cb] + pv
        m_ref[cb] = m_new

    @pl.when(i == 0)
    def _():
        k_edge = keys_of(edge)
        for cb in range(n_cb):
            stage1(k_edge, cb, q_tile=i)

    def advance(j, rd, wr):
        vts = values_of(jnp.where(j == 0, edge, j - 1))
        k = keys_of(j)
        in_place = rd == wr
        lead = 0 if in_place else STAGE1_LEAD
        for cb in range(lead):
            stage1(k, cb, wr, split=in_place)
        for cb in range(n_cb):
            stage2(vts, cb, rd)
            if cb + lead < n_cb:
                stage1(k, cb + lead, wr, split=in_place)

    def single(j, carry):
        advance(j, 0, 0)
        return carry

    def double(pair, carry):
        advance(odd + 2 * pair, 0, 1)
        advance(odd + 2 * pair + 1, 1, 0)
        return carry

    odd = edge % 2
    lax.fori_loop(0, odd, single, 0)
    lax.fori_loop(0, edge // 2, double, 0)

    last_vts = values_of(jnp.where(edge == 0, edge, edge - 1))
    nxt = jnp.minimum(i + 1, pl.num_programs(1) - 1)
    k_next = keys_of(nxt // q_tiles_per_k_tile)
    def drain(cb):
        stage2(last_vts, cb)
        stage1(k_next, cb, 0, q_tile=nxt, q_ref=qn_ref)

    def epilogue_stages(g):
        rows = slice(g * COL_BLOCK, (g + 1) * COL_BLOCK)
        o_lat = jnp.concatenate([(acc_ref[hd * groups + g] / l_ref[hd * groups + g]).T
                                 for hd in range(N_HEADS)], axis=1)
        o = _dot(o_lat.astype(_BF16), w_uv_ref[...])
        yield
        branch_a = _dot((o * ga_ref[0, rows, :].astype(_F32)).astype(_BF16), w_om_ref[...])
        yield
        merged = sa_ref[0, rows, :].astype(_F32) * branch_a + mb_ref[0, rows, :].astype(_F32)
        z = _dot(merged.astype(_BF16), w_out_ref[...])
        yield
        y_ref[0, rows, :] = x_ref[0, rows, :] + _rms(z, post_g_ref[...])
        yield

    pending = iter(())
    for g in range(groups):
        blocks = [hd * groups + g for hd in range(N_HEADS)]
        chunk = -(-len(blocks) // EPILOGUE_STAGES)
        for start in range(0, len(blocks), chunk):
            next(pending, None)
            for cb in blocks[start:start + chunk]:
                drain(cb)
        for _ in pending:
            pass
        pending = epilogue_stages(g)
    for _ in pending:
        pass


def _attend_prompt(qt, kc, vt, ga, sa, mb, x, w, *, tq, tk):
    b, t, _ = x.shape
    n_q = t // tq

    def row_spec(width):
        return pl.BlockSpec((1, tq, width), lambda bi, qi: (bi, qi, 0))

    def full(a):
        return pl.BlockSpec(a.shape, lambda bi, qi: (0,) * a.ndim)

    k_chunk = np.arange(tk)[None, :, None] // CHUNK
    q_chunk = (np.arange(tk // COL_BLOCK)[:, None, None] * COL_BLOCK
               + np.arange(COL_BLOCK)[None, None, :]) // CHUNK
    bias = jnp.asarray(np.where(k_chunk <= q_chunk, 0.0, NEG_INF), _F32)
    n_cb = N_HEADS * tq // COL_BLOCK

    weights = (w["w_uv"], w["w_om"], w["w_out"], w["post_g"])
    return pl.pallas_call(
        functools.partial(_attn_prompt_kernel, tq=tq, tk=tk),
        grid=(b, t // tq),
        in_specs=[pl.BlockSpec((1, 1, n_cb, KEY_DIM, COL_BLOCK), lambda bi, qi: (bi, qi, 0, 0, 0)),
                  pl.BlockSpec((1, 1, n_cb, KEY_DIM, COL_BLOCK),
                               lambda bi, qi: (bi, jnp.minimum(qi + 1, n_q - 1), 0, 0, 0)),
                  pl.BlockSpec((1, t, KEY_DIM), lambda bi, qi: (bi, 0, 0), pipeline_mode=pl.Buffered(1)),
                  pl.BlockSpec((1, t // MXU_DIM, KV_LORA, MXU_DIM), lambda bi, qi: (bi, 0, 0, 0),
                               pipeline_mode=pl.Buffered(1)),
                  full(bias),
                  row_spec(MLA_WIDTH), row_spec(D_MODEL), row_spec(D_MODEL), row_spec(D_MODEL)]
                 + [full(a) for a in weights],
        out_specs=row_spec(D_MODEL),
        out_shape=jax.ShapeDtypeStruct((b, t, D_MODEL), _F32),
        scratch_shapes=[pltpu.VMEM((n_cb, 1, COL_BLOCK), _F32), pltpu.VMEM((n_cb, 1, COL_BLOCK), _F32),
                        pltpu.VMEM((n_cb, KV_LORA, COL_BLOCK), _F32),
                        pltpu.VMEM((2, n_cb, tk, COL_BLOCK), _F32), pltpu.VMEM((2, n_cb, 1, COL_BLOCK), _F32)],
        compiler_params=pltpu.CompilerParams(
            dimension_semantics=("arbitrary", "arbitrary"), vmem_limit_bytes=VMEM_LIMIT_BYTES),
        name="attn_prompt",
    )(qt, qt, kc, vt, bias, ga, sa, mb, x, *weights)


def _attn_first(q, k, m_ref, l_ref, acc_ref, visible=None):
    s = _dot_nt(q, k)
    if visible is not None:
        s = jnp.where(visible, s, NEG_INF)
    m = jnp.max(s, axis=1, keepdims=True)
    p = jnp.exp2(s - m)
    m_ref[...] = jnp.broadcast_to(m, m_ref.shape)
    l_ref[...] = jnp.broadcast_to(jnp.sum(p, axis=1, keepdims=True), l_ref.shape)
    acc_ref[...] = _dot(p.astype(_BF16), k[:, :KV_LORA])


def _attn_update(q, k_lat, k_rope_t, m_ref, l_ref, acc_ref):
    s = _dot_nt(q[:, :KV_LORA], k_lat) + _dot(q[:, KV_LORA:], k_rope_t)
    m_prev = m_ref[...]
    m_new = jnp.maximum(m_prev, jnp.max(s, axis=1, keepdims=True))
    alpha = jnp.exp2(m_prev - m_new)
    p = jnp.exp2(s - jnp.concatenate([m_new] * (s.shape[1] // LANES), axis=1))
    l_ref[...] = alpha * l_ref[...] + jnp.sum(p, axis=1, keepdims=True)
    acc_ref[...] = alpha * acc_ref[...] + _dot(p.astype(_BF16), k_lat)
    m_ref[...] = m_new


def _attn_sample_kernel(q_ref, past_lat_ref, past_rope_ref, knew_ref, ga_ref, sa_ref, mb_ref, x_ref,
                        w_uv_ref, w_om_ref, w_out_ref, post_g_ref, y_ref,
                        m_ref, l_ref, acc_ref, o_ref, *, new_visible):
    bi = pl.program_id(0)
    per_step, t_new, _ = knew_ref.shape
    rows = N_HEADS * t_new
    if new_visible is None:
        visible = None
    else:
        q_pos = lax.broadcasted_iota(jnp.int32, (rows, t_new), 0) % t_new
        k_pos = lax.broadcasted_iota(jnp.int32, (rows, t_new), 1)
        visible = (k_pos + new_visible[0]) // CHUNK <= (q_pos + new_visible[0]) // CHUNK
    for g in range(per_step):
        q = q_ref[0, :, g * t_new:(g + 1) * t_new, :].reshape(rows, KEY_DIM)
        m_g, l_g, acc_g = m_ref.at[g], l_ref.at[g], acc_ref.at[g]
        _attn_first(q, knew_ref[g], m_g, l_g, acc_g, visible=visible)
        _attn_update(q, past_lat_ref[g].astype(_BF16), past_rope_ref[g].astype(_BF16), m_g, l_g, acc_g)
        o = acc_g[...] / l_g[...]
        r0 = pl.multiple_of((bi * per_step + g) * t_new, t_new)
        o_ref[pl.ds(r0, t_new), :] = jnp.concatenate(
            [o[hd * t_new:(hd + 1) * t_new] for hd in range(N_HEADS)], axis=1)

    @pl.when(bi == pl.num_programs(0) - 1)
    def _():
        y_ref[...] = _epilogue(o_ref[...], ga_ref[...], sa_ref[...], mb_ref[...], x_ref[...],
                               w_uv_ref, w_om_ref, w_out_ref, post_g_ref)


def _attend_sample(q, past_lat, past_rope, k_new, ga, sa, mb, x, w):
    nb, t_new, _ = k_new.shape
    past_len = past_lat.shape[1]
    n_rows = nb * t_new
    rows = N_HEADS * t_new
    per_step = SAMPLE_STREAMS_PER_STEP if nb % SAMPLE_STREAMS_PER_STEP == 0 else 1
    last_q, first_q = past_len + t_new - 1, past_len
    new_visible = None if last_q // CHUNK == first_q // CHUNK else (past_len,)

    def full(a):
        return pl.BlockSpec(a.shape, lambda bi: (0,) * a.ndim)

    weights = (w["w_uv"], w["w_om"], w["w_out"], w["post_g"])
    return pl.pallas_call(
        functools.partial(_attn_sample_kernel, new_visible=new_visible),
        grid=(nb // per_step,),
        in_specs=[pl.BlockSpec((1, N_HEADS, per_step * t_new, KEY_DIM), lambda bi: (0, 0, bi, 0)),
                  pl.BlockSpec((per_step, past_len, KV_LORA), lambda bi: (bi, 0, 0)),
                  pl.BlockSpec((per_step, QK_ROPE, past_len), lambda bi: (bi, 0, 0)),
                  pl.BlockSpec((per_step, t_new, KEY_DIM), lambda bi: (bi, 0, 0)),
                  full(ga), full(sa), full(mb), full(x)] + [full(a) for a in weights],
        out_specs=pl.BlockSpec((n_rows, D_MODEL), lambda bi: (0, 0)),
        out_shape=jax.ShapeDtypeStruct((n_rows, D_MODEL), _F32),
        scratch_shapes=[pltpu.VMEM((per_step, rows, LANES), _F32), pltpu.VMEM((per_step, rows, LANES), _F32),
                        pltpu.VMEM((per_step, rows, KV_LORA), _F32),
                        pltpu.VMEM((n_rows, N_HEADS * KV_LORA), _F32)],
        compiler_params=pltpu.CompilerParams(
            dimension_semantics=("arbitrary",), vmem_limit_bytes=VMEM_LIMIT_BYTES),
        name="attn_sample",
    )(q, past_lat, past_rope, k_new, ga, sa, mb, x, *weights)


def _absorb_kernel(w_uk_ref, w_uqn_ref, o_ref):
    for hd in range(N_HEADS):
        o_ref[hd] = lax.dot_general(w_uk_ref[hd], w_uqn_ref[hd], _NT, precision=lax.Precision.HIGHEST,
                                    preferred_element_type=_F32).astype(o_ref.dtype)


def _absorbed_query_weight(w_uk, w_uqn):
    out = pl.pallas_call(
        _absorb_kernel,
        out_shape=jax.ShapeDtypeStruct((N_HEADS, KV_LORA, Q_LORA), _BF16),
        name="absorb_q_weight",
    )(jnp.transpose(w_uk, (1, 0, 2)), jnp.transpose(w_uqn, (1, 0, 2)))
    return out.reshape(N_HEADS * KV_LORA, Q_LORA)


def _prep_weights(pre_norm, w_in, q_norm, w_uq, kv_norm, w_uk, w_uv, w_o_mla, conv_w, w_o_conv, w_out, post_norm):
    assert w_in.shape[1] == _O_TAIL + _TAIL_COLS
    wq = w_uq.reshape(Q_LORA, N_HEADS, QK_NOPE + QK_ROPE)
    w_uqp = wq[:, :, QK_NOPE:].reshape(Q_LORA, N_HEADS * QK_ROPE).astype(_BF16)
    return {
        "pre_g": pre_norm.reshape(1, D_MODEL),
        "w_in_t": w_in.T.astype(_BF16),
        "q_g": q_norm.reshape(1, Q_LORA),
        "w_qabs": _absorbed_query_weight(w_uk, wq[:, :, :QK_NOPE]),
        "w_uqp": w_uqp,
        "w_uqp_t": w_uqp.T,
        "kv_g": kv_norm.reshape(1, KV_LORA),
        "conv_w": conv_w,
        "w_oc": w_o_conv.astype(_BF16),
        "w_uv": (jnp.transpose(w_uv, (1, 0, 2))[:, :, None, :]
                 * jnp.eye(N_HEADS, dtype=_F32)[:, None, :, None]
                 ).reshape(N_HEADS * KV_LORA, N_HEADS * V_HEAD).astype(_BF16),
        "w_om": w_o_mla.astype(_BF16),
        "w_out": w_out.astype(_BF16),
        "post_g": post_norm.reshape(1, D_MODEL),
    }


def _rope_angles(pos, frequency_major=False):
    inv = ROPE_BASE ** (-jnp.arange(HALF_ROPE, dtype=_F32) / HALF_ROPE)
    if frequency_major:
        ang = inv[:, None] * pos.astype(_F32)[None, :]
    else:
        ang = pos.astype(_F32)[:, None] * inv[None, :]
    return jnp.cos(ang), jnp.sin(ang)


def _rope_tables_rows(pos):
    cos, sin = _rope_angles(pos)
    zero = jnp.zeros_like(sin)
    reps = LANES // QK_ROPE
    return (jnp.tile(jnp.concatenate([cos, cos], axis=1), (1, reps)),
            jnp.tile(jnp.concatenate([zero, sin], axis=1), (1, reps)),
            jnp.tile(jnp.concatenate([-sin, zero], axis=1), (1, reps)))


SAMPLE_STREAMS_PER_STEP = 4
PROMPT_ROW_TILE = 1024
PROMPT_Q_TILE = 512
PROMPT_K_TILE = 512


def kernel(x_prompt, x_sample, cache_kv_latent, cache_k_rope, state_conv, pre_norm, w_in, q_norm, w_uq, kv_norm,
           w_uk, w_uv, w_o_mla, conv_w, w_o_conv, w_out, post_norm):
    depth = pre_norm.shape[0]
    assert depth == 1
    b, t, _ = x_prompt.shape
    nb, t_new, _ = x_sample.shape
    past_len = cache_kv_latent.shape[2]
    lyr = 0
    w = _prep_weights(pre_norm[lyr], w_in[lyr], q_norm[lyr], w_uq[lyr], kv_norm[lyr], w_uk[lyr], w_uv[lyr],
                      w_o_mla[lyr], conv_w[lyr], w_o_conv[lyr], w_out[lyr], post_norm[lyr])

    cos_t, sin_t = _rope_angles(jnp.arange(t, dtype=jnp.int32), frequency_major=True)
    zero_state = jnp.zeros((b, CONV_K - 1, CONV_WIDTH), _F32)
    qt, kc, vt, ckv_p, kpe_t, ga, sa, mb, cv_p = _project_prompt(
        x_prompt, cos_t, sin_t, zero_state, w, tm=PROMPT_ROW_TILE, tq=PROMPT_Q_TILE)
    y_p = _attend_prompt(qt, kc, vt, ga, sa, mb, x_prompt, w, tq=PROMPT_Q_TILE, tk=PROMPT_K_TILE)

    n_rows = nb * t_new
    tabs_s = _rope_tables_rows(past_len + jnp.arange(n_rows, dtype=jnp.int32) % t_new)
    xs = x_sample.reshape(1, n_rows, D_MODEL)
    q_s, kc_s, ckv_s, kpe_s, ga_s, sa_s, mb_s, cv_s = _project_sample(
        xs, *tabs_s, state_conv[lyr], w, n_seq=nb)
    y_s = _attend_sample(q_s, cache_kv_latent[lyr], jnp.swapaxes(cache_k_rope[lyr], 1, 2),
                         kc_s.reshape(nb, t_new, KEY_DIM),
                         ga_s[0], sa_s[0], mb_s[0], xs[0], w)

    return (y_p, y_s.reshape(nb, t_new, D_MODEL),
            ckv_p[None], jnp.swapaxes(kpe_t, 1, 2)[None], cv_p[None],
            ckv_s.reshape(1, nb, t_new, KV_LORA), kpe_s.reshape(1, nb, t_new, QK_ROPE), cv_s[None])
```

```python
import functools

import numpy as np
import jax
import jax.numpy as jnp
from jax import lax
from jax.experimental import pallas as pl
from jax.experimental.pallas import tpu as pltpu

N_HEADS = 8
QK_NOPE = 64
QK_ROPE = 32
V_HEAD = 64
Q_LORA = 256
KV_LORA = 128
MLA_WIDTH = N_HEADS * V_HEAD
CONV_WIDTH = 512
CONV_K = 3
D_MODEL = 1024
CHUNK = 64
ROPE_BASE = 10000.0
EPS = 1e-6
SM_SCALE = (QK_NOPE + QK_ROPE) ** -0.5
NEG_INF = -1e30
LOG2E = 1.4426950408889634

LANES = 128
MXU_DIM = 256
COL_BLOCK = MXU_DIM
REDUCE_SLAB = 256
EPILOGUE_STAGES = 4
STAGE1_LEAD = 1
HALF_ROPE = QK_ROPE // 2
KEY_DIM = KV_LORA + QK_ROPE
HEADS_PER_ROPE_BLOCK = LANES // QK_ROPE
Q_SCALE = SM_SCALE * LOG2E
VMEM_LIMIT_BYTES = 56 * 1024 * 1024

_O_TAIL = Q_LORA + KV_LORA + QK_ROPE
_QKV_COLS = -(-_O_TAIL // MXU_DIM) * MXU_DIM
_O_GM = 0
_O_CB = _O_GM + MLA_WIDTH
_O_CC = _O_CB + CONV_WIDTH
_O_CX = _O_CC + CONV_WIDTH
_O_GC = _O_CX + CONV_WIDTH
_O_MM = _O_GC + CONV_WIDTH
_O_MC = _O_MM + D_MODEL
_TAIL_COLS = _O_MC + D_MODEL

_F32 = jnp.float32
_BF16 = jnp.bfloat16
_NT = (((1,), (1,)), ((), ()))


def _rms(v, g):
    return v * lax.rsqrt(jnp.mean(v * v, axis=-1, keepdims=True) + EPS) * g


def _sigmoid(v):
    return 0.5 * jnp.tanh(0.5 * v) + 0.5


def _silu(v):
    return v * _sigmoid(v)


def _dot(a, b):
    return jnp.dot(a, b, preferred_element_type=_F32)


def _dot_nt(a, b):
    return lax.dot_general(a, b, _NT, preferred_element_type=_F32)


def _tail_proj(h, w_in_t_ref):
    def proj(off, n):
        return _dot_nt(h, w_in_t_ref[_O_TAIL + off:_O_TAIL + off + n, :])
    return proj


def _conv_branch(proj, st_ref, conv_w_ref, w_oc_ref, mb_ref, cst_ref, ub_ref, carry_ref, *,
                 tm, n_seq, carry_state):
    seq_len = tm // n_seq
    u = proj(_O_CC, CONV_WIDTH) * proj(_O_CX, CONV_WIDTH)
    w0 = conv_w_ref[0:1, :]
    w1 = conv_w_ref[1:2, :]
    w2 = conv_w_ref[2:3, :]
    if carry_state:
        @pl.when(pl.program_id(1) == 0)
        def _():
            carry_ref[0:CONV_K - 1, :] = st_ref[0]
    convs = []
    for j in range(n_seq):
        u_j = u[j * seq_len:(j + 1) * seq_len]
        prev = carry_ref[0:CONV_K - 1, :] if carry_state else st_ref[j]
        row = lax.broadcasted_iota(jnp.int32, (seq_len, 1), 0)
        u_1 = jnp.where(row == 0, prev[1:2, :], pltpu.roll(u_j, 1, 0))
        u_2 = jnp.where(row == 0, prev[0:1, :], jnp.where(row == 1, prev[1:2, :], pltpu.roll(u_j, 2, 0)))
        convs.append(w0 * u_2 + w1 * u_1 + w2 * u_j)
        new_state = u_j[seq_len - (CONV_K - 1):, :]
        cst_ref[j] = new_state
        if carry_state:
            carry_ref[0:CONV_K - 1, :] = new_state
    conv = convs[0] if n_seq == 1 else jnp.concatenate(convs, axis=0)
    c_b = proj(_O_CB, CONV_WIDTH)
    g_conv = proj(_O_GC, CONV_WIDTH)
    merge_gate = _sigmoid(proj(_O_MC, D_MODEL))
    bb = (c_b * conv * _silu(g_conv)).astype(_BF16)
    branch_b = _dot(bb, w_oc_ref[...])
    mb_ref[0] = (merge_gate * branch_b).astype(_BF16)


def _proj_prompt_kernel(x_ref, cos_ref, sin_ref, st_ref, pre_g_ref, w_in_t_ref, q_g_ref,
                        w_qabs_ref, w_uqp_t_ref, kv_g_ref, conv_w_ref, w_oc_ref,
                        qt_ref, kc_ref, vt_ref, ckv_ref, kpet_ref, ga_ref, sa_ref, mb_ref, cst_ref,
                        ub_ref, carry_ref, *, tq):
    tm = x_ref.shape[1]
    h = _rms(x_ref[0], pre_g_ref[...]).astype(_BF16)
    cos_t = cos_ref[...]
    sin_t = sin_ref[...]

    qkv = _dot_nt(h, w_in_t_ref[0:_QKV_COLS, :])
    proj = _tail_proj(h, w_in_t_ref)
    sa_ref[0] = _sigmoid(proj(_O_MM, D_MODEL)).astype(_BF16)
    ga_ref[0] = _silu(proj(_O_GM, MLA_WIDTH)).astype(_BF16)
    qn = _rms(qkv[:, :Q_LORA], q_g_ref[...]).astype(_BF16)
    q_abs_t = _dot_nt(w_qabs_ref[...], qn)
    q_pe_t = _dot_nt(w_uqp_t_ref[...], qn).reshape(N_HEADS, QK_ROPE, tm)
    x1 = q_pe_t[:, :HALF_ROPE, :]
    x2 = q_pe_t[:, HALF_ROPE:, :]
    r1 = (x1 * cos_t - x2 * sin_t) * Q_SCALE
    r2 = (x2 * cos_t + x1 * sin_t) * Q_SCALE
    groups = tq // COL_BLOCK
    for j in range(tm // tq):
        for hd in range(N_HEADS):
            for g in range(groups):
                tok = slice(j * tq + g * COL_BLOCK, j * tq + (g + 1) * COL_BLOCK)
                cb = hd * groups + g
                qt_ref[0, j, cb, 0:KV_LORA, :] = (
                    q_abs_t[hd * KV_LORA:(hd + 1) * KV_LORA, tok] * Q_SCALE).astype(_BF16)
                qt_ref[0, j, cb, KV_LORA:KV_LORA + HALF_ROPE, :] = r1[hd][:, tok].astype(_BF16)
                qt_ref[0, j, cb, KV_LORA + HALF_ROPE:KEY_DIM, :] = r2[hd][:, tok].astype(_BF16)

    kvr = qkv[:, Q_LORA:]
    ckv = _rms(kvr[:, :KV_LORA], kv_g_ref[...])
    ckv_ref[0] = ckv
    ckv_t = ckv.T.astype(_BF16)
    for u in range(tm // MXU_DIM):
        vt_ref[0, u] = ckv_t[:, u * MXU_DIM:(u + 1) * MXU_DIM]
    kr_t = kvr[:, KV_LORA:].T
    k1 = kr_t[0:HALF_ROPE]
    k2 = kr_t[HALF_ROPE:QK_ROPE]
    kpe_t = jnp.concatenate([k1 * cos_t - k2 * sin_t, k2 * cos_t + k1 * sin_t], axis=0)
    kpet_ref[0] = kpe_t
    kpe = jnp.concatenate([kpe_t, jnp.zeros((LANES - QK_ROPE, tm), _F32)], axis=0).T
    kc_ref[0, :, 0:KV_LORA] = ckv.astype(_BF16)
    kc_ref[0, :, KV_LORA:KEY_DIM] = kpe[:, :QK_ROPE].astype(_BF16)

    _conv_branch(proj, st_ref, conv_w_ref, w_oc_ref, mb_ref, cst_ref, ub_ref, carry_ref,
                 tm=tm, n_seq=1, carry_state=True)


def _project_prompt(x, cos_t, sin_t, state, w, *, tm, tq):
    b, t, _ = x.shape

    def rows(width, dtype):
        return (jax.ShapeDtypeStruct((b, t, width), dtype),
                pl.BlockSpec((1, tm, width), lambda bi, ti: (bi, ti, 0)))

    def full(a):
        return pl.BlockSpec(a.shape, lambda bi, ti: (0,) * a.ndim)

    outs = [
        (jax.ShapeDtypeStruct((b, t // tq, N_HEADS * tq // COL_BLOCK, KEY_DIM, COL_BLOCK), _BF16),
         pl.BlockSpec((1, tm // tq, N_HEADS * tq // COL_BLOCK, KEY_DIM, COL_BLOCK),
                      lambda bi, ti: (bi, ti, 0, 0, 0))),
        rows(KEY_DIM, _BF16),
        (jax.ShapeDtypeStruct((b, t // MXU_DIM, KV_LORA, MXU_DIM), _BF16),
         pl.BlockSpec((1, tm // MXU_DIM, KV_LORA, MXU_DIM), lambda bi, ti: (bi, ti, 0, 0))),
        rows(KV_LORA, _F32),
        (jax.ShapeDtypeStruct((b, QK_ROPE, t), _F32),
         pl.BlockSpec((1, QK_ROPE, tm), lambda bi, ti: (bi, 0, ti))),
        rows(MLA_WIDTH, _BF16),
        rows(D_MODEL, _BF16),
        rows(D_MODEL, _BF16),
        (jax.ShapeDtypeStruct((b, CONV_K - 1, CONV_WIDTH), _F32),
         pl.BlockSpec((1, CONV_K - 1, CONV_WIDTH), lambda bi, ti: (bi, 0, 0))),
    ]
    table_spec = pl.BlockSpec((HALF_ROPE, tm), lambda bi, ti: (0, ti))
    weights = (w["pre_g"], w["w_in_t"], w["q_g"], w["w_qabs"], w["w_uqp_t"],
               w["kv_g"], w["conv_w"], w["w_oc"])
    return pl.pallas_call(
        functools.partial(_proj_prompt_kernel, tq=tq),
        grid=(b, t // tm),
        in_specs=[pl.BlockSpec((1, tm, D_MODEL), lambda bi, ti: (bi, ti, 0)),
                  table_spec, table_spec,
                  pl.BlockSpec((1, CONV_K - 1, CONV_WIDTH), lambda bi, ti: (bi, 0, 0))]
                 + [full(a) for a in weights],
        out_specs=[o[1] for o in outs],
        out_shape=[o[0] for o in outs],
        scratch_shapes=[pltpu.VMEM((1, tm + 8, CONV_WIDTH), _F32),
                        pltpu.VMEM((8, CONV_WIDTH), _F32)],
        compiler_params=pltpu.CompilerParams(
            dimension_semantics=("arbitrary", "arbitrary"), vmem_limit_bytes=VMEM_LIMIT_BYTES),
        name="proj_prompt",
    )(x, cos_t, sin_t, state, *weights)


def _proj_sample_kernel(x_ref, cos_ref, sn1_ref, sn2_ref, st_ref, pre_g_ref, w_in_t_ref, q_g_ref,
                        w_qabs_ref, w_uqp_ref, kv_g_ref, conv_w_ref, w_oc_ref,
                        q_ref, kc_ref, ckv_ref, kpe_ref, ga_ref, sa_ref, mb_ref, cst_ref,
                        ub_ref, carry_ref, *, n_seq):
    h = _rms(x_ref[0], pre_g_ref[...]).astype(_BF16)
    cos = cos_ref[...]
    sn1 = sn1_ref[...]
    sn2 = sn2_ref[...]

    def rope(v):
        return (v * cos + pltpu.roll(v, HALF_ROPE, 1) * sn1
                + pltpu.roll(v, LANES - HALF_ROPE, 1) * sn2)

    qkv = _dot_nt(h, w_in_t_ref[0:_QKV_COLS, :])
    qn = _rms(qkv[:, :Q_LORA], q_g_ref[...]).astype(_BF16)
    q_abs = _dot_nt(qn, w_qabs_ref[...])
    q_pe = _dot(qn, w_uqp_ref[...])
    n_rope_blocks = N_HEADS // HEADS_PER_ROPE_BLOCK
    q_rot = [rope(q_pe[:, j * LANES:(j + 1) * LANES]) * Q_SCALE for j in range(n_rope_blocks)]
    for hd in range(N_HEADS):
        q_ref[0, hd, :, 0:KV_LORA] = (q_abs[:, hd * LANES:(hd + 1) * LANES] * Q_SCALE).astype(_BF16)
        grp = hd % HEADS_PER_ROPE_BLOCK
        blk = q_rot[hd // HEADS_PER_ROPE_BLOCK]
        if grp:
            blk = pltpu.roll(blk, LANES - grp * QK_ROPE, 1)
        q_ref[0, hd, :, KV_LORA:KEY_DIM] = blk[:, :QK_ROPE].astype(_BF16)

    kvr = qkv[:, Q_LORA:]
    ckv = _rms(kvr[:, :KV_LORA], kv_g_ref[...])
    k_rot = rope(kvr[:, KV_LORA:])
    ckv_ref[0] = ckv
    kpe_ref[0] = k_rot[:, :QK_ROPE]
    kc_ref[0, :, 0:KV_LORA] = ckv.astype(_BF16)
    kc_ref[0, :, KV_LORA:KEY_DIM] = k_rot[:, :QK_ROPE].astype(_BF16)

    proj = _tail_proj(h, w_in_t_ref)
    sa_ref[0] = _sigmoid(proj(_O_MM, D_MODEL)).astype(_BF16)
    ga_ref[0] = _silu(proj(_O_GM, MLA_WIDTH)).astype(_BF16)
    _conv_branch(proj, st_ref, conv_w_ref, w_oc_ref, mb_ref, cst_ref, ub_ref, carry_ref,
                 tm=x_ref.shape[1], n_seq=n_seq, carry_state=False)


def _project_sample(x, cos, sn1, sn2, state, w, *, n_seq):
    _, tm, _ = x.shape

    def rows(width, dtype):
        return (jax.ShapeDtypeStruct((1, tm, width), dtype), pl.BlockSpec((1, tm, width), lambda i: (0, 0, 0)))

    def full(a):
        return pl.BlockSpec(a.shape, lambda i: (0,) * a.ndim)

    outs = [
        (jax.ShapeDtypeStruct((1, N_HEADS, tm, KEY_DIM), _BF16),
         pl.BlockSpec((1, N_HEADS, tm, KEY_DIM), lambda i: (0, 0, 0, 0))),
        rows(KEY_DIM, _BF16), rows(KV_LORA, _F32), rows(QK_ROPE, _F32),
        rows(MLA_WIDTH, _BF16), rows(D_MODEL, _BF16), rows(D_MODEL, _BF16),
        (jax.ShapeDtypeStruct((n_seq, CONV_K - 1, CONV_WIDTH), _F32),
         pl.BlockSpec((n_seq, CONV_K - 1, CONV_WIDTH), lambda i: (0, 0, 0))),
    ]
    weights = (w["pre_g"], w["w_in_t"], w["q_g"], w["w_qabs"], w["w_uqp"],
               w["kv_g"], w["conv_w"], w["w_oc"])
    return pl.pallas_call(
        functools.partial(_proj_sample_kernel, n_seq=n_seq),
        grid=(1,),
        in_specs=[full(x), full(cos), full(sn1), full(sn2), full(state)] + [full(a) for a in weights],
        out_specs=[o[1] for o in outs],
        out_shape=[o[0] for o in outs],
        scratch_shapes=[pltpu.VMEM((n_seq, tm // n_seq + 8, CONV_WIDTH), _F32),
                        pltpu.VMEM((8, CONV_WIDTH), _F32)],
        compiler_params=pltpu.CompilerParams(
            dimension_semantics=("arbitrary",), vmem_limit_bytes=VMEM_LIMIT_BYTES),
        name="proj_sample",
    )(x, cos, sn1, sn2, state, *weights)


def _epilogue(o_lat, ga, sa, mb, x, w_uv_ref, w_om_ref, w_out_ref, post_g_ref):
    o = _dot(o_lat.astype(_BF16), w_uv_ref[...])
    branch_a = _dot((o * ga.astype(_F32)).astype(_BF16), w_om_ref[...])
    merged = sa.astype(_F32) * branch_a + mb.astype(_F32)
    z = _dot(merged.astype(_BF16), w_out_ref[...])
    return x + _rms(z, post_g_ref[...])


def _col_reduce(v, op):
    n_keys, cols = v.shape
    slabs = max(n_keys // REDUCE_SLAB, 1)
    part = op(v.reshape(slabs, n_keys // slabs, cols), axis=0)
    return op(part, axis=0, keepdims=True)


def _attn_prompt_kernel(qt_ref, qn_ref, kc_ref, vt_ref, bias_ref, ga_ref, sa_ref, mb_ref, x_ref,
                        w_uv_ref, w_om_ref, w_out_ref, post_g_ref, y_ref,
                        m_ref, l_ref, acc_ref, s_ref, mc_ref, *, tq, tk):
    i = pl.program_id(1)
    units = tk // MXU_DIM
    q_tiles_per_k_tile = tk // tq
    edge = i // q_tiles_per_k_tile
    groups = tq // COL_BLOCK
    n_cb = N_HEADS * groups

    m_ref[...] = jnp.full(m_ref.shape, NEG_INF, _F32)
    l_ref[...] = jnp.zeros(l_ref.shape, _F32)
    acc_ref[...] = jnp.zeros(acc_ref.shape, _F32)

    def keys_of(tile):
        return kc_ref[0, pl.ds(pl.multiple_of(tile * tk, tk), tk), :]

    def values_of(tile):
        return jnp.concatenate([vt_ref[0, tile * units + u] for u in range(units)], axis=1)

    def stage1(k, cb, slot=0, q_tile=None, q_ref=qt_ref, split=True):
        qb = q_ref[0, 0, cb]
        if split:
            s = jnp.concatenate([_dot(k[u * MXU_DIM:(u + 1) * MXU_DIM], qb) for u in range(units)], axis=0)
        else:
            s = _dot(k, qb)
        if q_tile is not None:
            s = s + bias_ref[(q_tile % q_tiles_per_k_tile) * groups + cb % groups]
        s_ref[slot, cb] = s
        mc_ref[slot, cb] = _col_reduce(s, jnp.max)

    def stage2(vts, cb, slot=0):
        m_prev = m_ref[cb]
        m_new = jnp.maximum(m_prev, mc_ref[slot, cb])
        alpha = jnp.exp2(m_prev - m_new)
        p = jnp.exp2(s_ref[slot, cb] - m_new)
        l_ref[cb] = alpha * l_ref[cb] + _col_reduce(p, jnp.sum)
        pv = _dot(vts, p.astype(_BF16))
        acc_ref[cb] = alpha * acc_ref[cb] + pv
        m_ref[cb] = m_new

    @pl.when(i == 0)
    def _():
        k_edge = keys_of(edge)
        for cb in range(n_cb):
            stage1(k_edge, cb, q_tile=i)

    def advance(j, rd, wr):
        vts = values_of(jnp.where(j == 0, edge, j - 1))
        k = keys_of(j)
        in_place = rd == wr
        lead = 0 if in_place else STAGE1_LEAD
        for cb in range(lead):
            stage1(k, cb, wr, split=in_place)
        for cb in range(n_cb):
            stage2(vts, cb, rd)
            if cb + lead < n_cb:
                stage1(k, cb + lead, wr, split=in_place)

    def single(j, carry):
        advance(j, 0, 0)
        return carry

    def double(pair, carry):
        advance(odd + 2 * pair, 0, 1)
        advance(odd + 2 * pair + 1, 1, 0)
        return carry

    odd = edge % 2
    lax.fori_loop(0, odd, single, 0)
    lax.fori_loop(0, edge // 2, double, 0)

    last_vts = values_of(jnp.where(edge == 0, edge, edge - 1))
    nxt = jnp.minimum(i + 1, pl.num_programs(1) - 1)
    k_next = keys_of(nxt // q_tiles_per_k_tile)
    def drain(cb):
        stage2(last_vts, cb)
        stage1(k_next, cb, 0, q_tile=nxt, q_ref=qn_ref)

    def epilogue_stages(g):
        rows = slice(g * COL_BLOCK, (g + 1) * COL_BLOCK)
        o_lat = jnp.concatenate([(acc_ref[hd * groups + g] / l_ref[hd * groups + g]).T
                                 for hd in range(N_HEADS)], axis=1)
        o = _dot(o_lat.astype(_BF16), w_uv_ref[...])
        yield
        branch_a = _dot((o * ga_ref[0, rows, :].astype(_F32)).astype(_BF16), w_om_ref[...])
        yield
        merged = sa_ref[0, rows, :].astype(_F32) * branch_a + mb_ref[0, rows, :].astype(_F32)
        z = _dot(merged.astype(_BF16), w_out_ref[...])
        yield
        y_ref[0, rows, :] = x_ref[0, rows, :] + _rms(z, post_g_ref[...])
        yield

    pending = iter(())
    for g in range(groups):
        blocks = [hd * groups + g for hd in range(N_HEADS)]
        chunk = -(-len(blocks) // EPILOGUE_STAGES)
        for start in range(0, len(blocks), chunk):
            next(pending, None)
            for cb in blocks[start:start + chunk]:
                drain(cb)
        for _ in pending:
            pass
        pending = epilogue_stages(g)
    for _ in pending:
        pass


def _attend_prompt(qt, kc, vt, ga, sa, mb, x, w, *, tq, tk):
    b, t, _ = x.shape
    n_q = t // tq

    def row_spec(width):
        return pl.BlockSpec((1, tq, width), lambda bi, qi: (bi, qi, 0))

    def full(a):
        return pl.BlockSpec(a.shape, lambda bi, qi: (0,) * a.ndim)

    k_chunk = np.arange(tk)[None, :, None] // CHUNK
    q_chunk = (np.arange(tk // COL_BLOCK)[:, None, None] * COL_BLOCK
               + np.arange(COL_BLOCK)[None, None, :]) // CHUNK
    bias = jnp.asarray(np.where(k_chunk <= q_chunk, 0.0, NEG_INF), _F32)
    n_cb = N_HEADS * tq // COL_BLOCK

    weights = (w["w_uv"], w["w_om"], w["w_out"], w["post_g"])
    return pl.pallas_call(
        functools.partial(_attn_prompt_kernel, tq=tq, tk=tk),
        grid=(b, t // tq),
        in_specs=[pl.BlockSpec((1, 1, n_cb, KEY_DIM, COL_BLOCK), lambda bi, qi: (bi, qi, 0, 0, 0)),
                  pl.BlockSpec((1, 1, n_cb, KEY_DIM, COL_BLOCK),
                               lambda bi, qi: (bi, jnp.minimum(qi + 1, n_q - 1), 0, 0, 0)),
                  pl.BlockSpec((1, t, KEY_DIM), lambda bi, qi: (bi, 0, 0), pipeline_mode=pl.Buffered(1)),
                  pl.BlockSpec((1, t // MXU_DIM, KV_LORA, MXU_DIM), lambda bi, qi: (bi, 0, 0, 0),
                               pipeline_mode=pl.Buffered(1)),
                  full(bias),
                  row_spec(MLA_WIDTH), row_spec(D_MODEL), row_spec(D_MODEL), row_spec(D_MODEL)]
                 + [full(a) for a in weights],
        out_specs=row_spec(D_MODEL),
        out_shape=jax.ShapeDtypeStruct((b, t, D_MODEL), _F32),
        scratch_shapes=[pltpu.VMEM((n_cb, 1, COL_BLOCK), _F32), pltpu.VMEM((n_cb, 1, COL_BLOCK), _F32),
                        pltpu.VMEM((n_cb, KV_LORA, COL_BLOCK), _F32),
                        pltpu.VMEM((2, n_cb, tk, COL_BLOCK), _F32), pltpu.VMEM((2, n_cb, 1, COL_BLOCK), _F32)],
        compiler_params=pltpu.CompilerParams(
            dimension_semantics=("arbitrary", "arbitrary"), vmem_limit_bytes=VMEM_LIMIT_BYTES),
        name="attn_prompt",
    )(qt, qt, kc, vt, bias, ga, sa, mb, x, *weights)


def _attn_first(q, k, m_ref, l_ref, acc_ref, visible=None):
    s = _dot_nt(q, k)
    if visible is not None:
        s = jnp.where(visible, s, NEG_INF)
    m = jnp.max(s, axis=1, keepdims=True)
    p = jnp.exp2(s - m)
    m_ref[...] = jnp.broadcast_to(m, m_ref.shape)
    l_ref[...] = jnp.broadcast_to(jnp.sum(p, axis=1, keepdims=True), l_ref.shape)
    acc_ref[...] = _dot(p.astype(_BF16), k[:, :KV_LORA])


def _attn_update(q, k_lat, k_rope_t, m_ref, l_ref, acc_ref):
    s = _dot_nt(q[:, :KV_LORA], k_lat) + _dot(q[:, KV_LORA:], k_rope_t)
    m_prev = m_ref[...]
    m_new = jnp.maximum(m_prev, jnp.max(s, axis=1, keepdims=True))
    alpha = jnp.exp2(m_prev - m_new)
    p = jnp.exp2(s - jnp.concatenate([m_new] * (s.shape[1] // LANES), axis=1))
    l_ref[...] = alpha * l_ref[...] + jnp.sum(p, axis=1, keepdims=True)
    acc_ref[...] = alpha * acc_ref[...] + _dot(p.astype(_BF16), k_lat)
    m_ref[...] = m_new


def _attn_sample_kernel(q_ref, past_lat_ref, past_rope_ref, knew_ref, ga_ref, sa_ref, mb_ref, x_ref,
                        w_uv_ref, w_om_ref, w_out_ref, post_g_ref, y_ref,
                        m_ref, l_ref, acc_ref, o_ref, *, new_visible):
    bi = pl.program_id(0)
    per_step, t_new, _ = knew_ref.shape
    rows = N_HEADS * t_new
    if new_visible is None:
        visible = None
    else:
        q_pos = lax.broadcasted_iota(jnp.int32, (rows, t_new), 0) % t_new
        k_pos = lax.broadcasted_iota(jnp.int32, (rows, t_new), 1)
        visible = (k_pos + new_visible[0]) // CHUNK <= (q_pos + new_visible[0]) // CHUNK
    for g in range(per_step):
        q = q_ref[0, :, g * t_new:(g + 1) * t_new, :].reshape(rows, KEY_DIM)
        m_g, l_g, acc_g = m_ref.at[g], l_ref.at[g], acc_ref.at[g]
        _attn_first(q, knew_ref[g], m_g, l_g, acc_g, visible=visible)
        _attn_update(q, past_lat_ref[g].astype(_BF16), past_rope_ref[g].astype(_BF16), m_g, l_g, acc_g)
        o = acc_g[...] / l_g[...]
        r0 = pl.multiple_of((bi * per_step + g) * t_new, t_new)
        o_ref[pl.ds(r0, t_new), :] = jnp.concatenate(
            [o[hd * t_new:(hd + 1) * t_new] for hd in range(N_HEADS)], axis=1)

    @pl.when(bi == pl.num_programs(0) - 1)
    def _():
        y_ref[...] = _epilogue(o_ref[...], ga_ref[...], sa_ref[...], mb_ref[...], x_ref[...],
                               w_uv_ref, w_om_ref, w_out_ref, post_g_ref)


def _attend_sample(q, past_lat, past_rope, k_new, ga, sa, mb, x, w):
    nb, t_new, _ = k_new.shape
    past_len = past_lat.shape[1]
    n_rows = nb * t_new
    rows = N_HEADS * t_new
    per_step = SAMPLE_STREAMS_PER_STEP if nb % SAMPLE_STREAMS_PER_STEP == 0 else 1
    last_q, first_q = past_len + t_new - 1, past_len
    new_visible = None if last_q // CHUNK == first_q // CHUNK else (past_len,)

    def full(a):
        return pl.BlockSpec(a.shape, lambda bi: (0,) * a.ndim)

    weights = (w["w_uv"], w["w_om"], w["w_out"], w["post_g"])
    return pl.pallas_call(
        functools.partial(_attn_sample_kernel, new_visible=new_visible),
        grid=(nb // per_step,),
        in_specs=[pl.BlockSpec((1, N_HEADS, per_step * t_new, KEY_DIM), lambda bi: (0, 0, bi, 0)),
                  pl.BlockSpec((per_step, past_len, KV_LORA), lambda bi: (bi, 0, 0)),
                  pl.BlockSpec((per_step, QK_ROPE, past_len), lambda bi: (bi, 0, 0)),
                  pl.BlockSpec((per_step, t_new, KEY_DIM), lambda bi: (bi, 0, 0)),
                  full(ga), full(sa), full(mb), full(x)] + [full(a) for a in weights],
        out_specs=pl.BlockSpec((n_rows, D_MODEL), lambda bi: (0, 0)),
        out_shape=jax.ShapeDtypeStruct((n_rows, D_MODEL), _F32),
        scratch_shapes=[pltpu.VMEM((per_step, rows, LANES), _F32), pltpu.VMEM((per_step, rows, LANES), _F32),
                        pltpu.VMEM((per_step, rows, KV_LORA), _F32),
                        pltpu.VMEM((n_rows, N_HEADS * KV_LORA), _F32)],
        compiler_params=pltpu.CompilerParams(
            dimension_semantics=("arbitrary",), vmem_limit_bytes=VMEM_LIMIT_BYTES),
        name="attn_sample",
    )(q, past_lat, past_rope, k_new, ga, sa, mb, x, *weights)


def _absorb_kernel(w_uk_ref, w_uqn_ref, o_ref):
    for hd in range(N_HEADS):
        o_ref[hd] = lax.dot_general(w_uk_ref[hd], w_uqn_ref[hd], _NT, precision=lax.Precision.HIGHEST,
                                    preferred_element_type=_F32).astype(o_ref.dtype)


def _absorbed_query_weight(w_uk, w_uqn):
    out = pl.pallas_call(
        _absorb_kernel,
        out_shape=jax.ShapeDtypeStruct((N_HEADS, KV_LORA, Q_LORA), _BF16),
        name="absorb_q_weight",
    )(jnp.transpose(w_uk, (1, 0, 2)), jnp.transpose(w_uqn, (1, 0, 2)))
    return out.reshape(N_HEADS * KV_LORA, Q_LORA)


def _prep_weights(pre_norm, w_in, q_norm, w_uq, kv_norm, w_uk, w_uv, w_o_mla, conv_w, w_o_conv, w_out, post_norm):
    assert w_in.shape[1] == _O_TAIL + _TAIL_COLS
    wq = w_uq.reshape(Q_LORA, N_HEADS, QK_NOPE + QK_ROPE)
    w_uqp = wq[:, :, QK_NOPE:].reshape(Q_LORA, N_HEADS * QK_ROPE).astype(_BF16)
    return {
        "pre_g": pre_norm.reshape(1, D_MODEL),
        "w_in_t": w_in.T.astype(_BF16),
        "q_g": q_norm.reshape(1, Q_LORA),
        "w_qabs": _absorbed_query_weight(w_uk, wq[:, :, :QK_NOPE]),
        "w_uqp": w_uqp,
        "w_uqp_t": w_uqp.T,
        "kv_g": kv_norm.reshape(1, KV_LORA),
        "conv_w": conv_w,
        "w_oc": w_o_conv.astype(_BF16),
        "w_uv": (jnp.transpose(w_uv, (1, 0, 2))[:, :, None, :]
                 * jnp.eye(N_HEADS, dtype=_F32)[:, None, :, None]
                 ).reshape(N_HEADS * KV_LORA, N_HEADS * V_HEAD).astype(_BF16),
        "w_om": w_o_mla.astype(_BF16),
        "w_out": w_out.astype(_BF16),
        "post_g": post_norm.reshape(1, D_MODEL),
    }


def _rope_angles(pos, frequency_major=False):
    inv = ROPE_BASE ** (-jnp.arange(HALF_ROPE, dtype=_F32) / HALF_ROPE)
    if frequency_major:
        ang = inv[:, None] * pos.astype(_F32)[None, :]
    else:
        ang = pos.astype(_F32)[:, None] * inv[None, :]
    return jnp.cos(ang), jnp.sin(ang)


def _rope_tables_rows(pos):
    cos, sin = _rope_angles(pos)
    zero = jnp.zeros_like(sin)
    reps = LANES // QK_ROPE
    return (jnp.tile(jnp.concatenate([cos, cos], axis=1), (1, reps)),
            jnp.tile(jnp.concatenate([zero, sin], axis=1), (1, reps)),
            jnp.tile(jnp.concatenate([-sin, zero], axis=1), (1, reps)))


SAMPLE_STREAMS_PER_STEP = 4
PROMPT_ROW_TILE = 1024
PROMPT_Q_TILE = 512
PROMPT_K_TILE = 512


def kernel(x_prompt, x_sample, cache_kv_latent, cache_k_rope, state_conv, pre_norm, w_in, q_norm, w_uq, kv_norm,
           w_uk, w_uv, w_o_mla, conv_w, w_o_conv, w_out, post_norm):
    depth = pre_norm.shape[0]
    assert depth == 1
    b, t, _ = x_prompt.shape
    nb, t_new, _ = x_sample.shape
    past_len = cache_kv_latent.shape[2]
    lyr = 0
    w = _prep_weights(pre_norm[lyr], w_in[lyr], q_norm[lyr], w_uq[lyr], kv_norm[lyr], w_uk[lyr], w_uv[lyr],
                      w_o_mla[lyr], conv_w[lyr], w_o_conv[lyr], w_out[lyr], post_norm[lyr])

    cos_t, sin_t = _rope_angles(jnp.arange(t, dtype=jnp.int32), frequency_major=True)
    zero_state = jnp.zeros((b, CONV_K - 1, CONV_WIDTH), _F32)
    qt, kc, vt, ckv_p, kpe_t, ga, sa, mb, cv_p = _project_prompt(
        x_prompt, cos_t, sin_t, zero_state, w, tm=PROMPT_ROW_TILE, tq=PROMPT_Q_TILE)
    y_p = _attend_prompt(qt, kc, vt, ga, sa, mb, x_prompt, w, tq=PROMPT_Q_TILE, tk=PROMPT_K_TILE)

    n_rows = nb * t_new
    tabs_s = _rope_tables_rows(past_len + jnp.arange(n_rows, dtype=jnp.int32) % t_new)
    xs = x_sample.reshape(1, n_rows, D_MODEL)
    q_s, kc_s, ckv_s, kpe_s, ga_s, sa_s, mb_s, cv_s = _project_sample(
        xs, *tabs_s, state_conv[lyr], w, n_seq=nb)
    y_s = _attend_sample(q_s, cache_kv_latent[lyr], jnp.swapaxes(cache_k_rope[lyr], 1, 2),
                         kc_s.reshape(nb, t_new, KEY_DIM),
                         ga_s[0], sa_s[0], mb_s[0], xs[0], w)

    return (y_p, y_s.reshape(nb, t_new, D_MODEL),
            ckv_p[None], jnp.swapaxes(kpe_t, 1, 2)[None], cv_p[None],
            ckv_s.reshape(1, nb, t_new, KV_LORA), kpe_s.reshape(1, nb, t_new, QK_ROPE), cv_s[None])
```

```python
import functools

import numpy as np
import jax
import jax.numpy as jnp
from jax import lax
from jax.experimental import pallas as pl
from jax.experimental.pallas import tpu as pltpu

N_HEADS = 8
QK_NOPE = 64
QK_ROPE = 32
V_HEAD = 64
Q_LORA = 256
KV_LORA = 128
MLA_WIDTH = N_HEADS * V_HEAD
CONV_WIDTH = 512
CONV_K = 3
D_MODEL = 1024
CHUNK = 64
ROPE_BASE = 10000.0
EPS = 1e-6
SM_SCALE = (QK_NOPE + QK_ROPE) ** -0.5
NEG_INF = -1e30
LOG2E = 1.4426950408889634

LANES = 128
MXU_DIM = 256
COL_BLOCK = MXU_DIM
REDUCE_SLAB = 256
EPILOGUE_STAGES = 4
STAGE1_LEAD = 1
HALF_ROPE = QK_ROPE // 2
KEY_DIM = KV_LORA + QK_ROPE
HEADS_PER_ROPE_BLOCK = LANES // QK_ROPE
Q_SCALE = SM_SCALE * LOG2E
VMEM_LIMIT_BYTES = 56 * 1024 * 1024

_O_TAIL = Q_LORA + KV_LORA + QK_ROPE
_QKV_COLS = -(-_O_TAIL // MXU_DIM) * MXU_DIM
_O_GM = 0
_O_CB = _O_GM + MLA_WIDTH
_O_CC = _O_CB + CONV_WIDTH
_O_CX = _O_CC + CONV_WIDTH
_O_GC = _O_CX + CONV_WIDTH
_O_MM = _O_GC + CONV_WIDTH
_O_MC = _O_MM + D_MODEL
_TAIL_COLS = _O_MC + D_MODEL

_F32 = jnp.float32
_BF16 = jnp.bfloat16
_NT = (((1,), (1,)), ((), ()))


def _rms(v, g):
    return v * lax.rsqrt(jnp.mean(v * v, axis=-1, keepdims=True) + EPS) * g


def _sigmoid(v):
    return 0.5 * jnp.tanh(0.5 * v) + 0.5


def _silu(v):
    return v * _sigmoid(v)


def _dot(a, b):
    return jnp.dot(a, b, preferred_element_type=_F32)


def _dot_nt(a, b):
    return lax.dot_general(a, b, _NT, preferred_element_type=_F32)


def _tail_proj(h, w_in_t_ref):
    def proj(off, n):
        return _dot_nt(h, w_in_t_ref[_O_TAIL + off:_O_TAIL + off + n, :])
    return proj


def _conv_branch(proj, st_ref, conv_w_ref, w_oc_ref, mb_ref, cst_ref, carry_ref, *,
                 tm, n_seq, carry_state):
    seq_len = tm // n_seq
    u = proj(_O_CC, CONV_WIDTH) * proj(_O_CX, CONV_WIDTH)
    w0 = conv_w_ref[0:1, :]
    w1 = conv_w_ref[1:2, :]
    w2 = conv_w_ref[2:3, :]
    if carry_state:
        @pl.when(pl.program_id(1) == 0)
        def _():
            carry_ref[0:CONV_K - 1, :] = st_ref[0]
    convs = []
    for j in range(n_seq):
        u_j = u[j * seq_len:(j + 1) * seq_len]
        prev = carry_ref[0:CONV_K - 1, :] if carry_state else st_ref[j]
        row = lax.broadcasted_iota(jnp.int32, (seq_len, 1), 0)
        u_1 = jnp.where(row == 0, prev[1:2, :], pltpu.roll(u_j, 1, 0))
        u_2 = jnp.where(row == 0, prev[0:1, :], jnp.where(row == 1, prev[1:2, :], pltpu.roll(u_j, 2, 0)))
        convs.append(w0 * u_2 + w1 * u_1 + w2 * u_j)
        new_state = u_j[seq_len - (CONV_K - 1):, :]
        cst_ref[j] = new_state
        if carry_state:
            carry_ref[0:CONV_K - 1, :] = new_state
    conv = convs[0] if n_seq == 1 else jnp.concatenate(convs, axis=0)
    c_b = proj(_O_CB, CONV_WIDTH)
    g_conv = proj(_O_GC, CONV_WIDTH)
    merge_gate = _sigmoid(proj(_O_MC, D_MODEL))
    bb = (c_b * conv * _silu(g_conv)).astype(_BF16)
    branch_b = _dot(bb, w_oc_ref[...])
    mb_ref[0] = (merge_gate * branch_b).astype(_BF16)


def _proj_prompt_kernel(x_ref, cos_ref, sin_ref, st_ref, pre_g_ref, w_in_t_ref, q_g_ref,
                        w_qabs_ref, w_uqp_t_ref, kv_g_ref, conv_w_ref, w_oc_ref,
                        qt_ref, kc_ref, vt_ref, ckv_ref, kpet_ref, ga_ref, sa_ref, mb_ref, cst_ref,
                        carry_ref, *, tq):
    tm = x_ref.shape[1]
    h = _rms(x_ref[0], pre_g_ref[...]).astype(_BF16)
    cos_t = cos_ref[...]
    sin_t = sin_ref[...]

    qkv = _dot_nt(h, w_in_t_ref[0:_QKV_COLS, :])
    proj = _tail_proj(h, w_in_t_ref)
    sa_ref[0] = _sigmoid(proj(_O_MM, D_MODEL)).astype(_BF16)
    ga_ref[0] = _silu(proj(_O_GM, MLA_WIDTH)).astype(_BF16)
    qn = _rms(qkv[:, :Q_LORA], q_g_ref[...]).astype(_BF16)
    q_abs_t = _dot_nt(w_qabs_ref[...], qn)
    q_pe_t = _dot_nt(w_uqp_t_ref[...], qn).reshape(N_HEADS, QK_ROPE, tm)
    x1 = q_pe_t[:, :HALF_ROPE, :]
    x2 = q_pe_t[:, HALF_ROPE:, :]
    r1 = (x1 * cos_t - x2 * sin_t) * Q_SCALE
    r2 = (x2 * cos_t + x1 * sin_t) * Q_SCALE
    groups = tq // COL_BLOCK
    for j in range(tm // tq):
        for hd in range(N_HEADS):
            for g in range(groups):
                tok = slice(j * tq + g * COL_BLOCK, j * tq + (g + 1) * COL_BLOCK)
                cb = hd * groups + g
                qt_ref[0, j, cb, 0:KV_LORA, :] = (
                    q_abs_t[hd * KV_LORA:(hd + 1) * KV_LORA, tok] * Q_SCALE).astype(_BF16)
                qt_ref[0, j, cb, KV_LORA:KV_LORA + HALF_ROPE, :] = r1[hd][:, tok].astype(_BF16)
                qt_ref[0, j, cb, KV_LORA + HALF_ROPE:KEY_DIM, :] = r2[hd][:, tok].astype(_BF16)

    kvr = qkv[:, Q_LORA:]
    ckv = _rms(kvr[:, :KV_LORA], kv_g_ref[...])
    ckv_ref[0] = ckv
    ckv_t = ckv.T.astype(_BF16)
    for u in range(tm // MXU_DIM):
        vt_ref[0, u] = ckv_t[:, u * MXU_DIM:(u + 1) * MXU_DIM]
    kr_t = kvr[:, KV_LORA:].T
    k1 = kr_t[0:HALF_ROPE]
    k2 = kr_t[HALF_ROPE:QK_ROPE]
    kpe_t = jnp.concatenate([k1 * cos_t - k2 * sin_t, k2 * cos_t + k1 * sin_t], axis=0)
    kpet_ref[0] = kpe_t
    kpe = jnp.concatenate([kpe_t, jnp.zeros((LANES - QK_ROPE, tm), _F32)], axis=0).T
    kc_ref[0, :, 0:KV_LORA] = ckv.astype(_BF16)
    kc_ref[0, :, KV_LORA:KEY_DIM] = kpe[:, :QK_ROPE].astype(_BF16)

    _conv_branch(proj, st_ref, conv_w_ref, w_oc_ref, mb_ref, cst_ref, carry_ref,
                 tm=tm, n_seq=1, carry_state=True)


def _project_prompt(x, cos_t, sin_t, state, w, *, tm, tq):
    b, t, _ = x.shape

    def rows(width, dtype):
        return (jax.ShapeDtypeStruct((b, t, width), dtype),
                pl.BlockSpec((1, tm, width), lambda bi, ti: (bi, ti, 0)))

    def full(a):
        return pl.BlockSpec(a.shape, lambda bi, ti: (0,) * a.ndim)

    outs = [
        (jax.ShapeDtypeStruct((b, t // tq, N_HEADS * tq // COL_BLOCK, KEY_DIM, COL_BLOCK), _BF16),
         pl.BlockSpec((1, tm // tq, N_HEADS * tq // COL_BLOCK, KEY_DIM, COL_BLOCK),
                      lambda bi, ti: (bi, ti, 0, 0, 0))),
        rows(KEY_DIM, _BF16),
        (jax.ShapeDtypeStruct((b, t // MXU_DIM, KV_LORA, MXU_DIM), _BF16),
         pl.BlockSpec((1, tm // MXU_DIM, KV_LORA, MXU_DIM), lambda bi, ti: (bi, ti, 0, 0))),
        rows(KV_LORA, _F32),
        (jax.ShapeDtypeStruct((b, QK_ROPE, t), _F32),
         pl.BlockSpec((1, QK_ROPE, tm), lambda bi, ti: (bi, 0, ti))),
        rows(MLA_WIDTH, _BF16),
        rows(D_MODEL, _BF16),
        rows(D_MODEL, _BF16),
        (jax.ShapeDtypeStruct((b, CONV_K - 1, CONV_WIDTH), _F32),
         pl.BlockSpec((1, CONV_K - 1, CONV_WIDTH), lambda bi, ti: (bi, 0, 0))),
    ]
    table_spec = pl.BlockSpec((HALF_ROPE, tm), lambda bi, ti: (0, ti))
    weights = (w["pre_g"], w["w_in_t"], w["q_g"], w["w_qabs"], w["w_uqp_t"],
               w["kv_g"], w["conv_w"], w["w_oc"])
    return pl.pallas_call(
        functools.partial(_proj_prompt_kernel, tq=tq),
        grid=(b, t // tm),
        in_specs=[pl.BlockSpec((1, tm, D_MODEL), lambda bi, ti: (bi, ti, 0)),
                  table_spec, table_spec,
                  pl.BlockSpec((1, CONV_K - 1, CONV_WIDTH), lambda bi, ti: (bi, 0, 0))]
                 + [full(a) for a in weights],
        out_specs=[o[1] for o in outs],
        out_shape=[o[0] for o in outs],
        scratch_shapes=[pltpu.VMEM((8, CONV_WIDTH), _F32)],
        compiler_params=pltpu.CompilerParams(
            dimension_semantics=("arbitrary", "arbitrary"), vmem_limit_bytes=VMEM_LIMIT_BYTES),
        name="proj_prompt",
    )(x, cos_t, sin_t, state, *weights)


def _proj_sample_kernel(x_ref, cos_ref, sn1_ref, sn2_ref, st_ref, pre_g_ref, w_in_t_ref, q_g_ref,
                        w_qabs_ref, w_uqp_ref, kv_g_ref, conv_w_ref, w_oc_ref,
                        q_ref, kc_ref, ckv_ref, kpe_ref, ga_ref, sa_ref, mb_ref, cst_ref,
                        carry_ref, *, n_seq):
    h = _rms(x_ref[0], pre_g_ref[...]).astype(_BF16)
    cos = cos_ref[...]
    sn1 = sn1_ref[...]
    sn2 = sn2_ref[...]

    def rope(v):
        return (v * cos + pltpu.roll(v, HALF_ROPE, 1) * sn1
                + pltpu.roll(v, LANES - HALF_ROPE, 1) * sn2)

    qkv = _dot_nt(h, w_in_t_ref[0:_QKV_COLS, :])
    qn = _rms(qkv[:, :Q_LORA], q_g_ref[...]).astype(_BF16)
    q_abs = _dot_nt(qn, w_qabs_ref[...])
    q_pe = _dot(qn, w_uqp_ref[...])
    n_rope_blocks = N_HEADS // HEADS_PER_ROPE_BLOCK
    q_rot = [rope(q_pe[:, j * LANES:(j + 1) * LANES]) * Q_SCALE for j in range(n_rope_blocks)]
    for hd in range(N_HEADS):
        q_ref[0, hd, :, 0:KV_LORA] = (q_abs[:, hd * LANES:(hd + 1) * LANES] * Q_SCALE).astype(_BF16)
        grp = hd % HEADS_PER_ROPE_BLOCK
        blk = q_rot[hd // HEADS_PER_ROPE_BLOCK]
        if grp:
            blk = pltpu.roll(blk, LANES - grp * QK_ROPE, 1)
        q_ref[0, hd, :, KV_LORA:KEY_DIM] = blk[:, :QK_ROPE].astype(_BF16)

    kvr = qkv[:, Q_LORA:]
    ckv = _rms(kvr[:, :KV_LORA], kv_g_ref[...])
    k_rot = rope(kvr[:, KV_LORA:])
    ckv_ref[0] = ckv
    kpe_ref[0] = k_rot[:, :QK_ROPE]
    kc_ref[0, :, 0:KV_LORA] = ckv.astype(_BF16)
    kc_ref[0, :, KV_LORA:KEY_DIM] = k_rot[:, :QK_ROPE].astype(_BF16)

    proj = _tail_proj(h, w_in_t_ref)
    sa_ref[0] = _sigmoid(proj(_O_MM, D_MODEL)).astype(_BF16)
    ga_ref[0] = _silu(proj(_O_GM, MLA_WIDTH)).astype(_BF16)
    _conv_branch(proj, st_ref, conv_w_ref, w_oc_ref, mb_ref, cst_ref, carry_ref,
                 tm=x_ref.shape[1], n_seq=n_seq, carry_state=False)


def _project_sample(x, cos, sn1, sn2, state, w, *, n_seq):
    _, tm, _ = x.shape

    def rows(width, dtype):
        return (jax.ShapeDtypeStruct((1, tm, width), dtype), pl.BlockSpec((1, tm, width), lambda i: (0, 0, 0)))

    def full(a):
        return pl.BlockSpec(a.shape, lambda i: (0,) * a.ndim)

    outs = [
        (jax.ShapeDtypeStruct((1, N_HEADS, tm, KEY_DIM), _BF16),
         pl.BlockSpec((1, N_HEADS, tm, KEY_DIM), lambda i: (0, 0, 0, 0))),
        rows(KEY_DIM, _BF16), rows(KV_LORA, _F32), rows(QK_ROPE, _F32),
        rows(MLA_WIDTH, _BF16), rows(D_MODEL, _BF16), rows(D_MODEL, _BF16),
        (jax.ShapeDtypeStruct((n_seq, CONV_K - 1, CONV_WIDTH), _F32),
         pl.BlockSpec((n_seq, CONV_K - 1, CONV_WIDTH), lambda i: (0, 0, 0))),
    ]
    weights = (w["pre_g"], w["w_in_t"], w["q_g"], w["w_qabs"], w["w_uqp"],
               w["kv_g"], w["conv_w"], w["w_oc"])
    return pl.pallas_call(
        functools.partial(_proj_sample_kernel, n_seq=n_seq),
        grid=(1,),
        in_specs=[full(x), full(cos), full(sn1), full(sn2), full(state)] + [full(a) for a in weights],
        out_specs=[o[1] for o in outs],
        out_shape=[o[0] for o in outs],
        scratch_shapes=[pltpu.VMEM((8, CONV_WIDTH), _F32)],
        compiler_params=pltpu.CompilerParams(
            dimension_semantics=("arbitrary",), vmem_limit_bytes=VMEM_LIMIT_BYTES),
        name="proj_sample",
    )(x, cos, sn1, sn2, state, *weights)


def _epilogue(o_lat, ga, sa, mb, x, w_uv_ref, w_om_ref, w_out_ref, post_g_ref):
    o = _dot(o_lat.astype(_BF16), w_uv_ref[...])
    branch_a = _dot((o * ga.astype(_F32)).astype(_BF16), w_om_ref[...])
    merged = sa.astype(_F32) * branch_a + mb.astype(_F32)
    z = _dot(merged.astype(_BF16), w_out_ref[...])
    return x + _rms(z, post_g_ref[...])


def _col_reduce(v, op):
    n_keys, cols = v.shape
    slabs = max(n_keys // REDUCE_SLAB, 1)
    part = op(v.reshape(slabs, n_keys // slabs, cols), axis=0)
    return op(part, axis=0, keepdims=True)


def _attn_prompt_kernel(qt_ref, qn_ref, kc_ref, vt_ref, bias_ref, ga_ref, sa_ref, mb_ref, x_ref,
                        w_uv_ref, w_om_ref, w_out_ref, post_g_ref, y_ref,
                        m_ref, l_ref, acc_ref, s_ref, mc_ref, *, tq, tk):
    i = pl.program_id(1)
    units = tk // MXU_DIM
    q_tiles_per_k_tile = tk // tq
    edge = i // q_tiles_per_k_tile
    groups = tq // COL_BLOCK
    n_cb = N_HEADS * groups

    m_ref[...] = jnp.full(m_ref.shape, NEG_INF, _F32)
    l_ref[...] = jnp.zeros(l_ref.shape, _F32)
    acc_ref[...] = jnp.zeros(acc_ref.shape, _F32)

    def keys_of(tile):
        return kc_ref[0, pl.ds(pl.multiple_of(tile * tk, tk), tk), :]

    def values_of(tile):
        return jnp.concatenate([vt_ref[0, tile * units + u] for u in range(units)], axis=1)

    def stage1(k, cb, slot=0, q_tile=None, q_ref=qt_ref, split=True):
        qb = q_ref[0, 0, cb]
        if split:
            s = jnp.concatenate([_dot(k[u * MXU_DIM:(u + 1) * MXU_DIM], qb) for u in range(units)], axis=0)
        else:
            s = _dot(k, qb)
        if q_tile is not None:
            s = s + bias_ref[(q_tile % q_tiles_per_k_tile) * groups + cb % groups]
        s_ref[slot, cb] = s
        mc_ref[slot, cb] = _col_reduce(s, jnp.max)

    def stage2(vts, cb, slot=0):
        m_prev = m_ref[cb]
        m_new = jnp.maximum(m_prev, mc_ref[slot, cb])
        alpha = jnp.exp2(m_prev - m_new)
        p = jnp.exp2(s_ref[slot, cb] - m_new)
        l_ref[cb] = alpha * l_ref[cb] + _col_reduce(p, jnp.sum)
        pv = _dot(vts, p.astype(_BF16))
        acc_ref[cb] = alpha * acc_ref[cb] + pv
        m_ref[cb] = m_new

    @pl.when(i == 0)
    def _():
        k_edge = keys_of(edge)
        for cb in range(n_cb):
            stage1(k_edge, cb, q_tile=i)

    def advance(j, rd, wr):
        vts = values_of(jnp.where(j == 0, edge, j - 1))
        k = keys_of(j)
        in_place = rd == wr
        lead = 0 if in_place else STAGE1_LEAD
        for cb in range(lead):
            stage1(k, cb, wr, split=in_place)
        for cb in range(n_cb):
            stage2(vts, cb, rd)
            if cb + lead < n_cb:
                stage1(k, cb + lead, wr, split=in_place)

    def single(j, carry):
        advance(j, 0, 0)
        return carry

    def double(pair, carry):
        advance(odd + 2 * pair, 0, 1)
        advance(odd + 2 * pair + 1, 1, 0)
        return carry

    odd = edge % 2
    lax.fori_loop(0, odd, single, 0)
    lax.fori_loop(0, edge // 2, double, 0)

    last_vts = values_of(jnp.where(edge == 0, edge, edge - 1))
    nxt = jnp.minimum(i + 1, pl.num_programs(1) - 1)
    k_next = keys_of(nxt // q_tiles_per_k_tile)
    def drain(cb):
        stage2(last_vts, cb)
        stage1(k_next, cb, 0, q_tile=nxt, q_ref=qn_ref)

    def epilogue_stages(g):
        rows = slice(g * COL_BLOCK, (g + 1) * COL_BLOCK)
        o_lat = jnp.concatenate([(acc_ref[hd * groups + g] / l_ref[hd * groups + g]).T
                                 for hd in range(N_HEADS)], axis=1)
        o = _dot(o_lat.astype(_BF16), w_uv_ref[...])
        yield
        branch_a = _dot((o * ga_ref[0, rows, :].astype(_F32)).astype(_BF16), w_om_ref[...])
        yield
        merged = sa_ref[0, rows, :].astype(_F32) * branch_a + mb_ref[0, rows, :].astype(_F32)
        z = _dot(merged.astype(_BF16), w_out_ref[...])
        yield
        y_ref[0, rows, :] = x_ref[0, rows, :] + _rms(z, post_g_ref[...])
        yield

    pending = iter(())
    for g in range(groups):
        blocks = [hd * groups + g for hd in range(N_HEADS)]
        chunk = -(-len(blocks) // EPILOGUE_STAGES)
        for start in range(0, len(blocks), chunk):
            next(pending, None)
            for cb in blocks[start:start + chunk]:
                drain(cb)
        for _ in pending:
            pass
        pending = epilogue_stages(g)
    for _ in pending:
        pass


def _attend_prompt(qt, kc, vt, ga, sa, mb, x, w, *, tq, tk):
    b, t, _ = x.shape
    n_q = t // tq

    def row_spec(width):
        return pl.BlockSpec((1, tq, width), lambda bi, qi: (bi, qi, 0))

    def full(a):
        return pl.BlockSpec(a.shape, lambda bi, qi: (0,) * a.ndim)

    k_chunk = np.arange(tk)[None, :, None] // CHUNK
    q_chunk = (np.arange(tk // COL_BLOCK)[:, None, None] * COL_BLOCK
               + np.arange(COL_BLOCK)[None, None, :]) // CHUNK
    bias = jnp.asarray(np.where(k_chunk <= q_chunk, 0.0, NEG_INF), _F32)
    n_cb = N_HEADS * tq // COL_BLOCK

    weights = (w["w_uv"], w["w_om"], w["w_out"], w["post_g"])
    return pl.pallas_call(
        functools.partial(_attn_prompt_kernel, tq=tq, tk=tk),
        grid=(b, t // tq),
        in_specs=[pl.BlockSpec((1, 1, n_cb, KEY_DIM, COL_BLOCK), lambda bi, qi: (bi, qi, 0, 0, 0)),
                  pl.BlockSpec((1, 1, n_cb, KEY_DIM, COL_BLOCK),
                               lambda bi, qi: (bi, jnp.minimum(qi + 1, n_q - 1), 0, 0, 0)),
                  pl.BlockSpec((1, t, KEY_DIM), lambda bi, qi: (bi, 0, 0), pipeline_mode=pl.Buffered(1)),
                  pl.BlockSpec((1, t // MXU_DIM, KV_LORA, MXU_DIM), lambda bi, qi: (bi, 0, 0, 0),
                               pipeline_mode=pl.Buffered(1)),
                  full(bias),
                  row_spec(MLA_WIDTH), row_spec(D_MODEL), row_spec(D_MODEL), row_spec(D_MODEL)]
                 + [full(a) for a in weights],
        out_specs=row_spec(D_MODEL),
        out_shape=jax.ShapeDtypeStruct((b, t, D_MODEL), _F32),
        scratch_shapes=[pltpu.VMEM((n_cb, 1, COL_BLOCK), _F32), pltpu.VMEM((n_cb, 1, COL_BLOCK), _F32),
                        pltpu.VMEM((n_cb, KV_LORA, COL_BLOCK), _F32),
                        pltpu.VMEM((2, n_cb, tk, COL_BLOCK), _F32), pltpu.VMEM((2, n_cb, 1, COL_BLOCK), _F32)],
        compiler_params=pltpu.CompilerParams(
            dimension_semantics=("arbitrary", "arbitrary"), vmem_limit_bytes=VMEM_LIMIT_BYTES),
        name="attn_prompt",
    )(qt, qt, kc, vt, bias, ga, sa, mb, x, *weights)


def _attn_first(q, k, m_ref, l_ref, acc_ref, visible=None):
    s = _dot_nt(q, k)
    if visible is not None:
        s = jnp.where(visible, s, NEG_INF)
    m = jnp.max(s, axis=1, keepdims=True)
    p = jnp.exp2(s - m)
    m_ref[...] = jnp.broadcast_to(m, m_ref.shape)
    l_ref[...] = jnp.broadcast_to(jnp.sum(p, axis=1, keepdims=True), l_ref.shape)
    acc_ref[...] = _dot(p.astype(_BF16), k[:, :KV_LORA])


def _attn_update(q, k_lat, k_rope_t, m_ref, l_ref, acc_ref):
    s = _dot_nt(q[:, :KV_LORA], k_lat) + _dot(q[:, KV_LORA:], k_rope_t)
    m_prev = m_ref[...]
    m_new = jnp.maximum(m_prev, jnp.max(s, axis=1, keepdims=True))
    alpha = jnp.exp2(m_prev - m_new)
    p = jnp.exp2(s - jnp.concatenate([m_new] * (s.shape[1] // LANES), axis=1))
    l_ref[...] = alpha * l_ref[...] + jnp.sum(p, axis=1, keepdims=True)
    acc_ref[...] = alpha * acc_ref[...] + _dot(p.astype(_BF16), k_lat)
    m_ref[...] = m_new


def _attn_sample_kernel(q_ref, past_lat_ref, past_rope_ref, knew_ref, ga_ref, sa_ref, mb_ref, x_ref,
                        w_uv_ref, w_om_ref, w_out_ref, post_g_ref, y_ref,
                        m_ref, l_ref, acc_ref, o_ref, *, new_visible):
    bi = pl.program_id(0)
    per_step, t_new, _ = knew_ref.shape
    rows = N_HEADS * t_new
    if new_visible is None:
        visible = None
    else:
        q_pos = lax.broadcasted_iota(jnp.int32, (rows, t_new), 0) % t_new
        k_pos = lax.broadcasted_iota(jnp.int32, (rows, t_new), 1)
        visible = (k_pos + new_visible[0]) // CHUNK <= (q_pos + new_visible[0]) // CHUNK
    for g in range(per_step):
        q = q_ref[0, :, g * t_new:(g + 1) * t_new, :].reshape(rows, KEY_DIM)
        m_g, l_g, acc_g = m_ref.at[g], l_ref.at[g], acc_ref.at[g]
        _attn_first(q, knew_ref[g], m_g, l_g, acc_g, visible=visible)
        _attn_update(q, past_lat_ref[g].astype(_BF16), past_rope_ref[g].astype(_BF16), m_g, l_g, acc_g)
        o = acc_g[...] / l_g[...]
        r0 = pl.multiple_of((bi * per_step + g) * t_new, t_new)
        o_ref[pl.ds(r0, t_new), :] = jnp.concatenate(
            [o[hd * t_new:(hd + 1) * t_new] for hd in range(N_HEADS)], axis=1)

    @pl.when(bi == pl.num_programs(0) - 1)
    def _():
        y_ref[...] = _epilogue(o_ref[...], ga_ref[...], sa_ref[...], mb_ref[...], x_ref[...],
                               w_uv_ref, w_om_ref, w_out_ref, post_g_ref)


def _attend_sample(q, past_lat, past_rope, k_new, ga, sa, mb, x, w):
    nb, t_new, _ = k_new.shape
    past_len = past_lat.shape[1]
    n_rows = nb * t_new
    rows = N_HEADS * t_new
    per_step = SAMPLE_STREAMS_PER_STEP if nb % SAMPLE_STREAMS_PER_STEP == 0 else 1
    last_q, first_q = past_len + t_new - 1, past_len
    new_visible = None if last_q // CHUNK == first_q // CHUNK else (past_len,)

    def full(a):
        return pl.BlockSpec(a.shape, lambda bi: (0,) * a.ndim)

    weights = (w["w_uv"], w["w_om"], w["w_out"], w["post_g"])
    return pl.pallas_call(
        functools.partial(_attn_sample_kernel, new_visible=new_visible),
        grid=(nb // per_step,),
        in_specs=[pl.BlockSpec((1, N_HEADS, per_step * t_new, KEY_DIM), lambda bi: (0, 0, bi, 0)),
                  pl.BlockSpec((per_step, past_len, KV_LORA), lambda bi: (bi, 0, 0)),
                  pl.BlockSpec((per_step, QK_ROPE, past_len), lambda bi: (bi, 0, 0)),
                  pl.BlockSpec((per_step, t_new, KEY_DIM), lambda bi: (bi, 0, 0)),
                  full(ga), full(sa), full(mb), full(x)] + [full(a) for a in weights],
        out_specs=pl.BlockSpec((n_rows, D_MODEL), lambda bi: (0, 0)),
        out_shape=jax.ShapeDtypeStruct((n_rows, D_MODEL), _F32),
        scratch_shapes=[pltpu.VMEM((per_step, rows, LANES), _F32), pltpu.VMEM((per_step, rows, LANES), _F32),
                        pltpu.VMEM((per_step, rows, KV_LORA), _F32),
                        pltpu.VMEM((n_rows, N_HEADS * KV_LORA), _F32)],
        compiler_params=pltpu.CompilerParams(
            dimension_semantics=("arbitrary",), vmem_limit_bytes=VMEM_LIMIT_BYTES),
        name="attn_sample",
    )(q, past_lat, past_rope, k_new, ga, sa, mb, x, *weights)


def _absorb_kernel(w_uk_ref, w_uqn_ref, o_ref):
    for hd in range(N_HEADS):
        o_ref[hd] = lax.dot_general(w_uk_ref[hd], w_uqn_ref[hd], _NT, precision=lax.Precision.HIGHEST,
                                    preferred_element_type=_F32).astype(o_ref.dtype)


def _absorbed_query_weight(w_uk, w_uqn):
    out = pl.pallas_call(
        _absorb_kernel,
        out_shape=jax.ShapeDtypeStruct((N_HEADS, KV_LORA, Q_LORA), _BF16),
        name="absorb_q_weight",
    )(jnp.transpose(w_uk, (1, 0, 2)), jnp.transpose(w_uqn, (1, 0, 2)))
    return out.reshape(N_HEADS * KV_LORA, Q_LORA)


def _prep_weights(pre_norm, w_in, q_norm, w_uq, kv_norm, w_uk, w_uv, w_o_mla, conv_w, w_o_conv, w_out, post_norm):
    assert w_in.shape[1] == _O_TAIL + _TAIL_COLS
    wq = w_uq.reshape(Q_LORA, N_HEADS, QK_NOPE + QK_ROPE)
    w_uqp = wq[:, :, QK_NOPE:].reshape(Q_LORA, N_HEADS * QK_ROPE).astype(_BF16)
    return {
        "pre_g": pre_norm.reshape(1, D_MODEL),
        "w_in_t": w_in.T.astype(_BF16),
        "q_g": q_norm.reshape(1, Q_LORA),
        "w_qabs": _absorbed_query_weight(w_uk, wq[:, :, :QK_NOPE]),
        "w_uqp": w_uqp,
        "w_uqp_t": w_uqp.T,
        "kv_g": kv_norm.reshape(1, KV_LORA),
        "conv_w": conv_w,
        "w_oc": w_o_conv.astype(_BF16),
        "w_uv": (jnp.transpose(w_uv, (1, 0, 2))[:, :, None, :]
                 * jnp.eye(N_HEADS, dtype=_F32)[:, None, :, None]
                 ).reshape(N_HEADS * KV_LORA, N_HEADS * V_HEAD).astype(_BF16),
        "w_om": w_o_mla.astype(_BF16),
        "w_out": w_out.astype(_BF16),
        "post_g": post_norm.reshape(1, D_MODEL),
    }


def _rope_angles(pos, frequency_major=False):
    inv = ROPE_BASE ** (-jnp.arange(HALF_ROPE, dtype=_F32) / HALF_ROPE)
    if frequency_major:
        ang = inv[:, None] * pos.astype(_F32)[None, :]
    else:
        ang = pos.astype(_F32)[:, None] * inv[None, :]
    return jnp.cos(ang), jnp.sin(ang)


def _rope_tables_rows(pos):
    cos, sin = _rope_angles(pos)
    zero = jnp.zeros_like(sin)
    reps = LANES // QK_ROPE
    return (jnp.tile(jnp.concatenate([cos, cos], axis=1), (1, reps)),
            jnp.tile(jnp.concatenate([zero, sin], axis=1), (1, reps)),
            jnp.tile(jnp.concatenate([-sin, zero], axis=1), (1, reps)))


SAMPLE_STREAMS_PER_STEP = 4
PROMPT_ROW_TILE = 1024
PROMPT_Q_TILE = 512
PROMPT_K_TILE = 512


def kernel(x_prompt, x_sample, cache_kv_latent, cache_k_rope, state_conv, pre_norm, w_in, q_norm, w_uq, kv_norm,
           w_uk, w_uv, w_o_mla, conv_w, w_o_conv, w_out, post_norm):
    depth = pre_norm.shape[0]
    assert depth == 1
    b, t, _ = x_prompt.shape
    nb, t_new, _ = x_sample.shape
    past_len = cache_kv_latent.shape[2]
    lyr = 0
    w = _prep_weights(pre_norm[lyr], w_in[lyr], q_norm[lyr], w_uq[lyr], kv_norm[lyr], w_uk[lyr], w_uv[lyr],
                      w_o_mla[lyr], conv_w[lyr], w_o_conv[lyr], w_out[lyr], post_norm[lyr])

    cos_t, sin_t = _rope_angles(jnp.arange(t, dtype=jnp.int32), frequency_major=True)
    zero_state = jnp.zeros((b, CONV_K - 1, CONV_WIDTH), _F32)
    qt, kc, vt, ckv_p, kpe_t, ga, sa, mb, cv_p = _project_prompt(
        x_prompt, cos_t, sin_t, zero_state, w, tm=PROMPT_ROW_TILE, tq=PROMPT_Q_TILE)
    y_p = _attend_prompt(qt, kc, vt, ga, sa, mb, x_prompt, w, tq=PROMPT_Q_TILE, tk=PROMPT_K_TILE)

    n_rows = nb * t_new
    tabs_s = _rope_tables_rows(past_len + jnp.arange(n_rows, dtype=jnp.int32) % t_new)
    xs = x_sample.reshape(1, n_rows, D_MODEL)
    q_s, kc_s, ckv_s, kpe_s, ga_s, sa_s, mb_s, cv_s = _project_sample(
        xs, *tabs_s, state_conv[lyr], w, n_seq=nb)
    y_s = _attend_sample(q_s, cache_kv_latent[lyr], jnp.swapaxes(cache_k_rope[lyr], 1, 2),
                         kc_s.reshape(nb, t_new, KEY_DIM),
                         ga_s[0], sa_s[0], mb_s[0], xs[0], w)

    return (y_p, y_s.reshape(nb, t_new, D_MODEL),
            ckv_p[None], jnp.swapaxes(kpe_t, 1, 2)[None], cv_p[None],
            ckv_s.reshape(1, nb, t_new, KV_LORA), kpe_s.reshape(1, nb, t_new, QK_ROPE), cv_s[None])
```
